```python
import jax, jax.numpy as jnp
from jax import lax
import numpy as np

D_MODEL = 1024
BATCH = 16
SEQ = 2048
DEPTH = 4

N_MIXERS = 2
N_MLA_LAYERS = (DEPTH + 1) // 2
N_GMLP_LAYERS = DEPTH // 2
MLA_HEADS = 8
QK_NOPE_DIM = 128
QK_ROPE_DIM = 64
V_HEAD_DIM = 128
Q_LORA_RANK = 384
KV_LORA_RANK = 256
ROPE_BASE = 10000.0
Q_BLOCK = 128
GMLP_CHUNK = 128
GMLP_HALF = 2 * D_MODEL
GMLP_GROUPS = 8
GMLP_GROUP_DIM = GMLP_HALF // GMLP_GROUPS
D_FF = 4 * D_MODEL
PLE_DIM = 256
NORM_EPS = 1e-6
MAX_POS_OFFSET = 4096

kernel_name = "hybrid_mla_chunked_gmlp_trunk"


def rms_norm(x, g):
    xf = x.astype(jnp.float32)
    y = xf * lax.rsqrt(jnp.mean(xf * xf, axis=-1, keepdims=True) + NORM_EPS)
    return (y * g.astype(jnp.float32)).astype(x.dtype)


def layer_norm(x, g, b):
    xf = x.astype(jnp.float32)
    mu = jnp.mean(xf, axis=-1, keepdims=True)
    xc = xf - mu
    y = xc * lax.rsqrt(jnp.mean(xc * xc, axis=-1, keepdims=True) + NORM_EPS)
    return (y * g.astype(jnp.float32) + b.astype(jnp.float32)).astype(x.dtype)


def rope_cos_sin(positions):
    inv_freq = ROPE_BASE ** (-(jnp.arange(0, QK_ROPE_DIM, 2, dtype=jnp.float32) / QK_ROPE_DIM))
    ang = positions.astype(jnp.float32)[..., None] * inv_freq
    return jnp.cos(ang), jnp.sin(ang)


def apply_rope(x, cos, sin):
    x1, x2 = jnp.split(x.astype(jnp.float32), 2, axis=-1)
    return jnp.concatenate([x1 * cos - x2 * sin, x2 * cos + x1 * sin], axis=-1).astype(x.dtype)


def mla_mixer(hn, cos, sin, w_down, q_lora_g, kv_lora_g, w_uq, w_ukv,
              q_nope_g, q_rope_g, k_nope_g, k_rope_g, w_out):
    B, S, _ = hn.shape
    H = MLA_HEADS
    lat = hn @ w_down
    c_q, c_kv, k_rope = jnp.split(lat, [Q_LORA_RANK, Q_LORA_RANK + KV_LORA_RANK], axis=-1)
    c_q = rms_norm(c_q, q_lora_g)
    c_kv = rms_norm(c_kv, kv_lora_g)
    q = (c_q @ w_uq).reshape(B, S, H, QK_NOPE_DIM + QK_ROPE_DIM)
    q_nope, q_rope = jnp.split(q, [QK_NOPE_DIM], axis=-1)
    kv = (c_kv @ w_ukv).reshape(B, S, H, QK_NOPE_DIM + V_HEAD_DIM)
    k_nope, v = jnp.split(kv, [QK_NOPE_DIM], axis=-1)
    q_nope = rms_norm(q_nope, q_nope_g)
    q_rope = apply_rope(rms_norm(q_rope, q_rope_g), cos[:, :, None], sin[:, :, None])
    k_nope = rms_norm(k_nope, k_nope_g)
    k_rope = apply_rope(rms_norm(k_rope, k_rope_g), cos, sin)
    scale = (QK_NOPE_DIM + QK_ROPE_DIM) ** -0.5
    outs = []
    for j in range(S // Q_BLOCK):
        q0 = j * Q_BLOCK
        kend = q0 + Q_BLOCK
        s = (jnp.einsum('bqhd,bkhd->bhqk', q_nope[:, q0:kend], k_nope[:, :kend])
             + jnp.einsum('bqhr,bkr->bhqk', q_rope[:, q0:kend], k_rope[:, :kend]))
        s = s.astype(jnp.float32) * scale
        causal = jnp.arange(kend)[None, :] <= (q0 + jnp.arange(Q_BLOCK))[:, None]
        s = jnp.where(causal, s, -jnp.inf)
        pr = jax.nn.softmax(s, axis=-1).astype(v.dtype)
        outs.append(jnp.einsum('bhqk,bkhd->bqhd', pr, v[:, :kend]))
    o = jnp.concatenate(outs, axis=1).reshape(B, S, H * V_HEAD_DIM)
    return o @ w_out


def gmlp_mixer(hn, w_in, ln_g, ln_b, w_s, b_s, w_out):
    B, S, _ = hn.shape
    z = jax.nn.gelu(hn @ w_in)
    u, v = jnp.split(z, 2, axis=-1)
    v = layer_norm(v, ln_g, ln_b)
    v = v.reshape(B, S // GMLP_CHUNK, GMLP_CHUNK, GMLP_GROUPS, GMLP_GROUP_DIM)
    mask = jnp.tril(jnp.ones((GMLP_CHUNK, GMLP_CHUNK), dtype=w_s.dtype))
    ws = w_s * mask
    sv = jnp.einsum('gts,bnsgd->bntgd', ws, v) + b_s.T[None, None, :, :, None]
    y = u * sv.reshape(B, S, GMLP_HALF)
    return y @ w_out


def _fwd_setup_inputs(seed: int = 0) -> dict:
    key = jax.random.key(seed)
    ks = iter(jax.random.split(key, 40))

    def nrm(shape, scale):
        return jax.random.normal(next(ks), shape, dtype=jnp.float32) * scale

    def gain(shape):
        return 1.0 + nrm(shape, 0.02)

    nA, nB, D = N_MLA_LAYERS, N_GMLP_LAYERS, D_MODEL
    x = nrm((BATCH, SEQ, D), 1.0)
    p = nrm((DEPTH, BATCH, SEQ, PLE_DIM), 1.0)
    offs = jax.random.randint(next(ks), (BATCH, 1), 0, MAX_POS_OFFSET, dtype=jnp.int32)
    positions = offs + jnp.arange(SEQ, dtype=jnp.int32)[None, :]
    down_w = Q_LORA_RANK + KV_LORA_RANK + QK_ROPE_DIM
    return {
        "x": x,
        "p": p,
        "positions": positions,
        "norm_mix": gain((DEPTH, D)),
        "norm_ffn": gain((DEPTH, D)),
        "norm_ple": gain((DEPTH, D)),
        "mla_w_down": nrm((nA, D, down_w), D ** -0.5),
        "mla_q_lora_g": gain((nA, Q_LORA_RANK)),
        "mla_kv_lora_g": gain((nA, KV_LORA_RANK)),
        "mla_w_uq": nrm((nA, Q_LORA_RANK, MLA_HEADS * (QK_NOPE_DIM + QK_ROPE_DIM)), Q_LORA_RANK ** -0.5),
        "mla_w_ukv": nrm((nA, KV_LORA_RANK, MLA_HEADS * (QK_NOPE_DIM + V_HEAD_DIM)), KV_LORA_RANK ** -0.5),
        "mla_q_nope_g": gain((nA, QK_NOPE_DIM)),
        "mla_q_rope_g": gain((nA, QK_ROPE_DIM)),
        "mla_k_nope_g": gain((nA, QK_NOPE_DIM)),
        "mla_k_rope_g": gain((nA, QK_ROPE_DIM)),
        "mla_w_out": nrm((nA, MLA_HEADS * V_HEAD_DIM, D), 0.5 * (MLA_HEADS * V_HEAD_DIM) ** -0.5),
        "gmlp_w_in": nrm((nB, D, 2 * GMLP_HALF), D ** -0.5),
        "gmlp_ln_g": gain((nB, GMLP_HALF)),
        "gmlp_ln_b": nrm((nB, GMLP_HALF), 0.02),
        "gmlp_w_s": nrm((nB, GMLP_GROUPS, GMLP_CHUNK, GMLP_CHUNK), GMLP_CHUNK ** -0.5),
        "gmlp_b_s": 1.0 + nrm((nB, GMLP_GROUPS, GMLP_CHUNK), 0.1),
        "gmlp_w_out": nrm((nB, GMLP_HALF, D), 0.5 * GMLP_HALF ** -0.5),
        "ffn_w_up": nrm((DEPTH, D, D_FF), D ** -0.5),
        "ffn_w_down": nrm((DEPTH, D_FF, D), 0.5 * D_FF ** -0.5),
        "ple_w_gate": nrm((DEPTH, D, D), D ** -0.5),
        "ple_w_proj": nrm((DEPTH, PLE_DIM, D), PLE_DIM ** -0.5),
    }


def _fwd_reference(x, p, positions, norm_mix, norm_ffn, norm_ple,
              mla_w_down, mla_q_lora_g, mla_kv_lora_g, mla_w_uq, mla_w_ukv,
              mla_q_nope_g, mla_q_rope_g, mla_k_nope_g, mla_k_rope_g, mla_w_out,
              gmlp_w_in, gmlp_ln_g, gmlp_ln_b, gmlp_w_s, gmlp_b_s, gmlp_w_out,
              ffn_w_up, ffn_w_down, ple_w_gate, ple_w_proj):
    cos, sin = rope_cos_sin(positions)
    h = x
    for i in range(DEPTH):
        hn = rms_norm(h, norm_mix[i])
        j = i // N_MIXERS
        if i % N_MIXERS == 0:
            mix = mla_mixer(hn, cos, sin, mla_w_down[j], mla_q_lora_g[j], mla_kv_lora_g[j],
                            mla_w_uq[j], mla_w_ukv[j], mla_q_nope_g[j], mla_q_rope_g[j],
                            mla_k_nope_g[j], mla_k_rope_g[j], mla_w_out[j])
        else:
            mix = gmlp_mixer(hn, gmlp_w_in[j], gmlp_ln_g[j], gmlp_ln_b[j],
                             gmlp_w_s[j], gmlp_b_s[j], gmlp_w_out[j])
        h = h + mix
        hn = rms_norm(h, norm_ffn[i])
        h = h + jnp.square(jax.nn.relu(hn @ ffn_w_up[i])) @ ffn_w_down[i]
        hn = rms_norm(h, norm_ple[i])
        h = h + jax.nn.sigmoid(hn @ ple_w_gate[i]) * (p[i] @ ple_w_proj[i])
    return h


import jax as _jax
import jax.numpy as _jnp

TWIN_FORMAT = 'train_step'
FWD_PARAMS = ['x', 'p', 'positions', 'norm_mix', 'norm_ffn', 'norm_ple', 'mla_w_down', 'mla_q_lora_g', 'mla_kv_lora_g', 'mla_w_uq', 'mla_w_ukv', 'mla_q_nope_g', 'mla_q_rope_g', 'mla_k_nope_g', 'mla_k_rope_g', 'mla_w_out', 'gmlp_w_in', 'gmlp_ln_g', 'gmlp_ln_b', 'gmlp_w_s', 'gmlp_b_s', 'gmlp_w_out', 'ffn_w_up', 'ffn_w_down', 'ple_w_gate', 'ple_w_proj']
TWIN_WEIGHTS = ['norm_mix', 'norm_ffn', 'norm_ple', 'mla_w_down', 'mla_q_lora_g', 'mla_kv_lora_g', 'mla_w_uq', 'mla_w_ukv', 'mla_q_nope_g', 'mla_q_rope_g', 'mla_k_nope_g', 'mla_k_rope_g', 'mla_w_out', 'gmlp_w_in', 'gmlp_ln_g', 'gmlp_ln_b', 'gmlp_w_s', 'gmlp_b_s', 'gmlp_w_out', 'ffn_w_up', 'ffn_w_down', 'ple_w_gate', 'ple_w_proj']
TWIN_DIFF_INPUT = 'x'
TWIN_INPUTS = ['x', 'p', 'positions', 'norm_mix', 'norm_ffn', 'norm_ple', 'mla_w_down', 'mla_q_lora_g', 'mla_kv_lora_g', 'mla_w_uq', 'mla_w_ukv', 'mla_q_nope_g', 'mla_q_rope_g', 'mla_k_nope_g', 'mla_k_rope_g', 'mla_w_out', 'gmlp_w_in', 'gmlp_ln_g', 'gmlp_ln_b', 'gmlp_w_s', 'gmlp_b_s', 'gmlp_w_out', 'ffn_w_up', 'ffn_w_down', 'ple_w_gate', 'ple_w_proj', 'loss_target', 'm_norm_mix', 'm_norm_ffn', 'm_norm_ple', 'm_mla_w_down', 'm_mla_q_lora_g', 'm_mla_kv_lora_g', 'm_mla_w_uq', 'm_mla_w_ukv', 'm_mla_q_nope_g', 'm_mla_q_rope_g', 'm_mla_k_nope_g', 'm_mla_k_rope_g', 'm_mla_w_out', 'm_gmlp_w_in', 'm_gmlp_ln_g', 'm_gmlp_ln_b', 'm_gmlp_w_s', 'm_gmlp_b_s', 'm_gmlp_w_out', 'm_ffn_w_up', 'm_ffn_w_down', 'm_ple_w_gate', 'm_ple_w_proj', 'v_norm_mix', 'v_norm_ffn', 'v_norm_ple', 'v_mla_w_down', 'v_mla_q_lora_g', 'v_mla_kv_lora_g', 'v_mla_w_uq', 'v_mla_w_ukv', 'v_mla_q_nope_g', 'v_mla_q_rope_g', 'v_mla_k_nope_g', 'v_mla_k_rope_g', 'v_mla_w_out', 'v_gmlp_w_in', 'v_gmlp_ln_g', 'v_gmlp_ln_b', 'v_gmlp_w_s', 'v_gmlp_b_s', 'v_gmlp_w_out', 'v_ffn_w_up', 'v_ffn_w_down', 'v_ple_w_gate', 'v_ple_w_proj']
TWIN_OUTPUTS = ['loss', 'grad_x', 'grad_norm_mix', 'grad_norm_ffn', 'grad_norm_ple', 'grad_mla_w_down', 'grad_mla_q_lora_g', 'grad_mla_kv_lora_g', 'grad_mla_w_uq', 'grad_mla_w_ukv', 'grad_mla_q_nope_g', 'grad_mla_q_rope_g', 'grad_mla_k_nope_g', 'grad_mla_k_rope_g', 'grad_mla_w_out', 'grad_gmlp_w_in', 'grad_gmlp_ln_g', 'grad_gmlp_ln_b', 'grad_gmlp_w_s', 'grad_gmlp_b_s', 'grad_gmlp_w_out', 'grad_ffn_w_up', 'grad_ffn_w_down', 'grad_ple_w_gate', 'grad_ple_w_proj', 'delta_norm_mix', 'delta_norm_ffn', 'delta_norm_ple', 'delta_mla_w_down', 'delta_mla_q_lora_g', 'delta_mla_kv_lora_g', 'delta_mla_w_uq', 'delta_mla_w_ukv', 'delta_mla_q_nope_g', 'delta_mla_q_rope_g', 'delta_mla_k_nope_g', 'delta_mla_k_rope_g', 'delta_mla_w_out', 'delta_gmlp_w_in', 'delta_gmlp_ln_g', 'delta_gmlp_ln_b', 'delta_gmlp_w_s', 'delta_gmlp_b_s', 'delta_gmlp_w_out', 'delta_ffn_w_up', 'delta_ffn_w_down', 'delta_ple_w_gate', 'delta_ple_w_proj', 'new_m_norm_mix', 'new_m_norm_ffn', 'new_m_norm_ple', 'new_m_mla_w_down', 'new_m_mla_q_lora_g', 'new_m_mla_kv_lora_g', 'new_m_mla_w_uq', 'new_m_mla_w_ukv', 'new_m_mla_q_nope_g', 'new_m_mla_q_rope_g', 'new_m_mla_k_nope_g', 'new_m_mla_k_rope_g', 'new_m_mla_w_out', 'new_m_gmlp_w_in', 'new_m_gmlp_ln_g', 'new_m_gmlp_ln_b', 'new_m_gmlp_w_s', 'new_m_gmlp_b_s', 'new_m_gmlp_w_out', 'new_m_ffn_w_up', 'new_m_ffn_w_down', 'new_m_ple_w_gate', 'new_m_ple_w_proj', 'new_v_norm_mix', 'new_v_norm_ffn', 'new_v_norm_ple', 'new_v_mla_w_down', 'new_v_mla_q_lora_g', 'new_v_mla_kv_lora_g', 'new_v_mla_w_uq', 'new_v_mla_w_ukv', 'new_v_mla_q_nope_g', 'new_v_mla_q_rope_g', 'new_v_mla_k_nope_g', 'new_v_mla_k_rope_g', 'new_v_mla_w_out', 'new_v_gmlp_w_in', 'new_v_gmlp_ln_g', 'new_v_gmlp_ln_b', 'new_v_gmlp_w_s', 'new_v_gmlp_b_s', 'new_v_gmlp_w_out', 'new_v_ffn_w_up', 'new_v_ffn_w_down', 'new_v_ple_w_gate', 'new_v_ple_w_proj']
TWIN_LEAF_KINDS = {'loss': 'loss', 'grad_x': 'grad_x', 'grad_norm_mix': 'grad_w', 'grad_norm_ffn': 'grad_w', 'grad_norm_ple': 'grad_w', 'grad_mla_w_down': 'grad_w', 'grad_mla_q_lora_g': 'grad_w', 'grad_mla_kv_lora_g': 'grad_w', 'grad_mla_w_uq': 'grad_w', 'grad_mla_w_ukv': 'grad_w', 'grad_mla_q_nope_g': 'grad_w', 'grad_mla_q_rope_g': 'grad_w', 'grad_mla_k_nope_g': 'grad_w', 'grad_mla_k_rope_g': 'grad_w', 'grad_mla_w_out': 'grad_w', 'grad_gmlp_w_in': 'grad_w', 'grad_gmlp_ln_g': 'grad_w', 'grad_gmlp_ln_b': 'grad_w', 'grad_gmlp_w_s': 'grad_w', 'grad_gmlp_b_s': 'grad_w', 'grad_gmlp_w_out': 'grad_w', 'grad_ffn_w_up': 'grad_w', 'grad_ffn_w_down': 'grad_w', 'grad_ple_w_gate': 'grad_w', 'grad_ple_w_proj': 'grad_w', 'delta_norm_mix': 'delta_w', 'delta_norm_ffn': 'delta_w', 'delta_norm_ple': 'delta_w', 'delta_mla_w_down': 'delta_w', 'delta_mla_q_lora_g': 'delta_w', 'delta_mla_kv_lora_g': 'delta_w', 'delta_mla_w_uq': 'delta_w', 'delta_mla_w_ukv': 'delta_w', 'delta_mla_q_nope_g': 'delta_w', 'delta_mla_q_rope_g': 'delta_w', 'delta_mla_k_nope_g': 'delta_w', 'delta_mla_k_rope_g': 'delta_w', 'delta_mla_w_out': 'delta_w', 'delta_gmlp_w_in': 'delta_w', 'delta_gmlp_ln_g': 'delta_w', 'delta_gmlp_ln_b': 'delta_w', 'delta_gmlp_w_s': 'delta_w', 'delta_gmlp_b_s': 'delta_w', 'delta_gmlp_w_out': 'delta_w', 'delta_ffn_w_up': 'delta_w', 'delta_ffn_w_down': 'delta_w', 'delta_ple_w_gate': 'delta_w', 'delta_ple_w_proj': 'delta_w', 'new_m_norm_mix': 'new_m', 'new_m_norm_ffn': 'new_m', 'new_m_norm_ple': 'new_m', 'new_m_mla_w_down': 'new_m', 'new_m_mla_q_lora_g': 'new_m', 'new_m_mla_kv_lora_g': 'new_m', 'new_m_mla_w_uq': 'new_m', 'new_m_mla_w_ukv': 'new_m', 'new_m_mla_q_nope_g': 'new_m', 'new_m_mla_q_rope_g': 'new_m', 'new_m_mla_k_nope_g': 'new_m', 'new_m_mla_k_rope_g': 'new_m', 'new_m_mla_w_out': 'new_m', 'new_m_gmlp_w_in': 'new_m', 'new_m_gmlp_ln_g': 'new_m', 'new_m_gmlp_ln_b': 'new_m', 'new_m_gmlp_w_s': 'new_m', 'new_m_gmlp_b_s': 'new_m', 'new_m_gmlp_w_out': 'new_m', 'new_m_ffn_w_up': 'new_m', 'new_m_ffn_w_down': 'new_m', 'new_m_ple_w_gate': 'new_m', 'new_m_ple_w_proj': 'new_m', 'new_v_norm_mix': 'new_v', 'new_v_norm_ffn': 'new_v', 'new_v_norm_ple': 'new_v', 'new_v_mla_w_down': 'new_v', 'new_v_mla_q_lora_g': 'new_v', 'new_v_mla_kv_lora_g': 'new_v', 'new_v_mla_w_uq': 'new_v', 'new_v_mla_w_ukv': 'new_v', 'new_v_mla_q_nope_g': 'new_v', 'new_v_mla_q_rope_g': 'new_v', 'new_v_mla_k_nope_g': 'new_v', 'new_v_mla_k_rope_g': 'new_v', 'new_v_mla_w_out': 'new_v', 'new_v_gmlp_w_in': 'new_v', 'new_v_gmlp_ln_g': 'new_v', 'new_v_gmlp_ln_b': 'new_v', 'new_v_gmlp_w_s': 'new_v', 'new_v_gmlp_b_s': 'new_v', 'new_v_gmlp_w_out': 'new_v', 'new_v_ffn_w_up': 'new_v', 'new_v_ffn_w_down': 'new_v', 'new_v_ple_w_gate': 'new_v', 'new_v_ple_w_proj': 'new_v'}


def _forward(args):
    return _fwd_reference(*[args[k] for k in FWD_PARAMS])


def _output_shape():
    out = _jax.eval_shape(lambda: _forward(_fwd_setup_inputs(0)))
    return out.shape, out.dtype

N_MICROBATCH = 1
ADAM_LR = 0.001
ADAM_B1 = 0.9
ADAM_B2 = 0.999
ADAM_EPS = 1e-08
ADAM_WD = 0.01
ADAM_STEP = 10
PER_EXAMPLE_BATCH_AXIS = {'x': 0, 'p': 1, 'positions': 0, 'loss_target': 0}
SHARED_INPUTS = []
_WEIGHT_DTYPES = {'norm_mix': _jnp.float32, 'norm_ffn': _jnp.float32, 'norm_ple': _jnp.float32, 'mla_w_down': _jnp.float32, 'mla_q_lora_g': _jnp.float32, 'mla_kv_lora_g': _jnp.float32, 'mla_w_uq': _jnp.float32, 'mla_w_ukv': _jnp.float32, 'mla_q_nope_g': _jnp.float32, 'mla_q_rope_g': _jnp.float32, 'mla_k_nope_g': _jnp.float32, 'mla_k_rope_g': _jnp.float32, 'mla_w_out': _jnp.float32, 'gmlp_w_in': _jnp.float32, 'gmlp_ln_g': _jnp.float32, 'gmlp_ln_b': _jnp.float32, 'gmlp_w_s': _jnp.float32, 'gmlp_b_s': _jnp.float32, 'gmlp_w_out': _jnp.float32, 'ffn_w_up': _jnp.float32, 'ffn_w_down': _jnp.float32, 'ple_w_gate': _jnp.float32, 'ple_w_proj': _jnp.float32}
MOMENT_SCALE = {'norm_mix': 4.566780e+00, 'norm_ffn': 2.442181e+01, 'norm_ple': 1.289850e+00, 'mla_w_down': 2.268637e+00, 'mla_q_lora_g': 1.092135e-01, 'mla_kv_lora_g': 4.195047e+00, 'mla_w_uq': 5.338171e-02, 'mla_w_ukv': 1.299232e+00, 'mla_q_nope_g': 2.559886e-01, 'mla_q_rope_g': 2.323989e-01, 'mla_k_nope_g': 2.583110e-01, 'mla_k_rope_g': 2.387208e-01, 'mla_w_out': 3.846177e+00, 'gmlp_w_in': 6.851402e-01, 'gmlp_ln_g': 8.927933e-01, 'gmlp_ln_b': 2.310070e-01, 'gmlp_w_s': 1.871370e-01, 'gmlp_b_s': 3.420653e+00, 'gmlp_w_out': 6.928502e+00, 'ffn_w_up': 1.133194e+00, 'ffn_w_down': 1.173402e+01, 'ple_w_gate': 7.805450e-01, 'ple_w_proj': 4.852651e-01}


def _to_microbatches(a, axis):
    t = _jnp.moveaxis(a, axis, 0)
    t = t.reshape((N_MICROBATCH, t.shape[0] // N_MICROBATCH) + t.shape[1:])
    return _jnp.moveaxis(t, 1, axis + 1)


def setup_inputs(seed: int = 0) -> dict:
    inp = _fwd_setup_inputs(seed)
    key = _jax.random.fold_in(_jax.random.key(seed), 7919)
    shape, _ = _output_shape()
    out = dict(inp)
    out["loss_target"] = _jax.random.normal(_jax.random.fold_in(key, 0), shape, _jnp.float32)
    for i, name in enumerate(TWIN_WEIGHTS):
        w = inp[name].astype(_jnp.float32)
        if MOMENT_SCALE is None:
            s = _jnp.sqrt(_jnp.mean(_jnp.square(w)) + 1e-30)
        else:
            s = MOMENT_SCALE[name]
        km, kv = _jax.random.split(_jax.random.fold_in(key, i + 1))
        out[name] = w
        out["m_" + name] = s * _jax.random.normal(km, w.shape, _jnp.float32)
        out["v_" + name] = (s * s) * _jax.random.uniform(kv, w.shape, _jnp.float32, 0.5, 1.5)
    if N_MICROBATCH > 1:
        for name, axis in PER_EXAMPLE_BATCH_AXIS.items():
            out[name] = _to_microbatches(out[name], axis)
    return {'x': out['x'], 'p': out['p'], 'positions': out['positions'], 'norm_mix': out['norm_mix'], 'norm_ffn': out['norm_ffn'], 'norm_ple': out['norm_ple'], 'mla_w_down': out['mla_w_down'], 'mla_q_lora_g': out['mla_q_lora_g'], 'mla_kv_lora_g': out['mla_kv_lora_g'], 'mla_w_uq': out['mla_w_uq'], 'mla_w_ukv': out['mla_w_ukv'], 'mla_q_nope_g': out['mla_q_nope_g'], 'mla_q_rope_g': out['mla_q_rope_g'], 'mla_k_nope_g': out['mla_k_nope_g'], 'mla_k_rope_g': out['mla_k_rope_g'], 'mla_w_out': out['mla_w_out'], 'gmlp_w_in': out['gmlp_w_in'], 'gmlp_ln_g': out['gmlp_ln_g'], 'gmlp_ln_b': out['gmlp_ln_b'], 'gmlp_w_s': out['gmlp_w_s'], 'gmlp_b_s': out['gmlp_b_s'], 'gmlp_w_out': out['gmlp_w_out'], 'ffn_w_up': out['ffn_w_up'], 'ffn_w_down': out['ffn_w_down'], 'ple_w_gate': out['ple_w_gate'], 'ple_w_proj': out['ple_w_proj'], 'loss_target': out['loss_target'], 'm_norm_mix': out['m_norm_mix'], 'm_norm_ffn': out['m_norm_ffn'], 'm_norm_ple': out['m_norm_ple'], 'm_mla_w_down': out['m_mla_w_down'], 'm_mla_q_lora_g': out['m_mla_q_lora_g'], 'm_mla_kv_lora_g': out['m_mla_kv_lora_g'], 'm_mla_w_uq': out['m_mla_w_uq'], 'm_mla_w_ukv': out['m_mla_w_ukv'], 'm_mla_q_nope_g': out['m_mla_q_nope_g'], 'm_mla_q_rope_g': out['m_mla_q_rope_g'], 'm_mla_k_nope_g': out['m_mla_k_nope_g'], 'm_mla_k_rope_g': out['m_mla_k_rope_g'], 'm_mla_w_out': out['m_mla_w_out'], 'm_gmlp_w_in': out['m_gmlp_w_in'], 'm_gmlp_ln_g': out['m_gmlp_ln_g'], 'm_gmlp_ln_b': out['m_gmlp_ln_b'], 'm_gmlp_w_s': out['m_gmlp_w_s'], 'm_gmlp_b_s': out['m_gmlp_b_s'], 'm_gmlp_w_out': out['m_gmlp_w_out'], 'm_ffn_w_up': out['m_ffn_w_up'], 'm_ffn_w_down': out['m_ffn_w_down'], 'm_ple_w_gate': out['m_ple_w_gate'], 'm_ple_w_proj': out['m_ple_w_proj'], 'v_norm_mix': out['v_norm_mix'], 'v_norm_ffn': out['v_norm_ffn'], 'v_norm_ple': out['v_norm_ple'], 'v_mla_w_down': out['v_mla_w_down'], 'v_mla_q_lora_g': out['v_mla_q_lora_g'], 'v_mla_kv_lora_g': out['v_mla_kv_lora_g'], 'v_mla_w_uq': out['v_mla_w_uq'], 'v_mla_w_ukv': out['v_mla_w_ukv'], 'v_mla_q_nope_g': out['v_mla_q_nope_g'], 'v_mla_q_rope_g': out['v_mla_q_rope_g'], 'v_mla_k_nope_g': out['v_mla_k_nope_g'], 'v_mla_k_rope_g': out['v_mla_k_rope_g'], 'v_mla_w_out': out['v_mla_w_out'], 'v_gmlp_w_in': out['v_gmlp_w_in'], 'v_gmlp_ln_g': out['v_gmlp_ln_g'], 'v_gmlp_ln_b': out['v_gmlp_ln_b'], 'v_gmlp_w_s': out['v_gmlp_w_s'], 'v_gmlp_b_s': out['v_gmlp_b_s'], 'v_gmlp_w_out': out['v_gmlp_w_out'], 'v_ffn_w_up': out['v_ffn_w_up'], 'v_ffn_w_down': out['v_ffn_w_down'], 'v_ple_w_gate': out['v_ple_w_gate'], 'v_ple_w_proj': out['v_ple_w_proj']}


def _loss(weights, diff, rest, loss_target):
    with _jax.named_scope("forward"):
        args = {**rest, TWIN_DIFF_INPUT: diff, **{k: w.astype(_WEIGHT_DTYPES[k]) for k, w in weights.items()}}
        y = _forward(args)
    with _jax.named_scope("loss_head"):
        err = _jnp.square(y.astype(_jnp.float32) - loss_target)
        return 0.5 * _jnp.sum(_jnp.mean(err, axis=-1)) if err.ndim else 0.5 * err


def _adamw(w, g, m, v):
    m = ADAM_B1 * m + (1.0 - ADAM_B1) * g
    v = ADAM_B2 * v + (1.0 - ADAM_B2) * _jnp.square(g)
    m_hat = m / (1.0 - ADAM_B1 ** ADAM_STEP)
    v_hat = v / (1.0 - ADAM_B2 ** ADAM_STEP)
    delta = -ADAM_LR * (m_hat / (_jnp.sqrt(v_hat) + ADAM_EPS) + ADAM_WD * w)
    return delta, m, v


def reference(x, p, positions, norm_mix, norm_ffn, norm_ple, mla_w_down, mla_q_lora_g, mla_kv_lora_g, mla_w_uq, mla_w_ukv, mla_q_nope_g, mla_q_rope_g, mla_k_nope_g, mla_k_rope_g, mla_w_out, gmlp_w_in, gmlp_ln_g, gmlp_ln_b, gmlp_w_s, gmlp_b_s, gmlp_w_out, ffn_w_up, ffn_w_down, ple_w_gate, ple_w_proj, loss_target, m_norm_mix, m_norm_ffn, m_norm_ple, m_mla_w_down, m_mla_q_lora_g, m_mla_kv_lora_g, m_mla_w_uq, m_mla_w_ukv, m_mla_q_nope_g, m_mla_q_rope_g, m_mla_k_nope_g, m_mla_k_rope_g, m_mla_w_out, m_gmlp_w_in, m_gmlp_ln_g, m_gmlp_ln_b, m_gmlp_w_s, m_gmlp_b_s, m_gmlp_w_out, m_ffn_w_up, m_ffn_w_down, m_ple_w_gate, m_ple_w_proj, v_norm_mix, v_norm_ffn, v_norm_ple, v_mla_w_down, v_mla_q_lora_g, v_mla_kv_lora_g, v_mla_w_uq, v_mla_w_ukv, v_mla_q_nope_g, v_mla_q_rope_g, v_mla_k_nope_g, v_mla_k_rope_g, v_mla_w_out, v_gmlp_w_in, v_gmlp_ln_g, v_gmlp_ln_b, v_gmlp_w_s, v_gmlp_b_s, v_gmlp_w_out, v_ffn_w_up, v_ffn_w_down, v_ple_w_gate, v_ple_w_proj):
    given = dict(x=x, p=p, positions=positions, norm_mix=norm_mix, norm_ffn=norm_ffn, norm_ple=norm_ple, mla_w_down=mla_w_down, mla_q_lora_g=mla_q_lora_g, mla_kv_lora_g=mla_kv_lora_g, mla_w_uq=mla_w_uq, mla_w_ukv=mla_w_ukv, mla_q_nope_g=mla_q_nope_g, mla_q_rope_g=mla_q_rope_g, mla_k_nope_g=mla_k_nope_g, mla_k_rope_g=mla_k_rope_g, mla_w_out=mla_w_out, gmlp_w_in=gmlp_w_in, gmlp_ln_g=gmlp_ln_g, gmlp_ln_b=gmlp_ln_b, gmlp_w_s=gmlp_w_s, gmlp_b_s=gmlp_b_s, gmlp_w_out=gmlp_w_out, ffn_w_up=ffn_w_up, ffn_w_down=ffn_w_down, ple_w_gate=ple_w_gate, ple_w_proj=ple_w_proj, loss_target=loss_target, m_norm_mix=m_norm_mix, m_norm_ffn=m_norm_ffn, m_norm_ple=m_norm_ple, m_mla_w_down=m_mla_w_down, m_mla_q_lora_g=m_mla_q_lora_g, m_mla_kv_lora_g=m_mla_kv_lora_g, m_mla_w_uq=m_mla_w_uq, m_mla_w_ukv=m_mla_w_ukv, m_mla_q_nope_g=m_mla_q_nope_g, m_mla_q_rope_g=m_mla_q_rope_g, m_mla_k_nope_g=m_mla_k_nope_g, m_mla_k_rope_g=m_mla_k_rope_g, m_mla_w_out=m_mla_w_out, m_gmlp_w_in=m_gmlp_w_in, m_gmlp_ln_g=m_gmlp_ln_g, m_gmlp_ln_b=m_gmlp_ln_b, m_gmlp_w_s=m_gmlp_w_s, m_gmlp_b_s=m_gmlp_b_s, m_gmlp_w_out=m_gmlp_w_out, m_ffn_w_up=m_ffn_w_up, m_ffn_w_down=m_ffn_w_down, m_ple_w_gate=m_ple_w_gate, m_ple_w_proj=m_ple_w_proj, v_norm_mix=v_norm_mix, v_norm_ffn=v_norm_ffn, v_norm_ple=v_norm_ple, v_mla_w_down=v_mla_w_down, v_mla_q_lora_g=v_mla_q_lora_g, v_mla_kv_lora_g=v_mla_kv_lora_g, v_mla_w_uq=v_mla_w_uq, v_mla_w_ukv=v_mla_w_ukv, v_mla_q_nope_g=v_mla_q_nope_g, v_mla_q_rope_g=v_mla_q_rope_g, v_mla_k_nope_g=v_mla_k_nope_g, v_mla_k_rope_g=v_mla_k_rope_g, v_mla_w_out=v_mla_w_out, v_gmlp_w_in=v_gmlp_w_in, v_gmlp_ln_g=v_gmlp_ln_g, v_gmlp_ln_b=v_gmlp_ln_b, v_gmlp_w_s=v_gmlp_w_s, v_gmlp_b_s=v_gmlp_b_s, v_gmlp_w_out=v_gmlp_w_out, v_ffn_w_up=v_ffn_w_up, v_ffn_w_down=v_ffn_w_down, v_ple_w_gate=v_ple_w_gate, v_ple_w_proj=v_ple_w_proj)
    weights = {n: given[n] for n in TWIN_WEIGHTS}
    shared = {n: given[n] for n in SHARED_INPUTS}
    per_example = {n: given[n] for n in ['x', 'p', 'positions']}
    grad_fn = _jax.value_and_grad(_loss, argnums=(0, 1))

    def one_microbatch(ex, loss_target):
        ex = dict(ex)
        diff = ex.pop(TWIN_DIFF_INPUT)
        return grad_fn(weights, diff, {**shared, **ex}, loss_target)

    if N_MICROBATCH == 1:
        loss, (grad_w, grad_x) = one_microbatch(per_example, given["loss_target"])
    else:
        def body(carry, xs):
            loss_sum, grad_sum = carry
            l_k, (gw_k, gx_k) = one_microbatch(xs[0], xs[1])
            with _jax.named_scope("update"):
                return (loss_sum + l_k, _jax.tree.map(_jnp.add, grad_sum, gw_k)), gx_k

        init = (_jnp.zeros((), _jnp.float32), _jax.tree.map(_jnp.zeros_like, weights))
        (loss, grad_w), grad_x = _jax.lax.scan(body, init, (per_example, given["loss_target"]))
    with _jax.named_scope("update"):
        delta_w, new_m, new_v = {}, {}, {}
        for n in TWIN_WEIGHTS:
            delta_w[n], new_m[n], new_v[n] = _adamw(weights[n], grad_w[n], given["m_" + n], given["v_" + n])
    return (loss, grad_x, *[grad_w[n] for n in TWIN_WEIGHTS], *[delta_w[n] for n in TWIN_WEIGHTS],
            *[new_m[n] for n in TWIN_WEIGHTS], *[new_v[n] for n in TWIN_WEIGHTS])
```

```python
import math

import numpy as np
import jax
import jax.numpy as jnp
from jax import lax
from jax.experimental import pallas as pl
from jax.experimental.pallas import tpu as pltpu

F32 = jnp.float32
BF16 = jnp.bfloat16

N_DEV = 8
MESH_AXES = ("x", "y", "c")
HEADS = 8
NOPE = 128
ROPE = 64
VDIM = 128
QK = NOPE + ROPE
Q_LORA = 384
KV_LORA = 256
ROPE_BASE = 10000.0
CHUNK = 128
GROUPS = 8
EPS = 1e-6
LR, B1, B2, ADAM_EPS, WD, STEP = 0.001, 0.9, 0.999, 1e-08, 0.01, 10
LANE = 128
VMEM_LIMIT = 48 * 1024 * 1024

WEIGHTS = ['norm_mix', 'norm_ffn', 'norm_ple', 'mla_w_down', 'mla_q_lora_g', 'mla_kv_lora_g', 'mla_w_uq',
           'mla_w_ukv', 'mla_q_nope_g', 'mla_q_rope_g', 'mla_k_nope_g', 'mla_k_rope_g', 'mla_w_out', 'gmlp_w_in',
           'gmlp_ln_g', 'gmlp_ln_b', 'gmlp_w_s', 'gmlp_b_s', 'gmlp_w_out', 'ffn_w_up', 'ffn_w_down', 'ple_w_gate',
           'ple_w_proj']
SHARD_AXIS = {'mla_w_down': 1, 'mla_w_uq': 2, 'mla_w_ukv': 2, 'mla_w_out': 1, 'gmlp_w_in': 2, 'gmlp_ln_g': 1,
              'gmlp_ln_b': 1, 'gmlp_w_out': 1, 'ffn_w_up': 2, 'ffn_w_down': 1, 'ple_w_gate': 1, 'ple_w_proj': 2}
SHARDED = list(SHARD_AXIS)
REPLICATED = [n for n in WEIGHTS if n not in SHARD_AXIS]
F32_PAYLOAD = ('gmlp_ln_g', 'gmlp_ln_b')


def _pick(dim, pref, align=LANE):
    if dim <= pref:
        return dim
    b = (pref // align) * align
    while b >= align:
        if dim % b == 0:
            return b
        b -= align
    return dim


def _params(sem):
    return pltpu.CompilerParams(dimension_semantics=sem, vmem_limit_bytes=VMEM_LIMIT)


def _mm(a, b, *, ta=False, tb=False, extras=(), epilogue=None, out_dtypes=(F32,), name,
        bm=512, bn=512, bk=1024):
    if ta:
        K, M = a.shape
    else:
        M, K = a.shape
    if tb:
        N, K2 = b.shape
    else:
        K2, N = b.shape
    assert K == K2, (a.shape, b.shape, ta, tb)
    bm, bn, bk = _pick(M, bm), _pick(N, bn), _pick(K, bk)
    nk = K // bk
    ne, no = len(extras), len(out_dtypes)
    dims = (((0,) if ta else (1,), (1,) if tb else (0,)), ((), ()))

    def body(*refs):
        a_ref, b_ref = refs[0], refs[1]
        e_refs = refs[2:2 + ne]
        o_refs = refs[2 + ne:2 + ne + no]
        acc = refs[-1]
        k = pl.program_id(2)

        @pl.when(k == 0)
        def _():
            acc[...] = jnp.zeros_like(acc)

        acc[...] += lax.dot_general(a_ref[...].astype(BF16), b_ref[...].astype(BF16), dims,
                                    preferred_element_type=F32)

        @pl.when(k == nk - 1)
        def _():
            r = acc[...]
            outs = epilogue(r, *[e[...] for e in e_refs]) if epilogue is not None else (r,)
            for o, v in zip(o_refs, outs):
                o[...] = v.astype(o.dtype)

    a_spec = pl.BlockSpec((bk, bm), lambda i, j, k: (k, i)) if ta else pl.BlockSpec((bm, bk), lambda i, j, k: (i, k))
    b_spec = pl.BlockSpec((bn, bk), lambda i, j, k: (j, k)) if tb else pl.BlockSpec((bk, bn), lambda i, j, k: (k, j))
    tile = lambda: pl.BlockSpec((bm, bn), lambda i, j, k: (i, j))
    outs = pl.pallas_call(
        body, name=name,
        grid=(M // bm, N // bn, nk),
        in_specs=[a_spec, b_spec] + [tile() for _ in extras],
        out_specs=[tile() for _ in out_dtypes],
        out_shape=[jax.ShapeDtypeStruct((M, N), dt) for dt in out_dtypes],
        scratch_shapes=[pltpu.VMEM((bm, bn), F32)],
        compiler_params=_params(("parallel", "parallel", "arbitrary")),
    )(a, b, *extras)
    return outs[0] if no == 1 else outs


def _rowcall(fn, rows, params, row_outs, acc_outs=(), *, bm=256, name):
    T = rows[0].shape[0]
    bm = _pick(T, bm, 8)
    nr, npar, nro, nao = len(rows), len(params), len(row_outs), len(acc_outs)

    def body(*refs):
        vals = [r[...] for r in refs[:nr + npar]]
        res = fn(*vals)
        ro = refs[nr + npar:nr + npar + nro]
        ao = refs[nr + npar + nro:]
        for r, v in zip(ro, res[:nro]):
            r[...] = v.astype(r.dtype)
        if nao:
            @pl.when(pl.program_id(0) == 0)
            def _():
                for r in ao:
                    r[...] = jnp.zeros_like(r)

            for r, v in zip(ao, res[nro:]):
                r[...] += v

    def whole(shape):
        nd = len(shape)
        return pl.BlockSpec(tuple(shape), lambda i: (0,) * nd)

    outs = pl.pallas_call(
        body, name=name,
        grid=(T // bm,),
        in_specs=[pl.BlockSpec((bm, r.shape[1]), lambda i: (i, 0)) for r in rows] + [whole(q.shape) for q in params],
        out_specs=[pl.BlockSpec((bm, c), lambda i: (i, 0)) for c, _ in row_outs] + [whole(s) for s in acc_outs],
        out_shape=[jax.ShapeDtypeStruct((T, c), dt) for c, dt in row_outs]
        + [jax.ShapeDtypeStruct(tuple(s), F32) for s in acc_outs],
        compiler_params=_params(("arbitrary",) if nao else ("parallel",)),
    )(*rows, *params)
    return outs


def _rmsn(x, g):
    return x * lax.rsqrt(jnp.mean(x * x, axis=-1, keepdims=True) + EPS) * g


def _gelu(x):
    return 0.5 * x * (1.0 + jnp.tanh(math.sqrt(2.0 / math.pi) * (x + 0.044715 * (x * x * x))))


def _layer_norm(x, g, b):
    mu = jnp.mean(x, axis=-1, keepdims=True)
    xc = x - mu
    return xc * lax.rsqrt(jnp.mean(xc * xc, axis=-1, keepdims=True) + EPS) * g + b


def _sigmoid(x):
    return 1.0 / (1.0 + jnp.exp(-x))


def _rot(x, cos, sin, rmat):
    return x * cos + jnp.dot(x, rmat, precision=lax.Precision.HIGHEST, preferred_element_type=F32) * sin


def _rms_fwd(h, g, name):
    return _rowcall(lambda x, gg: (_rmsn(x, gg),), [h], [g], [(h.shape[1], BF16)], name=name)[0]


def _rms_bwd(h, d_hn, dh_in, g, name):
    def fn(x, dy, dres, gg):
        _, vjp = jax.vjp(_rmsn, x, gg)
        dx, dg = vjp(dy)
        return dres + dx, dg

    return _rowcall(fn, [h, d_hn, dh_in], [g], [(h.shape[1], F32)], [g.shape], name=name)


def _rope_tables(pos, name):
    inv = np.float32(ROPE_BASE) ** (-(np.arange(0, ROPE, 2, dtype=np.float32) / np.float32(ROPE)))
    inv = jnp.asarray(np.concatenate([inv, inv])[None, :].astype(np.float32))

    def fn(pp, iv):
        ang = pp.astype(F32) * iv
        return jnp.cos(ang), jnp.sin(ang)

    return _rowcall(fn, [pos], [inv], [(ROPE, F32), (ROPE, F32)], name=name)


def _rot_matrix():
    r = np.zeros((ROPE, ROPE), np.float32)
    half = ROPE // 2
    for j in range(half):
        r[j + half, j] = -1.0
        r[j, j + half] = 1.0
    return jnp.asarray(r)


def _prep1_fwd(lat, gq, gkv, name):
    def fn(l, a, b):
        return _rmsn(l[:, :Q_LORA], a), _rmsn(l[:, Q_LORA:Q_LORA + KV_LORA], b)

    return _rowcall(fn, [lat], [gq, gkv], [(Q_LORA, BF16), (KV_LORA, BF16)], name=name)


def _prep1_bwd(lat, d_cq, d_ckv, d_kr, gq, gkv, name):
    def fn(l, dq, dkv, dkr, a, b):
        _, vq = jax.vjp(_rmsn, l[:, :Q_LORA], a)
        _, vkv = jax.vjp(_rmsn, l[:, Q_LORA:Q_LORA + KV_LORA], b)
        dxq, dga = vq(dq)
        dxkv, dgb = vkv(dkv)
        return jnp.concatenate([dxq, dxkv, dkr], axis=1), dga, dgb

    return _rowcall(fn, [lat, d_cq, d_ckv, d_kr], [gq, gkv], [(lat.shape[1], BF16)], [gq.shape, gkv.shape], name=name)


def _qk_fn(qn_raw, qr_raw, kn_raw, kr_raw, gqn, gqr, gkn, gkr, cos, sin, rmat):
    return (_rmsn(qn_raw, gqn), _rot(_rmsn(qr_raw, gqr), cos, sin, rmat),
            _rmsn(kn_raw, gkn), _rot(_rmsn(kr_raw, gkr), cos, sin, rmat))


def _prep2_fwd(q_raw, kv_raw, lat, cos, sin, rmat, gains, name, bm=256):
    H, T, _ = q_raw.shape
    bm = _pick(T, bm, 8)
    kr0 = Q_LORA + KV_LORA

    def body(q_ref, kv_ref, lat_ref, cos_ref, sin_ref, r_ref, gqn, gqr, gkn, gkr, qo, ko, vo):
        qr, kvr = q_ref[...], kv_ref[...]
        qn, qro, kn, kro = _qk_fn(qr[:, :NOPE], qr[:, NOPE:], kvr[:, :NOPE], lat_ref[:, kr0:kr0 + ROPE],
                                  gqn[...], gqr[...], gkn[...], gkr[...], cos_ref[...], sin_ref[...], r_ref[...])
        qo[:, :NOPE] = qn.astype(BF16)
        qo[:, NOPE:] = qro.astype(BF16)
        ko[:, :NOPE] = kn.astype(BF16)
        ko[:, NOPE:] = kro.astype(BF16)
        vo[...] = kvr[:, NOPE:].astype(BF16)

    hb = lambda c: pl.BlockSpec((None, bm, c), lambda m, h: (h, m, 0))
    rb = lambda c: pl.BlockSpec((bm, c), lambda m, h: (m, 0))
    wb = lambda s: pl.BlockSpec(tuple(s), lambda m, h: (0, 0))
    return pl.pallas_call(
        body, name=name, grid=(T // bm, H),
        in_specs=[hb(QK), hb(NOPE + VDIM), rb(lat.shape[1]), rb(ROPE), rb(ROPE), wb(rmat.shape)]
        + [wb(g.shape) for g in gains],
        out_specs=[hb(QK), hb(QK), hb(VDIM)],
        out_shape=[jax.ShapeDtypeStruct((H, T, QK), BF16), jax.ShapeDtypeStruct((H, T, QK), BF16),
                   jax.ShapeDtypeStruct((H, T, VDIM), BF16)],
        compiler_params=_params(("parallel", "parallel")),
    )(q_raw, kv_raw, lat, cos, sin, rmat, *gains)


def _prep2_bwd(q_raw, kv_raw, lat, cos, sin, rmat, gains, dq, dk, dv, name, bm=256):
    H, T, _ = q_raw.shape
    bm = _pick(T, bm, 8)
    kr0 = Q_LORA + KV_LORA

    def body(q_ref, kv_ref, lat_ref, cos_ref, sin_ref, r_ref, gqn, gqr, gkn, gkr, dq_ref, dk_ref, dv_ref,
             dqo, dkvo, dkro, o_gqn, o_gqr, o_gkn, o_gkr):
        m, h = pl.program_id(0), pl.program_id(1)
        qr, kvr = q_ref[...], kv_ref[...]
        cos_v, sin_v, r_v = cos_ref[...], sin_ref[...], r_ref[...]
        f = lambda a, b, c, d, g1, g2, g3, g4: _qk_fn(a, b, c, d, g1, g2, g3, g4, cos_v, sin_v, r_v)
        _, vjp = jax.vjp(f, qr[:, :NOPE], qr[:, NOPE:], kvr[:, :NOPE], lat_ref[:, kr0:kr0 + ROPE],
                         gqn[...], gqr[...], gkn[...], gkr[...])
        dqv, dkv_ = dq_ref[...], dk_ref[...]
        d_qn, d_qr, d_kn, d_kr, g1, g2, g3, g4 = vjp((dqv[:, :NOPE], dqv[:, NOPE:], dkv_[:, :NOPE], dkv_[:, NOPE:]))
        dqo[:, :NOPE] = d_qn.astype(BF16)
        dqo[:, NOPE:] = d_qr.astype(BF16)
        dkvo[:, :NOPE] = d_kn.astype(BF16)
        dkvo[:, NOPE:] = dv_ref[...].astype(BF16)

        @pl.when(h == 0)
        def _():
            dkro[...] = jnp.zeros_like(dkro)

        dkro[...] += d_kr

        @pl.when((h == 0) & (m == 0))
        def _():
            for o in (o_gqn, o_gqr, o_gkn, o_gkr):
                o[...] = jnp.zeros_like(o)

        for o, g in zip((o_gqn, o_gqr, o_gkn, o_gkr), (g1, g2, g3, g4)):
            o[...] += g

    hb = lambda c: pl.BlockSpec((None, bm, c), lambda m, h: (h, m, 0))
    rb = lambda c: pl.BlockSpec((bm, c), lambda m, h: (m, 0))
    wb = lambda s: pl.BlockSpec(tuple(s), lambda m, h: (0, 0))
    return pl.pallas_call(
        body, name=name, grid=(T // bm, H),
        in_specs=[hb(QK), hb(NOPE + VDIM), rb(lat.shape[1]), rb(ROPE), rb(ROPE), wb(rmat.shape)]
        + [wb(g.shape) for g in gains] + [hb(QK), hb(QK), hb(VDIM)],
        out_specs=[hb(QK), hb(NOPE + VDIM), rb(ROPE)] + [wb(g.shape) for g in gains],
        out_shape=[jax.ShapeDtypeStruct((H, T, QK), BF16), jax.ShapeDtypeStruct((H, T, NOPE + VDIM), BF16),
                   jax.ShapeDtypeStruct((T, ROPE), F32)] + [jax.ShapeDtypeStruct(g.shape, F32) for g in gains],
        compiler_params=_params(("arbitrary", "arbitrary")),
    )(q_raw, kv_raw, lat, cos, sin, rmat, *gains, dq, dk, dv)


_NT = (((1,), (1,)), ((), ()))
_TN = (((0,), (0,)), ((), ()))


def _attn_fwd(q, k, v, seq, name, blk=256):
    H, T, _ = q.shape
    nb = T // seq
    blk = _pick(seq, blk)
    nq = seq // blk
    scale = float(QK) ** -0.5

    def body(q_ref, k_ref, v_ref, o_ref, lse_ref):
        qi = pl.program_id(2)
        qb = q_ref[...]
        rows = qi * blk + lax.broadcasted_iota(jnp.int32, (blk, blk), 0)

        def step(j, carry):
            m, l, acc = carry
            ks = pl.ds(pl.multiple_of(j * blk, blk), blk)
            s = lax.dot_general(qb, k_ref[ks, :], _NT, preferred_element_type=F32) * scale
            cols = j * blk + lax.broadcasted_iota(jnp.int32, (blk, blk), 1)
            s = jnp.where(cols <= rows, s, -jnp.inf)
            m_new = jnp.maximum(m, jnp.max(s, axis=1, keepdims=True))
            pr = jnp.exp(s - m_new)
            alpha = jnp.exp(m - m_new)
            l = alpha * l + jnp.sum(pr, axis=1, keepdims=True)
            acc = alpha * acc + jnp.dot(pr.astype(BF16), v_ref[ks, :], preferred_element_type=F32)
            return m_new, l, acc

        init = (jnp.full((blk, 1), -jnp.inf, F32), jnp.zeros((blk, 1), F32), jnp.zeros((blk, VDIM), F32))
        m, l, acc = lax.fori_loop(0, qi + 1, step, init)
        o_ref[...] = (acc / l).astype(o_ref.dtype)
        lse_ref[...] = m + jnp.log(l)

    return pl.pallas_call(
        body, name=name, grid=(H, nb, nq),
        in_specs=[pl.BlockSpec((None, blk, QK), lambda h, b, i: (h, b * nq + i, 0)),
                  pl.BlockSpec((None, seq, QK), lambda h, b, i: (h, b, 0)),
                  pl.BlockSpec((None, seq, VDIM), lambda h, b, i: (h, b, 0))],
        out_specs=[pl.BlockSpec((blk, VDIM), lambda h, b, i: (b * nq + i, h)),
                   pl.BlockSpec((None, blk, 1), lambda h, b, i: (h, b * nq + i, 0))],
        out_shape=[jax.ShapeDtypeStruct((T, H * VDIM), BF16), jax.ShapeDtypeStruct((H, T, 1), F32)],
        compiler_params=_params(("parallel", "parallel", "parallel")),
    )(q, k, v)


def _attn_bwd(q, k, v, o, do, lse, seq, name, blk=256):
    H, T, _ = q.shape
    nb = T // seq
    blk = _pick(seq, blk)
    nq = seq // blk
    scale = float(QK) ** -0.5

    def body(q_ref, k_ref, v_ref, o_ref, do_ref, lse_ref, dq_ref, dk_ref, dv_ref):
        dk_ref[...] = jnp.zeros_like(dk_ref)
        dv_ref[...] = jnp.zeros_like(dv_ref)

        def qloop(i, carry):
            qs = pl.ds(pl.multiple_of(i * blk, blk), blk)
            qb = q_ref[qs, :]
            dof = do_ref[qs, :]
            dob = dof.astype(BF16)
            lse_b = lse_ref[qs, :]
            delta = jnp.sum(dof * o_ref[qs, :].astype(F32), axis=1, keepdims=True)
            rows = i * blk + lax.broadcasted_iota(jnp.int32, (blk, blk), 0)

            def kloop(j, dq_acc):
                ks = pl.ds(pl.multiple_of(j * blk, blk), blk)
                kb = k_ref[ks, :]
                vb = v_ref[ks, :]
                s = lax.dot_general(qb, kb, _NT, preferred_element_type=F32) * scale
                cols = j * blk + lax.broadcasted_iota(jnp.int32, (blk, blk), 1)
                pr = jnp.where(cols <= rows, jnp.exp(s - lse_b), 0.0)
                dp = lax.dot_general(dob, vb, _NT, preferred_element_type=F32)
                ds = (pr * (dp - delta) * scale).astype(BF16)
                prb = pr.astype(BF16)
                dv_ref[ks, :] += lax.dot_general(prb, dob, _TN, preferred_element_type=F32)
                dk_ref[ks, :] += lax.dot_general(ds, qb, _TN, preferred_element_type=F32)
                return dq_acc + jnp.dot(ds, kb, preferred_element_type=F32)

            dq_ref[qs, :] = lax.fori_loop(0, i + 1, kloop, jnp.zeros((blk, QK), F32))
            return carry

        lax.fori_loop(0, nq, qloop, 0)

    hb = lambda c: pl.BlockSpec((None, seq, c), lambda h, b: (h, b, 0))
    cb = lambda: pl.BlockSpec((seq, VDIM), lambda h, b: (b, h))
    return pl.pallas_call(
        body, name=name, grid=(H, nb),
        in_specs=[hb(QK), hb(QK), hb(VDIM), cb(), cb(), hb(1)],
        out_specs=[hb(QK), hb(QK), hb(VDIM)],
        out_shape=[jax.ShapeDtypeStruct((H, T, QK), F32), jax.ShapeDtypeStruct((H, T, QK), F32),
                   jax.ShapeDtypeStruct((H, T, VDIM), F32)],
        compiler_params=_params(("parallel", "parallel")),
    )(q, k, v, o, do, lse)


def _gelu_ln_fwd(z, g, b, name):
    half = z.shape[1] // 2

    def fn(zz, gg, bb):
        return _gelu(zz[:, :half]), _layer_norm(_gelu(zz[:, half:]), gg, bb)

    return _rowcall(fn, [z], [g, b], [(half, F32), (half, BF16)], name=name)


def _gelu_ln_bwd(z, d_u, d_vn, g, b, name):
    half = z.shape[1] // 2

    def fn(zz, du, dvn, gg, bb):
        _, vu = jax.vjp(_gelu, zz[:, :half])
        _, vv = jax.vjp(lambda t, a, c: _layer_norm(_gelu(t), a, c), zz[:, half:], gg, bb)
        (dzu,) = vu(du)
        dzv, dg, db = vv(dvn)
        return jnp.concatenate([dzu, dzv], axis=1), dg, db

    return _rowcall(fn, [z, d_u, d_vn], [g, b], [(z.shape[1], BF16)], [g.shape, b.shape], bm=128, name=name)


def _tril_bf16(ws):
    t = lax.broadcasted_iota(jnp.int32, ws.shape, 0)
    s = lax.broadcasted_iota(jnp.int32, ws.shape, 1)
    return jnp.where(s <= t, ws, 0.0).astype(BF16)


def _sgu_fwd(u, vn, ws, bs, name, bm=512):
    T, half = u.shape
    gd = half // GROUPS
    bm = _pick(T, bm, CHUNK)
    nc = bm // CHUNK

    def body(u_ref, vn_ref, ws_ref, bs_ref, y_ref):
        wm = _tril_bf16(ws_ref[...])
        bias = bs_ref[...]
        for c in range(nc):
            rs = slice(c * CHUNK, (c + 1) * CHUNK)
            sv = jnp.dot(wm, vn_ref[rs, :], preferred_element_type=F32) + bias
            y_ref[rs, :] = (u_ref[rs, :] * sv).astype(y_ref.dtype)

    tb = lambda: pl.BlockSpec((bm, gd), lambda g, i: (i, g))
    return pl.pallas_call(
        body, name=name, grid=(GROUPS, T // bm),
        in_specs=[tb(), tb(), pl.BlockSpec((None, CHUNK, CHUNK), lambda g, i: (g, 0, 0)),
                  pl.BlockSpec((None, CHUNK, 1), lambda g, i: (g, 0, 0))],
        out_specs=tb(),
        out_shape=jax.ShapeDtypeStruct((T, half), BF16),
        compiler_params=_params(("parallel", "parallel")),
    )(u, vn, ws, bs)


def _sgu_bwd(u, vn, dy, ws, bs, name, bm=512):
    T, half = u.shape
    gd = half // GROUPS
    bm = _pick(T, bm, CHUNK)
    nc = bm // CHUNK

    def body(u_ref, vn_ref, dy_ref, ws_ref, bs_ref, du_ref, dvn_ref, dws_ref, dbs_ref):
        @pl.when(pl.program_id(1) == 0)
        def _():
            dws_ref[...] = jnp.zeros_like(dws_ref)
            dbs_ref[...] = jnp.zeros_like(dbs_ref)

        wm = _tril_bf16(ws_ref[...])
        bias = bs_ref[...]
        dws = jnp.zeros((CHUNK, CHUNK), F32)
        dbs = jnp.zeros((CHUNK, 1), F32)
        for c in range(nc):
            rs = slice(c * CHUNK, (c + 1) * CHUNK)
            vb = vn_ref[rs, :]
            dyb = dy_ref[rs, :]
            sv = jnp.dot(wm, vb, preferred_element_type=F32) + bias
            du_ref[rs, :] = dyb * sv
            dsv = dyb * u_ref[rs, :]
            dsb = dsv.astype(BF16)
            dvn_ref[rs, :] = lax.dot_general(wm, dsb, _TN, preferred_element_type=F32)
            dws = dws + lax.dot_general(dsb, vb, _NT, preferred_element_type=F32)
            dbs = dbs + jnp.sum(dsv, axis=1, keepdims=True)
        t = lax.broadcasted_iota(jnp.int32, (CHUNK, CHUNK), 0)
        s = lax.broadcasted_iota(jnp.int32, (CHUNK, CHUNK), 1)
        dws_ref[...] += jnp.where(s <= t, dws, 0.0)
        dbs_ref[...] += dbs

    tb = lambda: pl.BlockSpec((bm, gd), lambda g, i: (i, g))
    wsb = lambda: pl.BlockSpec((None, CHUNK, CHUNK), lambda g, i: (g, 0, 0))
    bsb = lambda: pl.BlockSpec((None, CHUNK, 1), lambda g, i: (g, 0, 0))
    return pl.pallas_call(
        body, name=name, grid=(GROUPS, T // bm),
        in_specs=[tb(), tb(), tb(), wsb(), bsb()],
        out_specs=[tb(), tb(), wsb(), bsb()],
        out_shape=[jax.ShapeDtypeStruct((T, half), F32), jax.ShapeDtypeStruct((T, half), F32),
                   jax.ShapeDtypeStruct(ws.shape, F32), jax.ShapeDtypeStruct(bs.shape, F32)],
        compiler_params=_params(("parallel", "arbitrary")),
    )(u, vn, dy, ws, bs)


def _loss_head(y, t, name):
    d_model = y.shape[1]

    def fn(yy, tt):
        d = yy - tt
        part = 0.5 * jnp.sum(jnp.mean(d * d, axis=-1, keepdims=True), axis=0, keepdims=True)
        return d / d_model, jnp.zeros((1, LANE), F32) + part

    dy, part = _rowcall(fn, [y, t], [], [(d_model, F32)], [(1, LANE)], name=name)
    return dy, part[0, 0]


def _adamw(parts, w, m, v, name):
    R, C = w.shape
    br = _pick(R, max(8, (128 * 1024) // C // 8 * 8), 8)
    c1 = 1.0 - B1 ** STEP
    c2 = 1.0 - B2 ** STEP

    def body(p_ref, w_ref, m_ref, v_ref, g_o, d_o, m_o, v_o):
        g = p_ref[0]
        for d in range(1, N_DEV):
            g = g + p_ref[d]
        mn = B1 * m_ref[...] + (1.0 - B1) * g
        vn = B2 * v_ref[...] + (1.0 - B2) * (g * g)
        g_o[...] = g
        m_o[...] = mn
        v_o[...] = vn
        d_o[...] = -LR * ((mn / c1) / (jnp.sqrt(vn / c2) + ADAM_EPS) + WD * w_ref[...])

    blk = lambda: pl.BlockSpec((br, C), lambda i: (i, 0))
    return pl.pallas_call(
        body, name=name, grid=(R // br,),
        in_specs=[pl.BlockSpec((N_DEV, br, C), lambda i: (0, i, 0)), blk(), blk(), blk()],
        out_specs=[blk(), blk(), blk(), blk()],
        out_shape=[jax.ShapeDtypeStruct((R, C), F32)] * 4,
        compiler_params=_params(("parallel",)),
    )(parts, w, m, v)


def _mesh_pos():
    return lax.axis_index("x"), lax.axis_index("y"), lax.axis_index("c")


def _flip(pos, k):
    x, y, c = pos
    px = 1 - x if k & 4 else x
    py = 1 - y if k & 2 else y
    pc = 1 - c if k & 1 else c
    return px, py, pc


def _exchange(arrs, scatter, name):
    n = len(arrs)

    def body(*refs):
        ins, outs = refs[:n], refs[n:2 * n]
        send, recv, loc = refs[2 * n:]
        pos = _mesh_pos()
        me = 4 * pos[0] + 2 * pos[1] + pos[2]
        copies = []
        for a in range(n):
            src = ins[a].at[me] if scatter else ins[a]
            cp = pltpu.make_async_copy(src, outs[a].at[me], loc.at[a])
            cp.start()
            copies.append(cp)
        for k in range(1, N_DEV):
            peer = _flip(pos, k)
            peer_id = 4 * peer[0] + 2 * peer[1] + peer[2]
            for a in range(n):
                src = ins[a].at[peer_id] if scatter else ins[a]
                cp = pltpu.make_async_remote_copy(
                    src_ref=src, dst_ref=outs[a].at[me], send_sem=send.at[a, k - 1], recv_sem=recv.at[a, k - 1],
                    device_id=peer, device_id_type=pl.DeviceIdType.MESH)
                cp.start()
                copies.append(cp)
        for cp in copies:
            cp.wait()

    out_shape = [jax.ShapeDtypeStruct(a.shape if scatter else (N_DEV,) + a.shape, a.dtype) for a in arrs]
    return pl.pallas_call(
        body, name=name,
        in_specs=[pl.BlockSpec(memory_space=pl.ANY)] * n,
        out_specs=[pl.BlockSpec(memory_space=pl.ANY)] * n,
        out_shape=out_shape,
        scratch_shapes=[pltpu.SemaphoreType.DMA((n, N_DEV - 1)), pltpu.SemaphoreType.DMA((n, N_DEV - 1)),
                        pltpu.SemaphoreType.DMA((n,))],
        compiler_params=pltpu.CompilerParams(has_side_effects=True),
    )(*arrs)


def _assemble(gathered, axis):
    g = jnp.moveaxis(gathered, 0, axis)
    s = g.shape
    return g.reshape(s[:axis] + (s[axis] * s[axis + 1],) + s[axis + 2:])


def _split(full, axis):
    s = full.shape
    g = full.reshape(s[:axis] + (N_DEV, s[axis] // N_DEV) + s[axis + 1:])
    return jnp.moveaxis(g, axis, 0)


def _heads(a2d):
    T = a2d.shape[0]
    return jnp.transpose(a2d.reshape(T, HEADS, -1), (1, 0, 2))


def _unheads(a3d):
    H, T, c = a3d.shape
    return jnp.transpose(a3d, (1, 0, 2)).reshape(T, H * c)


def _pack(parts):
    flat = jnp.concatenate([q.reshape(-1) for q in parts])
    pad = (-flat.shape[0]) % (8 * LANE)
    return jnp.pad(flat, (0, pad)).reshape(-1, LANE)


def _unpack(packed, like):
    flat = packed.reshape(-1)
    out, o = [], 0
    for q in like:
        out.append(flat[o:o + q.size].reshape(q.shape))
        o += q.size
    return out


def kernel(x, p, positions, norm_mix, norm_ffn, norm_ple, mla_w_down, mla_q_lora_g, mla_kv_lora_g, mla_w_uq, mla_w_ukv, mla_q_nope_g, mla_q_rope_g, mla_k_nope_g, mla_k_rope_g, mla_w_out, gmlp_w_in, gmlp_ln_g, gmlp_ln_b, gmlp_w_s, gmlp_b_s, gmlp_w_out, ffn_w_up, ffn_w_down, ple_w_gate, ple_w_proj, loss_target, m_norm_mix, m_norm_ffn, m_norm_ple, m_mla_w_down, m_mla_q_lora_g, m_mla_kv_lora_g, m_mla_w_uq, m_mla_w_ukv, m_mla_q_nope_g, m_mla_q_rope_g, m_mla_k_nope_g, m_mla_k_rope_g, m_mla_w_out, m_gmlp_w_in, m_gmlp_ln_g, m_gmlp_ln_b, m_gmlp_w_s, m_gmlp_b_s, m_gmlp_w_out, m_ffn_w_up, m_ffn_w_down, m_ple_w_gate, m_ple_w_proj, v_norm_mix, v_norm_ffn, v_norm_ple, v_mla_w_down, v_mla_q_lora_g, v_mla_kv_lora_g, v_mla_w_uq, v_mla_w_ukv, v_mla_q_nope_g, v_mla_q_rope_g, v_mla_k_nope_g, v_mla_k_rope_g, v_mla_w_out, v_gmlp_w_in, v_gmlp_ln_g, v_gmlp_ln_b, v_gmlp_w_s, v_gmlp_b_s, v_gmlp_w_out, v_ffn_w_up, v_ffn_w_down, v_ple_w_gate, v_ple_w_proj):
    W = dict(zip(WEIGHTS, (norm_mix, norm_ffn, norm_ple, mla_w_down, mla_q_lora_g, mla_kv_lora_g, mla_w_uq, mla_w_ukv, mla_q_nope_g, mla_q_rope_g, mla_k_nope_g, mla_k_rope_g, mla_w_out, gmlp_w_in, gmlp_ln_g, gmlp_ln_b, gmlp_w_s, gmlp_b_s, gmlp_w_out, ffn_w_up, ffn_w_down, ple_w_gate, ple_w_proj)))
    M1 = dict(zip(WEIGHTS, (m_norm_mix, m_norm_ffn, m_norm_ple, m_mla_w_down, m_mla_q_lora_g, m_mla_kv_lora_g, m_mla_w_uq, m_mla_w_ukv, m_mla_q_nope_g, m_mla_q_rope_g, m_mla_k_nope_g, m_mla_k_rope_g, m_mla_w_out, m_gmlp_w_in, m_gmlp_ln_g, m_gmlp_ln_b, m_gmlp_w_s, m_gmlp_b_s, m_gmlp_w_out, m_ffn_w_up, m_ffn_w_down, m_ple_w_gate, m_ple_w_proj)))
    M2 = dict(zip(WEIGHTS, (v_norm_mix, v_norm_ffn, v_norm_ple, v_mla_w_down, v_mla_q_lora_g, v_mla_kv_lora_g, v_mla_w_uq, v_mla_w_ukv, v_mla_q_nope_g, v_mla_q_rope_g, v_mla_k_nope_g, v_mla_k_rope_g, v_mla_w_out, v_gmlp_w_in, v_gmlp_ln_g, v_gmlp_ln_b, v_gmlp_w_s, v_gmlp_b_s, v_gmlp_w_out, v_ffn_w_up, v_ffn_w_down, v_ple_w_gate, v_ple_w_proj)))

    nb, seq, d_model = x.shape
    T = nb * seq
    depth = norm_mix.shape[0]
    h = x.reshape(T, d_model)
    target = loss_target.reshape(T, d_model)
    p_bf = p.reshape(depth, T, p.shape[-1]).astype(BF16)

    payload = [W[n] if n in F32_PAYLOAD else W[n].astype(BF16) for n in SHARDED]
    gathered = _exchange(payload, False, "weights_all_gather")
    FW = {n: _assemble(g, SHARD_AXIS[n]) for n, g in zip(SHARDED, gathered)}

    row = lambda a: a.reshape(1, -1)
    cos, sin = _rope_tables(positions.reshape(T, 1), "rope_tables")
    rmat = _rot_matrix()

    saved = []
    for i in range(depth):
        j = i // 2
        s = {}
        s['h0'] = h
        hn = _rms_fwd(h, row(W['norm_mix'][i]), "rms_fwd")
        s['hn'] = hn
        if i % 2 == 0:
            gains = [row(W['mla_q_nope_g'][j]), row(W['mla_q_rope_g'][j]), row(W['mla_k_nope_g'][j]),
                     row(W['mla_k_rope_g'][j])]
            lat = _mm(hn, FW['mla_w_down'][j], name="mla_down")
            cq, ckv = _prep1_fwd(lat, row(W['mla_q_lora_g'][j]), row(W['mla_kv_lora_g'][j]), "mla_prep1")
            q_raw = _heads(_mm(cq, FW['mla_w_uq'][j], name="mla_uq"))
            kv_raw = _heads(_mm(ckv, FW['mla_w_ukv'][j], name="mla_ukv"))
            q, k, v = _prep2_fwd(q_raw, kv_raw, lat, cos, sin, rmat, gains, "mla_prep2")
            o, lse = _attn_fwd(q, k, v, seq, "attn_fwd")
            h = _mm(o, FW['mla_w_out'][j], extras=(h,), epilogue=lambda acc, res: (res + acc,), name="mla_out")
            s.update(lat=lat, cq=cq, ckv=ckv, q_raw=q_raw, kv_raw=kv_raw, q=q, k=k, v=v, o=o, lse=lse, gains=gains)
        else:
            z = _mm(hn, FW['gmlp_w_in'][j], name="gmlp_in")
            u, vn = _gelu_ln_fwd(z, row(FW['gmlp_ln_g'][j]), row(FW['gmlp_ln_b'][j]), "gmlp_gelu_ln")
            bs3 = W['gmlp_b_s'][j][:, :, None]
            y = _sgu_fwd(u, vn, W['gmlp_w_s'][j], bs3, "gmlp_sgu")
            h = _mm(y, FW['gmlp_w_out'][j], extras=(h,), epilogue=lambda acc, res: (res + acc,), name="gmlp_out")
            s.update(z=z, u=u, vn=vn, y=y, bs3=bs3)
        s['h1'] = h
        hn2 = _rms_fwd(h, row(W['norm_ffn'][i]), "rms_fwd")
        a, r = _mm(hn2, FW['ffn_w_up'][i], epilogue=lambda acc: (acc, jnp.square(jnp.maximum(acc, 0.0))),
                   out_dtypes=(F32, BF16), name="ffn_up")
        h = _mm(r, FW['ffn_w_down'][i], extras=(h,), epilogue=lambda acc, res: (res + acc,), name="ffn_down")
        s.update(hn2=hn2, a=a, r=r, h2=h)
        hn3 = _rms_fwd(h, row(W['norm_ple'][i]), "rms_fwd")
        gt = _mm(hn3, FW['ple_w_gate'][i], name="ple_gate")
        pp, h = _mm(p_bf[i], FW['ple_w_proj'][i], extras=(gt, h),
                    epilogue=lambda acc, g_, res: (acc, res + _sigmoid(g_) * acc), out_dtypes=(F32, F32),
                    name="ple_proj")
        s.update(hn3=hn3, gt=gt, pp=pp)
        saved.append(s)

    dh, loss_part = _loss_head(h, target, "loss_head")
    loss = lax.psum(loss_part, MESH_AXES)

    G = {n: [None] * W[n].shape[0] for n in WEIGHTS}
    for i in reversed(range(depth)):
        j = i // 2
        s = saved[i]
        def ple_elem(d, g_, pq):
            sg = _sigmoid(g_)
            return d * sg, d * pq * sg * (1.0 - sg)

        d_pp, d_gt = _rowcall(ple_elem, [dh, s['gt'], s['pp']], [], [(d_model, BF16), (d_model, BF16)], name="ple_bwd")
        G['ple_w_proj'][i] = _mm(p_bf[i], d_pp, ta=True, name="ple_proj_dw")
        G['ple_w_gate'][i] = _mm(s['hn3'], d_gt, ta=True, name="ple_gate_dw")
        d_hn3 = _mm(d_gt, FW['ple_w_gate'][i], tb=True, name="ple_gate_dx")
        dh, dg = _rms_bwd(s['h2'], d_hn3, dh, row(W['norm_ple'][i]), "rms_bwd")
        G['norm_ple'][i] = dg[0]
        d_a = _mm(dh, FW['ffn_w_down'][i], tb=True, extras=(s['a'],),
                  epilogue=lambda acc, a_: (acc * (2.0 * jnp.maximum(a_, 0.0)),), out_dtypes=(BF16,), name="ffn_down_dx")
        G['ffn_w_down'][i] = _mm(s['r'], dh, ta=True, name="ffn_down_dw")
        G['ffn_w_up'][i] = _mm(s['hn2'], d_a, ta=True, name="ffn_up_dw")
        d_hn2 = _mm(d_a, FW['ffn_w_up'][i], tb=True, name="ffn_up_dx")
        dh, dg = _rms_bwd(s['h1'], d_hn2, dh, row(W['norm_ffn'][i]), "rms_bwd")
        G['norm_ffn'][i] = dg[0]
        if i % 2 == 0:
            d_o = _mm(dh, FW['mla_w_out'][j], tb=True, name="mla_out_dx")
            G['mla_w_out'][j] = _mm(s['o'], dh, ta=True, name="mla_out_dw")
            dq, dk, dv = _attn_bwd(s['q'], s['k'], s['v'], s['o'], d_o, s['lse'], seq, "attn_bwd")
            d_q_raw, d_kv_raw, d_kr, g1, g2, g3, g4 = _prep2_bwd(
                s['q_raw'], s['kv_raw'], s['lat'], cos, sin, rmat, s['gains'], dq, dk, dv, "mla_prep2_bwd")
            G['mla_q_nope_g'][j], G['mla_q_rope_g'][j] = g1[0], g2[0]
            G['mla_k_nope_g'][j], G['mla_k_rope_g'][j] = g3[0], g4[0]
            d_q2, d_kv2 = _unheads(d_q_raw), _unheads(d_kv_raw)
            G['mla_w_uq'][j] = _mm(s['cq'], d_q2, ta=True, name="mla_uq_dw")
            G['mla_w_ukv'][j] = _mm(s['ckv'], d_kv2, ta=True, name="mla_ukv_dw")
            d_cq = _mm(d_q2, FW['mla_w_uq'][j], tb=True, name="mla_uq_dx")
            d_ckv = _mm(d_kv2, FW['mla_w_ukv'][j], tb=True, name="mla_ukv_dx")
            d_lat, dga, dgb = _prep1_bwd(s['lat'], d_cq, d_ckv, d_kr, row(W['mla_q_lora_g'][j]),
                                         row(W['mla_kv_lora_g'][j]), "mla_prep1_bwd")
            G['mla_q_lora_g'][j], G['mla_kv_lora_g'][j] = dga[0], dgb[0]
            G['mla_w_down'][j] = _mm(s['hn'], d_lat, ta=True, name="mla_down_dw")
            d_hn = _mm(d_lat, FW['mla_w_down'][j], tb=True, name="mla_down_dx")
        else:
            d_y = _mm(dh, FW['gmlp_w_out'][j], tb=True, name="gmlp_out_dx")
            G['gmlp_w_out'][j] = _mm(s['y'], dh, ta=True, name="gmlp_out_dw")
            d_u, d_vn, d_ws, d_bs = _sgu_bwd(s['u'], s['vn'], d_y, W['gmlp_w_s'][j], s['bs3'], "gmlp_sgu_bwd")
            G['gmlp_w_s'][j], G['gmlp_b_s'][j] = d_ws, d_bs[:, :, 0]
            d_z, d_lg, d_lb = _gelu_ln_bwd(s['z'], d_u, d_vn, row(FW['gmlp_ln_g'][j]), row(FW['gmlp_ln_b'][j]),
                                           "gmlp_gelu_ln_bwd")
            G['gmlp_ln_g'][j], G['gmlp_ln_b'][j] = d_lg[0], d_lb[0]
            G['gmlp_w_in'][j] = _mm(s['hn'], d_z, ta=True, name="gmlp_in_dw")
            d_hn = _mm(d_z, FW['gmlp_w_in'][j], tb=True, name="gmlp_in_dx")
        dh, dg = _rms_bwd(s['h0'], d_hn, dh, row(W['norm_mix'][i]), "rms_bwd")
        G['norm_mix'][i] = dg[0]
    grad_x = dh.reshape(x.shape)

    full = {n: jnp.stack(G[n]) for n in WEIGHTS}
    scattered = _exchange([_split(full[n], SHARD_AXIS[n]) for n in SHARDED], True, "grads_reduce_scatter")
    (small_parts,) = _exchange([_pack([full[n] for n in REPLICATED])], False, "small_grads_all_gather")

    res = {}
    for n, parts in zip(SHARDED, scattered):
        shp = W[n].shape
        two = (-1, shp[-1])
        outs = _adamw(parts.reshape((N_DEV,) + W[n].reshape(two).shape), W[n].reshape(two), M1[n].reshape(two),
                      M2[n].reshape(two), "adamw_" + n)
        res[n] = [o.reshape(shp) for o in outs]
    like = [W[n] for n in REPLICATED]
    outs = _adamw(small_parts, _pack(like), _pack([M1[n] for n in REPLICATED]), _pack([M2[n] for n in REPLICATED]),
                  "adamw_small")
    unpacked = [_unpack(o, like) for o in outs]
    for idx, n in enumerate(REPLICATED):
        res[n] = [unpacked[q][idx] for q in range(4)]

    return (loss, grad_x, *[res[n][0] for n in WEIGHTS], *[res[n][1] for n in WEIGHTS],
            *[res[n][2] for n in WEIGHTS], *[res[n][3] for n in WEIGHTS])
```

```python
import math

import numpy as np
import jax
import jax.numpy as jnp
from jax import lax
from jax.experimental import pallas as pl
from jax.experimental.pallas import tpu as pltpu

F32 = jnp.float32
BF16 = jnp.bfloat16

N_DEV = 8
MESH_AXES = ("x", "y", "c")
HEADS = 8
NOPE = 128
ROPE = 64
VDIM = 128
QK = NOPE + ROPE
Q_LORA = 384
KV_LORA = 256
ROPE_BASE = 10000.0
CHUNK = 128
GROUPS = 8
EPS = 1e-6
LR, B1, B2, ADAM_EPS, WD, STEP = 0.001, 0.9, 0.999, 1e-08, 0.01, 10
LANE = 128
VMEM_LIMIT = 48 * 1024 * 1024

WEIGHTS = ['norm_mix', 'norm_ffn', 'norm_ple', 'mla_w_down', 'mla_q_lora_g', 'mla_kv_lora_g', 'mla_w_uq',
           'mla_w_ukv', 'mla_q_nope_g', 'mla_q_rope_g', 'mla_k_nope_g', 'mla_k_rope_g', 'mla_w_out', 'gmlp_w_in',
           'gmlp_ln_g', 'gmlp_ln_b', 'gmlp_w_s', 'gmlp_b_s', 'gmlp_w_out', 'ffn_w_up', 'ffn_w_down', 'ple_w_gate',
           'ple_w_proj']
SHARD_AXIS = {'mla_w_down': 1, 'mla_w_uq': 2, 'mla_w_ukv': 2, 'mla_w_out': 1, 'gmlp_w_in': 2, 'gmlp_ln_g': 1,
              'gmlp_ln_b': 1, 'gmlp_w_out': 1, 'ffn_w_up': 2, 'ffn_w_down': 1, 'ple_w_gate': 1, 'ple_w_proj': 2}
SHARDED = list(SHARD_AXIS)
REPLICATED = [n for n in WEIGHTS if n not in SHARD_AXIS]
F32_PAYLOAD = ('gmlp_ln_g', 'gmlp_ln_b')


def _pick(dim, pref, align=LANE):
    if dim <= pref:
        return dim
    b = (pref // align) * align
    while b >= align:
        if dim % b == 0:
            return b
        b -= align
    return dim


def _params(sem):
    return pltpu.CompilerParams(dimension_semantics=sem, vmem_limit_bytes=VMEM_LIMIT)


def _mm(a, b, *, ta=False, tb=False, extras=(), epilogue=None, out_dtypes=(F32,), name,
        bm=512, bn=512, bk=1024):
    if ta:
        K, M = a.shape
    else:
        M, K = a.shape
    if tb:
        N, K2 = b.shape
    else:
        K2, N = b.shape
    assert K == K2, (a.shape, b.shape, ta, tb)
    bm, bn, bk = _pick(M, bm), _pick(N, bn), _pick(K, bk)
    nk = K // bk
    ne, no = len(extras), len(out_dtypes)
    dims = (((0,) if ta else (1,), (1,) if tb else (0,)), ((), ()))

    def body(*refs):
        a_ref, b_ref = refs[0], refs[1]
        e_refs = refs[2:2 + ne]
        o_refs = refs[2 + ne:2 + ne + no]
        acc = refs[-1]
        k = pl.program_id(2)

        @pl.when(k == 0)
        def _():
            acc[...] = jnp.zeros_like(acc)

        acc[...] += lax.dot_general(a_ref[...].astype(BF16), b_ref[...].astype(BF16), dims,
                                    preferred_element_type=F32)

        @pl.when(k == nk - 1)
        def _():
            r = acc[...]
            outs = epilogue(r, *[e[...] for e in e_refs]) if epilogue is not None else (r,)
            for o, v in zip(o_refs, outs):
                o[...] = v.astype(o.dtype)

    a_spec = pl.BlockSpec((bk, bm), lambda i, j, k: (k, i)) if ta else pl.BlockSpec((bm, bk), lambda i, j, k: (i, k))
    b_spec = pl.BlockSpec((bn, bk), lambda i, j, k: (j, k)) if tb else pl.BlockSpec((bk, bn), lambda i, j, k: (k, j))
    tile = lambda: pl.BlockSpec((bm, bn), lambda i, j, k: (i, j))
    outs = pl.pallas_call(
        body, name=name,
        grid=(M // bm, N // bn, nk),
        in_specs=[a_spec, b_spec] + [tile() for _ in extras],
        out_specs=[tile() for _ in out_dtypes],
        out_shape=[jax.ShapeDtypeStruct((M, N), dt) for dt in out_dtypes],
        scratch_shapes=[pltpu.VMEM((bm, bn), F32)],
        compiler_params=_params(("parallel", "parallel", "arbitrary")),
    )(a, b, *extras)
    return outs[0] if no == 1 else outs


def _rowcall(fn, rows, params, row_outs, acc_outs=(), *, bm=256, name):
    T = rows[0].shape[0]
    bm = _pick(T, bm, 8)
    nr, npar, nro, nao = len(rows), len(params), len(row_outs), len(acc_outs)

    def body(*refs):
        vals = [r[...] for r in refs[:nr + npar]]
        res = fn(*vals)
        ro = refs[nr + npar:nr + npar + nro]
        ao = refs[nr + npar + nro:]
        for r, v in zip(ro, res[:nro]):
            r[...] = v.astype(r.dtype)
        if nao:
            @pl.when(pl.program_id(0) == 0)
            def _():
                for r in ao:
                    r[...] = jnp.zeros_like(r)

            for r, v in zip(ao, res[nro:]):
                r[...] += v

    def whole(shape):
        nd = len(shape)
        return pl.BlockSpec(tuple(shape), lambda i: (0,) * nd)

    outs = pl.pallas_call(
        body, name=name,
        grid=(T // bm,),
        in_specs=[pl.BlockSpec((bm, r.shape[1]), lambda i: (i, 0)) for r in rows] + [whole(q.shape) for q in params],
        out_specs=[pl.BlockSpec((bm, c), lambda i: (i, 0)) for c, _ in row_outs] + [whole(s) for s in acc_outs],
        out_shape=[jax.ShapeDtypeStruct((T, c), dt) for c, dt in row_outs]
        + [jax.ShapeDtypeStruct(tuple(s), F32) for s in acc_outs],
        compiler_params=_params(("arbitrary",) if nao else ("parallel",)),
    )(*rows, *params)
    return outs


def _rmsn(x, g):
    return x * lax.rsqrt(jnp.mean(x * x, axis=-1, keepdims=True) + EPS) * g


def _gelu(x):
    return 0.5 * x * (1.0 + jnp.tanh(math.sqrt(2.0 / math.pi) * (x + 0.044715 * (x * x * x))))


def _layer_norm(x, g, b):
    mu = jnp.mean(x, axis=-1, keepdims=True)
    xc = x - mu
    return xc * lax.rsqrt(jnp.mean(xc * xc, axis=-1, keepdims=True) + EPS) * g + b


def _sigmoid(x):
    return 1.0 / (1.0 + jnp.exp(-x))


def _rot(x, cos, sin, rmat):
    return x * cos + jnp.dot(x, rmat, precision=lax.Precision.HIGHEST, preferred_element_type=F32) * sin


def _rms_fwd(h, g, name):
    return _rowcall(lambda x, gg: (_rmsn(x, gg),), [h], [g], [(h.shape[1], BF16)], name=name)[0]


def _rms_bwd(h, d_hn, dh_in, g, name):
    def fn(x, dy, dres, gg):
        _, vjp = jax.vjp(_rmsn, x, gg)
        dx, dg = vjp(dy)
        return dres + dx, dg

    return _rowcall(fn, [h, d_hn, dh_in], [g], [(h.shape[1], F32)], [g.shape], name=name)


def _rope_tables(pos, name):
    inv = np.float32(ROPE_BASE) ** (-(np.arange(0, ROPE, 2, dtype=np.float32) / np.float32(ROPE)))
    inv = jnp.asarray(np.concatenate([inv, inv])[None, :].astype(np.float32))

    def fn(pp, iv):
        ang = pp.astype(F32) * iv
        return jnp.cos(ang), jnp.sin(ang)

    return _rowcall(fn, [pos], [inv], [(ROPE, F32), (ROPE, F32)], name=name)


def _rot_matrix():
    r = np.zeros((ROPE, ROPE), np.float32)
    half = ROPE // 2
    for j in range(half):
        r[j + half, j] = -1.0
        r[j, j + half] = 1.0
    return jnp.asarray(r)


def _prep1_fwd(lat, gq, gkv, name):
    def fn(l, a, b):
        return _rmsn(l[:, :Q_LORA], a), _rmsn(l[:, Q_LORA:Q_LORA + KV_LORA], b)

    return _rowcall(fn, [lat], [gq, gkv], [(Q_LORA, BF16), (KV_LORA, BF16)], name=name)


def _prep1_bwd(lat, d_cq, d_ckv, d_kr, gq, gkv, name):
    def fn(l, dq, dkv, dkr, a, b):
        _, vq = jax.vjp(_rmsn, l[:, :Q_LORA], a)
        _, vkv = jax.vjp(_rmsn, l[:, Q_LORA:Q_LORA + KV_LORA], b)
        dxq, dga = vq(dq)
        dxkv, dgb = vkv(dkv)
        return jnp.concatenate([dxq, dxkv, dkr], axis=1), dga, dgb

    return _rowcall(fn, [lat, d_cq, d_ckv, d_kr], [gq, gkv], [(lat.shape[1], BF16)], [gq.shape, gkv.shape], name=name)


def _qk_fn(qn_raw, qr_raw, kn_raw, kr_raw, gqn, gqr, gkn, gkr, cos, sin, rmat):
    return (_rmsn(qn_raw, gqn), _rot(_rmsn(qr_raw, gqr), cos, sin, rmat),
            _rmsn(kn_raw, gkn), _rot(_rmsn(kr_raw, gkr), cos, sin, rmat))


def _prep2_fwd(q_raw, kv_raw, lat, cos, sin, rmat, gains, name, bm=256):
    H, T, _ = q_raw.shape
    bm = _pick(T, bm, 8)
    kr0 = Q_LORA + KV_LORA

    def body(q_ref, kv_ref, lat_ref, cos_ref, sin_ref, r_ref, gqn, gqr, gkn, gkr, qo, ko, vo):
        qr, kvr = q_ref[...], kv_ref[...]
        qn, qro, kn, kro = _qk_fn(qr[:, :NOPE], qr[:, NOPE:], kvr[:, :NOPE], lat_ref[:, kr0:kr0 + ROPE],
                                  gqn[...], gqr[...], gkn[...], gkr[...], cos_ref[...], sin_ref[...], r_ref[...])
        qo[:, :NOPE] = qn.astype(BF16)
        qo[:, NOPE:] = qro.astype(BF16)
        ko[:, :NOPE] = kn.astype(BF16)
        ko[:, NOPE:] = kro.astype(BF16)
        vo[...] = kvr[:, NOPE:].astype(BF16)

    hb = lambda c: pl.BlockSpec((None, bm, c), lambda m, h: (h, m, 0))
    rb = lambda c: pl.BlockSpec((bm, c), lambda m, h: (m, 0))
    wb = lambda s: pl.BlockSpec(tuple(s), lambda m, h: (0, 0))
    return pl.pallas_call(
        body, name=name, grid=(T // bm, H),
        in_specs=[hb(QK), hb(NOPE + VDIM), rb(lat.shape[1]), rb(ROPE), rb(ROPE), wb(rmat.shape)]
        + [wb(g.shape) for g in gains],
        out_specs=[hb(QK), hb(QK), hb(VDIM)],
        out_shape=[jax.ShapeDtypeStruct((H, T, QK), BF16), jax.ShapeDtypeStruct((H, T, QK), BF16),
                   jax.ShapeDtypeStruct((H, T, VDIM), BF16)],
        compiler_params=_params(("parallel", "parallel")),
    )(q_raw, kv_raw, lat, cos, sin, rmat, *gains)


def _prep2_bwd(q_raw, kv_raw, lat, cos, sin, rmat, gains, dq, dk, dv, name, bm=256):
    H, T, _ = q_raw.shape
    bm = _pick(T, bm, 8)
    kr0 = Q_LORA + KV_LORA

    def body(q_ref, kv_ref, lat_ref, cos_ref, sin_ref, r_ref, gqn, gqr, gkn, gkr, dq_ref, dk_ref, dv_ref,
             dqo, dkvo, dkro, o_gqn, o_gqr, o_gkn, o_gkr):
        m, h = pl.program_id(0), pl.program_id(1)
        qr, kvr = q_ref[...], kv_ref[...]
        cos_v, sin_v, r_v = cos_ref[...], sin_ref[...], r_ref[...]
        f = lambda a, b, c, d, g1, g2, g3, g4: _qk_fn(a, b, c, d, g1, g2, g3, g4, cos_v, sin_v, r_v)
        _, vjp = jax.vjp(f, qr[:, :NOPE], qr[:, NOPE:], kvr[:, :NOPE], lat_ref[:, kr0:kr0 + ROPE],
                         gqn[...], gqr[...], gkn[...], gkr[...])
        dqv, dkv_ = dq_ref[...], dk_ref[...]
        d_qn, d_qr, d_kn, d_kr, g1, g2, g3, g4 = vjp((dqv[:, :NOPE], dqv[:, NOPE:], dkv_[:, :NOPE], dkv_[:, NOPE:]))
        dqo[:, :NOPE] = d_qn.astype(BF16)
        dqo[:, NOPE:] = d_qr.astype(BF16)
        dkvo[:, :NOPE] = d_kn.astype(BF16)
        dkvo[:, NOPE:] = dv_ref[...].astype(BF16)

        @pl.when(h == 0)
        def _():
            dkro[...] = jnp.zeros_like(dkro)

        dkro[...] += d_kr

        @pl.when((h == 0) & (m == 0))
        def _():
            for o in (o_gqn, o_gqr, o_gkn, o_gkr):
                o[...] = jnp.zeros_like(o)

        for o, g in zip((o_gqn, o_gqr, o_gkn, o_gkr), (g1, g2, g3, g4)):
            o[...] += g

    hb = lambda c: pl.BlockSpec((None, bm, c), lambda m, h: (h, m, 0))
    rb = lambda c: pl.BlockSpec((bm, c), lambda m, h: (m, 0))
    wb = lambda s: pl.BlockSpec(tuple(s), lambda m, h: (0, 0))
    return pl.pallas_call(
        body, name=name, grid=(T // bm, H),
        in_specs=[hb(QK), hb(NOPE + VDIM), rb(lat.shape[1]), rb(ROPE), rb(ROPE), wb(rmat.shape)]
        + [wb(g.shape) for g in gains] + [hb(QK), hb(QK), hb(VDIM)],
        out_specs=[hb(QK), hb(NOPE + VDIM), rb(ROPE)] + [wb(g.shape) for g in gains],
        out_shape=[jax.ShapeDtypeStruct((H, T, QK), BF16), jax.ShapeDtypeStruct((H, T, NOPE + VDIM), BF16),
                   jax.ShapeDtypeStruct((T, ROPE), F32)] + [jax.ShapeDtypeStruct(g.shape, F32) for g in gains],
        compiler_params=_params(("arbitrary", "arbitrary")),
    )(q_raw, kv_raw, lat, cos, sin, rmat, *gains, dq, dk, dv)


_NT = (((1,), (1,)), ((), ()))
_TN = (((0,), (0,)), ((), ()))


def _attn_fwd(q, k, v, seq, name, blk=256):
    H, T, _ = q.shape
    nb = T // seq
    blk = _pick(seq, blk)
    nq = seq // blk
    scale = float(QK) ** -0.5

    def body(q_ref, k_ref, v_ref, o_ref, lse_ref):
        qi = pl.program_id(2)
        qb = q_ref[...]
        rows = qi * blk + lax.broadcasted_iota(jnp.int32, (blk, blk), 0)

        def step(j, carry):
            m, l, acc = carry
            ks = pl.ds(pl.multiple_of(j * blk, blk), blk)
            s = lax.dot_general(qb, k_ref[ks, :], _NT, preferred_element_type=F32) * scale
            cols = j * blk + lax.broadcasted_iota(jnp.int32, (blk, blk), 1)
            s = jnp.where(cols <= rows, s, -jnp.inf)
            m_new = jnp.maximum(m, jnp.max(s, axis=1, keepdims=True))
            pr = jnp.exp(s - m_new)
            alpha = jnp.exp(m - m_new)
            l = alpha * l + jnp.sum(pr, axis=1, keepdims=True)
            acc = alpha * acc + jnp.dot(pr.astype(BF16), v_ref[ks, :], preferred_element_type=F32)
            return m_new, l, acc

        init = (jnp.full((blk, 1), -jnp.inf, F32), jnp.zeros((blk, 1), F32), jnp.zeros((blk, VDIM), F32))
        m, l, acc = lax.fori_loop(0, qi + 1, step, init)
        o_ref[...] = (acc / l).astype(o_ref.dtype)
        lse_ref[...] = m + jnp.log(l)

    return pl.pallas_call(
        body, name=name, grid=(H, nb, nq),
        in_specs=[pl.BlockSpec((None, blk, QK), lambda h, b, i: (h, b * nq + i, 0)),
                  pl.BlockSpec((None, seq, QK), lambda h, b, i: (h, b, 0)),
                  pl.BlockSpec((None, seq, VDIM), lambda h, b, i: (h, b, 0))],
        out_specs=[pl.BlockSpec((blk, VDIM), lambda h, b, i: (b * nq + i, h)),
                   pl.BlockSpec((None, blk, 1), lambda h, b, i: (h, b * nq + i, 0))],
        out_shape=[jax.ShapeDtypeStruct((T, H * VDIM), BF16), jax.ShapeDtypeStruct((H, T, 1), F32)],
        compiler_params=_params(("parallel", "parallel", "parallel")),
    )(q, k, v)


def _attn_bwd(q, k, v, o, do, lse, seq, name, blk=256):
    H, T, _ = q.shape
    nb = T // seq
    blk = _pick(seq, blk)
    nq = seq // blk
    scale = float(QK) ** -0.5

    def body(q_ref, k_ref, v_ref, o_ref, do_ref, lse_ref, dq_ref, dk_ref, dv_ref):
        dk_ref[...] = jnp.zeros_like(dk_ref)
        dv_ref[...] = jnp.zeros_like(dv_ref)

        def qloop(i, carry):
            qs = pl.ds(pl.multiple_of(i * blk, blk), blk)
            qb = q_ref[qs, :]
            dof = do_ref[qs, :]
            dob = dof.astype(BF16)
            lse_b = lse_ref[qs, :]
            delta = jnp.sum(dof * o_ref[qs, :].astype(F32), axis=1, keepdims=True)
            rows = i * blk + lax.broadcasted_iota(jnp.int32, (blk, blk), 0)

            def kloop(j, dq_acc):
                ks = pl.ds(pl.multiple_of(j * blk, blk), blk)
                kb = k_ref[ks, :]
                vb = v_ref[ks, :]
                s = lax.dot_general(qb, kb, _NT, preferred_element_type=F32) * scale
                cols = j * blk + lax.broadcasted_iota(jnp.int32, (blk, blk), 1)
                pr = jnp.where(cols <= rows, jnp.exp(s - lse_b), 0.0)
                dp = lax.dot_general(dob, vb, _NT, preferred_element_type=F32)
                ds = (pr * (dp - delta) * scale).astype(BF16)
                prb = pr.astype(BF16)
                dv_ref[ks, :] += lax.dot_general(prb, dob, _TN, preferred_element_type=F32)
                dk_ref[ks, :] += lax.dot_general(ds, qb, _TN, preferred_element_type=F32)
                return dq_acc + jnp.dot(ds, kb, preferred_element_type=F32)

            dq_ref[qs, :] = lax.fori_loop(0, i + 1, kloop, jnp.zeros((blk, QK), F32))
            return carry

        lax.fori_loop(0, nq, qloop, 0)

    hb = lambda c: pl.BlockSpec((None, seq, c), lambda h, b: (h, b, 0))
    cb = lambda: pl.BlockSpec((seq, VDIM), lambda h, b: (b, h))
    return pl.pallas_call(
        body, name=name, grid=(H, nb),
        in_specs=[hb(QK), hb(QK), hb(VDIM), cb(), cb(), hb(1)],
        out_specs=[hb(QK), hb(QK), hb(VDIM)],
        out_shape=[jax.ShapeDtypeStruct((H, T, QK), F32), jax.ShapeDtypeStruct((H, T, QK), F32),
                   jax.ShapeDtypeStruct((H, T, VDIM), F32)],
        compiler_params=_params(("parallel", "parallel")),
    )(q, k, v, o, do, lse)


def _gelu_ln_fwd(z, g, b, name):
    half = z.shape[1] // 2

    def fn(zz, gg, bb):
        return _gelu(zz[:, :half]), _layer_norm(_gelu(zz[:, half:]), gg, bb)

    return _rowcall(fn, [z], [g, b], [(half, F32), (half, BF16)], name=name)


def _gelu_ln_bwd(z, d_u, d_vn, g, b, name):
    half = z.shape[1] // 2

    def fn(zz, du, dvn, gg, bb):
        _, vu = jax.vjp(_gelu, zz[:, :half])
        _, vv = jax.vjp(lambda t, a, c: _layer_norm(_gelu(t), a, c), zz[:, half:], gg, bb)
        (dzu,) = vu(du)
        dzv, dg, db = vv(dvn)
        return jnp.concatenate([dzu, dzv], axis=1), dg, db

    return _rowcall(fn, [z, d_u, d_vn], [g, b], [(z.shape[1], BF16)], [g.shape, b.shape], bm=128, name=name)


def _tril_bf16(ws):
    t = lax.broadcasted_iota(jnp.int32, ws.shape, 0)
    s = lax.broadcasted_iota(jnp.int32, ws.shape, 1)
    return jnp.where(s <= t, ws, 0.0).astype(BF16)


def _sgu_fwd(u, vn, ws, bs, name, bm=512):
    T, half = u.shape
    gd = half // GROUPS
    bm = _pick(T, bm, CHUNK)
    nc = bm // CHUNK

    def body(u_ref, vn_ref, ws_ref, bs_ref, y_ref):
        wm = _tril_bf16(ws_ref[...])
        bias = bs_ref[...]
        for c in range(nc):
            rs = slice(c * CHUNK, (c + 1) * CHUNK)
            sv = jnp.dot(wm, vn_ref[rs, :], preferred_element_type=F32) + bias
            y_ref[rs, :] = (u_ref[rs, :] * sv).astype(y_ref.dtype)

    tb = lambda: pl.BlockSpec((bm, gd), lambda g, i: (i, g))
    return pl.pallas_call(
        body, name=name, grid=(GROUPS, T // bm),
        in_specs=[tb(), tb(), pl.BlockSpec((None, CHUNK, CHUNK), lambda g, i: (g, 0, 0)),
                  pl.BlockSpec((None, CHUNK, 1), lambda g, i: (g, 0, 0))],
        out_specs=tb(),
        out_shape=jax.ShapeDtypeStruct((T, half), BF16),
        compiler_params=_params(("parallel", "parallel")),
    )(u, vn, ws, bs)


def _sgu_bwd(u, vn, dy, ws, bs, name, bm=512):
    T, half = u.shape
    gd = half // GROUPS
    bm = _pick(T, bm, CHUNK)
    nc = bm // CHUNK

    def body(u_ref, vn_ref, dy_ref, ws_ref, bs_ref, du_ref, dvn_ref, dws_ref, dbs_ref):
        @pl.when(pl.program_id(1) == 0)
        def _():
            dws_ref[...] = jnp.zeros_like(dws_ref)
            dbs_ref[...] = jnp.zeros_like(dbs_ref)

        wm = _tril_bf16(ws_ref[...])
        bias = bs_ref[...]
        dws = jnp.zeros((CHUNK, CHUNK), F32)
        dbs = jnp.zeros((CHUNK, 1), F32)
        for c in range(nc):
            rs = slice(c * CHUNK, (c + 1) * CHUNK)
            vb = vn_ref[rs, :]
            dyb = dy_ref[rs, :]
            sv = jnp.dot(wm, vb, preferred_element_type=F32) + bias
            du_ref[rs, :] = dyb * sv
            dsv = dyb * u_ref[rs, :]
            dsb = dsv.astype(BF16)
            dvn_ref[rs, :] = lax.dot_general(wm, dsb, _TN, preferred_element_type=F32)
            dws = dws + lax.dot_general(dsb, vb, _NT, preferred_element_type=F32)
            dbs = dbs + jnp.sum(dsv, axis=1, keepdims=True)
        t = lax.broadcasted_iota(jnp.int32, (CHUNK, CHUNK), 0)
        s = lax.broadcasted_iota(jnp.int32, (CHUNK, CHUNK), 1)
        dws_ref[...] += jnp.where(s <= t, dws, 0.0)
        dbs_ref[...] += dbs

    tb = lambda: pl.BlockSpec((bm, gd), lambda g, i: (i, g))
    wsb = lambda: pl.BlockSpec((None, CHUNK, CHUNK), lambda g, i: (g, 0, 0))
    bsb = lambda: pl.BlockSpec((None, CHUNK, 1), lambda g, i: (g, 0, 0))
    return pl.pallas_call(
        body, name=name, grid=(GROUPS, T // bm),
        in_specs=[tb(), tb(), tb(), wsb(), bsb()],
        out_specs=[tb(), tb(), wsb(), bsb()],
        out_shape=[jax.ShapeDtypeStruct((T, half), F32), jax.ShapeDtypeStruct((T, half), F32),
                   jax.ShapeDtypeStruct(ws.shape, F32), jax.ShapeDtypeStruct(bs.shape, F32)],
        compiler_params=_params(("parallel", "arbitrary")),
    )(u, vn, dy, ws, bs)


def _loss_head(y, t, name):
    d_model = y.shape[1]

    def fn(yy, tt):
        d = yy - tt
        part = 0.5 * jnp.sum(jnp.mean(d * d, axis=-1, keepdims=True), axis=0, keepdims=True)
        return d / d_model, jnp.zeros((1, LANE), F32) + part

    dy, part = _rowcall(fn, [y, t], [], [(d_model, F32)], [(1, LANE)], name=name)
    return dy, part[0, 0]


def _adamw(parts, w, m, v, name):
    R, C = w.shape
    br = _pick(R, max(8, (128 * 1024) // C // 8 * 8), 8)
    c1 = 1.0 - B1 ** STEP
    c2 = 1.0 - B2 ** STEP

    def body(p_ref, w_ref, m_ref, v_ref, g_o, d_o, m_o, v_o):
        g = p_ref[0]
        for d in range(1, N_DEV):
            g = g + p_ref[d]
        mn = B1 * m_ref[...] + (1.0 - B1) * g
        vn = B2 * v_ref[...] + (1.0 - B2) * (g * g)
        g_o[...] = g
        m_o[...] = mn
        v_o[...] = vn
        d_o[...] = -LR * ((mn / c1) / (jnp.sqrt(vn / c2) + ADAM_EPS) + WD * w_ref[...])

    blk = lambda: pl.BlockSpec((br, C), lambda i: (i, 0))
    return pl.pallas_call(
        body, name=name, grid=(R // br,),
        in_specs=[pl.BlockSpec((N_DEV, br, C), lambda i: (0, i, 0)), blk(), blk(), blk()],
        out_specs=[blk(), blk(), blk(), blk()],
        out_shape=[jax.ShapeDtypeStruct((R, C), F32)] * 4,
        compiler_params=_params(("parallel",)),
    )(parts, w, m, v)


def _mesh_pos():
    return lax.axis_index("x"), lax.axis_index("y"), lax.axis_index("c")


def _flip(pos, k):
    x, y, c = pos
    px = 1 - x if k & 4 else x
    py = 1 - y if k & 2 else y
    pc = 1 - c if k & 1 else c
    return px, py, pc


def _exchange(arrs, scatter, name):
    n = len(arrs)

    def body(*refs):
        ins, outs = refs[:n], refs[n:2 * n]
        send, recv, loc = refs[2 * n:]
        pos = _mesh_pos()
        me = 4 * pos[0] + 2 * pos[1] + pos[2]
        copies = []
        for a in range(n):
            src = ins[a].at[me] if scatter else ins[a]
            cp = pltpu.make_async_copy(src, outs[a].at[me], loc.at[a])
            cp.start()
            copies.append(cp)
        for k in range(1, N_DEV):
            peer = _flip(pos, k)
            peer_id = 4 * peer[0] + 2 * peer[1] + peer[2]
            for a in range(n):
                src = ins[a].at[peer_id] if scatter else ins[a]
                cp = pltpu.make_async_remote_copy(
                    src_ref=src, dst_ref=outs[a].at[me], send_sem=send.at[a, k - 1], recv_sem=recv.at[a, k - 1],
                    device_id=peer, device_id_type=pl.DeviceIdType.MESH)
                cp.start()
                copies.append(cp)
        for cp in copies:
            cp.wait()

    out_shape = [jax.ShapeDtypeStruct(a.shape if scatter else (N_DEV,) + a.shape, a.dtype) for a in arrs]
    return pl.pallas_call(
        body, name=name,
        in_specs=[pl.BlockSpec(memory_space=pl.ANY)] * n,
        out_specs=[pl.BlockSpec(memory_space=pl.ANY)] * n,
        out_shape=out_shape,
        scratch_shapes=[pltpu.SemaphoreType.DMA((n, N_DEV - 1)), pltpu.SemaphoreType.DMA((n, N_DEV - 1)),
                        pltpu.SemaphoreType.DMA((n,))],
        compiler_params=pltpu.CompilerParams(has_side_effects=True),
    )(*arrs)


HBM_SPEC = pl.BlockSpec(memory_space=pltpu.HBM)
SEM_SPEC = pl.BlockSpec(memory_space=pltpu.SEMAPHORE)
EFFECT = pltpu.SideEffectType.DATAFLOW_SIDE_EFFECTING


def _hbm(a):
    return pltpu.with_memory_space_constraint(a, pltpu.HBM)


def _gather_start(srcs, name):
    n = len(srcs)

    def body(*refs):
        src, land = refs[:n], refs[n:2 * n]
        send, recv, token = refs[2 * n], refs[2 * n + 1], refs[-1]
        pos = _mesh_pos()
        me = 4 * pos[0] + 2 * pos[1] + pos[2]
        for k in range(1, N_DEV):
            peer = _flip(pos, k)
            for a in range(n):
                pltpu.make_async_remote_copy(
                    src_ref=src[a], dst_ref=land[a].at[me], send_sem=send.at[7 * a + k - 1],
                    recv_sem=recv.at[7 * a + k - 1], device_id=peer, device_id_type=pl.DeviceIdType.MESH).start()
        token[...] = jnp.zeros_like(token)

    lands = [lax.empty((N_DEV,) + s.shape, s.dtype) for s in srcs]
    outs = pl.pallas_call(
        body, name=name,
        out_shape=(pltpu.SemaphoreType.DMA((7 * n,)), pltpu.SemaphoreType.DMA((7 * n,)),
                   *[pltpu.HBM(s.shape, s.dtype) for s in srcs], *[pltpu.HBM(l.shape, l.dtype) for l in lands],
                   jax.ShapeDtypeStruct((8, LANE), F32)),
        in_specs=[HBM_SPEC] * (2 * n),
        out_specs=(SEM_SPEC, SEM_SPEC, *[HBM_SPEC] * (2 * n), pl.BlockSpec(memory_space=pltpu.VMEM)),
        input_output_aliases={q: 2 + q for q in range(2 * n)},
        compiler_params=pltpu.CompilerParams(has_side_effects=EFFECT),
    )(*[_hbm(s) for s in srcs], *[_hbm(l) for l in lands])
    return outs[0], outs[1], list(outs[2:2 + n]), list(outs[2 + n:2 + 2 * n]), outs[-1]


def _gather_wait(send, recv, idxs, srcs, lands, after, name):
    m = len(idxs)

    def body(*refs):
        src, land = refs[:m], refs[m:2 * m]
        send_r, recv_r = refs[2 * m], refs[2 * m + 1]
        pos = _mesh_pos()
        me = 4 * pos[0] + 2 * pos[1] + pos[2]
        for k in range(1, N_DEV):
            peer = _flip(pos, k)
            for q, a in enumerate(idxs):
                cp = pltpu.make_async_remote_copy(
                    src_ref=src[q], dst_ref=land[q].at[me], send_sem=send_r.at[7 * a + k - 1],
                    recv_sem=recv_r.at[7 * a + k - 1], device_id=peer, device_id_type=pl.DeviceIdType.MESH)
                cp.wait_send()
                cp.wait_recv()

    outs = pl.pallas_call(
        body, name=name,
        out_shape=[pltpu.HBM(s.shape, s.dtype) for s in srcs] + [pltpu.HBM(l.shape, l.dtype) for l in lands],
        in_specs=[HBM_SPEC] * (2 * m) + [SEM_SPEC, SEM_SPEC, pl.BlockSpec(memory_space=pl.ANY)],
        out_specs=[HBM_SPEC] * (2 * m),
        input_output_aliases={q: q for q in range(2 * m)},
        compiler_params=pltpu.CompilerParams(has_side_effects=EFFECT),
    )(*srcs, *lands, send, recv, after)
    return list(outs[:m]), list(outs[m:])


def _assemble(gathered, axis):
    g = jnp.moveaxis(gathered, 0, axis)
    s = g.shape
    return g.reshape(s[:axis] + (s[axis] * s[axis + 1],) + s[axis + 2:])


def _split(full, axis):
    s = full.shape
    g = full.reshape(s[:axis] + (N_DEV, s[axis] // N_DEV) + s[axis + 1:])
    return jnp.moveaxis(g, axis, 0)


def _heads(a2d):
    T = a2d.shape[0]
    return jnp.transpose(a2d.reshape(T, HEADS, -1), (1, 0, 2))


def _unheads(a3d):
    H, T, c = a3d.shape
    return jnp.transpose(a3d, (1, 0, 2)).reshape(T, H * c)


def _pack(parts):
    flat = jnp.concatenate([q.reshape(-1) for q in parts])
    pad = (-flat.shape[0]) % (8 * LANE)
    return jnp.pad(flat, (0, pad)).reshape(-1, LANE)


def _unpack(packed, like):
    flat = packed.reshape(-1)
    out, o = [], 0
    for q in like:
        out.append(flat[o:o + q.size].reshape(q.shape))
        o += q.size
    return out


def kernel(x, p, positions, norm_mix, norm_ffn, norm_ple, mla_w_down, mla_q_lora_g, mla_kv_lora_g, mla_w_uq, mla_w_ukv, mla_q_nope_g, mla_q_rope_g, mla_k_nope_g, mla_k_rope_g, mla_w_out, gmlp_w_in, gmlp_ln_g, gmlp_ln_b, gmlp_w_s, gmlp_b_s, gmlp_w_out, ffn_w_up, ffn_w_down, ple_w_gate, ple_w_proj, loss_target, m_norm_mix, m_norm_ffn, m_norm_ple, m_mla_w_down, m_mla_q_lora_g, m_mla_kv_lora_g, m_mla_w_uq, m_mla_w_ukv, m_mla_q_nope_g, m_mla_q_rope_g, m_mla_k_nope_g, m_mla_k_rope_g, m_mla_w_out, m_gmlp_w_in, m_gmlp_ln_g, m_gmlp_ln_b, m_gmlp_w_s, m_gmlp_b_s, m_gmlp_w_out, m_ffn_w_up, m_ffn_w_down, m_ple_w_gate, m_ple_w_proj, v_norm_mix, v_norm_ffn, v_norm_ple, v_mla_w_down, v_mla_q_lora_g, v_mla_kv_lora_g, v_mla_w_uq, v_mla_w_ukv, v_mla_q_nope_g, v_mla_q_rope_g, v_mla_k_nope_g, v_mla_k_rope_g, v_mla_w_out, v_gmlp_w_in, v_gmlp_ln_g, v_gmlp_ln_b, v_gmlp_w_s, v_gmlp_b_s, v_gmlp_w_out, v_ffn_w_up, v_ffn_w_down, v_ple_w_gate, v_ple_w_proj):
    W = dict(zip(WEIGHTS, (norm_mix, norm_ffn, norm_ple, mla_w_down, mla_q_lora_g, mla_kv_lora_g, mla_w_uq, mla_w_ukv, mla_q_nope_g, mla_q_rope_g, mla_k_nope_g, mla_k_rope_g, mla_w_out, gmlp_w_in, gmlp_ln_g, gmlp_ln_b, gmlp_w_s, gmlp_b_s, gmlp_w_out, ffn_w_up, ffn_w_down, ple_w_gate, ple_w_proj)))
    M1 = dict(zip(WEIGHTS, (m_norm_mix, m_norm_ffn, m_norm_ple, m_mla_w_down, m_mla_q_lora_g, m_mla_kv_lora_g, m_mla_w_uq, m_mla_w_ukv, m_mla_q_nope_g, m_mla_q_rope_g, m_mla_k_nope_g, m_mla_k_rope_g, m_mla_w_out, m_gmlp_w_in, m_gmlp_ln_g, m_gmlp_ln_b, m_gmlp_w_s, m_gmlp_b_s, m_gmlp_w_out, m_ffn_w_up, m_ffn_w_down, m_ple_w_gate, m_ple_w_proj)))
    M2 = dict(zip(WEIGHTS, (v_norm_mix, v_norm_ffn, v_norm_ple, v_mla_w_down, v_mla_q_lora_g, v_mla_kv_lora_g, v_mla_w_uq, v_mla_w_ukv, v_mla_q_nope_g, v_mla_q_rope_g, v_mla_k_nope_g, v_mla_k_rope_g, v_mla_w_out, v_gmlp_w_in, v_gmlp_ln_g, v_gmlp_ln_b, v_gmlp_w_s, v_gmlp_b_s, v_gmlp_w_out, v_ffn_w_up, v_ffn_w_down, v_ple_w_gate, v_ple_w_proj)))

    nb, seq, d_model = x.shape
    T = nb * seq
    depth = norm_mix.shape[0]
    h = x.reshape(T, d_model)
    target = loss_target.reshape(T, d_model)
    p_bf = p.reshape(depth, T, p.shape[-1]).astype(BF16)

    pos = _mesh_pos()
    me = 4 * pos[0] + 2 * pos[1] + pos[2]
    keys = []
    for i in range(depth):
        mix = (['mla_w_down', 'mla_w_uq', 'mla_w_ukv', 'mla_w_out'] if i % 2 == 0 else
               ['gmlp_w_in', 'gmlp_ln_g', 'gmlp_ln_b', 'gmlp_w_out'])
        keys.append([(n, i // 2) for n in mix] + [(n, i) for n in ('ffn_w_up', 'ffn_w_down', 'ple_w_gate', 'ple_w_proj')])
    flat = [k for grp in keys for k in grp]
    srcs = [W[n][l] if n in F32_PAYLOAD else W[n][l].astype(BF16) for n, l in flat]
    send, recv, srcs_t, lands_t, token = _gather_start(srcs, "weights_gather_start")
    FW = {n: {} for n in SHARDED}

    def wait_weights(i, after):
        base = sum(len(g) for g in keys[:i])
        idxs = list(range(base, base + len(keys[i])))
        own, lands = _gather_wait(send, recv, idxs, [srcs_t[q] for q in idxs], [lands_t[q] for q in idxs], after,
                                  "weights_gather_wait%d" % i)
        for (n, l), mine, land in zip(keys[i], own, lands):
            FW[n][l] = _assemble(lax.dynamic_update_index_in_dim(land, mine, me, 0), SHARD_AXIS[n] - 1)

    row = lambda a: a.reshape(1, -1)
    cos, sin = _rope_tables(positions.reshape(T, 1), "rope_tables")
    rmat = _rot_matrix()

    saved = []
    for i in range(depth):
        j = i // 2
        wait_weights(i, token if i == 0 else h)
        s = {}
        s['h0'] = h
        hn = _rms_fwd(h, row(W['norm_mix'][i]), "rms_fwd")
        s['hn'] = hn
        if i % 2 == 0:
            gains = [row(W['mla_q_nope_g'][j]), row(W['mla_q_rope_g'][j]), row(W['mla_k_nope_g'][j]),
                     row(W['mla_k_rope_g'][j])]
            lat = _mm(hn, FW['mla_w_down'][j], name="mla_down")
            cq, ckv = _prep1_fwd(lat, row(W['mla_q_lora_g'][j]), row(W['mla_kv_lora_g'][j]), "mla_prep1")
            q_raw = _heads(_mm(cq, FW['mla_w_uq'][j], name="mla_uq"))
            kv_raw = _heads(_mm(ckv, FW['mla_w_ukv'][j], name="mla_ukv"))
            q, k, v = _prep2_fwd(q_raw, kv_raw, lat, cos, sin, rmat, gains, "mla_prep2")
            o, lse = _attn_fwd(q, k, v, seq, "attn_fwd")
            h = _mm(o, FW['mla_w_out'][j], extras=(h,), epilogue=lambda acc, res: (res + acc,), name="mla_out")
            s.update(lat=lat, cq=cq, ckv=ckv, q_raw=q_raw, kv_raw=kv_raw, q=q, k=k, v=v, o=o, lse=lse, gains=gains)
        else:
            z = _mm(hn, FW['gmlp_w_in'][j], name="gmlp_in")
            u, vn = _gelu_ln_fwd(z, row(FW['gmlp_ln_g'][j]), row(FW['gmlp_ln_b'][j]), "gmlp_gelu_ln")
            bs3 = W['gmlp_b_s'][j][:, :, None]
            y = _sgu_fwd(u, vn, W['gmlp_w_s'][j], bs3, "gmlp_sgu")
            h = _mm(y, FW['gmlp_w_out'][j], extras=(h,), epilogue=lambda acc, res: (res + acc,), name="gmlp_out")
            s.update(z=z, u=u, vn=vn, y=y, bs3=bs3)
        s['h1'] = h
        hn2 = _rms_fwd(h, row(W['norm_ffn'][i]), "rms_fwd")
        a, r = _mm(hn2, FW['ffn_w_up'][i], epilogue=lambda acc: (acc, jnp.square(jnp.maximum(acc, 0.0))),
                   out_dtypes=(F32, BF16), name="ffn_up")
        h = _mm(r, FW['ffn_w_down'][i], extras=(h,), epilogue=lambda acc, res: (res + acc,), name="ffn_down")
        s.update(hn2=hn2, a=a, r=r, h2=h)
        hn3 = _rms_fwd(h, row(W['norm_ple'][i]), "rms_fwd")
        gt = _mm(hn3, FW['ple_w_gate'][i], name="ple_gate")
        pp, h = _mm(p_bf[i], FW['ple_w_proj'][i], extras=(gt, h),
                    epilogue=lambda acc, g_, res: (acc, res + _sigmoid(g_) * acc), out_dtypes=(F32, F32),
                    name="ple_proj")
        s.update(hn3=hn3, gt=gt, pp=pp)
        saved.append(s)

    dh, loss_part = _loss_head(h, target, "loss_head")
    loss = lax.psum(loss_part, MESH_AXES)

    G = {n: [None] * W[n].shape[0] for n in WEIGHTS}
    for i in reversed(range(depth)):
        j = i // 2
        s = saved[i]
        def ple_elem(d, g_, pq):
            sg = _sigmoid(g_)
            return d * sg, d * pq * sg * (1.0 - sg)

        d_pp, d_gt = _rowcall(ple_elem, [dh, s['gt'], s['pp']], [], [(d_model, BF16), (d_model, BF16)], name="ple_bwd")
        G['ple_w_proj'][i] = _mm(p_bf[i], d_pp, ta=True, name="ple_proj_dw")
        G['ple_w_gate'][i] = _mm(s['hn3'], d_gt, ta=True, name="ple_gate_dw")
        d_hn3 = _mm(d_gt, FW['ple_w_gate'][i], tb=True, name="ple_gate_dx")
        dh, dg = _rms_bwd(s['h2'], d_hn3, dh, row(W['norm_ple'][i]), "rms_bwd")
        G['norm_ple'][i] = dg[0]
        d_a = _mm(dh, FW['ffn_w_down'][i], tb=True, extras=(s['a'],),
                  epilogue=lambda acc, a_: (acc * (2.0 * jnp.maximum(a_, 0.0)),), out_dtypes=(BF16,), name="ffn_down_dx")
        G['ffn_w_down'][i] = _mm(s['r'], dh, ta=True, name="ffn_down_dw")
        G['ffn_w_up'][i] = _mm(s['hn2'], d_a, ta=True, name="ffn_up_dw")
        d_hn2 = _mm(d_a, FW['ffn_w_up'][i], tb=True, name="ffn_up_dx")
        dh, dg = _rms_bwd(s['h1'], d_hn2, dh, row(W['norm_ffn'][i]), "rms_bwd")
        G['norm_ffn'][i] = dg[0]
        if i % 2 == 0:
            d_o = _mm(dh, FW['mla_w_out'][j], tb=True, name="mla_out_dx")
            G['mla_w_out'][j] = _mm(s['o'], dh, ta=True, name="mla_out_dw")
            dq, dk, dv = _attn_bwd(s['q'], s['k'], s['v'], s['o'], d_o, s['lse'], seq, "attn_bwd")
            d_q_raw, d_kv_raw, d_kr, g1, g2, g3, g4 = _prep2_bwd(
                s['q_raw'], s['kv_raw'], s['lat'], cos, sin, rmat, s['gains'], dq, dk, dv, "mla_prep2_bwd")
            G['mla_q_nope_g'][j], G['mla_q_rope_g'][j] = g1[0], g2[0]
            G['mla_k_nope_g'][j], G['mla_k_rope_g'][j] = g3[0], g4[0]
            d_q2, d_kv2 = _unheads(d_q_raw), _unheads(d_kv_raw)
            G['mla_w_uq'][j] = _mm(s['cq'], d_q2, ta=True, name="mla_uq_dw")
            G['mla_w_ukv'][j] = _mm(s['ckv'], d_kv2, ta=True, name="mla_ukv_dw")
            d_cq = _mm(d_q2, FW['mla_w_uq'][j], tb=True, name="mla_uq_dx")
            d_ckv = _mm(d_kv2, FW['mla_w_ukv'][j], tb=True, name="mla_ukv_dx")
            d_lat, dga, dgb = _prep1_bwd(s['lat'], d_cq, d_ckv, d_kr, row(W['mla_q_lora_g'][j]),
                                         row(W['mla_kv_lora_g'][j]), "mla_prep1_bwd")
            G['mla_q_lora_g'][j], G['mla_kv_lora_g'][j] = dga[0], dgb[0]
            G['mla_w_down'][j] = _mm(s['hn'], d_lat, ta=True, name="mla_down_dw")
            d_hn = _mm(d_lat, FW['mla_w_down'][j], tb=True, name="mla_down_dx")
        else:
            d_y = _mm(dh, FW['gmlp_w_out'][j], tb=True, name="gmlp_out_dx")
            G['gmlp_w_out'][j] = _mm(s['y'], dh, ta=True, name="gmlp_out_dw")
            d_u, d_vn, d_ws, d_bs = _sgu_bwd(s['u'], s['vn'], d_y, W['gmlp_w_s'][j], s['bs3'], "gmlp_sgu_bwd")
            G['gmlp_w_s'][j], G['gmlp_b_s'][j] = d_ws, d_bs[:, :, 0]
            d_z, d_lg, d_lb = _gelu_ln_bwd(s['z'], d_u, d_vn, row(FW['gmlp_ln_g'][j]), row(FW['gmlp_ln_b'][j]),
                                           "gmlp_gelu_ln_bwd")
            G['gmlp_ln_g'][j], G['gmlp_ln_b'][j] = d_lg[0], d_lb[0]
            G['gmlp_w_in'][j] = _mm(s['hn'], d_z, ta=True, name="gmlp_in_dw")
            d_hn = _mm(d_z, FW['gmlp_w_in'][j], tb=True, name="gmlp_in_dx")
        dh, dg = _rms_bwd(s['h0'], d_hn, dh, row(W['norm_mix'][i]), "rms_bwd")
        G['norm_mix'][i] = dg[0]
    grad_x = dh.reshape(x.shape)

    full = {n: jnp.stack(G[n]) for n in WEIGHTS}
    scattered = _exchange([_split(full[n], SHARD_AXIS[n]) for n in SHARDED], True, "grads_reduce_scatter")
    (small_parts,) = _exchange([_pack([full[n] for n in REPLICATED])], False, "small_grads_all_gather")

    res = {}
    for n, parts in zip(SHARDED, scattered):
        shp = W[n].shape
        two = (-1, shp[-1])
        outs = _adamw(parts.reshape((N_DEV,) + W[n].reshape(two).shape), W[n].reshape(two), M1[n].reshape(two),
                      M2[n].reshape(two), "adamw_" + n)
        res[n] = [o.reshape(shp) for o in outs]
    like = [W[n] for n in REPLICATED]
    outs = _adamw(small_parts, _pack(like), _pack([M1[n] for n in REPLICATED]), _pack([M2[n] for n in REPLICATED]),
                  "adamw_small")
    unpacked = [_unpack(o, like) for o in outs]
    for idx, n in enumerate(REPLICATED):
        res[n] = [unpacked[q][idx] for q in range(4)]

    return (loss, grad_x, *[res[n][0] for n in WEIGHTS], *[res[n][1] for n in WEIGHTS],
            *[res[n][2] for n in WEIGHTS], *[res[n][3] for n in WEIGHTS])
```

```python
import math

import numpy as np
import jax
import jax.numpy as jnp
from jax import lax
from jax.experimental import pallas as pl
from jax.experimental.pallas import tpu as pltpu

F32 = jnp.float32
BF16 = jnp.bfloat16

N_DEV = 8
MESH_AXES = ("x", "y", "c")
HEADS = 8
NOPE = 128
ROPE = 64
VDIM = 128
QK = NOPE + ROPE
Q_LORA = 384
KV_LORA = 256
ROPE_BASE = 10000.0
CHUNK = 128
GROUPS = 8
EPS = 1e-6
LR, B1, B2, ADAM_EPS, WD, STEP = 0.001, 0.9, 0.999, 1e-08, 0.01, 10
LANE = 128
VMEM_LIMIT = 48 * 1024 * 1024

WEIGHTS = ['norm_mix', 'norm_ffn', 'norm_ple', 'mla_w_down', 'mla_q_lora_g', 'mla_kv_lora_g', 'mla_w_uq',
           'mla_w_ukv', 'mla_q_nope_g', 'mla_q_rope_g', 'mla_k_nope_g', 'mla_k_rope_g', 'mla_w_out', 'gmlp_w_in',
           'gmlp_ln_g', 'gmlp_ln_b', 'gmlp_w_s', 'gmlp_b_s', 'gmlp_w_out', 'ffn_w_up', 'ffn_w_down', 'ple_w_gate',
           'ple_w_proj']
SHARD_AXIS = {'mla_w_down': 1, 'mla_w_uq': 2, 'mla_w_ukv': 2, 'mla_w_out': 1, 'gmlp_w_in': 2, 'gmlp_ln_g': 1,
              'gmlp_ln_b': 1, 'gmlp_w_out': 1, 'ffn_w_up': 2, 'ffn_w_down': 1, 'ple_w_gate': 1, 'ple_w_proj': 2}
SHARDED = list(SHARD_AXIS)
REPLICATED = [n for n in WEIGHTS if n not in SHARD_AXIS]
F32_PAYLOAD = ('gmlp_ln_g', 'gmlp_ln_b')


def _pick(dim, pref, align=LANE):
    if dim <= pref:
        return dim
    b = (pref // align) * align
    while b >= align:
        if dim % b == 0:
            return b
        b -= align
    return dim


def _params(sem):
    return pltpu.CompilerParams(dimension_semantics=sem, vmem_limit_bytes=VMEM_LIMIT)


def _mm(a, b, *, ta=False, tb=False, extras=(), epilogue=None, out_dtypes=(F32,), out_blocks=None, name,
        bm=1024, bn=1024, bk=1024):
    a3, b3 = a.ndim == 3, b.ndim == 3
    assert not (ta and a3)
    if ta:
        K, M = a.shape
        ka = K
    elif a3:
        M, ka = a.shape[1:]
        K = a.shape[0] * ka
    else:
        M, K = a.shape
        ka = K
    if tb:
        N, kb = b.shape[-2:]
        nb = N
        K2 = b.shape[0] * kb if b3 else kb
    else:
        kb, nb = b.shape[-2:]
        K2 = kb
        N = b.shape[0] * nb if b3 else nb
    assert K == K2, (a.shape, b.shape, ta, tb)
    no_ = N // out_blocks if out_blocks else N
    assert not (out_blocks and extras)
    bm, bn, bk = _pick(M, bm), _pick(min(nb, no_), bn), _pick(min(ka, kb), bk)
    nk = K // bk
    ne, no = len(extras), len(out_dtypes)
    dims = (((0,) if ta else (1,), (1,) if tb else (0,)), ((), ()))

    def finish(r, e_refs, o_refs):
        outs = epilogue(r, *[e[...] for e in e_refs]) if epilogue is not None else (r,)
        for o, v in zip(o_refs, outs):
            o[...] = v.astype(o.dtype)

    def body(*refs):
        a_ref, b_ref = refs[0], refs[1]
        e_refs = refs[2:2 + ne]
        o_refs = refs[2 + ne:2 + ne + no]
        part = lax.dot_general(a_ref[...].astype(BF16), b_ref[...].astype(BF16), dims, preferred_element_type=F32)
        if nk == 1:
            finish(part, e_refs, o_refs)
            return
        acc = refs[-1]
        k = pl.program_id(2)

        @pl.when(k == 0)
        def _():
            acc[...] = part

        @pl.when(k > 0)
        def _():
            acc[...] += part

        @pl.when(k == nk - 1)
        def _():
            finish(acc[...], e_refs, o_refs)

    ka_t, kb_t, nb_t, no_t = ka // bk, kb // bk, nb // bn, no_ // bn
    if ta:
        a_spec = pl.BlockSpec((bk, bm), lambda i, j, k: (k, i))
    elif a3:
        a_spec = pl.BlockSpec((None, bm, bk), lambda i, j, k: (k // ka_t, i, k % ka_t))
    else:
        a_spec = pl.BlockSpec((bm, bk), lambda i, j, k: (i, k))
    if tb:
        b_spec = (pl.BlockSpec((None, bn, bk), lambda i, j, k: (k // kb_t, j, k % kb_t)) if b3 else
                  pl.BlockSpec((bn, bk), lambda i, j, k: (j, k)))
    else:
        b_spec = (pl.BlockSpec((None, bk, bn), lambda i, j, k: (j // nb_t, k, j % nb_t)) if b3 else
                  pl.BlockSpec((bk, bn), lambda i, j, k: (k, j)))
    if out_blocks:
        o_spec = lambda: pl.BlockSpec((None, bm, bn), lambda i, j, k: (j // no_t, i, j % no_t))
        o_shape = (out_blocks, M, no_)
    else:
        o_spec = lambda: pl.BlockSpec((bm, bn), lambda i, j, k: (i, j))
        o_shape = (M, N)
    outs = pl.pallas_call(
        body, name=name,
        grid=(M // bm, N // bn, nk),
        in_specs=[a_spec, b_spec] + [pl.BlockSpec((bm, bn), lambda i, j, k: (i, j)) for _ in extras],
        out_specs=[o_spec() for _ in out_dtypes],
        out_shape=[jax.ShapeDtypeStruct(o_shape, dt) for dt in out_dtypes],
        scratch_shapes=[pltpu.VMEM((bm, bn), F32)] if nk > 1 else [],
        compiler_params=_params(("parallel", "parallel", "arbitrary")),
    )(a, b, *extras)
    return outs[0] if no == 1 else outs


def _rowcall(fn, rows, params, row_outs, acc_outs=(), *, bm=256, name):
    T = rows[0].shape[0]
    bm = _pick(T, bm, 8)
    nr, npar, nro, nao = len(rows), len(params), len(row_outs), len(acc_outs)

    def body(*refs):
        vals = [r[...] for r in refs[:nr + npar]]
        res = fn(*vals)
        ro = refs[nr + npar:nr + npar + nro]
        ao = refs[nr + npar + nro:]
        for r, v in zip(ro, res[:nro]):
            r[...] = v.astype(r.dtype)
        if nao:
            @pl.when(pl.program_id(0) == 0)
            def _():
                for r in ao:
                    r[...] = jnp.zeros_like(r)

            for r, v in zip(ao, res[nro:]):
                r[...] += v

    def whole(shape):
        nd = len(shape)
        return pl.BlockSpec(tuple(shape), lambda i: (0,) * nd)

    outs = pl.pallas_call(
        body, name=name,
        grid=(T // bm,),
        in_specs=[pl.BlockSpec((bm, r.shape[1]), lambda i: (i, 0)) for r in rows] + [whole(q.shape) for q in params],
        out_specs=[pl.BlockSpec((bm, c), lambda i: (i, 0)) for c, _ in row_outs] + [whole(s) for s in acc_outs],
        out_shape=[jax.ShapeDtypeStruct((T, c), dt) for c, dt in row_outs]
        + [jax.ShapeDtypeStruct(tuple(s), F32) for s in acc_outs],
        compiler_params=_params(("arbitrary",) if nao else ("parallel",)),
    )(*rows, *params)
    return outs


def _rmsn(x, g):
    return x * lax.rsqrt(jnp.mean(x * x, axis=-1, keepdims=True) + EPS) * g


def _gelu(x):
    return 0.5 * x * (1.0 + jnp.tanh(math.sqrt(2.0 / math.pi) * (x + 0.044715 * (x * x * x))))


def _layer_norm(x, g, b):
    mu = jnp.mean(x, axis=-1, keepdims=True)
    xc = x - mu
    return xc * lax.rsqrt(jnp.mean(xc * xc, axis=-1, keepdims=True) + EPS) * g + b


def _sigmoid(x):
    return 1.0 / (1.0 + jnp.exp(-x))


def _rot(x, cos, sin, rmat):
    return x * cos + jnp.dot(x, rmat, precision=lax.Precision.HIGHEST, preferred_element_type=F32) * sin


def _rms_fwd(h, g, name):
    return _rowcall(lambda x, gg: (_rmsn(x, gg),), [h], [g], [(h.shape[1], BF16)], name=name)[0]


def _rms_bwd(h, d_hn, dh_in, g, name):
    def fn(x, dy, dres, gg):
        _, vjp = jax.vjp(_rmsn, x, gg)
        dx, dg = vjp(dy)
        return dres + dx, dg

    return _rowcall(fn, [h, d_hn, dh_in], [g], [(h.shape[1], F32)], [g.shape], name=name)


def _rope_tables(pos, name):
    inv = np.float32(ROPE_BASE) ** (-(np.arange(0, ROPE, 2, dtype=np.float32) / np.float32(ROPE)))
    inv = jnp.asarray(np.concatenate([inv, inv])[None, :].astype(np.float32))

    def fn(pp, iv):
        ang = pp.astype(F32) * iv
        return jnp.cos(ang), jnp.sin(ang)

    return _rowcall(fn, [pos], [inv], [(ROPE, F32), (ROPE, F32)], name=name)


def _rot_matrix():
    r = np.zeros((ROPE, ROPE), np.float32)
    half = ROPE // 2
    for j in range(half):
        r[j + half, j] = -1.0
        r[j, j + half] = 1.0
    return jnp.asarray(r)


def _prep1_fwd(lat, gq, gkv, name):
    def fn(l, a, b):
        return _rmsn(l[:, :Q_LORA], a), _rmsn(l[:, Q_LORA:Q_LORA + KV_LORA], b)

    return _rowcall(fn, [lat], [gq, gkv], [(Q_LORA, BF16), (KV_LORA, BF16)], name=name)


def _prep1_bwd(lat, d_cq, d_ckv, d_kr, gq, gkv, name):
    def fn(l, dq, dkv, dkr, a, b):
        _, vq = jax.vjp(_rmsn, l[:, :Q_LORA], a)
        _, vkv = jax.vjp(_rmsn, l[:, Q_LORA:Q_LORA + KV_LORA], b)
        dxq, dga = vq(dq)
        dxkv, dgb = vkv(dkv)
        return jnp.concatenate([dxq, dxkv, dkr], axis=1), dga, dgb

    return _rowcall(fn, [lat, d_cq, d_ckv, d_kr], [gq, gkv], [(lat.shape[1], BF16)], [gq.shape, gkv.shape], name=name)


def _qk_fn(qn_raw, qr_raw, kn_raw, kr_raw, gqn, gqr, gkn, gkr, cos, sin, rmat):
    return (_rmsn(qn_raw, gqn), _rot(_rmsn(qr_raw, gqr), cos, sin, rmat),
            _rmsn(kn_raw, gkn), _rot(_rmsn(kr_raw, gkr), cos, sin, rmat))


def _prep2_fwd(q_raw, kv_raw, lat, cos, sin, rmat, gains, name, bm=256):
    H, T, _ = q_raw.shape
    bm = _pick(T, bm, 8)
    kr0 = Q_LORA + KV_LORA

    def body(q_ref, kv_ref, lat_ref, cos_ref, sin_ref, r_ref, gqn, gqr, gkn, gkr, qo, ko, vo):
        qr, kvr = q_ref[...], kv_ref[...]
        qn, qro, kn, kro = _qk_fn(qr[:, :NOPE], qr[:, NOPE:], kvr[:, :NOPE], lat_ref[:, kr0:kr0 + ROPE],
                                  gqn[...], gqr[...], gkn[...], gkr[...], cos_ref[...], sin_ref[...], r_ref[...])
        qo[:, :NOPE] = qn.astype(BF16)
        qo[:, NOPE:] = qro.astype(BF16)
        ko[:, :NOPE] = kn.astype(BF16)
        ko[:, NOPE:] = kro.astype(BF16)
        vo[...] = kvr[:, NOPE:].astype(BF16)

    hb = lambda c: pl.BlockSpec((None, bm, c), lambda m, h: (h, m, 0))
    rb = lambda c: pl.BlockSpec((bm, c), lambda m, h: (m, 0))
    wb = lambda s: pl.BlockSpec(tuple(s), lambda m, h: (0, 0))
    return pl.pallas_call(
        body, name=name, grid=(T // bm, H),
        in_specs=[hb(QK), hb(NOPE + VDIM), rb(lat.shape[1]), rb(ROPE), rb(ROPE), wb(rmat.shape)]
        + [wb(g.shape) for g in gains],
        out_specs=[hb(QK), hb(QK), hb(VDIM)],
        out_shape=[jax.ShapeDtypeStruct((H, T, QK), BF16), jax.ShapeDtypeStruct((H, T, QK), BF16),
                   jax.ShapeDtypeStruct((H, T, VDIM), BF16)],
        compiler_params=_params(("parallel", "parallel")),
    )(q_raw, kv_raw, lat, cos, sin, rmat, *gains)


def _prep2_bwd(q_raw, kv_raw, lat, cos, sin, rmat, gains, dq, dk, dv, name, bm=256):
    H, T, _ = q_raw.shape
    bm = _pick(T, bm, 8)
    kr0 = Q_LORA + KV_LORA

    def body(q_ref, kv_ref, lat_ref, cos_ref, sin_ref, r_ref, gqn, gqr, gkn, gkr, dq_ref, dk_ref, dv_ref,
             dqo, dkvo, dkro, o_gqn, o_gqr, o_gkn, o_gkr):
        m, h = pl.program_id(0), pl.program_id(1)
        qr, kvr = q_ref[...], kv_ref[...]
        cos_v, sin_v, r_v = cos_ref[...], sin_ref[...], r_ref[...]
        f = lambda a, b, c, d, g1, g2, g3, g4: _qk_fn(a, b, c, d, g1, g2, g3, g4, cos_v, sin_v, r_v)
        _, vjp = jax.vjp(f, qr[:, :NOPE], qr[:, NOPE:], kvr[:, :NOPE], lat_ref[:, kr0:kr0 + ROPE],
                         gqn[...], gqr[...], gkn[...], gkr[...])
        dqv, dkv_ = dq_ref[...], dk_ref[...]
        d_qn, d_qr, d_kn, d_kr, g1, g2, g3, g4 = vjp((dqv[:, :NOPE], dqv[:, NOPE:], dkv_[:, :NOPE], dkv_[:, NOPE:]))
        dqo[:, :NOPE] = d_qn.astype(BF16)
        dqo[:, NOPE:] = d_qr.astype(BF16)
        dkvo[:, :NOPE] = d_kn.astype(BF16)
        dkvo[:, NOPE:] = dv_ref[...].astype(BF16)

        @pl.when(h == 0)
        def _():
            dkro[...] = jnp.zeros_like(dkro)

        dkro[...] += d_kr

        @pl.when((h == 0) & (m == 0))
        def _():
            for o in (o_gqn, o_gqr, o_gkn, o_gkr):
                o[...] = jnp.zeros_like(o)

        for o, g in zip((o_gqn, o_gqr, o_gkn, o_gkr), (g1, g2, g3, g4)):
            o[...] += g

    hb = lambda c: pl.BlockSpec((None, bm, c), lambda m, h: (h, m, 0))
    rb = lambda c: pl.BlockSpec((bm, c), lambda m, h: (m, 0))
    wb = lambda s: pl.BlockSpec(tuple(s), lambda m, h: (0, 0))
    return pl.pallas_call(
        body, name=name, grid=(T // bm, H),
        in_specs=[hb(QK), hb(NOPE + VDIM), rb(lat.shape[1]), rb(ROPE), rb(ROPE), wb(rmat.shape)]
        + [wb(g.shape) for g in gains] + [hb(QK), hb(QK), hb(VDIM)],
        out_specs=[hb(QK), hb(NOPE + VDIM), rb(ROPE)] + [wb(g.shape) for g in gains],
        out_shape=[jax.ShapeDtypeStruct((H, T, QK), BF16), jax.ShapeDtypeStruct((H, T, NOPE + VDIM), BF16),
                   jax.ShapeDtypeStruct((T, ROPE), F32)] + [jax.ShapeDtypeStruct(g.shape, F32) for g in gains],
        compiler_params=_params(("arbitrary", "arbitrary")),
    )(q_raw, kv_raw, lat, cos, sin, rmat, *gains, dq, dk, dv)


_NT = (((1,), (1,)), ((), ()))
_TN = (((0,), (0,)), ((), ()))


def _attn_fwd(q, k, v, seq, name, blk=256):
    H, T, _ = q.shape
    nb = T // seq
    blk = _pick(seq, blk)
    nq = seq // blk
    scale = float(QK) ** -0.5

    def body(q_ref, k_ref, v_ref, o_ref, lse_ref):
        qi = pl.program_id(2)
        qb = q_ref[...]
        rows = qi * blk + lax.broadcasted_iota(jnp.int32, (blk, blk), 0)

        def step(j, carry):
            m, l, acc = carry
            ks = pl.ds(pl.multiple_of(j * blk, blk), blk)
            s = lax.dot_general(qb, k_ref[ks, :], _NT, preferred_element_type=F32) * scale
            cols = j * blk + lax.broadcasted_iota(jnp.int32, (blk, blk), 1)
            s = jnp.where(cols <= rows, s, -jnp.inf)
            m_new = jnp.maximum(m, jnp.max(s, axis=1, keepdims=True))
            pr = jnp.exp(s - m_new)
            alpha = jnp.exp(m - m_new)
            l = alpha * l + jnp.sum(pr, axis=1, keepdims=True)
            acc = alpha * acc + jnp.dot(pr.astype(BF16), v_ref[ks, :], preferred_element_type=F32)
            return m_new, l, acc

        init = (jnp.full((blk, 1), -jnp.inf, F32), jnp.zeros((blk, 1), F32), jnp.zeros((blk, VDIM), F32))
        m, l, acc = lax.fori_loop(0, qi + 1, step, init)
        o_ref[...] = (acc / l).astype(o_ref.dtype)
        lse_ref[...] = m + jnp.log(l)

    return pl.pallas_call(
        body, name=name, grid=(H, nb, nq),
        in_specs=[pl.BlockSpec((None, blk, QK), lambda h, b, i: (h, b * nq + i, 0)),
                  pl.BlockSpec((None, seq, QK), lambda h, b, i: (h, b, 0)),
                  pl.BlockSpec((None, seq, VDIM), lambda h, b, i: (h, b, 0))],
        out_specs=[pl.BlockSpec((blk, VDIM), lambda h, b, i: (b * nq + i, h)),
                   pl.BlockSpec((None, blk, 1), lambda h, b, i: (h, b * nq + i, 0))],
        out_shape=[jax.ShapeDtypeStruct((T, H * VDIM), BF16), jax.ShapeDtypeStruct((H, T, 1), F32)],
        compiler_params=_params(("parallel", "parallel", "parallel")),
    )(q, k, v)


def _attn_bwd(q, k, v, o, do, lse, seq, name, blk=256):
    H, T, _ = q.shape
    nb = T // seq
    blk = _pick(seq, blk)
    nq = seq // blk
    scale = float(QK) ** -0.5

    def body(q_ref, k_ref, v_ref, o_ref, do_ref, lse_ref, dq_ref, dk_ref, dv_ref):
        dk_ref[...] = jnp.zeros_like(dk_ref)
        dv_ref[...] = jnp.zeros_like(dv_ref)

        def qloop(i, carry):
            qs = pl.ds(pl.multiple_of(i * blk, blk), blk)
            qb = q_ref[qs, :]
            dof = do_ref[qs, :]
            dob = dof.astype(BF16)
            lse_b = lse_ref[qs, :]
            delta = jnp.sum(dof * o_ref[qs, :].astype(F32), axis=1, keepdims=True)
            rows = i * blk + lax.broadcasted_iota(jnp.int32, (blk, blk), 0)

            def kloop(j, dq_acc):
                ks = pl.ds(pl.multiple_of(j * blk, blk), blk)
                kb = k_ref[ks, :]
                vb = v_ref[ks, :]
                s = lax.dot_general(qb, kb, _NT, preferred_element_type=F32) * scale
                cols = j * blk + lax.broadcasted_iota(jnp.int32, (blk, blk), 1)
                pr = jnp.where(cols <= rows, jnp.exp(s - lse_b), 0.0)
                dp = lax.dot_general(dob, vb, _NT, preferred_element_type=F32)
                ds = (pr * (dp - delta) * scale).astype(BF16)
                prb = pr.astype(BF16)
                dv_ref[ks, :] += lax.dot_general(prb, dob, _TN, preferred_element_type=F32)
                dk_ref[ks, :] += lax.dot_general(ds, qb, _TN, preferred_element_type=F32)
                return dq_acc + jnp.dot(ds, kb, preferred_element_type=F32)

            dq_ref[qs, :] = lax.fori_loop(0, i + 1, kloop, jnp.zeros((blk, QK), F32))
            return carry

        lax.fori_loop(0, nq, qloop, 0)

    hb = lambda c: pl.BlockSpec((None, seq, c), lambda h, b: (h, b, 0))
    cb = lambda: pl.BlockSpec((seq, VDIM), lambda h, b: (b, h))
    return pl.pallas_call(
        body, name=name, grid=(H, nb),
        in_specs=[hb(QK), hb(QK), hb(VDIM), cb(), cb(), hb(1)],
        out_specs=[hb(QK), hb(QK), hb(VDIM)],
        out_shape=[jax.ShapeDtypeStruct((H, T, QK), F32), jax.ShapeDtypeStruct((H, T, QK), F32),
                   jax.ShapeDtypeStruct((H, T, VDIM), F32)],
        compiler_params=_params(("parallel", "parallel")),
    )(q, k, v, o, do, lse)


def _gelu_ln_fwd(z, g, b, name):
    half = z.shape[1] // 2

    def fn(zz, gg, bb):
        return _gelu(zz[:, :half]), _layer_norm(_gelu(zz[:, half:]), gg, bb)

    return _rowcall(fn, [z], [g, b], [(half, F32), (half, BF16)], name=name)


def _gelu_ln_bwd(z, d_u, d_vn, g, b, name):
    half = z.shape[1] // 2

    def fn(zz, du, dvn, gg, bb):
        _, vu = jax.vjp(_gelu, zz[:, :half])
        _, vv = jax.vjp(lambda t, a, c: _layer_norm(_gelu(t), a, c), zz[:, half:], gg, bb)
        (dzu,) = vu(du)
        dzv, dg, db = vv(dvn)
        return jnp.concatenate([dzu, dzv], axis=1), dg, db

    return _rowcall(fn, [z, d_u, d_vn], [g, b], [(z.shape[1], BF16)], [g.shape, b.shape], bm=128, name=name)


def _tril_bf16(ws):
    t = lax.broadcasted_iota(jnp.int32, ws.shape, 0)
    s = lax.broadcasted_iota(jnp.int32, ws.shape, 1)
    return jnp.where(s <= t, ws, 0.0).astype(BF16)


def _sgu_fwd(u, vn, ws, bs, name, bm=512):
    T, half = u.shape
    gd = half // GROUPS
    bm = _pick(T, bm, CHUNK)
    nc = bm // CHUNK

    def body(u_ref, vn_ref, ws_ref, bs_ref, y_ref):
        wm = _tril_bf16(ws_ref[...])
        bias = bs_ref[...]
        for c in range(nc):
            rs = slice(c * CHUNK, (c + 1) * CHUNK)
            sv = jnp.dot(wm, vn_ref[rs, :], preferred_element_type=F32) + bias
            y_ref[rs, :] = (u_ref[rs, :] * sv).astype(y_ref.dtype)

    tb = lambda: pl.BlockSpec((bm, gd), lambda g, i: (i, g))
    return pl.pallas_call(
        body, name=name, grid=(GROUPS, T // bm),
        in_specs=[tb(), tb(), pl.BlockSpec((None, CHUNK, CHUNK), lambda g, i: (g, 0, 0)),
                  pl.BlockSpec((None, CHUNK, 1), lambda g, i: (g, 0, 0))],
        out_specs=tb(),
        out_shape=jax.ShapeDtypeStruct((T, half), BF16),
        compiler_params=_params(("parallel", "parallel")),
    )(u, vn, ws, bs)


def _sgu_bwd(u, vn, dy, ws, bs, name, bm=512):
    T, half = u.shape
    gd = half // GROUPS
    bm = _pick(T, bm, CHUNK)
    nc = bm // CHUNK

    def body(u_ref, vn_ref, dy_ref, ws_ref, bs_ref, du_ref, dvn_ref, dws_ref, dbs_ref):
        @pl.when(pl.program_id(1) == 0)
        def _():
            dws_ref[...] = jnp.zeros_like(dws_ref)
            dbs_ref[...] = jnp.zeros_like(dbs_ref)

        wm = _tril_bf16(ws_ref[...])
        bias = bs_ref[...]
        dws = jnp.zeros((CHUNK, CHUNK), F32)
        dbs = jnp.zeros((CHUNK, 1), F32)
        for c in range(nc):
            rs = slice(c * CHUNK, (c + 1) * CHUNK)
            vb = vn_ref[rs, :]
            dyb = dy_ref[rs, :]
            sv = jnp.dot(wm, vb, preferred_element_type=F32) + bias
            du_ref[rs, :] = dyb * sv
            dsv = dyb * u_ref[rs, :]
            dsb = dsv.astype(BF16)
            dvn_ref[rs, :] = lax.dot_general(wm, dsb, _TN, preferred_element_type=F32)
            dws = dws + lax.dot_general(dsb, vb, _NT, preferred_element_type=F32)
            dbs = dbs + jnp.sum(dsv, axis=1, keepdims=True)
        t = lax.broadcasted_iota(jnp.int32, (CHUNK, CHUNK), 0)
        s = lax.broadcasted_iota(jnp.int32, (CHUNK, CHUNK), 1)
        dws_ref[...] += jnp.where(s <= t, dws, 0.0)
        dbs_ref[...] += dbs

    tb = lambda: pl.BlockSpec((bm, gd), lambda g, i: (i, g))
    wsb = lambda: pl.BlockSpec((None, CHUNK, CHUNK), lambda g, i: (g, 0, 0))
    bsb = lambda: pl.BlockSpec((None, CHUNK, 1), lambda g, i: (g, 0, 0))
    return pl.pallas_call(
        body, name=name, grid=(GROUPS, T // bm),
        in_specs=[tb(), tb(), tb(), wsb(), bsb()],
        out_specs=[tb(), tb(), wsb(), bsb()],
        out_shape=[jax.ShapeDtypeStruct((T, half), F32), jax.ShapeDtypeStruct((T, half), F32),
                   jax.ShapeDtypeStruct(ws.shape, F32), jax.ShapeDtypeStruct(bs.shape, F32)],
        compiler_params=_params(("parallel", "arbitrary")),
    )(u, vn, dy, ws, bs)


def _loss_head(y, t, name):
    d_model = y.shape[1]

    def fn(yy, tt):
        d = yy - tt
        part = 0.5 * jnp.sum(jnp.mean(d * d, axis=-1, keepdims=True), axis=0, keepdims=True)
        return d / d_model, jnp.zeros((1, LANE), F32) + part

    dy, part = _rowcall(fn, [y, t], [], [(d_model, F32)], [(1, LANE)], name=name)
    return dy, part[0, 0]


def _adamw(parts, w, m, v, prev, layer, name):
    L, R, C = w.shape
    br = _pick(R, max(8, (128 * 1024) // C // 8 * 8), 8)
    c1 = 1.0 - B1 ** STEP
    c2 = 1.0 - B2 ** STEP
    if prev is None:
        prev = [lax.empty(w.shape, F32) for _ in range(4)]

    def body(p_ref, w_ref, m_ref, v_ref, a0, a1, a2, a3, g_o, d_o, m_o, v_o):
        g = p_ref[0].astype(F32)
        for d in range(1, N_DEV):
            g = g + p_ref[d].astype(F32)
        mn = B1 * m_ref[...] + (1.0 - B1) * g
        vn = B2 * v_ref[...] + (1.0 - B2) * (g * g)
        g_o[...] = g
        m_o[...] = mn
        v_o[...] = vn
        d_o[...] = -LR * ((mn / c1) / (jnp.sqrt(vn / c2) + ADAM_EPS) + WD * w_ref[...])

    blk = lambda: pl.BlockSpec((None, br, C), lambda i: (layer, i, 0))
    anywhere = pl.BlockSpec(memory_space=pl.ANY)
    return pl.pallas_call(
        body, name=name, grid=(R // br,),
        in_specs=[pl.BlockSpec((N_DEV, br, C), lambda i: (0, i, 0)), blk(), blk(), blk()] + [anywhere] * 4,
        out_specs=[blk(), blk(), blk(), blk()],
        out_shape=[jax.ShapeDtypeStruct((L, R, C), F32)] * 4,
        input_output_aliases={4: 0, 5: 1, 6: 2, 7: 3},
        compiler_params=_params(("parallel",)),
    )(parts, w, m, v, *prev)


def _mesh_pos():
    return lax.axis_index("x"), lax.axis_index("y"), lax.axis_index("c")


def _flip(pos, k):
    x, y, c = pos
    px = 1 - x if k & 4 else x
    py = 1 - y if k & 2 else y
    pc = 1 - c if k & 1 else c
    return px, py, pc


def _exchange(arrs, scatter, name):
    n = len(arrs)

    def body(*refs):
        ins, outs = refs[:n], refs[n:2 * n]
        send, recv, loc = refs[2 * n:]
        pos = _mesh_pos()
        me = 4 * pos[0] + 2 * pos[1] + pos[2]
        copies = []
        for a in range(n):
            src = ins[a].at[me] if scatter else ins[a]
            cp = pltpu.make_async_copy(src, outs[a].at[me], loc.at[a])
            cp.start()
            copies.append(cp)
        for k in range(1, N_DEV):
            peer = _flip(pos, k)
            peer_id = 4 * peer[0] + 2 * peer[1] + peer[2]
            for a in range(n):
                src = ins[a].at[peer_id] if scatter else ins[a]
                cp = pltpu.make_async_remote_copy(
                    src_ref=src, dst_ref=outs[a].at[me], send_sem=send.at[a, k - 1], recv_sem=recv.at[a, k - 1],
                    device_id=peer, device_id_type=pl.DeviceIdType.MESH)
                cp.start()
                copies.append(cp)
        for cp in copies:
            cp.wait()

    out_shape = [jax.ShapeDtypeStruct(a.shape if scatter else (N_DEV,) + a.shape, a.dtype) for a in arrs]
    return pl.pallas_call(
        body, name=name,
        in_specs=[pl.BlockSpec(memory_space=pl.ANY)] * n,
        out_specs=[pl.BlockSpec(memory_space=pl.ANY)] * n,
        out_shape=out_shape,
        scratch_shapes=[pltpu.SemaphoreType.DMA((n, N_DEV - 1)), pltpu.SemaphoreType.DMA((n, N_DEV - 1)),
                        pltpu.SemaphoreType.DMA((n,))],
        compiler_params=pltpu.CompilerParams(has_side_effects=True),
    )(*arrs)


HBM_SPEC = pl.BlockSpec(memory_space=pltpu.HBM)
SEM_SPEC = pl.BlockSpec(memory_space=pltpu.SEMAPHORE)
EFFECT = pltpu.SideEffectType.DATAFLOW_SIDE_EFFECTING


def _hbm(a):
    return pltpu.with_memory_space_constraint(a, pltpu.HBM)


def _device_index():
    x, y, c = _mesh_pos()
    return 4 * x + 2 * y + c


def _peer_copy(src, land, send, recv, a, k, pos, scatter):
    peer = _flip(pos, k)
    me = 4 * pos[0] + 2 * pos[1] + pos[2]
    piece = src.at[4 * peer[0] + 2 * peer[1] + peer[2]] if scatter else src
    return pltpu.make_async_remote_copy(
        src_ref=piece, dst_ref=land.at[me], send_sem=send.at[7 * a + k - 1], recv_sem=recv.at[7 * a + k - 1],
        device_id=peer, device_id_type=pl.DeviceIdType.MESH)


def _xchg_start(srcs, scatter, after, name):
    n = len(srcs)

    def body(*refs):
        src, land = refs[:n], refs[n:2 * n]
        send, recv, token = refs[2 * n + 1], refs[2 * n + 2], refs[-1]
        pos = _mesh_pos()
        for k in range(1, N_DEV):
            for a in range(n):
                _peer_copy(src[a], land[a], send, recv, a, k, pos, scatter).start()
        token[...] = jnp.zeros_like(token)

    lands = [lax.empty(s.shape if scatter else (N_DEV,) + s.shape, s.dtype) for s in srcs]
    outs = pl.pallas_call(
        body, name=name,
        out_shape=(pltpu.SemaphoreType.DMA((7 * n,)), pltpu.SemaphoreType.DMA((7 * n,)),
                   *[pltpu.HBM(s.shape, s.dtype) for s in srcs], *[pltpu.HBM(l.shape, l.dtype) for l in lands],
                   jax.ShapeDtypeStruct((8, LANE), F32)),
        in_specs=[HBM_SPEC] * (2 * n) + [pl.BlockSpec(memory_space=pl.ANY)],
        out_specs=(SEM_SPEC, SEM_SPEC, *[HBM_SPEC] * (2 * n), pl.BlockSpec(memory_space=pltpu.VMEM)),
        input_output_aliases={q: 2 + q for q in range(2 * n)},
        compiler_params=pltpu.CompilerParams(has_side_effects=EFFECT),
    )(*[_hbm(s) for s in srcs], *[_hbm(l) for l in lands], after)
    handle = dict(send=outs[0], recv=outs[1], srcs=list(outs[2:2 + n]), lands=list(outs[2 + n:2 + 2 * n]),
                  scatter=scatter)
    return handle, outs[-1]


def _xchg_wait(handle, after, name):
    srcs, lands, scatter = handle['srcs'], handle['lands'], handle['scatter']
    n = len(srcs)

    def body(*refs):
        src, land = refs[:n], refs[n:2 * n]
        send, recv = refs[2 * n], refs[2 * n + 1]
        pos = _mesh_pos()
        for k in range(1, N_DEV):
            for a in range(n):
                cp = _peer_copy(src[a], land[a], send, recv, a, k, pos, scatter)
                cp.wait_send()
                cp.wait_recv()

    outs = pl.pallas_call(
        body, name=name,
        out_shape=[pltpu.HBM(s.shape, s.dtype) for s in srcs] + [pltpu.HBM(l.shape, l.dtype) for l in lands],
        in_specs=[HBM_SPEC] * (2 * n) + [SEM_SPEC, SEM_SPEC, pl.BlockSpec(memory_space=pl.ANY)],
        out_specs=[HBM_SPEC] * (2 * n),
        input_output_aliases={q: q for q in range(2 * n)},
        compiler_params=pltpu.CompilerParams(has_side_effects=EFFECT),
    )(*srcs, *lands, handle['send'], handle['recv'], after)
    me = _device_index()
    full = []
    for src, land in zip(outs[:n], outs[n:]):
        mine = lax.dynamic_index_in_dim(src, me, 0, keepdims=False) if scatter else src
        full.append(lax.dynamic_update_index_in_dim(land, mine, me, 0))
    return full


def _assemble(gathered, axis):
    g = jnp.moveaxis(gathered, 0, axis)
    s = g.shape
    return g.reshape(s[:axis] + (s[axis] * s[axis + 1],) + s[axis + 2:])


def _split(full, axis):
    s = full.shape
    g = full.reshape(s[:axis] + (N_DEV, s[axis] // N_DEV) + s[axis + 1:])
    return jnp.moveaxis(g, axis, 0)


def _heads(a2d):
    T = a2d.shape[0]
    return jnp.transpose(a2d.reshape(T, HEADS, -1), (1, 0, 2))


def _unheads(a3d):
    H, T, c = a3d.shape
    return jnp.transpose(a3d, (1, 0, 2)).reshape(T, H * c)


def _pack(parts):
    flat = jnp.concatenate([q.reshape(-1) for q in parts])
    pad = (-flat.shape[0]) % (8 * LANE)
    return jnp.pad(flat, (0, pad)).reshape(-1, LANE)


def _unpack(packed, like):
    flat = packed.reshape(-1)
    out, o = [], 0
    for q in like:
        out.append(flat[o:o + q.size].reshape(q.shape))
        o += q.size
    return out


def kernel(x, p, positions, norm_mix, norm_ffn, norm_ple, mla_w_down, mla_q_lora_g, mla_kv_lora_g, mla_w_uq, mla_w_ukv, mla_q_nope_g, mla_q_rope_g, mla_k_nope_g, mla_k_rope_g, mla_w_out, gmlp_w_in, gmlp_ln_g, gmlp_ln_b, gmlp_w_s, gmlp_b_s, gmlp_w_out, ffn_w_up, ffn_w_down, ple_w_gate, ple_w_proj, loss_target, m_norm_mix, m_norm_ffn, m_norm_ple, m_mla_w_down, m_mla_q_lora_g, m_mla_kv_lora_g, m_mla_w_uq, m_mla_w_ukv, m_mla_q_nope_g, m_mla_q_rope_g, m_mla_k_nope_g, m_mla_k_rope_g, m_mla_w_out, m_gmlp_w_in, m_gmlp_ln_g, m_gmlp_ln_b, m_gmlp_w_s, m_gmlp_b_s, m_gmlp_w_out, m_ffn_w_up, m_ffn_w_down, m_ple_w_gate, m_ple_w_proj, v_norm_mix, v_norm_ffn, v_norm_ple, v_mla_w_down, v_mla_q_lora_g, v_mla_kv_lora_g, v_mla_w_uq, v_mla_w_ukv, v_mla_q_nope_g, v_mla_q_rope_g, v_mla_k_nope_g, v_mla_k_rope_g, v_mla_w_out, v_gmlp_w_in, v_gmlp_ln_g, v_gmlp_ln_b, v_gmlp_w_s, v_gmlp_b_s, v_gmlp_w_out, v_ffn_w_up, v_ffn_w_down, v_ple_w_gate, v_ple_w_proj):
    W = dict(zip(WEIGHTS, (norm_mix, norm_ffn, norm_ple, mla_w_down, mla_q_lora_g, mla_kv_lora_g, mla_w_uq, mla_w_ukv, mla_q_nope_g, mla_q_rope_g, mla_k_nope_g, mla_k_rope_g, mla_w_out, gmlp_w_in, gmlp_ln_g, gmlp_ln_b, gmlp_w_s, gmlp_b_s, gmlp_w_out, ffn_w_up, ffn_w_down, ple_w_gate, ple_w_proj)))
    M1 = dict(zip(WEIGHTS, (m_norm_mix, m_norm_ffn, m_norm_ple, m_mla_w_down, m_mla_q_lora_g, m_mla_kv_lora_g, m_mla_w_uq, m_mla_w_ukv, m_mla_q_nope_g, m_mla_q_rope_g, m_mla_k_nope_g, m_mla_k_rope_g, m_mla_w_out, m_gmlp_w_in, m_gmlp_ln_g, m_gmlp_ln_b, m_gmlp_w_s, m_gmlp_b_s, m_gmlp_w_out, m_ffn_w_up, m_ffn_w_down, m_ple_w_gate, m_ple_w_proj)))
    M2 = dict(zip(WEIGHTS, (v_norm_mix, v_norm_ffn, v_norm_ple, v_mla_w_down, v_mla_q_lora_g, v_mla_kv_lora_g, v_mla_w_uq, v_mla_w_ukv, v_mla_q_nope_g, v_mla_q_rope_g, v_mla_k_nope_g, v_mla_k_rope_g, v_mla_w_out, v_gmlp_w_in, v_gmlp_ln_g, v_gmlp_ln_b, v_gmlp_w_s, v_gmlp_b_s, v_gmlp_w_out, v_ffn_w_up, v_ffn_w_down, v_ple_w_gate, v_ple_w_proj)))

    nb, seq, d_model = x.shape
    T = nb * seq
    depth = norm_mix.shape[0]
    h = x.reshape(T, d_model)
    target = loss_target.reshape(T, d_model)
    p_bf = p.reshape(depth, T, p.shape[-1]).astype(BF16)

    def stage_keys(st):
        i, second = divmod(st, 2)
        if second:
            return [(n, i) for n in ('ffn_w_up', 'ffn_w_down', 'ple_w_gate', 'ple_w_proj')]
        mix = (['mla_w_down', 'mla_w_uq', 'mla_w_ukv', 'mla_w_out'] if i % 2 == 0 else
               ['gmlp_w_in', 'gmlp_ln_g', 'gmlp_ln_b', 'gmlp_w_out'])
        return [(n, i // 2) for n in mix]

    FW = {n: {} for n in SHARDED}

    def start_weights(st, after):
        keys = stage_keys(st)
        srcs = [W[n][l] if n in F32_PAYLOAD else W[n][l].astype(BF16) for n, l in keys]
        handle, token = _xchg_start(srcs, False, after, "weights_start%d" % st)
        return (keys, handle), token[0, 0]

    def wait_weights(pending, st, after):
        keys, handle = pending
        for (n, l), full in zip(keys, _xchg_wait(handle, after, "weights_wait%d" % st)):
            FW[n][l] = full.reshape((-1,) + full.shape[2:]) if SHARD_AXIS[n] == 1 else full

    row = lambda a: a.reshape(1, -1)
    cos, sin = _rope_tables(positions.reshape(T, 1), "rope_tables")
    rmat = _rot_matrix()

    saved = []
    pending, _ = start_weights(0, h)
    for i in range(depth):
        j = i // 2
        wait_weights(pending, 2 * i, h)
        pending, token = start_weights(2 * i + 1, h)
        s = {}
        s['h0'] = h
        hn = _rms_fwd(h, row(W['norm_mix'][i]) + token, "rms_fwd")
        s['hn'] = hn
        if i % 2 == 0:
            gains = [row(W['mla_q_nope_g'][j]), row(W['mla_q_rope_g'][j]), row(W['mla_k_nope_g'][j]),
                     row(W['mla_k_rope_g'][j])]
            lat = _mm(hn, FW['mla_w_down'][j], name="mla_down")
            cq, ckv = _prep1_fwd(lat, row(W['mla_q_lora_g'][j]), row(W['mla_kv_lora_g'][j]), "mla_prep1")
            q_raw = _mm(cq, FW['mla_w_uq'][j], out_blocks=HEADS, name="mla_uq")
            kv_raw = _mm(ckv, FW['mla_w_ukv'][j], out_blocks=HEADS, name="mla_ukv")
            q, k, v = _prep2_fwd(q_raw, kv_raw, lat, cos, sin, rmat, gains, "mla_prep2")
            o, lse = _attn_fwd(q, k, v, seq, "attn_fwd")
            h = _mm(o, FW['mla_w_out'][j], extras=(h,), epilogue=lambda acc, res: (res + acc,), name="mla_out")
            s.update(lat=lat, cq=cq, ckv=ckv, q_raw=q_raw, kv_raw=kv_raw, q=q, k=k, v=v, o=o, lse=lse, gains=gains)
        else:
            z = _mm(hn, FW['gmlp_w_in'][j], name="gmlp_in")
            u, vn = _gelu_ln_fwd(z, row(FW['gmlp_ln_g'][j]), row(FW['gmlp_ln_b'][j]), "gmlp_gelu_ln")
            bs3 = W['gmlp_b_s'][j][:, :, None]
            y = _sgu_fwd(u, vn, W['gmlp_w_s'][j], bs3, "gmlp_sgu")
            h = _mm(y, FW['gmlp_w_out'][j], extras=(h,), epilogue=lambda acc, res: (res + acc,), name="gmlp_out")
            s.update(z=z, u=u, vn=vn, y=y, bs3=bs3)
        s['h1'] = h
        wait_weights(pending, 2 * i + 1, h)
        token = 0.0
        if i + 1 < depth:
            pending, token = start_weights(2 * i + 2, h)
        hn2 = _rms_fwd(h, row(W['norm_ffn'][i]) + token, "rms_fwd")
        a, r = _mm(hn2, FW['ffn_w_up'][i], epilogue=lambda acc: (acc, jnp.square(jnp.maximum(acc, 0.0))),
                   out_dtypes=(F32, BF16), name="ffn_up")
        h = _mm(r, FW['ffn_w_down'][i], extras=(h,), epilogue=lambda acc, res: (res + acc,), name="ffn_down")
        s.update(hn2=hn2, a=a, r=r, h2=h)
        hn3 = _rms_fwd(h, row(W['norm_ple'][i]), "rms_fwd")
        gt = _mm(hn3, FW['ple_w_gate'][i], name="ple_gate")
        pp, h = _mm(p_bf[i], FW['ple_w_proj'][i], extras=(gt, h),
                    epilogue=lambda acc, g_, res: (acc, res + _sigmoid(g_) * acc), out_dtypes=(F32, F32),
                    name="ple_proj")
        s.update(hn3=hn3, gt=gt, pp=pp)
        saved.append(s)

    dh, loss_part = _loss_head(h, target, "loss_head")
    loss = lax.psum(loss_part, MESH_AXES)

    G = {n: [None] * W[n].shape[0] for n in REPLICATED}
    res = {}
    flying = []

    def shard3(n):
        shp = W[n].shape
        return shp[0], int(np.prod(shp[1:-1])), shp[-1]

    def by_owner(g):
        return g.reshape((N_DEV, g.shape[0] // N_DEV) + g.shape[1:])

    def send_grads(tag, grads):
        handle, token = _xchg_start([g for _, g in grads], True, grads[-1][1], "grads_start_" + tag)
        flying.append((tag, [key for key, _ in grads], handle))
        return token[0, 0]

    def land_grads(after):
        tag, keys, handle = flying.pop(0)
        for (n, l), full in zip(keys, _xchg_wait(handle, after, "grads_wait_" + tag)):
            dims = shard3(n)
            res[n] = _adamw(full.reshape((N_DEV,) + dims[1:]), W[n].reshape(dims), M1[n].reshape(dims),
                            M2[n].reshape(dims), res.get(n), l, "adamw_" + n)

    def start_small(names, tag, after):
        handle, _ = _xchg_start([_pack([jnp.stack(G[n]) for n in names])], False, after, "small_start_" + tag)
        return names, handle

    def land_small(pending_small, tag, after):
        names, handle = pending_small
        (parts,) = _xchg_wait(handle, after, "small_wait_" + tag)
        like = [W[n] for n in names]
        outs = _adamw(parts, _pack(like)[None], _pack([M1[n] for n in names])[None],
                      _pack([M2[n] for n in names])[None], None, 0, "adamw_small_" + tag)
        unpacked = [_unpack(o, like) for o in outs]
        for idx, n in enumerate(names):
            res[n] = [unpacked[q][idx] for q in range(4)]

    spatial = ['gmlp_w_s', 'gmlp_b_s']
    token = 0.0
    for i in reversed(range(depth)):
        j = i // 2
        s = saved[i]
        def ple_elem(d, g_, pq):
            sg = _sigmoid(g_)
            return d * sg, d * pq * sg * (1.0 - sg)

        d_pp, d_gt = _rowcall(ple_elem, [dh, s['gt'], s['pp']], [], [(d_model, BF16), (d_model, BF16)], name="ple_bwd")
        g_proj = _mm(p_bf[i], d_pp, ta=True, out_blocks=N_DEV, out_dtypes=(BF16,), name="ple_proj_dw")
        g_gate = _mm(s['hn3'], d_gt, ta=True, out_dtypes=(BF16,), name="ple_gate_dw")
        d_hn3 = _mm(d_gt, FW['ple_w_gate'][i], tb=True, name="ple_gate_dx")
        dh, dg = _rms_bwd(s['h2'], d_hn3, dh, row(W['norm_ple'][i]) + token, "rms_bwd")
        G['norm_ple'][i] = dg[0]
        d_a = _mm(dh, FW['ffn_w_down'][i], tb=True, extras=(s['a'],),
                  epilogue=lambda acc, a_: (acc * (2.0 * jnp.maximum(a_, 0.0)),), out_dtypes=(BF16,), name="ffn_down_dx")
        g_down = _mm(s['r'], dh, ta=True, out_dtypes=(BF16,), name="ffn_down_dw")
        g_up = _mm(s['hn2'], d_a, ta=True, out_blocks=N_DEV, out_dtypes=(BF16,), name="ffn_up_dw")
        token = send_grads("mlp%d" % i, [(('ple_w_proj', i), g_proj), (('ple_w_gate', i), by_owner(g_gate)),
                                         (('ffn_w_down', i), by_owner(g_down)), (('ffn_w_up', i), g_up)])
        if len(flying) > 1:
            land_grads(g_up)
        d_hn2 = _mm(d_a, FW['ffn_w_up'][i], tb=True, name="ffn_up_dx")
        dh, dg = _rms_bwd(s['h1'], d_hn2, dh, row(W['norm_ffn'][i]) + token, "rms_bwd")
        G['norm_ffn'][i] = dg[0]
        if i % 2 == 0:
            d_o = _mm(dh, FW['mla_w_out'][j], tb=True, name="mla_out_dx")
            g_out = _mm(s['o'], dh, ta=True, out_dtypes=(BF16,), name="mla_out_dw")
            dq, dk, dv = _attn_bwd(s['q'], s['k'], s['v'], s['o'], d_o, s['lse'], seq, "attn_bwd")
            d_q_raw, d_kv_raw, d_kr, g1, g2, g3, g4 = _prep2_bwd(
                s['q_raw'], s['kv_raw'], s['lat'], cos, sin, rmat, s['gains'], dq, dk, dv, "mla_prep2_bwd")
            G['mla_q_nope_g'][j], G['mla_q_rope_g'][j] = g1[0], g2[0]
            G['mla_k_nope_g'][j], G['mla_k_rope_g'][j] = g3[0], g4[0]
            g_uq = _mm(s['cq'], d_q_raw, ta=True, out_blocks=N_DEV, out_dtypes=(BF16,), name="mla_uq_dw")
            g_ukv = _mm(s['ckv'], d_kv_raw, ta=True, out_blocks=N_DEV, out_dtypes=(BF16,), name="mla_ukv_dw")
            d_cq = _mm(d_q_raw, FW['mla_w_uq'][j], tb=True, name="mla_uq_dx")
            d_ckv = _mm(d_kv_raw, FW['mla_w_ukv'][j], tb=True, name="mla_ukv_dx")
            d_lat, dga, dgb = _prep1_bwd(s['lat'], d_cq, d_ckv, d_kr, row(W['mla_q_lora_g'][j]),
                                         row(W['mla_kv_lora_g'][j]), "mla_prep1_bwd")
            G['mla_q_lora_g'][j], G['mla_kv_lora_g'][j] = dga[0], dgb[0]
            g_down = _mm(s['hn'], d_lat, ta=True, out_dtypes=(BF16,), name="mla_down_dw")
            grads = [(('mla_w_out', j), by_owner(g_out)), (('mla_w_uq', j), g_uq), (('mla_w_ukv', j), g_ukv),
                     (('mla_w_down', j), by_owner(g_down))]
            d_hn = _mm(d_lat, FW['mla_w_down'][j], tb=True, name="mla_down_dx")
        else:
            d_y = _mm(dh, FW['gmlp_w_out'][j], tb=True, name="gmlp_out_dx")
            g_out = _mm(s['y'], dh, ta=True, out_dtypes=(BF16,), name="gmlp_out_dw")
            d_u, d_vn, d_ws, d_bs = _sgu_bwd(s['u'], s['vn'], d_y, W['gmlp_w_s'][j], s['bs3'], "gmlp_sgu_bwd")
            G['gmlp_w_s'][j], G['gmlp_b_s'][j] = d_ws, d_bs[:, :, 0]
            d_z, d_lg, d_lb = _gelu_ln_bwd(s['z'], d_u, d_vn, row(FW['gmlp_ln_g'][j]), row(FW['gmlp_ln_b'][j]),
                                           "gmlp_gelu_ln_bwd")
            g_in = _mm(s['hn'], d_z, ta=True, out_blocks=N_DEV, out_dtypes=(BF16,), name="gmlp_in_dw")
            grads = [(('gmlp_w_out', j), by_owner(g_out)), (('gmlp_ln_g', j), by_owner(d_lg[0])),
                     (('gmlp_ln_b', j), by_owner(d_lb[0])), (('gmlp_w_in', j), g_in)]
            d_hn = _mm(d_z, FW['gmlp_w_in'][j], tb=True, name="gmlp_in_dx")
        token = send_grads("mix%d" % i, grads)
        if len(flying) > 1:
            land_grads(grads[-1][1])
        dh, dg = _rms_bwd(s['h0'], d_hn, dh, row(W['norm_mix'][i]) + token, "rms_bwd")
        G['norm_mix'][i] = dg[0]
        if i == 1:
            small_a = start_small(spatial, "spatial", dh)
    grad_x = dh.reshape(x.shape)

    small_b = start_small([n for n in REPLICATED if n not in spatial], "gains", dh)
    while flying:
        land_grads(dh)
    land_small(small_a, "spatial", dh)
    land_small(small_b, "gains", dh)

    out = lambda q: [res[n][q].reshape(W[n].shape) for n in WEIGHTS]
    return (loss, grad_x, *out(0), *out(1), *out(2), *out(3))
```

```python
import math

import numpy as np
import jax
import jax.numpy as jnp
from jax import lax
from jax.experimental import pallas as pl
from jax.experimental.pallas import tpu as pltpu

F32 = jnp.float32
BF16 = jnp.bfloat16

N_DEV = 8
MESH_AXES = ("x", "y", "c")
HEADS = 8
NOPE = 128
ROPE = 64
VDIM = 128
QK = NOPE + ROPE
Q_LORA = 384
KV_LORA = 256
ROPE_BASE = 10000.0
CHUNK = 128
GROUPS = 8
EPS = 1e-6
LR, B1, B2, ADAM_EPS, WD, STEP = 0.001, 0.9, 0.999, 1e-08, 0.01, 10
LANE = 128
VMEM_LIMIT = 48 * 1024 * 1024

WEIGHTS = ['norm_mix', 'norm_ffn', 'norm_ple', 'mla_w_down', 'mla_q_lora_g', 'mla_kv_lora_g', 'mla_w_uq',
           'mla_w_ukv', 'mla_q_nope_g', 'mla_q_rope_g', 'mla_k_nope_g', 'mla_k_rope_g', 'mla_w_out', 'gmlp_w_in',
           'gmlp_ln_g', 'gmlp_ln_b', 'gmlp_w_s', 'gmlp_b_s', 'gmlp_w_out', 'ffn_w_up', 'ffn_w_down', 'ple_w_gate',
           'ple_w_proj']
SHARD_AXIS = {'mla_w_down': 1, 'mla_w_uq': 2, 'mla_w_ukv': 2, 'mla_w_out': 1, 'gmlp_w_in': 2, 'gmlp_ln_g': 1,
              'gmlp_ln_b': 1, 'gmlp_w_out': 1, 'ffn_w_up': 2, 'ffn_w_down': 1, 'ple_w_gate': 1, 'ple_w_proj': 2}
SHARDED = list(SHARD_AXIS)
REPLICATED = [n for n in WEIGHTS if n not in SHARD_AXIS]
F32_PAYLOAD = ('gmlp_ln_g', 'gmlp_ln_b')


def _pick(dim, pref, align=LANE):
    if dim <= pref:
        return dim
    b = (pref // align) * align
    while b >= align:
        if dim % b == 0:
            return b
        b -= align
    return dim


def _params(sem):
    return pltpu.CompilerParams(dimension_semantics=sem, vmem_limit_bytes=VMEM_LIMIT)


def _mm(a, b, *, ta=False, tb=False, extras=(), epilogue=None, out_dtypes=(F32,), out_blocks=None, name,
        bm=1024, bn=1024, bk=1024):
    a3, b3 = a.ndim == 3, b.ndim == 3
    assert not (ta and a3)
    if ta:
        K, M = a.shape
        ka = K
    elif a3:
        M, ka = a.shape[1:]
        K = a.shape[0] * ka
    else:
        M, K = a.shape
        ka = K
    if tb:
        N, kb = b.shape[-2:]
        nb = N
        K2 = b.shape[0] * kb if b3 else kb
    else:
        kb, nb = b.shape[-2:]
        K2 = kb
        N = b.shape[0] * nb if b3 else nb
    assert K == K2, (a.shape, b.shape, ta, tb)
    no_ = N // out_blocks if out_blocks else N
    assert not (out_blocks and extras)
    bm, bn, bk = _pick(M, bm), _pick(min(nb, no_), bn), _pick(min(ka, kb), bk)
    nk = K // bk
    ne, no = len(extras), len(out_dtypes)
    dims = (((0,) if ta else (1,), (1,) if tb else (0,)), ((), ()))

    def finish(r, e_refs, o_refs):
        outs = epilogue(r, *[e[...] for e in e_refs]) if epilogue is not None else (r,)
        for o, v in zip(o_refs, outs):
            o[...] = v.astype(o.dtype)

    def body(*refs):
        a_ref, b_ref = refs[0], refs[1]
        e_refs = refs[2:2 + ne]
        o_refs = refs[2 + ne:2 + ne + no]
        part = lax.dot_general(a_ref[...].astype(BF16), b_ref[...].astype(BF16), dims, preferred_element_type=F32)
        if nk == 1:
            finish(part, e_refs, o_refs)
            return
        acc = refs[-1]
        k = pl.program_id(2)

        @pl.when(k == 0)
        def _():
            acc[...] = part

        @pl.when(k > 0)
        def _():
            acc[...] += part

        @pl.when(k == nk - 1)
        def _():
            finish(acc[...], e_refs, o_refs)

    ka_t, kb_t, nb_t, no_t = ka // bk, kb // bk, nb // bn, no_ // bn
    if ta:
        a_spec = pl.BlockSpec((bk, bm), lambda i, j, k: (k, i))
    elif a3:
        a_spec = pl.BlockSpec((None, bm, bk), lambda i, j, k: (k // ka_t, i, k % ka_t))
    else:
        a_spec = pl.BlockSpec((bm, bk), lambda i, j, k: (i, k))
    if tb:
        b_spec = (pl.BlockSpec((None, bn, bk), lambda i, j, k: (k // kb_t, j, k % kb_t)) if b3 else
                  pl.BlockSpec((bn, bk), lambda i, j, k: (j, k)))
    else:
        b_spec = (pl.BlockSpec((None, bk, bn), lambda i, j, k: (j // nb_t, k, j % nb_t)) if b3 else
                  pl.BlockSpec((bk, bn), lambda i, j, k: (k, j)))
    if out_blocks:
        o_spec = lambda: pl.BlockSpec((None, bm, bn), lambda i, j, k: (j // no_t, i, j % no_t))
        o_shape = (out_blocks, M, no_)
    else:
        o_spec = lambda: pl.BlockSpec((bm, bn), lambda i, j, k: (i, j))
        o_shape = (M, N)
    outs = pl.pallas_call(
        body, name=name,
        grid=(M // bm, N // bn, nk),
        in_specs=[a_spec, b_spec] + [pl.BlockSpec((bm, bn), lambda i, j, k: (i, j)) for _ in extras],
        out_specs=[o_spec() for _ in out_dtypes],
        out_shape=[jax.ShapeDtypeStruct(o_shape, dt) for dt in out_dtypes],
        scratch_shapes=[pltpu.VMEM((bm, bn), F32)] if nk > 1 else [],
        compiler_params=_params(("parallel", "parallel", "arbitrary")),
    )(a, b, *extras)
    return outs[0] if no == 1 else outs


def _rowcall(fn, rows, params, row_outs, acc_outs=(), *, bm=256, name):
    T = rows[0].shape[0]
    bm = _pick(T, bm, 8)
    nr, npar, nro, nao = len(rows), len(params), len(row_outs), len(acc_outs)

    def body(*refs):
        vals = [r[...] for r in refs[:nr + npar]]
        res = fn(*vals)
        ro = refs[nr + npar:nr + npar + nro]
        ao = refs[nr + npar + nro:]
        for r, v in zip(ro, res[:nro]):
            r[...] = v.astype(r.dtype)
        if nao:
            @pl.when(pl.program_id(0) == 0)
            def _():
                for r in ao:
                    r[...] = jnp.zeros_like(r)

            for r, v in zip(ao, res[nro:]):
                r[...] += v

    def whole(shape):
        nd = len(shape)
        return pl.BlockSpec(tuple(shape), lambda i: (0,) * nd)

    outs = pl.pallas_call(
        body, name=name,
        grid=(T // bm,),
        in_specs=[pl.BlockSpec((bm, r.shape[1]), lambda i: (i, 0)) for r in rows] + [whole(q.shape) for q in params],
        out_specs=[pl.BlockSpec((bm, c), lambda i: (i, 0)) for c, _ in row_outs] + [whole(s) for s in acc_outs],
        out_shape=[jax.ShapeDtypeStruct((T, c), dt) for c, dt in row_outs]
        + [jax.ShapeDtypeStruct(tuple(s), F32) for s in acc_outs],
        compiler_params=_params(("arbitrary",) if nao else ("parallel",)),
    )(*rows, *params)
    return outs


def _rmsn(x, g):
    return x * lax.rsqrt(jnp.mean(x * x, axis=-1, keepdims=True) + EPS) * g


def _gelu(x):
    return 0.5 * x * (1.0 + jnp.tanh(math.sqrt(2.0 / math.pi) * (x + 0.044715 * (x * x * x))))


def _layer_norm(x, g, b):
    mu = jnp.mean(x, axis=-1, keepdims=True)
    xc = x - mu
    return xc * lax.rsqrt(jnp.mean(xc * xc, axis=-1, keepdims=True) + EPS) * g + b


def _sigmoid(x):
    return 1.0 / (1.0 + jnp.exp(-x))


def _rot(x, cos, sin, rmat):
    return x * cos + jnp.dot(x, rmat, precision=lax.Precision.HIGHEST, preferred_element_type=F32) * sin


def _rms_fwd(h, g, name):
    return _rowcall(lambda x, gg: (_rmsn(x, gg),), [h], [g], [(h.shape[1], BF16)], bm=512, name=name)[0]


def _rms_bwd(h, d_hn, dh_in, g, name):
    def fn(x, dy, dres, gg):
        _, vjp = jax.vjp(_rmsn, x, gg)
        dx, dg = vjp(dy)
        return dres + dx, dg

    return _rowcall(fn, [h, d_hn, dh_in], [g], [(h.shape[1], F32)], [g.shape], bm=512, name=name)


def _rope_tables(pos, name):
    inv = np.float32(ROPE_BASE) ** (-(np.arange(0, ROPE, 2, dtype=np.float32) / np.float32(ROPE)))
    inv = jnp.asarray(np.concatenate([inv, inv])[None, :].astype(np.float32))

    def fn(pp, iv):
        ang = pp.astype(F32) * iv
        return jnp.cos(ang), jnp.sin(ang)

    return _rowcall(fn, [pos], [inv], [(ROPE, F32), (ROPE, F32)], name=name)


def _rot_matrix():
    r = np.zeros((ROPE, ROPE), np.float32)
    half = ROPE // 2
    for j in range(half):
        r[j + half, j] = -1.0
        r[j, j + half] = 1.0
    return jnp.asarray(r)


def _prep1_fwd(lat, gq, gkv, name):
    def fn(l, a, b):
        return _rmsn(l[:, :Q_LORA], a), _rmsn(l[:, Q_LORA:Q_LORA + KV_LORA], b)

    return _rowcall(fn, [lat], [gq, gkv], [(Q_LORA, BF16), (KV_LORA, BF16)], name=name)


def _prep1_bwd(lat, d_cq, d_ckv, d_kr, gq, gkv, name):
    def fn(l, dq, dkv, dkr, a, b):
        _, vq = jax.vjp(_rmsn, l[:, :Q_LORA], a)
        _, vkv = jax.vjp(_rmsn, l[:, Q_LORA:Q_LORA + KV_LORA], b)
        dxq, dga = vq(dq)
        dxkv, dgb = vkv(dkv)
        return jnp.concatenate([dxq, dxkv, dkr], axis=1), dga, dgb

    return _rowcall(fn, [lat, d_cq, d_ckv, d_kr], [gq, gkv], [(lat.shape[1], BF16)], [gq.shape, gkv.shape], name=name)


def _qk_fn(qn_raw, qr_raw, kn_raw, kr_raw, gqn, gqr, gkn, gkr, cos, sin, rmat):
    return (_rmsn(qn_raw, gqn), _rot(_rmsn(qr_raw, gqr), cos, sin, rmat),
            _rmsn(kn_raw, gkn), _rot(_rmsn(kr_raw, gkr), cos, sin, rmat))


def _prep2_fwd(q_raw, kv_raw, lat, cos, sin, rmat, gains, name, bm=1024):
    H, T, _ = q_raw.shape
    bm = _pick(T, bm, 8)
    kr0 = Q_LORA + KV_LORA

    def body(q_ref, kv_ref, lat_ref, cos_ref, sin_ref, r_ref, gqn, gqr, gkn, gkr, qo, ko, vo):
        qr, kvr = q_ref[...], kv_ref[...]
        qn, qro, kn, kro = _qk_fn(qr[:, :NOPE], qr[:, NOPE:], kvr[:, :NOPE], lat_ref[:, kr0:kr0 + ROPE],
                                  gqn[...], gqr[...], gkn[...], gkr[...], cos_ref[...], sin_ref[...], r_ref[...])
        qo[:, :NOPE] = qn.astype(BF16)
        qo[:, NOPE:] = qro.astype(BF16)
        ko[:, :NOPE] = kn.astype(BF16)
        ko[:, NOPE:] = kro.astype(BF16)
        vo[...] = kvr[:, NOPE:].astype(BF16)

    hb = lambda c: pl.BlockSpec((None, bm, c), lambda m, h: (h, m, 0))
    rb = lambda c: pl.BlockSpec((bm, c), lambda m, h: (m, 0))
    wb = lambda s: pl.BlockSpec(tuple(s), lambda m, h: (0, 0))
    return pl.pallas_call(
        body, name=name, grid=(T // bm, H),
        in_specs=[hb(QK), hb(NOPE + VDIM), rb(lat.shape[1]), rb(ROPE), rb(ROPE), wb(rmat.shape)]
        + [wb(g.shape) for g in gains],
        out_specs=[hb(QK), hb(QK), hb(VDIM)],
        out_shape=[jax.ShapeDtypeStruct((H, T, QK), BF16), jax.ShapeDtypeStruct((H, T, QK), BF16),
                   jax.ShapeDtypeStruct((H, T, VDIM), BF16)],
        compiler_params=_params(("parallel", "parallel")),
    )(q_raw, kv_raw, lat, cos, sin, rmat, *gains)


def _prep2_bwd(q_raw, kv_raw, lat, cos, sin, rmat, gains, dq, dk, dv, name, bm=512):
    H, T, _ = q_raw.shape
    bm = _pick(T, bm, 8)
    kr0 = Q_LORA + KV_LORA

    def body(q_ref, kv_ref, lat_ref, cos_ref, sin_ref, r_ref, gqn, gqr, gkn, gkr, dq_ref, dk_ref, dv_ref,
             dqo, dkvo, dkro, o_gqn, o_gqr, o_gkn, o_gkr):
        m, h = pl.program_id(0), pl.program_id(1)
        qr, kvr = q_ref[...], kv_ref[...]
        cos_v, sin_v, r_v = cos_ref[...], sin_ref[...], r_ref[...]
        f = lambda a, b, c, d, g1, g2, g3, g4: _qk_fn(a, b, c, d, g1, g2, g3, g4, cos_v, sin_v, r_v)
        _, vjp = jax.vjp(f, qr[:, :NOPE], qr[:, NOPE:], kvr[:, :NOPE], lat_ref[:, kr0:kr0 + ROPE],
                         gqn[...], gqr[...], gkn[...], gkr[...])
        dqv, dkv_ = dq_ref[...], dk_ref[...]
        d_qn, d_qr, d_kn, d_kr, g1, g2, g3, g4 = vjp((dqv[:, :NOPE], dqv[:, NOPE:], dkv_[:, :NOPE], dkv_[:, NOPE:]))
        dqo[:, :NOPE] = d_qn.astype(BF16)
        dqo[:, NOPE:] = d_qr.astype(BF16)
        dkvo[:, :NOPE] = d_kn.astype(BF16)
        dkvo[:, NOPE:] = dv_ref[...].astype(BF16)

        @pl.when(h == 0)
        def _():
            dkro[...] = jnp.zeros_like(dkro)

        dkro[...] += d_kr

        @pl.when((h == 0) & (m == 0))
        def _():
            for o in (o_gqn, o_gqr, o_gkn, o_gkr):
                o[...] = jnp.zeros_like(o)

        for o, g in zip((o_gqn, o_gqr, o_gkn, o_gkr), (g1, g2, g3, g4)):
            o[...] += g

    hb = lambda c: pl.BlockSpec((None, bm, c), lambda m, h: (h, m, 0))
    rb = lambda c: pl.BlockSpec((bm, c), lambda m, h: (m, 0))
    wb = lambda s: pl.BlockSpec(tuple(s), lambda m, h: (0, 0))
    return pl.pallas_call(
        body, name=name, grid=(T // bm, H),
        in_specs=[hb(QK), hb(NOPE + VDIM), rb(lat.shape[1]), rb(ROPE), rb(ROPE), wb(rmat.shape)]
        + [wb(g.shape) for g in gains] + [hb(QK), hb(QK), hb(VDIM)],
        out_specs=[hb(QK), hb(NOPE + VDIM), rb(ROPE)] + [wb(g.shape) for g in gains],
        out_shape=[jax.ShapeDtypeStruct((H, T, QK), BF16), jax.ShapeDtypeStruct((H, T, NOPE + VDIM), BF16),
                   jax.ShapeDtypeStruct((T, ROPE), F32)] + [jax.ShapeDtypeStruct(g.shape, F32) for g in gains],
        compiler_params=_params(("arbitrary", "arbitrary")),
    )(q_raw, kv_raw, lat, cos, sin, rmat, *gains, dq, dk, dv)


_NT = (((1,), (1,)), ((), ()))
_TN = (((0,), (0,)), ((), ()))


def _causal(blk):
    return lax.broadcasted_iota(jnp.int32, (blk, blk), 1) <= lax.broadcasted_iota(jnp.int32, (blk, blk), 0)


def _attn_fwd(q, k, v, seq, name, blk=512):
    H, T, _ = q.shape
    nb = T // seq
    blk = _pick(seq, blk)
    nq = seq // blk
    scale = float(QK) ** -0.5

    def body(q_ref, k_ref, v_ref, o_ref, lse_ref):
        qi = pl.program_id(2)
        qb = q_ref[...]

        def step(j, carry, diagonal):
            m, l, acc = carry
            ks = pl.ds(pl.multiple_of(j * blk, blk), blk)
            s = lax.dot_general(qb, k_ref[ks, :], _NT, preferred_element_type=F32) * scale
            if diagonal:
                s = jnp.where(_causal(blk), s, -jnp.inf)
            m_new = jnp.maximum(m, jnp.max(s, axis=1, keepdims=True))
            pr = jnp.exp(s - m_new)
            alpha = jnp.exp(m - m_new)
            l = alpha * l + jnp.sum(pr, axis=1, keepdims=True)
            acc = alpha * acc + jnp.dot(pr.astype(BF16), v_ref[ks, :], preferred_element_type=F32)
            return m_new, l, acc

        init = (jnp.full((blk, 1), -jnp.inf, F32), jnp.zeros((blk, 1), F32), jnp.zeros((blk, VDIM), F32))
        below = lax.fori_loop(0, qi, lambda j, c: step(j, c, False), init)
        m, l, acc = step(qi, below, True)
        o_ref[...] = (acc / l).astype(o_ref.dtype)
        lse_ref[...] = m + jnp.log(l)

    return pl.pallas_call(
        body, name=name, grid=(H, nb, nq),
        in_specs=[pl.BlockSpec((None, blk, QK), lambda h, b, i: (h, b * nq + i, 0)),
                  pl.BlockSpec((None, seq, QK), lambda h, b, i: (h, b, 0)),
                  pl.BlockSpec((None, seq, VDIM), lambda h, b, i: (h, b, 0))],
        out_specs=[pl.BlockSpec((blk, VDIM), lambda h, b, i: (b * nq + i, h)),
                   pl.BlockSpec((None, blk, 1), lambda h, b, i: (h, b * nq + i, 0))],
        out_shape=[jax.ShapeDtypeStruct((T, H * VDIM), BF16), jax.ShapeDtypeStruct((H, T, 1), F32)],
        compiler_params=_params(("parallel", "parallel", "parallel")),
    )(q, k, v)


def _attn_bwd(q, k, v, o, do, lse, seq, name, blk=512):
    H, T, _ = q.shape
    nb = T // seq
    blk = _pick(seq, blk)
    nq = seq // blk
    scale = float(QK) ** -0.5

    def body(q_ref, k_ref, v_ref, o_ref, do_ref, lse_ref, dq_ref, dk_ref, dv_ref):
        dk_ref[...] = jnp.zeros_like(dk_ref)
        dv_ref[...] = jnp.zeros_like(dv_ref)

        def qloop(i, carry):
            qs = pl.ds(pl.multiple_of(i * blk, blk), blk)
            qb = q_ref[qs, :]
            dof = do_ref[qs, :]
            dob = dof.astype(BF16)
            lse_b = lse_ref[qs, :]
            delta = jnp.sum(dof * o_ref[qs, :].astype(F32), axis=1, keepdims=True)

            def kstep(j, dq_acc, diagonal):
                ks = pl.ds(pl.multiple_of(j * blk, blk), blk)
                kb = k_ref[ks, :]
                vb = v_ref[ks, :]
                s = lax.dot_general(qb, kb, _NT, preferred_element_type=F32) * scale
                pr = jnp.exp(s - lse_b)
                if diagonal:
                    pr = jnp.where(_causal(blk), pr, 0.0)
                dp = lax.dot_general(dob, vb, _NT, preferred_element_type=F32)
                ds = (pr * (dp - delta) * scale).astype(BF16)
                prb = pr.astype(BF16)
                dv_ref[ks, :] += lax.dot_general(prb, dob, _TN, preferred_element_type=F32)
                dk_ref[ks, :] += lax.dot_general(ds, qb, _TN, preferred_element_type=F32)
                return dq_acc + jnp.dot(ds, kb, preferred_element_type=F32)

            below = lax.fori_loop(0, i, lambda j, c: kstep(j, c, False), jnp.zeros((blk, QK), F32))
            dq_ref[qs, :] = kstep(i, below, True)
            return carry

        lax.fori_loop(0, nq, qloop, 0)

    hb = lambda c: pl.BlockSpec((None, seq, c), lambda h, b: (h, b, 0))
    cb = lambda: pl.BlockSpec((seq, VDIM), lambda h, b: (b, h))
    return pl.pallas_call(
        body, name=name, grid=(H, nb),
        in_specs=[hb(QK), hb(QK), hb(VDIM), cb(), cb(), hb(1)],
        out_specs=[hb(QK), hb(QK), hb(VDIM)],
        out_shape=[jax.ShapeDtypeStruct((H, T, QK), F32), jax.ShapeDtypeStruct((H, T, QK), F32),
                   jax.ShapeDtypeStruct((H, T, VDIM), F32)],
        compiler_params=_params(("parallel", "parallel")),
    )(q, k, v, o, do, lse)


def _gelu_ln_fwd(z, g, b, name):
    half = z.shape[1] // 2

    def fn(zz, gg, bb):
        return _gelu(zz[:, :half]), _layer_norm(_gelu(zz[:, half:]), gg, bb)

    return _rowcall(fn, [z], [g, b], [(half, F32), (half, BF16)], name=name)


def _gelu_ln_bwd(z, d_u, d_vn, g, b, name):
    half = z.shape[1] // 2

    def fn(zz, du, dvn, gg, bb):
        _, vu = jax.vjp(_gelu, zz[:, :half])
        _, vv = jax.vjp(lambda t, a, c: _layer_norm(_gelu(t), a, c), zz[:, half:], gg, bb)
        (dzu,) = vu(du)
        dzv, dg, db = vv(dvn)
        return jnp.concatenate([dzu, dzv], axis=1), dg, db

    return _rowcall(fn, [z, d_u, d_vn], [g, b], [(z.shape[1], BF16)], [g.shape, b.shape], bm=128, name=name)


def _tril_bf16(ws):
    t = lax.broadcasted_iota(jnp.int32, ws.shape, 0)
    s = lax.broadcasted_iota(jnp.int32, ws.shape, 1)
    return jnp.where(s <= t, ws, 0.0).astype(BF16)


def _sgu_fwd(u, vn, ws, bs, name, bm=512):
    T, half = u.shape
    gd = half // GROUPS
    bm = _pick(T, bm, CHUNK)
    nc = bm // CHUNK

    def body(u_ref, vn_ref, ws_ref, bs_ref, y_ref):
        wm = _tril_bf16(ws_ref[...])
        bias = bs_ref[...]
        for c in range(nc):
            rs = slice(c * CHUNK, (c + 1) * CHUNK)
            sv = jnp.dot(wm, vn_ref[rs, :], preferred_element_type=F32) + bias
            y_ref[rs, :] = (u_ref[rs, :] * sv).astype(y_ref.dtype)

    tb = lambda: pl.BlockSpec((bm, gd), lambda g, i: (i, g))
    return pl.pallas_call(
        body, name=name, grid=(GROUPS, T // bm),
        in_specs=[tb(), tb(), pl.BlockSpec((None, CHUNK, CHUNK), lambda g, i: (g, 0, 0)),
                  pl.BlockSpec((None, CHUNK, 1), lambda g, i: (g, 0, 0))],
        out_specs=tb(),
        out_shape=jax.ShapeDtypeStruct((T, half), BF16),
        compiler_params=_params(("parallel", "parallel")),
    )(u, vn, ws, bs)


def _sgu_bwd(u, vn, dy, ws, bs, name, bm=512):
    T, half = u.shape
    gd = half // GROUPS
    bm = _pick(T, bm, CHUNK)
    nc = bm // CHUNK

    def body(u_ref, vn_ref, dy_ref, ws_ref, bs_ref, du_ref, dvn_ref, dws_ref, dbs_ref):
        @pl.when(pl.program_id(1) == 0)
        def _():
            dws_ref[...] = jnp.zeros_like(dws_ref)
            dbs_ref[...] = jnp.zeros_like(dbs_ref)

        wm = _tril_bf16(ws_ref[...])
        bias = bs_ref[...]
        dws = jnp.zeros((CHUNK, CHUNK), F32)
        dbs = jnp.zeros((CHUNK, 1), F32)
        for c in range(nc):
            rs = slice(c * CHUNK, (c + 1) * CHUNK)
            vb = vn_ref[rs, :]
            dyb = dy_ref[rs, :]
            sv = jnp.dot(wm, vb, preferred_element_type=F32) + bias
            du_ref[rs, :] = dyb * sv
            dsv = dyb * u_ref[rs, :]
            dsb = dsv.astype(BF16)
            dvn_ref[rs, :] = lax.dot_general(wm, dsb, _TN, preferred_element_type=F32)
            dws = dws + lax.dot_general(dsb, vb, _NT, preferred_element_type=F32)
            dbs = dbs + jnp.sum(dsv, axis=1, keepdims=True)
        t = lax.broadcasted_iota(jnp.int32, (CHUNK, CHUNK), 0)
        s = lax.broadcasted_iota(jnp.int32, (CHUNK, CHUNK), 1)
        dws_ref[...] += jnp.where(s <= t, dws, 0.0)
        dbs_ref[...] += dbs

    tb = lambda: pl.BlockSpec((bm, gd), lambda g, i: (i, g))
    wsb = lambda: pl.BlockSpec((None, CHUNK, CHUNK), lambda g, i: (g, 0, 0))
    bsb = lambda: pl.BlockSpec((None, CHUNK, 1), lambda g, i: (g, 0, 0))
    return pl.pallas_call(
        body, name=name, grid=(GROUPS, T // bm),
        in_specs=[tb(), tb(), tb(), wsb(), bsb()],
        out_specs=[tb(), tb(), wsb(), bsb()],
        out_shape=[jax.ShapeDtypeStruct((T, half), F32), jax.ShapeDtypeStruct((T, half), F32),
                   jax.ShapeDtypeStruct(ws.shape, F32), jax.ShapeDtypeStruct(bs.shape, F32)],
        compiler_params=_params(("parallel", "arbitrary")),
    )(u, vn, dy, ws, bs)


def _loss_head(y, t, name):
    d_model = y.shape[1]

    def fn(yy, tt):
        d = yy - tt
        part = 0.5 * jnp.sum(jnp.mean(d * d, axis=-1, keepdims=True), axis=0, keepdims=True)
        return d / d_model, jnp.zeros((1, LANE), F32) + part

    dy, part = _rowcall(fn, [y, t], [], [(d_model, F32)], [(1, LANE)], name=name)
    return dy, part[0, 0]


def _adamw(parts, w, m, v, prev, layer, name):
    L, R, C = w.shape
    br = _pick(R, max(8, (128 * 1024) // C // 8 * 8), 8)
    c1 = 1.0 - B1 ** STEP
    c2 = 1.0 - B2 ** STEP
    if prev is None:
        prev = [lax.empty(w.shape, F32) for _ in range(4)]

    def body(p_ref, w_ref, m_ref, v_ref, a0, a1, a2, a3, g_o, d_o, m_o, v_o):
        g = p_ref[0].astype(F32)
        for d in range(1, N_DEV):
            g = g + p_ref[d].astype(F32)
        mn = B1 * m_ref[...] + (1.0 - B1) * g
        vn = B2 * v_ref[...] + (1.0 - B2) * (g * g)
        g_o[...] = g
        m_o[...] = mn
        v_o[...] = vn
        d_o[...] = -LR * ((mn / c1) / (jnp.sqrt(vn / c2) + ADAM_EPS) + WD * w_ref[...])

    blk = lambda: pl.BlockSpec((None, br, C), lambda i: (layer, i, 0))
    anywhere = pl.BlockSpec(memory_space=pl.ANY)
    return pl.pallas_call(
        body, name=name, grid=(R // br,),
        in_specs=[pl.BlockSpec((N_DEV, br, C), lambda i: (0, i, 0)), blk(), blk(), blk()] + [anywhere] * 4,
        out_specs=[blk(), blk(), blk(), blk()],
        out_shape=[jax.ShapeDtypeStruct((L, R, C), F32)] * 4,
        input_output_aliases={4: 0, 5: 1, 6: 2, 7: 3},
        compiler_params=_params(("parallel",)),
    )(parts, w, m, v, *prev)


def _mesh_pos():
    return lax.axis_index("x"), lax.axis_index("y"), lax.axis_index("c")


def _flip(pos, k):
    x, y, c = pos
    px = 1 - x if k & 4 else x
    py = 1 - y if k & 2 else y
    pc = 1 - c if k & 1 else c
    return px, py, pc


HBM_SPEC = pl.BlockSpec(memory_space=pltpu.HBM)
SEM_SPEC = pl.BlockSpec(memory_space=pltpu.SEMAPHORE)
EFFECT = pltpu.SideEffectType.DATAFLOW_SIDE_EFFECTING


def _hbm(a):
    return pltpu.with_memory_space_constraint(a, pltpu.HBM)


def _device_index():
    x, y, c = _mesh_pos()
    return 4 * x + 2 * y + c


def _peer_copy(src, land, send, recv, a, k, pos, scatter):
    peer = _flip(pos, k)
    me = 4 * pos[0] + 2 * pos[1] + pos[2]
    piece = src.at[4 * peer[0] + 2 * peer[1] + peer[2]] if scatter else src
    return pltpu.make_async_remote_copy(
        src_ref=piece, dst_ref=land.at[me], send_sem=send.at[7 * a + k - 1], recv_sem=recv.at[7 * a + k - 1],
        device_id=peer, device_id_type=pl.DeviceIdType.MESH)


def _xchg_start(srcs, scatter, after, name):
    n = len(srcs)

    def body(*refs):
        src, land = refs[:n], refs[n:2 * n]
        send, recv, token = refs[2 * n + 1], refs[2 * n + 2], refs[-1]
        pos = _mesh_pos()
        for k in range(1, N_DEV):
            for a in range(n):
                _peer_copy(src[a], land[a], send, recv, a, k, pos, scatter).start()
        token[...] = jnp.zeros_like(token)

    lands = [lax.empty(s.shape if scatter else (N_DEV,) + s.shape, s.dtype) for s in srcs]
    outs = pl.pallas_call(
        body, name=name,
        out_shape=(pltpu.SemaphoreType.DMA((7 * n,)), pltpu.SemaphoreType.DMA((7 * n,)),
                   *[pltpu.HBM(s.shape, s.dtype) for s in srcs], *[pltpu.HBM(l.shape, l.dtype) for l in lands],
                   jax.ShapeDtypeStruct((8, LANE), F32)),
        in_specs=[HBM_SPEC] * (2 * n) + [pl.BlockSpec(memory_space=pl.ANY)],
        out_specs=(SEM_SPEC, SEM_SPEC, *[HBM_SPEC] * (2 * n), pl.BlockSpec(memory_space=pltpu.VMEM)),
        input_output_aliases={q: 2 + q for q in range(2 * n)},
        compiler_params=pltpu.CompilerParams(has_side_effects=EFFECT),
    )(*[_hbm(s) for s in srcs], *[_hbm(l) for l in lands], after)
    handle = dict(send=outs[0], recv=outs[1], srcs=list(outs[2:2 + n]), lands=list(outs[2 + n:2 + 2 * n]),
                  scatter=scatter)
    return handle, outs[-1]


def _xchg_wait(handle, after, name):
    srcs, lands, scatter = handle['srcs'], handle['lands'], handle['scatter']
    n = len(srcs)

    def body(*refs):
        src, land = refs[:n], refs[n:2 * n]
        send, recv = refs[2 * n], refs[2 * n + 1]
        pos = _mesh_pos()
        for k in range(1, N_DEV):
            for a in range(n):
                cp = _peer_copy(src[a], land[a], send, recv, a, k, pos, scatter)
                cp.wait_send()
                cp.wait_recv()

    outs = pl.pallas_call(
        body, name=name,
        out_shape=[pltpu.HBM(s.shape, s.dtype) for s in srcs] + [pltpu.HBM(l.shape, l.dtype) for l in lands],
        in_specs=[HBM_SPEC] * (2 * n) + [SEM_SPEC, SEM_SPEC, pl.BlockSpec(memory_space=pl.ANY)],
        out_specs=[HBM_SPEC] * (2 * n),
        input_output_aliases={q: q for q in range(2 * n)},
        compiler_params=pltpu.CompilerParams(has_side_effects=EFFECT),
    )(*srcs, *lands, handle['send'], handle['recv'], after)
    me = _device_index()
    full = []
    for src, land in zip(outs[:n], outs[n:]):
        mine = lax.dynamic_index_in_dim(src, me, 0, keepdims=False) if scatter else src
        full.append(lax.dynamic_update_index_in_dim(land, mine, me, 0))
    return full


def _pack(parts):
    flat = jnp.concatenate([q.reshape(-1) for q in parts])
    pad = (-flat.shape[0]) % (8 * LANE)
    return jnp.pad(flat, (0, pad)).reshape(-1, LANE)


def _unpack(packed, like):
    flat = packed.reshape(-1)
    out, o = [], 0
    for q in like:
        out.append(flat[o:o + q.size].reshape(q.shape))
        o += q.size
    return out


def kernel(x, p, positions, norm_mix, norm_ffn, norm_ple, mla_w_down, mla_q_lora_g, mla_kv_lora_g, mla_w_uq, mla_w_ukv, mla_q_nope_g, mla_q_rope_g, mla_k_nope_g, mla_k_rope_g, mla_w_out, gmlp_w_in, gmlp_ln_g, gmlp_ln_b, gmlp_w_s, gmlp_b_s, gmlp_w_out, ffn_w_up, ffn_w_down, ple_w_gate, ple_w_proj, loss_target, m_norm_mix, m_norm_ffn, m_norm_ple, m_mla_w_down, m_mla_q_lora_g, m_mla_kv_lora_g, m_mla_w_uq, m_mla_w_ukv, m_mla_q_nope_g, m_mla_q_rope_g, m_mla_k_nope_g, m_mla_k_rope_g, m_mla_w_out, m_gmlp_w_in, m_gmlp_ln_g, m_gmlp_ln_b, m_gmlp_w_s, m_gmlp_b_s, m_gmlp_w_out, m_ffn_w_up, m_ffn_w_down, m_ple_w_gate, m_ple_w_proj, v_norm_mix, v_norm_ffn, v_norm_ple, v_mla_w_down, v_mla_q_lora_g, v_mla_kv_lora_g, v_mla_w_uq, v_mla_w_ukv, v_mla_q_nope_g, v_mla_q_rope_g, v_mla_k_nope_g, v_mla_k_rope_g, v_mla_w_out, v_gmlp_w_in, v_gmlp_ln_g, v_gmlp_ln_b, v_gmlp_w_s, v_gmlp_b_s, v_gmlp_w_out, v_ffn_w_up, v_ffn_w_down, v_ple_w_gate, v_ple_w_proj):
    W = dict(zip(WEIGHTS, (norm_mix, norm_ffn, norm_ple, mla_w_down, mla_q_lora_g, mla_kv_lora_g, mla_w_uq, mla_w_ukv, mla_q_nope_g, mla_q_rope_g, mla_k_nope_g, mla_k_rope_g, mla_w_out, gmlp_w_in, gmlp_ln_g, gmlp_ln_b, gmlp_w_s, gmlp_b_s, gmlp_w_out, ffn_w_up, ffn_w_down, ple_w_gate, ple_w_proj)))
    M1 = dict(zip(WEIGHTS, (m_norm_mix, m_norm_ffn, m_norm_ple, m_mla_w_down, m_mla_q_lora_g, m_mla_kv_lora_g, m_mla_w_uq, m_mla_w_ukv, m_mla_q_nope_g, m_mla_q_rope_g, m_mla_k_nope_g, m_mla_k_rope_g, m_mla_w_out, m_gmlp_w_in, m_gmlp_ln_g, m_gmlp_ln_b, m_gmlp_w_s, m_gmlp_b_s, m_gmlp_w_out, m_ffn_w_up, m_ffn_w_down, m_ple_w_gate, m_ple_w_proj)))
    M2 = dict(zip(WEIGHTS, (v_norm_mix, v_norm_ffn, v_norm_ple, v_mla_w_down, v_mla_q_lora_g, v_mla_kv_lora_g, v_mla_w_uq, v_mla_w_ukv, v_mla_q_nope_g, v_mla_q_rope_g, v_mla_k_nope_g, v_mla_k_rope_g, v_mla_w_out, v_gmlp_w_in, v_gmlp_ln_g, v_gmlp_ln_b, v_gmlp_w_s, v_gmlp_b_s, v_gmlp_w_out, v_ffn_w_up, v_ffn_w_down, v_ple_w_gate, v_ple_w_proj)))

    nb, seq, d_model = x.shape
    T = nb * seq
    depth = norm_mix.shape[0]
    h = x.reshape(T, d_model)
    target = loss_target.reshape(T, d_model)
    p_bf = p.reshape(depth, T, p.shape[-1]).astype(BF16)

    def stage_keys(st):
        i, second = divmod(st, 2)
        if second:
            return [(n, i) for n in ('ffn_w_up', 'ffn_w_down', 'ple_w_gate', 'ple_w_proj')]
        mix = (['mla_w_down', 'mla_w_uq', 'mla_w_ukv', 'mla_w_out'] if i % 2 == 0 else
               ['gmlp_w_in', 'gmlp_ln_g', 'gmlp_ln_b', 'gmlp_w_out'])
        return [(n, i // 2) for n in mix]

    FW = {n: {} for n in SHARDED}

    def start_weights(st, after):
        keys = stage_keys(st)
        srcs = [W[n][l] if n in F32_PAYLOAD else W[n][l].astype(BF16) for n, l in keys]
        handle, token = _xchg_start(srcs, False, after, "weights_start%d" % st)
        return (keys, handle), token[0, 0]

    def wait_weights(pending, st, after):
        keys, handle = pending
        for (n, l), full in zip(keys, _xchg_wait(handle, after, "weights_wait%d" % st)):
            if SHARD_AXIS[n] == 1:
                FW[n][l] = full.reshape((-1,) + full.shape[2:])
            elif n == 'ple_w_proj':
                FW[n][l] = jnp.transpose(full, (1, 0, 2)).reshape(full.shape[1], -1)
            else:
                FW[n][l] = full

    row = lambda a: a.reshape(1, -1)
    cos, sin = _rope_tables(positions.reshape(T, 1), "rope_tables")
    rmat = _rot_matrix()

    saved = []
    pending, _ = start_weights(0, h)
    for i in range(depth):
        j = i // 2
        wait_weights(pending, 2 * i, h)
        pending, token = start_weights(2 * i + 1, h)
        s = {}
        s['h0'] = h
        hn = _rms_fwd(h, row(W['norm_mix'][i]) + token, "rms_fwd")
        s['hn'] = hn
        if i % 2 == 0:
            gains = [row(W['mla_q_nope_g'][j]), row(W['mla_q_rope_g'][j]), row(W['mla_k_nope_g'][j]),
                     row(W['mla_k_rope_g'][j])]
            lat = _mm(hn, FW['mla_w_down'][j], name="mla_down")
            cq, ckv = _prep1_fwd(lat, row(W['mla_q_lora_g'][j]), row(W['mla_kv_lora_g'][j]), "mla_prep1")
            q_raw = _mm(cq, FW['mla_w_uq'][j], out_blocks=HEADS, name="mla_uq")
            kv_raw = _mm(ckv, FW['mla_w_ukv'][j], out_blocks=HEADS, name="mla_ukv")
            q, k, v = _prep2_fwd(q_raw, kv_raw, lat, cos, sin, rmat, gains, "mla_prep2")
            o, lse = _attn_fwd(q, k, v, seq, "attn_fwd")
            h = _mm(o, FW['mla_w_out'][j], extras=(h,), epilogue=lambda acc, res: (res + acc,), name="mla_out")
            s.update(lat=lat, cq=cq, ckv=ckv, q_raw=q_raw, kv_raw=kv_raw, q=q, k=k, v=v, o=o, lse=lse, gains=gains)
        else:
            z = _mm(hn, FW['gmlp_w_in'][j], name="gmlp_in")
            u, vn = _gelu_ln_fwd(z, row(FW['gmlp_ln_g'][j]), row(FW['gmlp_ln_b'][j]), "gmlp_gelu_ln")
            bs3 = W['gmlp_b_s'][j][:, :, None]
            y = _sgu_fwd(u, vn, W['gmlp_w_s'][j], bs3, "gmlp_sgu")
            h = _mm(y, FW['gmlp_w_out'][j], extras=(h,), epilogue=lambda acc, res: (res + acc,), name="gmlp_out")
            s.update(z=z, u=u, vn=vn, y=y, bs3=bs3)
        s['h1'] = h
        wait_weights(pending, 2 * i + 1, h)
        token = 0.0
        if i + 1 < depth:
            pending, token = start_weights(2 * i + 2, h)
        hn2 = _rms_fwd(h, row(W['norm_ffn'][i]) + token, "rms_fwd")
        a, r = _mm(hn2, FW['ffn_w_up'][i], epilogue=lambda acc: (acc, jnp.square(jnp.maximum(acc, 0.0))),
                   out_dtypes=(BF16, BF16), name="ffn_up")
        h = _mm(r, FW['ffn_w_down'][i], extras=(h,), epilogue=lambda acc, res: (res + acc,), name="ffn_down")
        s.update(hn2=hn2, a=a, r=r, h2=h)
        hn3 = _rms_fwd(h, row(W['norm_ple'][i]), "rms_fwd")
        gt = _mm(hn3, FW['ple_w_gate'][i], name="ple_gate")
        pp, h = _mm(p_bf[i], FW['ple_w_proj'][i], extras=(gt, h),
                    epilogue=lambda acc, g_, res: (acc, res + _sigmoid(g_) * acc), out_dtypes=(F32, F32),
                    name="ple_proj")
        s.update(hn3=hn3, gt=gt, pp=pp)
        saved.append(s)

    dh, loss_part = _loss_head(h, target, "loss_head")
    loss = lax.psum(loss_part, MESH_AXES)

    G = {n: [None] * W[n].shape[0] for n in REPLICATED}
    res = {}
    flying = []

    def shard3(n):
        shp = W[n].shape
        return shp[0], int(np.prod(shp[1:-1])), shp[-1]

    def by_owner(g):
        return g.reshape((N_DEV, g.shape[0] // N_DEV) + g.shape[1:])

    def send_grads(tag, grads):
        handle, token = _xchg_start([g for _, g in grads], True, grads[-1][1], "grads_start_" + tag)
        flying.append((tag, [key for key, _ in grads], handle))
        return token[0, 0]

    def land_grads(after):
        tag, keys, handle = flying.pop(0)
        for (n, l), full in zip(keys, _xchg_wait(handle, after, "grads_wait_" + tag)):
            dims = shard3(n)
            res[n] = _adamw(full.reshape((N_DEV,) + dims[1:]), W[n].reshape(dims), M1[n].reshape(dims),
                            M2[n].reshape(dims), res.get(n), l, "adamw_" + n)

    def start_small(names, tag, after):
        handle, _ = _xchg_start([_pack([jnp.stack(G[n]) for n in names])], False, after, "small_start_" + tag)
        return names, handle

    def land_small(pending_small, tag, after):
        names, handle = pending_small
        (parts,) = _xchg_wait(handle, after, "small_wait_" + tag)
        like = [W[n] for n in names]
        outs = _adamw(parts, _pack(like)[None], _pack([M1[n] for n in names])[None],
                      _pack([M2[n] for n in names])[None], None, 0, "adamw_small_" + tag)
        unpacked = [_unpack(o, like) for o in outs]
        for idx, n in enumerate(names):
            res[n] = [unpacked[q][idx] for q in range(4)]

    spatial = ['gmlp_w_s', 'gmlp_b_s']
    token = 0.0
    for i in reversed(range(depth)):
        j = i // 2
        s = saved[i]
        def ple_elem(d, g_, pq):
            sg = _sigmoid(g_)
            return d * sg, d * pq * sg * (1.0 - sg)

        d_pp, d_gt = _rowcall(ple_elem, [dh, s['gt'], s['pp']], [], [(d_model, BF16), (d_model, BF16)], name="ple_bwd")
        g_proj = _mm(p_bf[i], d_pp, ta=True, out_dtypes=(BF16,), name="ple_proj_dw")
        g_proj = jnp.transpose(g_proj.reshape(g_proj.shape[0], N_DEV, -1), (1, 0, 2))
        g_gate = _mm(s['hn3'], d_gt, ta=True, out_dtypes=(BF16,), name="ple_gate_dw")
        d_hn3 = _mm(d_gt, FW['ple_w_gate'][i], tb=True, name="ple_gate_dx")
        dh, dg = _rms_bwd(s['h2'], d_hn3, dh, row(W['norm_ple'][i]) + token, "rms_bwd")
        G['norm_ple'][i] = dg[0]
        d_a = _mm(dh, FW['ffn_w_down'][i], tb=True, extras=(s['a'],),
                  epilogue=lambda acc, a_: (acc * (2.0 * jnp.maximum(a_.astype(F32), 0.0)),), out_dtypes=(BF16,),
                  name="ffn_down_dx")
        g_down = _mm(s['r'], dh, ta=True, out_dtypes=(BF16,), name="ffn_down_dw")
        g_up = _mm(s['hn2'], d_a, ta=True, out_blocks=N_DEV, out_dtypes=(BF16,), name="ffn_up_dw")
        token = send_grads("mlp%d" % i, [(('ple_w_proj', i), g_proj), (('ple_w_gate', i), by_owner(g_gate)),
                                         (('ffn_w_down', i), by_owner(g_down)), (('ffn_w_up', i), g_up)])
        if len(flying) > 1:
            land_grads(g_up)
        d_hn2 = _mm(d_a, FW['ffn_w_up'][i], tb=True, name="ffn_up_dx")
        dh, dg = _rms_bwd(s['h1'], d_hn2, dh, row(W['norm_ffn'][i]) + token, "rms_bwd")
        G['norm_ffn'][i] = dg[0]
        if i % 2 == 0:
            d_o = _mm(dh, FW['mla_w_out'][j], tb=True, name="mla_out_dx")
            g_out = _mm(s['o'], dh, ta=True, out_dtypes=(BF16,), name="mla_out_dw")
            dq, dk, dv = _attn_bwd(s['q'], s['k'], s['v'], s['o'], d_o, s['lse'], seq, "attn_bwd")
            d_q_raw, d_kv_raw, d_kr, g1, g2, g3, g4 = _prep2_bwd(
                s['q_raw'], s['kv_raw'], s['lat'], cos, sin, rmat, s['gains'], dq, dk, dv, "mla_prep2_bwd")
            G['mla_q_nope_g'][j], G['mla_q_rope_g'][j] = g1[0], g2[0]
            G['mla_k_nope_g'][j], G['mla_k_rope_g'][j] = g3[0], g4[0]
            g_uq = _mm(s['cq'], d_q_raw, ta=True, out_blocks=N_DEV, out_dtypes=(BF16,), name="mla_uq_dw")
            g_ukv = _mm(s['ckv'], d_kv_raw, ta=True, out_blocks=N_DEV, out_dtypes=(BF16,), name="mla_ukv_dw")
            d_cq = _mm(d_q_raw, FW['mla_w_uq'][j], tb=True, name="mla_uq_dx")
            d_ckv = _mm(d_kv_raw, FW['mla_w_ukv'][j], tb=True, name="mla_ukv_dx")
            d_lat, dga, dgb = _prep1_bwd(s['lat'], d_cq, d_ckv, d_kr, row(W['mla_q_lora_g'][j]),
                                         row(W['mla_kv_lora_g'][j]), "mla_prep1_bwd")
            G['mla_q_lora_g'][j], G['mla_kv_lora_g'][j] = dga[0], dgb[0]
            g_down = _mm(s['hn'], d_lat, ta=True, out_dtypes=(BF16,), name="mla_down_dw")
            grads = [(('mla_w_out', j), by_owner(g_out)), (('mla_w_uq', j), g_uq), (('mla_w_ukv', j), g_ukv),
                     (('mla_w_down', j), by_owner(g_down))]
            d_hn = _mm(d_lat, FW['mla_w_down'][j], tb=True, name="mla_down_dx")
        else:
            d_y = _mm(dh, FW['gmlp_w_out'][j], tb=True, name="gmlp_out_dx")
            g_out = _mm(s['y'], dh, ta=True, out_dtypes=(BF16,), name="gmlp_out_dw")
            d_u, d_vn, d_ws, d_bs = _sgu_bwd(s['u'], s['vn'], d_y, W['gmlp_w_s'][j], s['bs3'], "gmlp_sgu_bwd")
            G['gmlp_w_s'][j], G['gmlp_b_s'][j] = d_ws, d_bs[:, :, 0]
            d_z, d_lg, d_lb = _gelu_ln_bwd(s['z'], d_u, d_vn, row(FW['gmlp_ln_g'][j]), row(FW['gmlp_ln_b'][j]),
                                           "gmlp_gelu_ln_bwd")
            g_in = _mm(s['hn'], d_z, ta=True, out_blocks=N_DEV, out_dtypes=(BF16,), name="gmlp_in_dw")
            grads = [(('gmlp_w_out', j), by_owner(g_out)), (('gmlp_ln_g', j), by_owner(d_lg[0])),
                     (('gmlp_ln_b', j), by_owner(d_lb[0])), (('gmlp_w_in', j), g_in)]
            d_hn = _mm(d_z, FW['gmlp_w_in'][j], tb=True, name="gmlp_in_dx")
        token = send_grads("mix%d" % i, grads)
        if len(flying) > 1:
            land_grads(grads[-1][1])
        dh, dg = _rms_bwd(s['h0'], d_hn, dh, row(W['norm_mix'][i]) + token, "rms_bwd")
        G['norm_mix'][i] = dg[0]
        if i == 1:
            small_a = start_small(spatial, "spatial", dh)
    grad_x = dh.reshape(x.shape)

    small_b = start_small([n for n in REPLICATED if n not in spatial], "gains", dh)
    while flying:
        land_grads(dh)
    land_small(small_a, "spatial", dh)
    land_small(small_b, "gains", dh)

    out = lambda q: [res[n][q].reshape(W[n].shape) for n in WEIGHTS]
    return (loss, grad_x, *out(0), *out(1), *out(2), *out(3))
```

```python
import math

import numpy as np
import jax
import jax.numpy as jnp
from jax import lax
from jax.experimental import pallas as pl
from jax.experimental.pallas import tpu as pltpu

F32 = jnp.float32
BF16 = jnp.bfloat16

N_DEV = 8
MESH_AXES = ("x", "y", "c")
HEADS = 8
NOPE = 128
ROPE = 64
VDIM = 128
QK = NOPE + ROPE
Q_LORA = 384
KV_LORA = 256
ROPE_BASE = 10000.0
CHUNK = 128
GROUPS = 8
EPS = 1e-6
LR, B1, B2, ADAM_EPS, WD, STEP = 0.001, 0.9, 0.999, 1e-08, 0.01, 10
LANE = 128
VMEM_LIMIT = 56 * 1024 * 1024
MM_VMEM_BUDGET = 40 * 1024 * 1024

WEIGHTS = ['norm_mix', 'norm_ffn', 'norm_ple', 'mla_w_down', 'mla_q_lora_g', 'mla_kv_lora_g', 'mla_w_uq',
           'mla_w_ukv', 'mla_q_nope_g', 'mla_q_rope_g', 'mla_k_nope_g', 'mla_k_rope_g', 'mla_w_out', 'gmlp_w_in',
           'gmlp_ln_g', 'gmlp_ln_b', 'gmlp_w_s', 'gmlp_b_s', 'gmlp_w_out', 'ffn_w_up', 'ffn_w_down', 'ple_w_gate',
           'ple_w_proj']
SHARD_AXIS = {'mla_w_down': 1, 'mla_w_uq': 2, 'mla_w_ukv': 2, 'mla_w_out': 1, 'gmlp_w_in': 2, 'gmlp_ln_g': 1,
              'gmlp_ln_b': 1, 'gmlp_w_out': 1, 'ffn_w_up': 2, 'ffn_w_down': 1, 'ple_w_gate': 1, 'ple_w_proj': 2}
SHARDED = list(SHARD_AXIS)
REPLICATED = [n for n in WEIGHTS if n not in SHARD_AXIS]
F32_PAYLOAD = ('gmlp_ln_g', 'gmlp_ln_b')


def _pick(dim, pref, align=LANE):
    if dim <= pref:
        return dim
    b = (pref // align) * align
    while b >= align:
        if dim % b == 0:
            return b
        b -= align
    return dim


def _params(sem):
    return pltpu.CompilerParams(dimension_semantics=sem, vmem_limit_bytes=VMEM_LIMIT)


def _mm(a, b, *, ta=False, tb=False, extras=(), epilogue=None, out_dtypes=(F32,), out_blocks=None, name,
        bn=1024):
    a3, b3 = a.ndim == 3, b.ndim == 3
    assert not (ta and a3)
    if ta:
        K, M = a.shape
        ka = K
    elif a3:
        M, ka = a.shape[1:]
        K = a.shape[0] * ka
    else:
        M, K = a.shape
        ka = K
    if tb:
        N, kb = b.shape[-2:]
        nb = N
        K2 = b.shape[0] * kb if b3 else kb
    else:
        kb, nb = b.shape[-2:]
        K2 = kb
        N = b.shape[0] * nb if b3 else nb
    assert K == K2, (a.shape, b.shape, ta, tb)
    no_ = N // out_blocks if out_blocks else N
    assert not (out_blocks and extras)
    size = lambda t: jnp.dtype(t).itemsize
    per_out = sum(size(e.dtype) for e in extras) + sum(size(t) for t in out_dtypes)
    bn = _pick(min(nb, no_), bn)
    k_lim = min(ka, kb)
    fits = lambda m, k: 2 * (m * k * size(a.dtype) + k * bn * size(b.dtype) + m * bn * per_out) + 4 * m * bn
    ms = [m for m in sorted({min(M, c) for c in (2048, 1024, 512, 256)}, reverse=True) if M % m == 0]
    ks = [k for k in dict.fromkeys((k_lim, 2048, 1024, 512, 256)) if k <= k_lim and k_lim % k == 0]
    bm, bk = next(((m, k) for k in ks for m in ms if fits(m, k) <= MM_VMEM_BUDGET), (ms[-1], ks[-1]))
    nk = K // bk
    ne, no = len(extras), len(out_dtypes)
    dims = (((0,) if ta else (1,), (1,) if tb else (0,)), ((), ()))

    def finish(r, e_refs, o_refs):
        outs = epilogue(r, *[e[...] for e in e_refs]) if epilogue is not None else (r,)
        for o, v in zip(o_refs, outs):
            o[...] = v.astype(o.dtype)

    def body(*refs):
        a_ref, b_ref = refs[0], refs[1]
        e_refs = refs[2:2 + ne]
        o_refs = refs[2 + ne:2 + ne + no]
        part = lax.dot_general(a_ref[...].astype(BF16), b_ref[...].astype(BF16), dims, preferred_element_type=F32)
        if nk == 1:
            finish(part, e_refs, o_refs)
            return
        acc = refs[-1]
        k = pl.program_id(2)

        @pl.when(k == 0)
        def _():
            acc[...] = part

        @pl.when(k > 0)
        def _():
            acc[...] += part

        @pl.when(k == nk - 1)
        def _():
            finish(acc[...], e_refs, o_refs)

    ka_t, kb_t, nb_t, no_t = ka // bk, kb // bk, nb // bn, no_ // bn
    if ta:
        a_spec = pl.BlockSpec((bk, bm), lambda i, j, k: (k, i))
    elif a3:
        a_spec = pl.BlockSpec((None, bm, bk), lambda i, j, k: (k // ka_t, i, k % ka_t))
    else:
        a_spec = pl.BlockSpec((bm, bk), lambda i, j, k: (i, k))
    if tb:
        b_spec = (pl.BlockSpec((None, bn, bk), lambda i, j, k: (k // kb_t, j, k % kb_t)) if b3 else
                  pl.BlockSpec((bn, bk), lambda i, j, k: (j, k)))
    else:
        b_spec = (pl.BlockSpec((None, bk, bn), lambda i, j, k: (j // nb_t, k, j % nb_t)) if b3 else
                  pl.BlockSpec((bk, bn), lambda i, j, k: (k, j)))
    if out_blocks:
        o_spec = lambda: pl.BlockSpec((None, bm, bn), lambda i, j, k: (j // no_t, i, j % no_t))
        o_shape = (out_blocks, M, no_)
    else:
        o_spec = lambda: pl.BlockSpec((bm, bn), lambda i, j, k: (i, j))
        o_shape = (M, N)
    outs = pl.pallas_call(
        body, name=name,
        grid=(M // bm, N // bn, nk),
        in_specs=[a_spec, b_spec] + [pl.BlockSpec((bm, bn), lambda i, j, k: (i, j)) for _ in extras],
        out_specs=[o_spec() for _ in out_dtypes],
        out_shape=[jax.ShapeDtypeStruct(o_shape, dt) for dt in out_dtypes],
        scratch_shapes=[pltpu.VMEM((bm, bn), F32)] if nk > 1 else [],
        compiler_params=_params(("parallel", "parallel", "arbitrary")),
    )(a, b, *extras)
    return outs[0] if no == 1 else outs


def _rowcall(fn, rows, params, row_outs, acc_outs=(), *, bm=256, name):
    T = rows[0].shape[0]
    bm = _pick(T, bm, 8)
    nr, npar, nro, nao = len(rows), len(params), len(row_outs), len(acc_outs)

    def body(*refs):
        vals = [r[...] for r in refs[:nr + npar]]
        res = fn(*vals)
        ro = refs[nr + npar:nr + npar + nro]
        ao = refs[nr + npar + nro:]
        for r, v in zip(ro, res[:nro]):
            r[...] = v.astype(r.dtype)
        if nao:
            @pl.when(pl.program_id(0) == 0)
            def _():
                for r in ao:
                    r[...] = jnp.zeros_like(r)

            for r, v in zip(ao, res[nro:]):
                r[...] += v

    def whole(shape):
        nd = len(shape)
        return pl.BlockSpec(tuple(shape), lambda i: (0,) * nd)

    outs = pl.pallas_call(
        body, name=name,
        grid=(T // bm,),
        in_specs=[pl.BlockSpec((bm, r.shape[1]), lambda i: (i, 0)) for r in rows] + [whole(q.shape) for q in params],
        out_specs=[pl.BlockSpec((bm, c), lambda i: (i, 0)) for c, _ in row_outs] + [whole(s) for s in acc_outs],
        out_shape=[jax.ShapeDtypeStruct((T, c), dt) for c, dt in row_outs]
        + [jax.ShapeDtypeStruct(tuple(s), F32) for s in acc_outs],
        compiler_params=_params(("arbitrary",) if nao else ("parallel",)),
    )(*rows, *params)
    return outs


def _rmsn(x, g):
    return x * lax.rsqrt(jnp.mean(x * x, axis=-1, keepdims=True) + EPS) * g


def _gelu(x):
    return 0.5 * x * (1.0 + jnp.tanh(math.sqrt(2.0 / math.pi) * (x + 0.044715 * (x * x * x))))


def _layer_norm(x, g, b):
    mu = jnp.mean(x, axis=-1, keepdims=True)
    xc = x - mu
    return xc * lax.rsqrt(jnp.mean(xc * xc, axis=-1, keepdims=True) + EPS) * g + b


def _sigmoid(x):
    return 1.0 / (1.0 + jnp.exp(-x))


def _rot(x, cos, sin, rmat):
    return x * cos + jnp.dot(x, rmat, precision=lax.Precision.HIGHEST, preferred_element_type=F32) * sin


def _rms_fwd(h, g, name):
    return _rowcall(lambda x, gg: (_rmsn(x, gg),), [h], [g], [(h.shape[1], BF16)], bm=512, name=name)[0]


def _rms_bwd(h, d_hn, dh_in, g, name):
    def fn(x, dy, dres, gg):
        _, vjp = jax.vjp(_rmsn, x, gg)
        dx, dg = vjp(dy)
        dh = dres + dx
        return dh, dh, dg

    d = h.shape[1]
    return _rowcall(fn, [h, d_hn, dh_in], [g], [(d, F32), (d, BF16)], [g.shape], bm=512, name=name)


def _rope_tables(pos, name):
    inv = np.float32(ROPE_BASE) ** (-(np.arange(0, ROPE, 2, dtype=np.float32) / np.float32(ROPE)))
    inv = jnp.asarray(np.concatenate([inv, inv])[None, :].astype(np.float32))

    def fn(pp, iv):
        ang = pp.astype(F32) * iv
        return jnp.cos(ang), jnp.sin(ang)

    return _rowcall(fn, [pos], [inv], [(ROPE, F32), (ROPE, F32)], name=name)


def _rot_matrix():
    r = np.zeros((ROPE, ROPE), np.float32)
    half = ROPE // 2
    for j in range(half):
        r[j + half, j] = -1.0
        r[j, j + half] = 1.0
    return jnp.asarray(r)


def _prep1_fwd(lat, gq, gkv, name):
    def fn(l, a, b):
        return _rmsn(l[:, :Q_LORA], a), _rmsn(l[:, Q_LORA:Q_LORA + KV_LORA], b)

    return _rowcall(fn, [lat], [gq, gkv], [(Q_LORA, BF16), (KV_LORA, BF16)], name=name)


def _prep1_bwd(lat, d_cq, d_ckv, d_kr, gq, gkv, name):
    def fn(l, dq, dkv, dkr, a, b):
        _, vq = jax.vjp(_rmsn, l[:, :Q_LORA], a)
        _, vkv = jax.vjp(_rmsn, l[:, Q_LORA:Q_LORA + KV_LORA], b)
        dxq, dga = vq(dq)
        dxkv, dgb = vkv(dkv)
        return jnp.concatenate([dxq, dxkv, dkr], axis=1), dga, dgb

    return _rowcall(fn, [lat, d_cq, d_ckv, d_kr], [gq, gkv], [(lat.shape[1], BF16)], [gq.shape, gkv.shape], name=name)


def _qk_fn(qn_raw, qr_raw, kn_raw, kr_raw, gqn, gqr, gkn, gkr, cos, sin, rmat):
    return (_rmsn(qn_raw, gqn), _rot(_rmsn(qr_raw, gqr), cos, sin, rmat),
            _rmsn(kn_raw, gkn), _rot(_rmsn(kr_raw, gkr), cos, sin, rmat))


def _prep2_fwd(q_raw, kv_raw, lat, cos, sin, rmat, gains, name, bm=1024):
    H, T, _ = q_raw.shape
    bm = _pick(T, bm, 8)
    kr0 = Q_LORA + KV_LORA

    def body(q_ref, kv_ref, lat_ref, cos_ref, sin_ref, r_ref, gqn, gqr, gkn, gkr, qo, ko, vo):
        qr, kvr = q_ref[...], kv_ref[...]
        qn, qro, kn, kro = _qk_fn(qr[:, :NOPE], qr[:, NOPE:], kvr[:, :NOPE], lat_ref[:, kr0:kr0 + ROPE],
                                  gqn[...], gqr[...], gkn[...], gkr[...], cos_ref[...], sin_ref[...], r_ref[...])
        qo[:, :NOPE] = qn.astype(BF16)
        qo[:, NOPE:] = qro.astype(BF16)
        ko[:, :NOPE] = kn.astype(BF16)
        ko[:, NOPE:] = kro.astype(BF16)
        vo[...] = kvr[:, NOPE:].astype(BF16)

    hb = lambda c: pl.BlockSpec((None, bm, c), lambda m, h: (h, m, 0))
    rb = lambda c: pl.BlockSpec((bm, c), lambda m, h: (m, 0))
    wb = lambda s: pl.BlockSpec(tuple(s), lambda m, h: (0, 0))
    return pl.pallas_call(
        body, name=name, grid=(T // bm, H),
        in_specs=[hb(QK), hb(NOPE + VDIM), rb(lat.shape[1]), rb(ROPE), rb(ROPE), wb(rmat.shape)]
        + [wb(g.shape) for g in gains],
        out_specs=[hb(QK), hb(QK), hb(VDIM)],
        out_shape=[jax.ShapeDtypeStruct((H, T, QK), BF16), jax.ShapeDtypeStruct((H, T, QK), BF16),
                   jax.ShapeDtypeStruct((H, T, VDIM), BF16)],
        compiler_params=_params(("parallel", "parallel")),
    )(q_raw, kv_raw, lat, cos, sin, rmat, *gains)


def _prep2_bwd(q_raw, kv_raw, lat, cos, sin, rmat, gains, dq, dk, dv, name, bm=512):
    H, T, _ = q_raw.shape
    bm = _pick(T, bm, 8)
    kr0 = Q_LORA + KV_LORA

    def body(q_ref, kv_ref, lat_ref, cos_ref, sin_ref, r_ref, gqn, gqr, gkn, gkr, dq_ref, dk_ref, dv_ref,
             dqo, dkvo, dkro, o_gqn, o_gqr, o_gkn, o_gkr):
        m, h = pl.program_id(0), pl.program_id(1)
        qr, kvr = q_ref[...], kv_ref[...]
        cos_v, sin_v, r_v = cos_ref[...], sin_ref[...], r_ref[...]
        f = lambda a, b, c, d, g1, g2, g3, g4: _qk_fn(a, b, c, d, g1, g2, g3, g4, cos_v, sin_v, r_v)
        _, vjp = jax.vjp(f, qr[:, :NOPE], qr[:, NOPE:], kvr[:, :NOPE], lat_ref[:, kr0:kr0 + ROPE],
                         gqn[...], gqr[...], gkn[...], gkr[...])
        dqv, dkv_ = dq_ref[...], dk_ref[...]
        d_qn, d_qr, d_kn, d_kr, g1, g2, g3, g4 = vjp((dqv[:, :NOPE], dqv[:, NOPE:], dkv_[:, :NOPE], dkv_[:, NOPE:]))
        dqo[:, :NOPE] = d_qn.astype(BF16)
        dqo[:, NOPE:] = d_qr.astype(BF16)
        dkvo[:, :NOPE] = d_kn.astype(BF16)
        dkvo[:, NOPE:] = dv_ref[...].astype(BF16)

        @pl.when(h == 0)
        def _():
            dkro[...] = jnp.zeros_like(dkro)

        dkro[...] += d_kr

        @pl.when((h == 0) & (m == 0))
        def _():
            for o in (o_gqn, o_gqr, o_gkn, o_gkr):
                o[...] = jnp.zeros_like(o)

        for o, g in zip((o_gqn, o_gqr, o_gkn, o_gkr), (g1, g2, g3, g4)):
            o[...] += g

    hb = lambda c: pl.BlockSpec((None, bm, c), lambda m, h: (h, m, 0))
    rb = lambda c: pl.BlockSpec((bm, c), lambda m, h: (m, 0))
    wb = lambda s: pl.BlockSpec(tuple(s), lambda m, h: (0, 0))
    return pl.pallas_call(
        body, name=name, grid=(T // bm, H),
        in_specs=[hb(QK), hb(NOPE + VDIM), rb(lat.shape[1]), rb(ROPE), rb(ROPE), wb(rmat.shape)]
        + [wb(g.shape) for g in gains] + [hb(QK), hb(QK), hb(VDIM)],
        out_specs=[hb(QK), hb(NOPE + VDIM), rb(ROPE)] + [wb(g.shape) for g in gains],
        out_shape=[jax.ShapeDtypeStruct((H, T, QK), BF16), jax.ShapeDtypeStruct((H, T, NOPE + VDIM), BF16),
                   jax.ShapeDtypeStruct((T, ROPE), F32)] + [jax.ShapeDtypeStruct(g.shape, F32) for g in gains],
        compiler_params=_params(("arbitrary", "arbitrary")),
    )(q_raw, kv_raw, lat, cos, sin, rmat, *gains, dq, dk, dv)


_NT = (((1,), (1,)), ((), ()))
_TN = (((0,), (0,)), ((), ()))


def _causal(blk):
    return lax.broadcasted_iota(jnp.int32, (blk, blk), 1) <= lax.broadcasted_iota(jnp.int32, (blk, blk), 0)


def _attn_fwd(q, k, v, seq, name, blk=512):
    H, T, _ = q.shape
    nb = T // seq
    blk = _pick(seq, blk)
    nq = seq // blk
    scale = float(QK) ** -0.5

    def body(q_ref, k_ref, v_ref, o_ref, lse_ref):
        qi = pl.program_id(2)
        qb = q_ref[...]

        def step(j, carry, diagonal):
            m, l, acc = carry
            ks = pl.ds(pl.multiple_of(j * blk, blk), blk)
            s = lax.dot_general(qb, k_ref[ks, :], _NT, preferred_element_type=F32) * scale
            if diagonal:
                s = jnp.where(_causal(blk), s, -jnp.inf)
            m_new = jnp.maximum(m, jnp.max(s, axis=1, keepdims=True))
            pr = jnp.exp(s - m_new)
            alpha = jnp.exp(m - m_new)
            l = alpha * l + jnp.sum(pr, axis=1, keepdims=True)
            acc = alpha * acc + jnp.dot(pr.astype(BF16), v_ref[ks, :], preferred_element_type=F32)
            return m_new, l, acc

        init = (jnp.full((blk, 1), -jnp.inf, F32), jnp.zeros((blk, 1), F32), jnp.zeros((blk, VDIM), F32))
        below = lax.fori_loop(0, qi, lambda j, c: step(j, c, False), init)
        m, l, acc = step(qi, below, True)
        o_ref[...] = (acc / l).astype(o_ref.dtype)
        lse_ref[...] = m + jnp.log(l)

    return pl.pallas_call(
        body, name=name, grid=(H, nb, nq),
        in_specs=[pl.BlockSpec((None, blk, QK), lambda h, b, i: (h, b * nq + i, 0)),
                  pl.BlockSpec((None, seq, QK), lambda h, b, i: (h, b, 0)),
                  pl.BlockSpec((None, seq, VDIM), lambda h, b, i: (h, b, 0))],
        out_specs=[pl.BlockSpec((blk, VDIM), lambda h, b, i: (b * nq + i, h)),
                   pl.BlockSpec((None, blk, 1), lambda h, b, i: (h, b * nq + i, 0))],
        out_shape=[jax.ShapeDtypeStruct((T, H * VDIM), BF16), jax.ShapeDtypeStruct((H, T, 1), F32)],
        compiler_params=_params(("parallel", "parallel", "parallel")),
    )(q, k, v)


def _attn_bwd(q, k, v, o, do, lse, seq, name, blk=512):
    H, T, _ = q.shape
    nb = T // seq
    blk = _pick(seq, blk)
    nq = seq // blk
    scale = float(QK) ** -0.5

    def body(q_ref, k_ref, v_ref, o_ref, do_ref, lse_ref, dq_ref, dk_ref, dv_ref):
        dk_ref[...] = jnp.zeros_like(dk_ref)
        dv_ref[...] = jnp.zeros_like(dv_ref)

        def qloop(i, carry):
            qs = pl.ds(pl.multiple_of(i * blk, blk), blk)
            qb = q_ref[qs, :]
            dof = do_ref[qs, :]
            dob = dof.astype(BF16)
            lse_b = lse_ref[qs, :]
            delta = jnp.sum(dof * o_ref[qs, :].astype(F32), axis=1, keepdims=True)

            def kstep(j, dq_acc, diagonal):
                ks = pl.ds(pl.multiple_of(j * blk, blk), blk)
                kb = k_ref[ks, :]
                vb = v_ref[ks, :]
                s = lax.dot_general(qb, kb, _NT, preferred_element_type=F32) * scale
                pr = jnp.exp(s - lse_b)
                if diagonal:
                    pr = jnp.where(_causal(blk), pr, 0.0)
                dp = lax.dot_general(dob, vb, _NT, preferred_element_type=F32)
                ds = (pr * (dp - delta) * scale).astype(BF16)
                prb = pr.astype(BF16)
                dv_ref[ks, :] += lax.dot_general(prb, dob, _TN, preferred_element_type=F32)
                dk_ref[ks, :] += lax.dot_general(ds, qb, _TN, preferred_element_type=F32)
                return dq_acc + jnp.dot(ds, kb, preferred_element_type=F32)

            below = lax.fori_loop(0, i, lambda j, c: kstep(j, c, False), jnp.zeros((blk, QK), F32))
            dq_ref[qs, :] = kstep(i, below, True)
            return carry

        lax.fori_loop(0, nq, qloop, 0)

    hb = lambda c: pl.BlockSpec((None, seq, c), lambda h, b: (h, b, 0))
    cb = lambda: pl.BlockSpec((seq, VDIM), lambda h, b: (b, h))
    return pl.pallas_call(
        body, name=name, grid=(H, nb),
        in_specs=[hb(QK), hb(QK), hb(VDIM), cb(), cb(), hb(1)],
        out_specs=[hb(QK), hb(QK), hb(VDIM)],
        out_shape=[jax.ShapeDtypeStruct((H, T, QK), F32), jax.ShapeDtypeStruct((H, T, QK), F32),
                   jax.ShapeDtypeStruct((H, T, VDIM), F32)],
        compiler_params=_params(("parallel", "parallel")),
    )(q, k, v, o, do, lse)


def _gelu_ln_fwd(z, g, b, name):
    half = z.shape[1] // 2

    def fn(zz, gg, bb):
        return _gelu(zz[:, :half]), _layer_norm(_gelu(zz[:, half:]), gg, bb)

    return _rowcall(fn, [z], [g, b], [(half, F32), (half, BF16)], name=name)


def _gelu_ln_bwd(z, d_u, d_vn, g, b, name):
    half = z.shape[1] // 2

    def gelu_and_slope(x):
        c, a = math.sqrt(2.0 / math.pi), 0.044715
        x2 = x * x
        t = jnp.tanh(c * x * (1.0 + a * x2))
        return 0.5 * x * (1.0 + t), 0.5 * (1.0 + t) + 0.5 * x * (1.0 - t * t) * (c * (1.0 + 3.0 * a * x2))

    def fn(zz, du, dvn, gg, bb):
        _, su = gelu_and_slope(zz[:, :half])
        v, sv = gelu_and_slope(zz[:, half:])
        xc = v - jnp.mean(v, axis=-1, keepdims=True)
        rstd = lax.rsqrt(jnp.mean(xc * xc, axis=-1, keepdims=True) + EPS)
        y = xc * rstd
        dy = dvn * gg
        dv = rstd * (dy - jnp.mean(dy, axis=-1, keepdims=True) - y * jnp.mean(dy * y, axis=-1, keepdims=True))
        dg = jnp.sum(dvn * y, axis=0, keepdims=True)
        db = jnp.sum(dvn, axis=0, keepdims=True)
        return jnp.concatenate([du * su, dv * sv], axis=1), dg, db

    return _rowcall(fn, [z, d_u, d_vn], [g, b], [(z.shape[1], BF16)], [g.shape, b.shape], bm=128, name=name)


def _tril_bf16(ws):
    t = lax.broadcasted_iota(jnp.int32, ws.shape, 0)
    s = lax.broadcasted_iota(jnp.int32, ws.shape, 1)
    return jnp.where(s <= t, ws, 0.0).astype(BF16)


def _sgu_fwd(u, vn, ws, bs, name, bm=512):
    T, half = u.shape
    gd = half // GROUPS
    bm = _pick(T, bm, CHUNK)
    nc = bm // CHUNK

    def body(u_ref, vn_ref, ws_ref, bs_ref, y_ref):
        wm = _tril_bf16(ws_ref[...])
        bias = bs_ref[...]
        for c in range(nc):
            rs = slice(c * CHUNK, (c + 1) * CHUNK)
            sv = jnp.dot(wm, vn_ref[rs, :], preferred_element_type=F32) + bias
            y_ref[rs, :] = (u_ref[rs, :] * sv).astype(y_ref.dtype)

    tb = lambda: pl.BlockSpec((bm, gd), lambda g, i: (i, g))
    return pl.pallas_call(
        body, name=name, grid=(GROUPS, T // bm),
        in_specs=[tb(), tb(), pl.BlockSpec((None, CHUNK, CHUNK), lambda g, i: (g, 0, 0)),
                  pl.BlockSpec((None, CHUNK, 1), lambda g, i: (g, 0, 0))],
        out_specs=tb(),
        out_shape=jax.ShapeDtypeStruct((T, half), BF16),
        compiler_params=_params(("parallel", "parallel")),
    )(u, vn, ws, bs)


def _sgu_bwd(u, vn, dy, ws, bs, name, bm=512):
    T, half = u.shape
    gd = half // GROUPS
    bm = _pick(T, bm, CHUNK)
    nc = bm // CHUNK

    def body(u_ref, vn_ref, dy_ref, ws_ref, bs_ref, du_ref, dvn_ref, dws_ref, dbs_ref):
        @pl.when(pl.program_id(1) == 0)
        def _():
            dws_ref[...] = jnp.zeros_like(dws_ref)
            dbs_ref[...] = jnp.zeros_like(dbs_ref)

        wm = _tril_bf16(ws_ref[...])
        bias = bs_ref[...]
        dws = jnp.zeros((CHUNK, CHUNK), F32)
        dbs = jnp.zeros((CHUNK, 1), F32)
        for c in range(nc):
            rs = slice(c * CHUNK, (c + 1) * CHUNK)
            vb = vn_ref[rs, :]
            dyb = dy_ref[rs, :]
            sv = jnp.dot(wm, vb, preferred_element_type=F32) + bias
            du_ref[rs, :] = dyb * sv
            dsv = dyb * u_ref[rs, :]
            dsb = dsv.astype(BF16)
            dvn_ref[rs, :] = lax.dot_general(wm, dsb, _TN, preferred_element_type=F32)
            dws = dws + lax.dot_general(dsb, vb, _NT, preferred_element_type=F32)
            dbs = dbs + jnp.sum(dsv, axis=1, keepdims=True)
        t = lax.broadcasted_iota(jnp.int32, (CHUNK, CHUNK), 0)
        s = lax.broadcasted_iota(jnp.int32, (CHUNK, CHUNK), 1)
        dws_ref[...] += jnp.where(s <= t, dws, 0.0)
        dbs_ref[...] += dbs

    tb = lambda: pl.BlockSpec((bm, gd), lambda g, i: (i, g))
    wsb = lambda: pl.BlockSpec((None, CHUNK, CHUNK), lambda g, i: (g, 0, 0))
    bsb = lambda: pl.BlockSpec((None, CHUNK, 1), lambda g, i: (g, 0, 0))
    return pl.pallas_call(
        body, name=name, grid=(GROUPS, T // bm),
        in_specs=[tb(), tb(), tb(), wsb(), bsb()],
        out_specs=[tb(), tb(), wsb(), bsb()],
        out_shape=[jax.ShapeDtypeStruct((T, half), F32), jax.ShapeDtypeStruct((T, half), F32),
                   jax.ShapeDtypeStruct(ws.shape, F32), jax.ShapeDtypeStruct(bs.shape, F32)],
        compiler_params=_params(("parallel", "arbitrary")),
    )(u, vn, dy, ws, bs)


def _loss_head(y, t, name):
    d_model = y.shape[1]

    def fn(yy, tt):
        d = yy - tt
        part = 0.5 * jnp.sum(jnp.mean(d * d, axis=-1, keepdims=True), axis=0, keepdims=True)
        return d / d_model, jnp.zeros((1, LANE), F32) + part

    dy, part = _rowcall(fn, [y, t], [], [(d_model, F32)], [(1, LANE)], name=name)
    return dy, part[0, 0]


def _adamw(parts, w, m, v, prev, layer, name):
    L, R, C = w.shape
    br = _pick(R, max(8, (128 * 1024) // C // 8 * 8), 8)
    c1 = 1.0 - B1 ** STEP
    c2 = 1.0 - B2 ** STEP
    if prev is None:
        prev = [lax.empty(w.shape, F32) for _ in range(4)]

    def body(p_ref, w_ref, m_ref, v_ref, a0, a1, a2, a3, g_o, d_o, m_o, v_o):
        g = p_ref[0].astype(F32)
        for d in range(1, N_DEV):
            g = g + p_ref[d].astype(F32)
        mn = B1 * m_ref[...] + (1.0 - B1) * g
        vn = B2 * v_ref[...] + (1.0 - B2) * (g * g)
        g_o[...] = g
        m_o[...] = mn
        v_o[...] = vn
        d_o[...] = -LR * ((mn / c1) / (jnp.sqrt(vn / c2) + ADAM_EPS) + WD * w_ref[...])

    blk = lambda: pl.BlockSpec((None, br, C), lambda i: (layer, i, 0))
    anywhere = pl.BlockSpec(memory_space=pl.ANY)
    return pl.pallas_call(
        body, name=name, grid=(R // br,),
        in_specs=[pl.BlockSpec((N_DEV, br, C), lambda i: (0, i, 0)), blk(), blk(), blk()] + [anywhere] * 4,
        out_specs=[blk(), blk(), blk(), blk()],
        out_shape=[jax.ShapeDtypeStruct((L, R, C), F32)] * 4,
        input_output_aliases={4: 0, 5: 1, 6: 2, 7: 3},
        compiler_params=_params(("parallel",)),
    )(parts, w, m, v, *prev)


def _mesh_pos():
    return lax.axis_index("x"), lax.axis_index("y"), lax.axis_index("c")


def _flip(pos, k):
    x, y, c = pos
    px = 1 - x if k & 4 else x
    py = 1 - y if k & 2 else y
    pc = 1 - c if k & 1 else c
    return px, py, pc


HBM_SPEC = pl.BlockSpec(memory_space=pltpu.HBM)
SEM_SPEC = pl.BlockSpec(memory_space=pltpu.SEMAPHORE)
EFFECT = pltpu.SideEffectType.DATAFLOW_SIDE_EFFECTING


def _hbm(a):
    return pltpu.with_memory_space_constraint(a, pltpu.HBM)


def _device_index():
    x, y, c = _mesh_pos()
    return 4 * x + 2 * y + c


def _peer_copy(src, land, send, recv, a, k, pos, scatter):
    peer = _flip(pos, k)
    me = 4 * pos[0] + 2 * pos[1] + pos[2]
    piece = src.at[4 * peer[0] + 2 * peer[1] + peer[2]] if scatter else src
    return pltpu.make_async_remote_copy(
        src_ref=piece, dst_ref=land.at[me], send_sem=send.at[7 * a + k - 1], recv_sem=recv.at[7 * a + k - 1],
        device_id=peer, device_id_type=pl.DeviceIdType.MESH)


def _xchg_start(srcs, scatter, after, name):
    n = len(srcs)

    def body(*refs):
        src, land = refs[:n], refs[n:2 * n]
        send, recv, token = refs[2 * n + 1], refs[2 * n + 2], refs[-1]
        pos = _mesh_pos()
        for k in range(1, N_DEV):
            for a in range(n):
                _peer_copy(src[a], land[a], send, recv, a, k, pos, scatter).start()
        token[...] = jnp.zeros_like(token)

    lands = [lax.empty(s.shape if scatter else (N_DEV,) + s.shape, s.dtype) for s in srcs]
    outs = pl.pallas_call(
        body, name=name,
        out_shape=(pltpu.SemaphoreType.DMA((7 * n,)), pltpu.SemaphoreType.DMA((7 * n,)),
                   *[pltpu.HBM(s.shape, s.dtype) for s in srcs], *[pltpu.HBM(l.shape, l.dtype) for l in lands],
                   jax.ShapeDtypeStruct((8, LANE), F32)),
        in_specs=[HBM_SPEC] * (2 * n) + [pl.BlockSpec(memory_space=pl.ANY)],
        out_specs=(SEM_SPEC, SEM_SPEC, *[HBM_SPEC] * (2 * n), pl.BlockSpec(memory_space=pltpu.VMEM)),
        input_output_aliases={q: 2 + q for q in range(2 * n)},
        compiler_params=pltpu.CompilerParams(has_side_effects=EFFECT),
    )(*[_hbm(s) for s in srcs], *[_hbm(l) for l in lands], after)
    handle = dict(send=outs[0], recv=outs[1], srcs=list(outs[2:2 + n]), lands=list(outs[2 + n:2 + 2 * n]),
                  scatter=scatter)
    return handle, outs[-1]


def _xchg_wait(handle, after, name):
    srcs, lands, scatter = handle['srcs'], handle['lands'], handle['scatter']
    n = len(srcs)

    def body(*refs):
        src, land = refs[:n], refs[n:2 * n]
        send, recv = refs[2 * n], refs[2 * n + 1]
        pos = _mesh_pos()
        for k in range(1, N_DEV):
            for a in range(n):
                cp = _peer_copy(src[a], land[a], send, recv, a, k, pos, scatter)
                cp.wait_send()
                cp.wait_recv()

    outs = pl.pallas_call(
        body, name=name,
        out_shape=[pltpu.HBM(s.shape, s.dtype) for s in srcs] + [pltpu.HBM(l.shape, l.dtype) for l in lands],
        in_specs=[HBM_SPEC] * (2 * n) + [SEM_SPEC, SEM_SPEC, pl.BlockSpec(memory_space=pl.ANY)],
        out_specs=[HBM_SPEC] * (2 * n),
        input_output_aliases={q: q for q in range(2 * n)},
        compiler_params=pltpu.CompilerParams(has_side_effects=EFFECT),
    )(*srcs, *lands, handle['send'], handle['recv'], after)
    me = _device_index()
    full = []
    for src, land in zip(outs[:n], outs[n:]):
        mine = lax.dynamic_index_in_dim(src, me, 0, keepdims=False) if scatter else src
        full.append(lax.dynamic_update_index_in_dim(land, mine, me, 0))
    return full


def _pack(parts):
    flat = jnp.concatenate([q.reshape(-1) for q in parts])
    pad = (-flat.shape[0]) % (8 * LANE)
    return jnp.pad(flat, (0, pad)).reshape(-1, LANE)


def _unpack(packed, like):
    flat = packed.reshape(-1)
    out, o = [], 0
    for q in like:
        out.append(flat[o:o + q.size].reshape(q.shape))
        o += q.size
    return out


def kernel(x, p, positions, norm_mix, norm_ffn, norm_ple, mla_w_down, mla_q_lora_g, mla_kv_lora_g, mla_w_uq, mla_w_ukv, mla_q_nope_g, mla_q_rope_g, mla_k_nope_g, mla_k_rope_g, mla_w_out, gmlp_w_in, gmlp_ln_g, gmlp_ln_b, gmlp_w_s, gmlp_b_s, gmlp_w_out, ffn_w_up, ffn_w_down, ple_w_gate, ple_w_proj, loss_target, m_norm_mix, m_norm_ffn, m_norm_ple, m_mla_w_down, m_mla_q_lora_g, m_mla_kv_lora_g, m_mla_w_uq, m_mla_w_ukv, m_mla_q_nope_g, m_mla_q_rope_g, m_mla_k_nope_g, m_mla_k_rope_g, m_mla_w_out, m_gmlp_w_in, m_gmlp_ln_g, m_gmlp_ln_b, m_gmlp_w_s, m_gmlp_b_s, m_gmlp_w_out, m_ffn_w_up, m_ffn_w_down, m_ple_w_gate, m_ple_w_proj, v_norm_mix, v_norm_ffn, v_norm_ple, v_mla_w_down, v_mla_q_lora_g, v_mla_kv_lora_g, v_mla_w_uq, v_mla_w_ukv, v_mla_q_nope_g, v_mla_q_rope_g, v_mla_k_nope_g, v_mla_k_rope_g, v_mla_w_out, v_gmlp_w_in, v_gmlp_ln_g, v_gmlp_ln_b, v_gmlp_w_s, v_gmlp_b_s, v_gmlp_w_out, v_ffn_w_up, v_ffn_w_down, v_ple_w_gate, v_ple_w_proj):
    W = dict(zip(WEIGHTS, (norm_mix, norm_ffn, norm_ple, mla_w_down, mla_q_lora_g, mla_kv_lora_g, mla_w_uq, mla_w_ukv, mla_q_nope_g, mla_q_rope_g, mla_k_nope_g, mla_k_rope_g, mla_w_out, gmlp_w_in, gmlp_ln_g, gmlp_ln_b, gmlp_w_s, gmlp_b_s, gmlp_w_out, ffn_w_up, ffn_w_down, ple_w_gate, ple_w_proj)))
    M1 = dict(zip(WEIGHTS, (m_norm_mix, m_norm_ffn, m_norm_ple, m_mla_w_down, m_mla_q_lora_g, m_mla_kv_lora_g, m_mla_w_uq, m_mla_w_ukv, m_mla_q_nope_g, m_mla_q_rope_g, m_mla_k_nope_g, m_mla_k_rope_g, m_mla_w_out, m_gmlp_w_in, m_gmlp_ln_g, m_gmlp_ln_b, m_gmlp_w_s, m_gmlp_b_s, m_gmlp_w_out, m_ffn_w_up, m_ffn_w_down, m_ple_w_gate, m_ple_w_proj)))
    M2 = dict(zip(WEIGHTS, (v_norm_mix, v_norm_ffn, v_norm_ple, v_mla_w_down, v_mla_q_lora_g, v_mla_kv_lora_g, v_mla_w_uq, v_mla_w_ukv, v_mla_q_nope_g, v_mla_q_rope_g, v_mla_k_nope_g, v_mla_k_rope_g, v_mla_w_out, v_gmlp_w_in, v_gmlp_ln_g, v_gmlp_ln_b, v_gmlp_w_s, v_gmlp_b_s, v_gmlp_w_out, v_ffn_w_up, v_ffn_w_down, v_ple_w_gate, v_ple_w_proj)))

    nb, seq, d_model = x.shape
    T = nb * seq
    depth = norm_mix.shape[0]
    h = x.reshape(T, d_model)
    target = loss_target.reshape(T, d_model)
    p_bf = p.reshape(depth, T, p.shape[-1]).astype(BF16)

    def stage_keys(st):
        i, second = divmod(st, 2)
        if second:
            return [(n, i) for n in ('ffn_w_up', 'ffn_w_down', 'ple_w_gate', 'ple_w_proj')]
        mix = (['mla_w_down', 'mla_w_uq', 'mla_w_ukv', 'mla_w_out'] if i % 2 == 0 else
               ['gmlp_w_in', 'gmlp_ln_g', 'gmlp_ln_b', 'gmlp_w_out'])
        return [(n, i // 2) for n in mix]

    FW = {n: {} for n in SHARDED}

    def start_weights(st, after):
        keys = stage_keys(st)
        srcs = [W[n][l] if n in F32_PAYLOAD else W[n][l].astype(BF16) for n, l in keys]
        handle, token = _xchg_start(srcs, False, after, "weights_start%d" % st)
        return (keys, handle), token[0, 0]

    def wait_weights(pending, st, after):
        keys, handle = pending
        for (n, l), full in zip(keys, _xchg_wait(handle, after, "weights_wait%d" % st)):
            if SHARD_AXIS[n] == 1:
                FW[n][l] = full.reshape((-1,) + full.shape[2:])
            elif n in ('mla_w_uq', 'mla_w_ukv'):
                FW[n][l] = full
            else:
                FW[n][l] = jnp.transpose(full, (1, 0, 2)).reshape(full.shape[1], -1)

    row = lambda a: a.reshape(1, -1)
    cos, sin = _rope_tables(positions.reshape(T, 1), "rope_tables")
    rmat = _rot_matrix()

    saved = []
    pending, _ = start_weights(0, h)
    for i in range(depth):
        j = i // 2
        wait_weights(pending, 2 * i, h)
        pending, token = start_weights(2 * i + 1, h)
        s = {}
        s['h0'] = h
        hn = _rms_fwd(h, row(W['norm_mix'][i]) + token, "rms_fwd")
        s['hn'] = hn
        if i % 2 == 0:
            gains = [row(W['mla_q_nope_g'][j]), row(W['mla_q_rope_g'][j]), row(W['mla_k_nope_g'][j]),
                     row(W['mla_k_rope_g'][j])]
            lat = _mm(hn, FW['mla_w_down'][j], name="mla_down")
            cq, ckv = _prep1_fwd(lat, row(W['mla_q_lora_g'][j]), row(W['mla_kv_lora_g'][j]), "mla_prep1")
            q_raw = _mm(cq, FW['mla_w_uq'][j], out_blocks=HEADS, name="mla_uq")
            kv_raw = _mm(ckv, FW['mla_w_ukv'][j], out_blocks=HEADS, name="mla_ukv")
            q, k, v = _prep2_fwd(q_raw, kv_raw, lat, cos, sin, rmat, gains, "mla_prep2")
            o, lse = _attn_fwd(q, k, v, seq, "attn_fwd")
            h = _mm(o, FW['mla_w_out'][j], extras=(h,), epilogue=lambda acc, res: (res + acc,), name="mla_out")
            s.update(lat=lat, cq=cq, ckv=ckv, q_raw=q_raw, kv_raw=kv_raw, q=q, k=k, v=v, o=o, lse=lse, gains=gains)
        else:
            z = _mm(hn, FW['gmlp_w_in'][j], name="gmlp_in")
            u, vn = _gelu_ln_fwd(z, row(FW['gmlp_ln_g'][j]), row(FW['gmlp_ln_b'][j]), "gmlp_gelu_ln")
            bs3 = W['gmlp_b_s'][j][:, :, None]
            y = _sgu_fwd(u, vn, W['gmlp_w_s'][j], bs3, "gmlp_sgu")
            h = _mm(y, FW['gmlp_w_out'][j], extras=(h,), epilogue=lambda acc, res: (res + acc,), name="gmlp_out")
            s.update(z=z, u=u, vn=vn, y=y, bs3=bs3)
        s['h1'] = h
        wait_weights(pending, 2 * i + 1, h)
        token = 0.0
        if i + 1 < depth:
            pending, token = start_weights(2 * i + 2, h)
        hn2 = _rms_fwd(h, row(W['norm_ffn'][i]) + token, "rms_fwd")
        a, r = _mm(hn2, FW['ffn_w_up'][i], epilogue=lambda acc: (acc, jnp.square(jnp.maximum(acc, 0.0))),
                   out_dtypes=(BF16, BF16), name="ffn_up")
        h = _mm(r, FW['ffn_w_down'][i], extras=(h,), epilogue=lambda acc, res: (res + acc,), name="ffn_down")
        s.update(hn2=hn2, a=a, r=r, h2=h)
        hn3 = _rms_fwd(h, row(W['norm_ple'][i]), "rms_fwd")
        gt = _mm(hn3, FW['ple_w_gate'][i], name="ple_gate")
        pp, h = _mm(p_bf[i], FW['ple_w_proj'][i], extras=(gt, h),
                    epilogue=lambda acc, g_, res: (acc, res + _sigmoid(g_) * acc), out_dtypes=(F32, F32),
                    name="ple_proj")
        s.update(hn3=hn3, gt=gt, pp=pp)
        saved.append(s)

    dh, loss_part = _loss_head(h, target, "loss_head")
    loss = lax.psum(loss_part, MESH_AXES)

    G = {n: [None] * W[n].shape[0] for n in REPLICATED}
    res = {}
    flying = []

    def shard3(n):
        shp = W[n].shape
        return shp[0], int(np.prod(shp[1:-1])), shp[-1]

    def by_owner(g):
        return g.reshape((N_DEV, g.shape[0] // N_DEV) + g.shape[1:])

    def send_grads(tag, grads):
        handle, token = _xchg_start([g for _, g in grads], True, grads[-1][1], "grads_start_" + tag)
        flying.append((tag, [key for key, _ in grads], handle))
        return token[0, 0]

    def land_grads(after):
        tag, keys, handle = flying.pop(0)
        for (n, l), full in zip(keys, _xchg_wait(handle, after, "grads_wait_" + tag)):
            dims = shard3(n)
            res[n] = _adamw(full.reshape((N_DEV,) + dims[1:]), W[n].reshape(dims), M1[n].reshape(dims),
                            M2[n].reshape(dims), res.get(n), l, "adamw_" + n)

    def start_small(names, tag, after):
        handle, _ = _xchg_start([_pack([jnp.stack(G[n]) for n in names])], False, after, "small_start_" + tag)
        return names, handle

    def land_small(pending_small, tag, after):
        names, handle = pending_small
        (parts,) = _xchg_wait(handle, after, "small_wait_" + tag)
        like = [W[n] for n in names]
        outs = _adamw(parts, _pack(like)[None], _pack([M1[n] for n in names])[None],
                      _pack([M2[n] for n in names])[None], None, 0, "adamw_small_" + tag)
        unpacked = [_unpack(o, like) for o in outs]
        for idx, n in enumerate(names):
            res[n] = [unpacked[q][idx] for q in range(4)]

    spatial = ['gmlp_w_s', 'gmlp_b_s']
    token = 0.0
    for i in reversed(range(depth)):
        j = i // 2
        s = saved[i]
        def ple_elem(d, g_, pq):
            sg = _sigmoid(g_)
            return d * sg, d * pq * sg * (1.0 - sg)

        d_pp, d_gt = _rowcall(ple_elem, [dh, s['gt'], s['pp']], [], [(d_model, BF16), (d_model, BF16)], name="ple_bwd")
        g_proj = _mm(p_bf[i], d_pp, ta=True, out_dtypes=(BF16,), name="ple_proj_dw")
        g_proj = jnp.transpose(g_proj.reshape(g_proj.shape[0], N_DEV, -1), (1, 0, 2))
        g_gate = _mm(s['hn3'], d_gt, ta=True, out_dtypes=(BF16,), name="ple_gate_dw")
        d_hn3 = _mm(d_gt, FW['ple_w_gate'][i], tb=True, name="ple_gate_dx")
        dh, dh_bf, dg = _rms_bwd(s['h2'], d_hn3, dh, row(W['norm_ple'][i]) + token, "rms_bwd")
        G['norm_ple'][i] = dg[0]
        d_a = _mm(dh_bf, FW['ffn_w_down'][i], tb=True, extras=(s['a'],),
                  epilogue=lambda acc, a_: (acc * (2.0 * jnp.maximum(a_.astype(F32), 0.0)),), out_dtypes=(BF16,),
                  name="ffn_down_dx")
        g_down = _mm(s['r'], dh_bf, ta=True, out_dtypes=(BF16,), name="ffn_down_dw")
        g_up = _mm(s['hn2'], d_a, ta=True, out_blocks=N_DEV, out_dtypes=(BF16,), name="ffn_up_dw")
        token = send_grads("mlp%d" % i, [(('ple_w_proj', i), g_proj), (('ple_w_gate', i), by_owner(g_gate)),
                                         (('ffn_w_down', i), by_owner(g_down)), (('ffn_w_up', i), g_up)])
        if len(flying) > 1:
            land_grads(g_up)
        d_hn2 = _mm(d_a, FW['ffn_w_up'][i], tb=True, name="ffn_up_dx")
        dh, dh_bf, dg = _rms_bwd(s['h1'], d_hn2, dh, row(W['norm_ffn'][i]) + token, "rms_bwd")
        G['norm_ffn'][i] = dg[0]
        if i % 2 == 0:
            d_o = _mm(dh_bf, FW['mla_w_out'][j], tb=True, name="mla_out_dx")
            g_out = _mm(s['o'], dh_bf, ta=True, out_dtypes=(BF16,), name="mla_out_dw")
            dq, dk, dv = _attn_bwd(s['q'], s['k'], s['v'], s['o'], d_o, s['lse'], seq, "attn_bwd")
            d_q_raw, d_kv_raw, d_kr, g1, g2, g3, g4 = _prep2_bwd(
                s['q_raw'], s['kv_raw'], s['lat'], cos, sin, rmat, s['gains'], dq, dk, dv, "mla_prep2_bwd")
            G['mla_q_nope_g'][j], G['mla_q_rope_g'][j] = g1[0], g2[0]
            G['mla_k_nope_g'][j], G['mla_k_rope_g'][j] = g3[0], g4[0]
            g_uq = _mm(s['cq'], d_q_raw, ta=True, out_blocks=N_DEV, out_dtypes=(BF16,), name="mla_uq_dw")
            g_ukv = _mm(s['ckv'], d_kv_raw, ta=True, out_blocks=N_DEV, out_dtypes=(BF16,), name="mla_ukv_dw")
            d_cq = _mm(d_q_raw, FW['mla_w_uq'][j], tb=True, name="mla_uq_dx")
            d_ckv = _mm(d_kv_raw, FW['mla_w_ukv'][j], tb=True, name="mla_ukv_dx")
            d_lat, dga, dgb = _prep1_bwd(s['lat'], d_cq, d_ckv, d_kr, row(W['mla_q_lora_g'][j]),
                                         row(W['mla_kv_lora_g'][j]), "mla_prep1_bwd")
            G['mla_q_lora_g'][j], G['mla_kv_lora_g'][j] = dga[0], dgb[0]
            g_down = _mm(s['hn'], d_lat, ta=True, out_dtypes=(BF16,), name="mla_down_dw")
            grads = [(('mla_w_out', j), by_owner(g_out)), (('mla_w_uq', j), g_uq), (('mla_w_ukv', j), g_ukv),
                     (('mla_w_down', j), by_owner(g_down))]
            d_hn = _mm(d_lat, FW['mla_w_down'][j], tb=True, name="mla_down_dx")
        else:
            d_y = _mm(dh_bf, FW['gmlp_w_out'][j], tb=True, name="gmlp_out_dx")
            g_out = _mm(s['y'], dh_bf, ta=True, out_dtypes=(BF16,), name="gmlp_out_dw")
            d_u, d_vn, d_ws, d_bs = _sgu_bwd(s['u'], s['vn'], d_y, W['gmlp_w_s'][j], s['bs3'], "gmlp_sgu_bwd")
            G['gmlp_w_s'][j], G['gmlp_b_s'][j] = d_ws, d_bs[:, :, 0]
            d_z, d_lg, d_lb = _gelu_ln_bwd(s['z'], d_u, d_vn, row(FW['gmlp_ln_g'][j]), row(FW['gmlp_ln_b'][j]),
                                           "gmlp_gelu_ln_bwd")
            g_in = _mm(s['hn'], d_z, ta=True, out_blocks=N_DEV, out_dtypes=(BF16,), name="gmlp_in_dw")
            grads = [(('gmlp_w_out', j), by_owner(g_out)), (('gmlp_ln_g', j), by_owner(d_lg[0])),
                     (('gmlp_ln_b', j), by_owner(d_lb[0])), (('gmlp_w_in', j), g_in)]
            d_hn = _mm(d_z, FW['gmlp_w_in'][j], tb=True, name="gmlp_in_dx")
        token = send_grads("mix%d" % i, grads)
        if len(flying) > 1:
            land_grads(grads[-1][1])
        dh, _, dg = _rms_bwd(s['h0'], d_hn, dh, row(W['norm_mix'][i]) + token, "rms_bwd")
        G['norm_mix'][i] = dg[0]
        if i == 1:
            small_a = start_small(spatial, "spatial", dh)
    grad_x = dh.reshape(x.shape)

    small_b = start_small([n for n in REPLICATED if n not in spatial], "gains", dh)
    while flying:
        land_grads(dh)
    land_small(small_a, "spatial", dh)
    land_small(small_b, "gains", dh)

    out = lambda q: [res[n][q].reshape(W[n].shape) for n in WEIGHTS]
    return (loss, grad_x, *out(0), *out(1), *out(2), *out(3))
```

```python
import math

import numpy as np
import jax
import jax.numpy as jnp
from jax import lax
from jax.experimental import pallas as pl
from jax.experimental.pallas import tpu as pltpu

F32 = jnp.float32
BF16 = jnp.bfloat16

N_DEV = 8
MESH_AXES = ("x", "y", "c")
HEADS = 8
NOPE = 128
ROPE = 64
VDIM = 128
QK = NOPE + ROPE
Q_LORA = 384
KV_LORA = 256
ROPE_BASE = 10000.0
CHUNK = 128
GROUPS = 8
EPS = 1e-6
LR, B1, B2, ADAM_EPS, WD, STEP = 0.001, 0.9, 0.999, 1e-08, 0.01, 10
LANE = 128
VMEM_LIMIT = 56 * 1024 * 1024
MM_VMEM_BUDGET = 40 * 1024 * 1024

WEIGHTS = ['norm_mix', 'norm_ffn', 'norm_ple', 'mla_w_down', 'mla_q_lora_g', 'mla_kv_lora_g', 'mla_w_uq',
           'mla_w_ukv', 'mla_q_nope_g', 'mla_q_rope_g', 'mla_k_nope_g', 'mla_k_rope_g', 'mla_w_out', 'gmlp_w_in',
           'gmlp_ln_g', 'gmlp_ln_b', 'gmlp_w_s', 'gmlp_b_s', 'gmlp_w_out', 'ffn_w_up', 'ffn_w_down', 'ple_w_gate',
           'ple_w_proj']
SHARD_AXIS = {'mla_w_down': 1, 'mla_w_uq': 2, 'mla_w_ukv': 2, 'mla_w_out': 1, 'gmlp_w_in': 2, 'gmlp_ln_g': 1,
              'gmlp_ln_b': 1, 'gmlp_w_out': 1, 'ffn_w_up': 2, 'ffn_w_down': 1, 'ple_w_gate': 1, 'ple_w_proj': 2}
SHARDED = list(SHARD_AXIS)
REPLICATED = [n for n in WEIGHTS if n not in SHARD_AXIS]
F32_PAYLOAD = ('gmlp_ln_g', 'gmlp_ln_b')


def _pick(dim, pref, align=LANE):
    if dim <= pref:
        return dim
    b = (pref // align) * align
    while b >= align:
        if dim % b == 0:
            return b
        b -= align
    return dim


def _params(sem):
    return pltpu.CompilerParams(dimension_semantics=sem, vmem_limit_bytes=VMEM_LIMIT)


def _mm(a, b, *, ta=False, tb=False, extras=(), epilogue=None, out_dtypes=(F32,), out_blocks=None, name,
        bn=1024):
    a3, b3 = a.ndim == 3, b.ndim == 3
    assert not (ta and a3)
    if ta:
        K, M = a.shape
        ka = K
    elif a3:
        M, ka = a.shape[1:]
        K = a.shape[0] * ka
    else:
        M, K = a.shape
        ka = K
    if tb:
        N, kb = b.shape[-2:]
        nb = N
        K2 = b.shape[0] * kb if b3 else kb
    else:
        kb, nb = b.shape[-2:]
        K2 = kb
        N = b.shape[0] * nb if b3 else nb
    assert K == K2, (a.shape, b.shape, ta, tb)
    no_ = N // out_blocks if out_blocks else N
    assert not (out_blocks and extras)
    size = lambda t: jnp.dtype(t).itemsize
    per_out = sum(size(e.dtype) for e in extras) + sum(size(t) for t in out_dtypes)
    bn = _pick(min(nb, no_), bn)
    k_lim = min(ka, kb)
    fits = lambda m, k: 2 * (m * k * size(a.dtype) + k * bn * size(b.dtype) + m * bn * per_out) + 4 * m * bn
    ms = [m for m in sorted({min(M, c) for c in (2048, 1024, 512, 256)}, reverse=True) if M % m == 0]
    ks = [k for k in dict.fromkeys((k_lim, 2048, 1024, 512, 256)) if k <= k_lim and k_lim % k == 0]
    bm, bk = next(((m, k) for k in ks for m in ms if fits(m, k) <= MM_VMEM_BUDGET), (ms[-1], ks[-1]))
    nk = K // bk
    ne, no = len(extras), len(out_dtypes)
    dims = (((0,) if ta else (1,), (1,) if tb else (0,)), ((), ()))

    def finish(r, e_refs, o_refs):
        outs = epilogue(r, *[e[...] for e in e_refs]) if epilogue is not None else (r,)
        for o, v in zip(o_refs, outs):
            o[...] = v.astype(o.dtype)

    def body(*refs):
        a_ref, b_ref = refs[0], refs[1]
        e_refs = refs[2:2 + ne]
        o_refs = refs[2 + ne:2 + ne + no]
        part = lax.dot_general(a_ref[...].astype(BF16), b_ref[...].astype(BF16), dims, preferred_element_type=F32)
        if nk == 1:
            finish(part, e_refs, o_refs)
            return
        acc = refs[-1]
        k = pl.program_id(2)

        @pl.when(k == 0)
        def _():
            acc[...] = part

        @pl.when(k > 0)
        def _():
            acc[...] += part

        @pl.when(k == nk - 1)
        def _():
            finish(acc[...], e_refs, o_refs)

    ka_t, kb_t, nb_t, no_t = ka // bk, kb // bk, nb // bn, no_ // bn
    if ta:
        a_spec = pl.BlockSpec((bk, bm), lambda i, j, k: (k, i))
    elif a3:
        a_spec = pl.BlockSpec((None, bm, bk), lambda i, j, k: (k // ka_t, i, k % ka_t))
    else:
        a_spec = pl.BlockSpec((bm, bk), lambda i, j, k: (i, k))
    if tb:
        b_spec = (pl.BlockSpec((None, bn, bk), lambda i, j, k: (k // kb_t, j, k % kb_t)) if b3 else
                  pl.BlockSpec((bn, bk), lambda i, j, k: (j, k)))
    else:
        b_spec = (pl.BlockSpec((None, bk, bn), lambda i, j, k: (j // nb_t, k, j % nb_t)) if b3 else
                  pl.BlockSpec((bk, bn), lambda i, j, k: (k, j)))
    if out_blocks:
        o_spec = lambda: pl.BlockSpec((None, bm, bn), lambda i, j, k: (j // no_t, i, j % no_t))
        o_shape = (out_blocks, M, no_)
    else:
        o_spec = lambda: pl.BlockSpec((bm, bn), lambda i, j, k: (i, j))
        o_shape = (M, N)
    outs = pl.pallas_call(
        body, name=name,
        grid=(M // bm, N // bn, nk),
        in_specs=[a_spec, b_spec] + [pl.BlockSpec((bm, bn), lambda i, j, k: (i, j)) for _ in extras],
        out_specs=[o_spec() for _ in out_dtypes],
        out_shape=[jax.ShapeDtypeStruct(o_shape, dt) for dt in out_dtypes],
        scratch_shapes=[pltpu.VMEM((bm, bn), F32)] if nk > 1 else [],
        compiler_params=_params(("parallel", "parallel", "arbitrary")),
    )(a, b, *extras)
    return outs[0] if no == 1 else outs


def _rowcall(fn, rows, params, row_outs, acc_outs=(), *, bm=256, name):
    T = rows[0].shape[0]
    bm = _pick(T, bm, 8)
    nr, npar, nro, nao = len(rows), len(params), len(row_outs), len(acc_outs)

    def body(*refs):
        vals = [r[...] for r in refs[:nr + npar]]
        res = fn(*vals)
        ro = refs[nr + npar:nr + npar + nro]
        ao = refs[nr + npar + nro:]
        for r, v in zip(ro, res[:nro]):
            r[...] = v.astype(r.dtype)
        if nao:
            @pl.when(pl.program_id(0) == 0)
            def _():
                for r in ao:
                    r[...] = jnp.zeros_like(r)

            for r, v in zip(ao, res[nro:]):
                r[...] += v

    def whole(shape):
        nd = len(shape)
        return pl.BlockSpec(tuple(shape), lambda i: (0,) * nd)

    outs = pl.pallas_call(
        body, name=name,
        grid=(T // bm,),
        in_specs=[pl.BlockSpec((bm, r.shape[1]), lambda i: (i, 0)) for r in rows] + [whole(q.shape) for q in params],
        out_specs=[pl.BlockSpec((bm, c), lambda i: (i, 0)) for c, _ in row_outs] + [whole(s) for s in acc_outs],
        out_shape=[jax.ShapeDtypeStruct((T, c), dt) for c, dt in row_outs]
        + [jax.ShapeDtypeStruct(tuple(s), F32) for s in acc_outs],
        compiler_params=_params(("arbitrary",) if nao else ("parallel",)),
    )(*rows, *params)
    return outs


def _rmsn(x, g):
    return x * lax.rsqrt(jnp.mean(x * x, axis=-1, keepdims=True) + EPS) * g


def _gelu(x):
    return 0.5 * x * (1.0 + jnp.tanh(math.sqrt(2.0 / math.pi) * (x + 0.044715 * (x * x * x))))


def _layer_norm(x, g, b):
    mu = jnp.mean(x, axis=-1, keepdims=True)
    xc = x - mu
    return xc * lax.rsqrt(jnp.mean(xc * xc, axis=-1, keepdims=True) + EPS) * g + b


def _sigmoid(x):
    return 1.0 / (1.0 + jnp.exp(-x))


def _rot(x, cos, sin, rmat):
    return x * cos + jnp.dot(x, rmat, precision=lax.Precision.HIGHEST, preferred_element_type=F32) * sin


def _rms_fwd(h, g, name):
    return _rowcall(lambda x, gg: (_rmsn(x, gg),), [h], [g], [(h.shape[1], BF16)], bm=512, name=name)[0]


def _rms_bwd(h, d_hn, dh_in, g, name):
    def fn(x, dy, dres, gg):
        _, vjp = jax.vjp(_rmsn, x, gg)
        dx, dg = vjp(dy)
        dh = dres + dx
        return dh, dh, dg

    d = h.shape[1]
    return _rowcall(fn, [h, d_hn, dh_in], [g], [(d, F32), (d, BF16)], [g.shape], bm=512, name=name)


def _rope_tables(pos, name):
    inv = np.float32(ROPE_BASE) ** (-(np.arange(0, ROPE, 2, dtype=np.float32) / np.float32(ROPE)))
    inv = jnp.asarray(np.concatenate([inv, inv])[None, :].astype(np.float32))

    def fn(pp, iv):
        ang = pp.astype(F32) * iv
        return jnp.cos(ang), jnp.sin(ang)

    return _rowcall(fn, [pos], [inv], [(ROPE, F32), (ROPE, F32)], name=name)


def _rot_matrix():
    r = np.zeros((ROPE, ROPE), np.float32)
    half = ROPE // 2
    for j in range(half):
        r[j + half, j] = -1.0
        r[j, j + half] = 1.0
    return jnp.asarray(r)


def _prep1_fwd(lat, gq, gkv, name):
    def fn(l, a, b):
        return _rmsn(l[:, :Q_LORA], a), _rmsn(l[:, Q_LORA:Q_LORA + KV_LORA], b)

    return _rowcall(fn, [lat], [gq, gkv], [(Q_LORA, BF16), (KV_LORA, BF16)], name=name)


def _prep1_bwd(lat, d_cq, d_ckv, d_kr, gq, gkv, name):
    def fn(l, dq, dkv, dkr, a, b):
        _, vq = jax.vjp(_rmsn, l[:, :Q_LORA], a)
        _, vkv = jax.vjp(_rmsn, l[:, Q_LORA:Q_LORA + KV_LORA], b)
        dxq, dga = vq(dq)
        dxkv, dgb = vkv(dkv)
        return jnp.concatenate([dxq, dxkv, dkr], axis=1), dga, dgb

    return _rowcall(fn, [lat, d_cq, d_ckv, d_kr], [gq, gkv], [(lat.shape[1], BF16)], [gq.shape, gkv.shape], name=name)


def _qk_fn(qn_raw, qr_raw, kn_raw, kr_raw, gqn, gqr, gkn, gkr, cos, sin, rmat):
    return (_rmsn(qn_raw, gqn), _rot(_rmsn(qr_raw, gqr), cos, sin, rmat),
            _rmsn(kn_raw, gkn), _rot(_rmsn(kr_raw, gkr), cos, sin, rmat))


def _prep2_fwd(q_raw, kv_raw, lat, cos, sin, rmat, gains, name, bm=1024):
    H, T, _ = q_raw.shape
    bm = _pick(T, bm, 8)
    kr0 = Q_LORA + KV_LORA

    def body(q_ref, kv_ref, lat_ref, cos_ref, sin_ref, r_ref, gqn, gqr, gkn, gkr, qo, ko, vo):
        qr, kvr = q_ref[...], kv_ref[...]
        qn, qro, kn, kro = _qk_fn(qr[:, :NOPE], qr[:, NOPE:], kvr[:, :NOPE], lat_ref[:, kr0:kr0 + ROPE],
                                  gqn[...], gqr[...], gkn[...], gkr[...], cos_ref[...], sin_ref[...], r_ref[...])
        qo[:, :NOPE] = qn.astype(BF16)
        qo[:, NOPE:] = qro.astype(BF16)
        ko[:, :NOPE] = kn.astype(BF16)
        ko[:, NOPE:] = kro.astype(BF16)
        vo[...] = kvr[:, NOPE:].astype(BF16)

    hb = lambda c: pl.BlockSpec((None, bm, c), lambda m, h: (h, m, 0))
    rb = lambda c: pl.BlockSpec((bm, c), lambda m, h: (m, 0))
    wb = lambda s: pl.BlockSpec(tuple(s), lambda m, h: (0, 0))
    return pl.pallas_call(
        body, name=name, grid=(T // bm, H),
        in_specs=[hb(QK), hb(NOPE + VDIM), rb(lat.shape[1]), rb(ROPE), rb(ROPE), wb(rmat.shape)]
        + [wb(g.shape) for g in gains],
        out_specs=[hb(QK), hb(QK), hb(VDIM)],
        out_shape=[jax.ShapeDtypeStruct((H, T, QK), BF16), jax.ShapeDtypeStruct((H, T, QK), BF16),
                   jax.ShapeDtypeStruct((H, T, VDIM), BF16)],
        compiler_params=_params(("parallel", "parallel")),
    )(q_raw, kv_raw, lat, cos, sin, rmat, *gains)


def _prep2_bwd(q_raw, kv_raw, lat, cos, sin, rmat, gains, dq, dk, dv, name, bm=512):
    H, T, _ = q_raw.shape
    bm = _pick(T, bm, 8)
    kr0 = Q_LORA + KV_LORA

    def body(q_ref, kv_ref, lat_ref, cos_ref, sin_ref, r_ref, gqn, gqr, gkn, gkr, dq_ref, dk_ref, dv_ref,
             dqo, dkvo, dkro, o_gqn, o_gqr, o_gkn, o_gkr):
        m, h = pl.program_id(0), pl.program_id(1)
        qr, kvr = q_ref[...], kv_ref[...]
        cos_v, sin_v, r_v = cos_ref[...], sin_ref[...], r_ref[...]
        f = lambda a, b, c, d, g1, g2, g3, g4: _qk_fn(a, b, c, d, g1, g2, g3, g4, cos_v, sin_v, r_v)
        _, vjp = jax.vjp(f, qr[:, :NOPE], qr[:, NOPE:], kvr[:, :NOPE], lat_ref[:, kr0:kr0 + ROPE],
                         gqn[...], gqr[...], gkn[...], gkr[...])
        dqv, dkv_ = dq_ref[...], dk_ref[...]
        d_qn, d_qr, d_kn, d_kr, g1, g2, g3, g4 = vjp((dqv[:, :NOPE], dqv[:, NOPE:], dkv_[:, :NOPE], dkv_[:, NOPE:]))
        dqo[:, :NOPE] = d_qn.astype(BF16)
        dqo[:, NOPE:] = d_qr.astype(BF16)
        dkvo[:, :NOPE] = d_kn.astype(BF16)
        dkvo[:, NOPE:] = dv_ref[...].astype(BF16)

        @pl.when(h == 0)
        def _():
            dkro[...] = jnp.zeros_like(dkro)

        dkro[...] += d_kr

        @pl.when((h == 0) & (m == 0))
        def _():
            for o in (o_gqn, o_gqr, o_gkn, o_gkr):
                o[...] = jnp.zeros_like(o)

        for o, g in zip((o_gqn, o_gqr, o_gkn, o_gkr), (g1, g2, g3, g4)):
            o[...] += g

    hb = lambda c: pl.BlockSpec((None, bm, c), lambda m, h: (h, m, 0))
    rb = lambda c: pl.BlockSpec((bm, c), lambda m, h: (m, 0))
    wb = lambda s: pl.BlockSpec(tuple(s), lambda m, h: (0, 0))
    return pl.pallas_call(
        body, name=name, grid=(T // bm, H),
        in_specs=[hb(QK), hb(NOPE + VDIM), rb(lat.shape[1]), rb(ROPE), rb(ROPE), wb(rmat.shape)]
        + [wb(g.shape) for g in gains] + [hb(QK), hb(QK), hb(VDIM)],
        out_specs=[hb(QK), hb(NOPE + VDIM), rb(ROPE)] + [wb(g.shape) for g in gains],
        out_shape=[jax.ShapeDtypeStruct((H, T, QK), BF16), jax.ShapeDtypeStruct((H, T, NOPE + VDIM), BF16),
                   jax.ShapeDtypeStruct((T, ROPE), F32)] + [jax.ShapeDtypeStruct(g.shape, F32) for g in gains],
        compiler_params=_params(("arbitrary", "arbitrary")),
    )(q_raw, kv_raw, lat, cos, sin, rmat, *gains, dq, dk, dv)


_NT = (((1,), (1,)), ((), ()))
_TN = (((0,), (0,)), ((), ()))


def _causal(blk):
    return lax.broadcasted_iota(jnp.int32, (blk, blk), 1) <= lax.broadcasted_iota(jnp.int32, (blk, blk), 0)


def _attn_fwd(q, k, v, seq, name, blk=512):
    H, T, _ = q.shape
    nb = T // seq
    blk = _pick(seq, blk)
    nq = seq // blk
    scale = float(QK) ** -0.5

    def body(q_ref, k_ref, v_ref, o_ref, lse_ref):
        qi = pl.program_id(2)
        qb = q_ref[...]

        def step(j, carry, diagonal):
            m, l, acc = carry
            ks = pl.ds(pl.multiple_of(j * blk, blk), blk)
            s = lax.dot_general(qb, k_ref[ks, :], _NT, preferred_element_type=F32) * scale
            if diagonal:
                s = jnp.where(_causal(blk), s, -jnp.inf)
            m_new = jnp.maximum(m, jnp.max(s, axis=1, keepdims=True))
            pr = jnp.exp(s - m_new)
            alpha = jnp.exp(m - m_new)
            l = alpha * l + jnp.sum(pr, axis=1, keepdims=True)
            acc = alpha * acc + jnp.dot(pr.astype(BF16), v_ref[ks, :], preferred_element_type=F32)
            return m_new, l, acc

        init = (jnp.full((blk, 1), -jnp.inf, F32), jnp.zeros((blk, 1), F32), jnp.zeros((blk, VDIM), F32))
        below = lax.fori_loop(0, qi, lambda j, c: step(j, c, False), init)
        m, l, acc = step(qi, below, True)
        o_ref[...] = (acc / l).astype(o_ref.dtype)
        lse_ref[...] = m + jnp.log(l)

    return pl.pallas_call(
        body, name=name, grid=(H, nb, nq),
        in_specs=[pl.BlockSpec((None, blk, QK), lambda h, b, i: (h, b * nq + i, 0)),
                  pl.BlockSpec((None, seq, QK), lambda h, b, i: (h, b, 0)),
                  pl.BlockSpec((None, seq, VDIM), lambda h, b, i: (h, b, 0))],
        out_specs=[pl.BlockSpec((blk, VDIM), lambda h, b, i: (b * nq + i, h)),
                   pl.BlockSpec((None, blk, 1), lambda h, b, i: (h, b * nq + i, 0))],
        out_shape=[jax.ShapeDtypeStruct((T, H * VDIM), BF16), jax.ShapeDtypeStruct((H, T, 1), F32)],
        compiler_params=_params(("parallel", "parallel", "parallel")),
    )(q, k, v)


def _attn_bwd(q, k, v, o, do, lse, seq, name, blk=512):
    H, T, _ = q.shape
    nb = T // seq
    blk = _pick(seq, blk)
    nq = seq // blk
    scale = float(QK) ** -0.5

    def body(q_ref, k_ref, v_ref, o_ref, do_ref, lse_ref, dq_ref, dk_ref, dv_ref):
        dk_ref[...] = jnp.zeros_like(dk_ref)
        dv_ref[...] = jnp.zeros_like(dv_ref)

        def qloop(i, carry):
            qs = pl.ds(pl.multiple_of(i * blk, blk), blk)
            qb = q_ref[qs, :]
            dof = do_ref[qs, :]
            dob = dof.astype(BF16)
            lse_b = lse_ref[qs, :]
            delta = jnp.sum(dof * o_ref[qs, :].astype(F32), axis=1, keepdims=True)

            def kstep(j, dq_acc, diagonal):
                ks = pl.ds(pl.multiple_of(j * blk, blk), blk)
                kb = k_ref[ks, :]
                vb = v_ref[ks, :]
                s = lax.dot_general(qb, kb, _NT, preferred_element_type=F32) * scale
                pr = jnp.exp(s - lse_b)
                if diagonal:
                    pr = jnp.where(_causal(blk), pr, 0.0)
                dp = lax.dot_general(dob, vb, _NT, preferred_element_type=F32)
                ds = (pr * (dp - delta) * scale).astype(BF16)
                prb = pr.astype(BF16)
                dv_ref[ks, :] += lax.dot_general(prb, dob, _TN, preferred_element_type=F32)
                dk_ref[ks, :] += lax.dot_general(ds, qb, _TN, preferred_element_type=F32)
                return dq_acc + jnp.dot(ds, kb, preferred_element_type=F32)

            below = lax.fori_loop(0, i, lambda j, c: kstep(j, c, False), jnp.zeros((blk, QK), F32))
            dq_ref[qs, :] = kstep(i, below, True)
            return carry

        lax.fori_loop(0, nq, qloop, 0)

    hb = lambda c: pl.BlockSpec((None, seq, c), lambda h, b: (h, b, 0))
    cb = lambda: pl.BlockSpec((seq, VDIM), lambda h, b: (b, h))
    return pl.pallas_call(
        body, name=name, grid=(H, nb),
        in_specs=[hb(QK), hb(QK), hb(VDIM), cb(), cb(), hb(1)],
        out_specs=[hb(QK), hb(QK), hb(VDIM)],
        out_shape=[jax.ShapeDtypeStruct((H, T, QK), F32), jax.ShapeDtypeStruct((H, T, QK), F32),
                   jax.ShapeDtypeStruct((H, T, VDIM), F32)],
        compiler_params=_params(("parallel", "parallel")),
    )(q, k, v, o, do, lse)


def _gelu_ln_fwd(z, g, b, name):
    half = z.shape[1] // 2

    def fn(zz, gg, bb):
        return _gelu(zz[:, :half]), _layer_norm(_gelu(zz[:, half:]), gg, bb)

    return _rowcall(fn, [z], [g, b], [(half, BF16), (half, BF16)], name=name)


def _gelu_ln_bwd(z, d_u, d_vn, g, b, name):
    half = z.shape[1] // 2

    def gelu_and_slope(x):
        c, a = math.sqrt(2.0 / math.pi), 0.044715
        x2 = x * x
        t = jnp.tanh(c * x * (1.0 + a * x2))
        return 0.5 * x * (1.0 + t), 0.5 * (1.0 + t) + 0.5 * x * (1.0 - t * t) * (c * (1.0 + 3.0 * a * x2))

    def fn(zz, du, dvn, gg, bb):
        _, su = gelu_and_slope(zz[:, :half])
        v, sv = gelu_and_slope(zz[:, half:])
        xc = v - jnp.mean(v, axis=-1, keepdims=True)
        rstd = lax.rsqrt(jnp.mean(xc * xc, axis=-1, keepdims=True) + EPS)
        y = xc * rstd
        dy = dvn * gg
        dv = rstd * (dy - jnp.mean(dy, axis=-1, keepdims=True) - y * jnp.mean(dy * y, axis=-1, keepdims=True))
        dg = jnp.sum(dvn * y, axis=0, keepdims=True)
        db = jnp.sum(dvn, axis=0, keepdims=True)
        return jnp.concatenate([du * su, dv * sv], axis=1), dg, db

    return _rowcall(fn, [z, d_u, d_vn], [g, b], [(z.shape[1], BF16)], [g.shape, b.shape], bm=128, name=name)


def _tril_bf16(ws):
    t = lax.broadcasted_iota(jnp.int32, ws.shape, 0)
    s = lax.broadcasted_iota(jnp.int32, ws.shape, 1)
    return jnp.where(s <= t, ws, 0.0).astype(BF16)


def _sgu_fwd(u, vn, ws, bs, name, bm=2048):
    T, half = u.shape
    gd = half // GROUPS
    bm = _pick(T, bm, CHUNK)
    nc = bm // CHUNK

    def body(u_ref, vn_ref, ws_ref, bs_ref, y_ref):
        wm = _tril_bf16(ws_ref[...])
        bias = bs_ref[...]
        for c in range(nc):
            rs = slice(c * CHUNK, (c + 1) * CHUNK)
            sv = jnp.dot(wm, vn_ref[rs, :], preferred_element_type=F32) + bias
            y_ref[rs, :] = (u_ref[rs, :].astype(F32) * sv).astype(y_ref.dtype)

    tb = lambda: pl.BlockSpec((bm, gd), lambda g, i: (i, g))
    return pl.pallas_call(
        body, name=name, grid=(GROUPS, T // bm),
        in_specs=[tb(), tb(), pl.BlockSpec((None, CHUNK, CHUNK), lambda g, i: (g, 0, 0)),
                  pl.BlockSpec((None, CHUNK, 1), lambda g, i: (g, 0, 0))],
        out_specs=tb(),
        out_shape=jax.ShapeDtypeStruct((T, half), BF16),
        compiler_params=_params(("parallel", "parallel")),
    )(u, vn, ws, bs)


def _sgu_bwd(u, vn, dy, ws, bs, name, bm=1024):
    T, half = u.shape
    gd = half // GROUPS
    bm = _pick(T, bm, CHUNK)
    nc = bm // CHUNK

    def body(u_ref, vn_ref, dy_ref, ws_ref, bs_ref, du_ref, dvn_ref, dws_ref, dbs_ref):
        @pl.when(pl.program_id(1) == 0)
        def _():
            dws_ref[...] = jnp.zeros_like(dws_ref)
            dbs_ref[...] = jnp.zeros_like(dbs_ref)

        wm = _tril_bf16(ws_ref[...])
        bias = bs_ref[...]
        dws = jnp.zeros((CHUNK, CHUNK), F32)
        dbs = jnp.zeros((CHUNK, 1), F32)
        for c in range(nc):
            rs = slice(c * CHUNK, (c + 1) * CHUNK)
            vb = vn_ref[rs, :]
            dyb = dy_ref[rs, :]
            sv = jnp.dot(wm, vb, preferred_element_type=F32) + bias
            du_ref[rs, :] = dyb * sv
            dsv = dyb * u_ref[rs, :].astype(F32)
            dsb = dsv.astype(BF16)
            dvn_ref[rs, :] = lax.dot_general(wm, dsb, _TN, preferred_element_type=F32)
            dws = dws + lax.dot_general(dsb, vb, _NT, preferred_element_type=F32)
            dbs = dbs + jnp.sum(dsv, axis=1, keepdims=True)
        t = lax.broadcasted_iota(jnp.int32, (CHUNK, CHUNK), 0)
        s = lax.broadcasted_iota(jnp.int32, (CHUNK, CHUNK), 1)
        dws_ref[...] += jnp.where(s <= t, dws, 0.0)
        dbs_ref[...] += dbs

    tb = lambda: pl.BlockSpec((bm, gd), lambda g, i: (i, g))
    wsb = lambda: pl.BlockSpec((None, CHUNK, CHUNK), lambda g, i: (g, 0, 0))
    bsb = lambda: pl.BlockSpec((None, CHUNK, 1), lambda g, i: (g, 0, 0))
    return pl.pallas_call(
        body, name=name, grid=(GROUPS, T // bm),
        in_specs=[tb(), tb(), tb(), wsb(), bsb()],
        out_specs=[tb(), tb(), wsb(), bsb()],
        out_shape=[jax.ShapeDtypeStruct((T, half), F32), jax.ShapeDtypeStruct((T, half), F32),
                   jax.ShapeDtypeStruct(ws.shape, F32), jax.ShapeDtypeStruct(bs.shape, F32)],
        compiler_params=_params(("parallel", "arbitrary")),
    )(u, vn, dy, ws, bs)


def _loss_head(y, t, name):
    d_model = y.shape[1]

    def fn(yy, tt):
        d = yy - tt
        part = 0.5 * jnp.sum(jnp.mean(d * d, axis=-1, keepdims=True), axis=0, keepdims=True)
        return d / d_model, jnp.zeros((1, LANE), F32) + part

    dy, part = _rowcall(fn, [y, t], [], [(d_model, F32)], [(1, LANE)], name=name)
    return dy, part[0, 0]


def _adamw(parts, w, m, v, prev, layer, name):
    L, R, C = w.shape
    br = _pick(R, max(8, (128 * 1024) // C // 8 * 8), 8)
    c1 = 1.0 - B1 ** STEP
    c2 = 1.0 - B2 ** STEP
    if prev is None:
        prev = [lax.empty(w.shape, F32) for _ in range(4)]

    def body(p_ref, w_ref, m_ref, v_ref, a0, a1, a2, a3, g_o, d_o, m_o, v_o, token):
        g = p_ref[0].astype(F32)
        for d in range(1, N_DEV):
            g = g + p_ref[d].astype(F32)
        mn = B1 * m_ref[...] + (1.0 - B1) * g
        vn = B2 * v_ref[...] + (1.0 - B2) * (g * g)
        g_o[...] = g
        m_o[...] = mn
        v_o[...] = vn
        d_o[...] = -LR * ((mn / c1) / (jnp.sqrt(vn / c2) + ADAM_EPS) + WD * w_ref[...])
        token[...] = jnp.zeros_like(token)

    blk = lambda: pl.BlockSpec((None, br, C), lambda i: (layer, i, 0))
    anywhere = pl.BlockSpec(memory_space=pl.ANY)
    outs = pl.pallas_call(
        body, name=name, grid=(R // br,),
        in_specs=[pl.BlockSpec((N_DEV, br, C), lambda i: (0, i, 0)), blk(), blk(), blk()] + [anywhere] * 4,
        out_specs=[blk(), blk(), blk(), blk(), pl.BlockSpec((8, LANE), lambda i: (0, 0))],
        out_shape=[jax.ShapeDtypeStruct((L, R, C), F32)] * 4 + [jax.ShapeDtypeStruct((8, LANE), F32)],
        input_output_aliases={4: 0, 5: 1, 6: 2, 7: 3},
        compiler_params=_params(("arbitrary",)),
    )(parts, w, m, v, *prev)
    return list(outs[:4]), outs[4][0, 0]


def _mesh_pos():
    return lax.axis_index("x"), lax.axis_index("y"), lax.axis_index("c")


def _flip(pos, k):
    x, y, c = pos
    px = 1 - x if k & 4 else x
    py = 1 - y if k & 2 else y
    pc = 1 - c if k & 1 else c
    return px, py, pc


HBM_SPEC = pl.BlockSpec(memory_space=pltpu.HBM)
SEM_SPEC = pl.BlockSpec(memory_space=pltpu.SEMAPHORE)
EFFECT = pltpu.SideEffectType.DATAFLOW_SIDE_EFFECTING


def _hbm(a):
    return pltpu.with_memory_space_constraint(a, pltpu.HBM)


def _device_index():
    x, y, c = _mesh_pos()
    return 4 * x + 2 * y + c


def _peer_copy(src, land, send, recv, a, k, pos, scatter):
    peer = _flip(pos, k)
    me = 4 * pos[0] + 2 * pos[1] + pos[2]
    piece = src.at[4 * peer[0] + 2 * peer[1] + peer[2]] if scatter else src
    return pltpu.make_async_remote_copy(
        src_ref=piece, dst_ref=land.at[me], send_sem=send.at[7 * a + k - 1], recv_sem=recv.at[7 * a + k - 1],
        device_id=peer, device_id_type=pl.DeviceIdType.MESH)


def _xchg_start(srcs, scatter, after, name):
    n = len(srcs)

    def body(*refs):
        src, land = refs[:n], refs[n:2 * n]
        send, recv, token = refs[2 * n + 1], refs[2 * n + 2], refs[-1]
        pos = _mesh_pos()
        for k in range(1, N_DEV):
            for a in range(n):
                _peer_copy(src[a], land[a], send, recv, a, k, pos, scatter).start()
        token[...] = jnp.zeros_like(token)

    lands = [lax.empty(s.shape if scatter else (N_DEV,) + s.shape, s.dtype) for s in srcs]
    outs = pl.pallas_call(
        body, name=name,
        out_shape=(pltpu.SemaphoreType.DMA((7 * n,)), pltpu.SemaphoreType.DMA((7 * n,)),
                   *[pltpu.HBM(s.shape, s.dtype) for s in srcs], *[pltpu.HBM(l.shape, l.dtype) for l in lands],
                   jax.ShapeDtypeStruct((8, LANE), F32)),
        in_specs=[HBM_SPEC] * (2 * n) + [pl.BlockSpec(memory_space=pl.ANY)],
        out_specs=(SEM_SPEC, SEM_SPEC, *[HBM_SPEC] * (2 * n), pl.BlockSpec(memory_space=pltpu.VMEM)),
        input_output_aliases={q: 2 + q for q in range(2 * n)},
        compiler_params=pltpu.CompilerParams(has_side_effects=EFFECT),
    )(*[_hbm(s) for s in srcs], *[_hbm(l) for l in lands], after)
    handle = dict(send=outs[0], recv=outs[1], srcs=list(outs[2:2 + n]), lands=list(outs[2 + n:2 + 2 * n]),
                  scatter=scatter)
    return handle, outs[-1]


def _xchg_wait(handle, after, name):
    srcs, lands, scatter = handle['srcs'], handle['lands'], handle['scatter']
    n = len(srcs)

    def body(*refs):
        src, land = refs[:n], refs[n:2 * n]
        send, recv = refs[2 * n], refs[2 * n + 1]
        pos = _mesh_pos()
        for k in range(1, N_DEV):
            for a in range(n):
                cp = _peer_copy(src[a], land[a], send, recv, a, k, pos, scatter)
                cp.wait_send()
                cp.wait_recv()

    outs = pl.pallas_call(
        body, name=name,
        out_shape=[pltpu.HBM(s.shape, s.dtype) for s in srcs] + [pltpu.HBM(l.shape, l.dtype) for l in lands],
        in_specs=[HBM_SPEC] * (2 * n) + [SEM_SPEC, SEM_SPEC, pl.BlockSpec(memory_space=pl.ANY)],
        out_specs=[HBM_SPEC] * (2 * n),
        input_output_aliases={q: q for q in range(2 * n)},
        compiler_params=pltpu.CompilerParams(has_side_effects=EFFECT),
    )(*srcs, *lands, handle['send'], handle['recv'], after)
    me = _device_index()
    full = []
    for src, land in zip(outs[:n], outs[n:]):
        mine = lax.dynamic_index_in_dim(src, me, 0, keepdims=False) if scatter else src
        full.append(lax.dynamic_update_index_in_dim(land, mine, me, 0))
    return full


def _pack(parts):
    flat = jnp.concatenate([q.reshape(-1) for q in parts])
    pad = (-flat.shape[0]) % (8 * LANE)
    return jnp.pad(flat, (0, pad)).reshape(-1, LANE)


def _unpack(packed, like):
    flat = packed.reshape(-1)
    out, o = [], 0
    for q in like:
        out.append(flat[o:o + q.size].reshape(q.shape))
        o += q.size
    return out


def kernel(x, p, positions, norm_mix, norm_ffn, norm_ple, mla_w_down, mla_q_lora_g, mla_kv_lora_g, mla_w_uq, mla_w_ukv, mla_q_nope_g, mla_q_rope_g, mla_k_nope_g, mla_k_rope_g, mla_w_out, gmlp_w_in, gmlp_ln_g, gmlp_ln_b, gmlp_w_s, gmlp_b_s, gmlp_w_out, ffn_w_up, ffn_w_down, ple_w_gate, ple_w_proj, loss_target, m_norm_mix, m_norm_ffn, m_norm_ple, m_mla_w_down, m_mla_q_lora_g, m_mla_kv_lora_g, m_mla_w_uq, m_mla_w_ukv, m_mla_q_nope_g, m_mla_q_rope_g, m_mla_k_nope_g, m_mla_k_rope_g, m_mla_w_out, m_gmlp_w_in, m_gmlp_ln_g, m_gmlp_ln_b, m_gmlp_w_s, m_gmlp_b_s, m_gmlp_w_out, m_ffn_w_up, m_ffn_w_down, m_ple_w_gate, m_ple_w_proj, v_norm_mix, v_norm_ffn, v_norm_ple, v_mla_w_down, v_mla_q_lora_g, v_mla_kv_lora_g, v_mla_w_uq, v_mla_w_ukv, v_mla_q_nope_g, v_mla_q_rope_g, v_mla_k_nope_g, v_mla_k_rope_g, v_mla_w_out, v_gmlp_w_in, v_gmlp_ln_g, v_gmlp_ln_b, v_gmlp_w_s, v_gmlp_b_s, v_gmlp_w_out, v_ffn_w_up, v_ffn_w_down, v_ple_w_gate, v_ple_w_proj):
    W = dict(zip(WEIGHTS, (norm_mix, norm_ffn, norm_ple, mla_w_down, mla_q_lora_g, mla_kv_lora_g, mla_w_uq, mla_w_ukv, mla_q_nope_g, mla_q_rope_g, mla_k_nope_g, mla_k_rope_g, mla_w_out, gmlp_w_in, gmlp_ln_g, gmlp_ln_b, gmlp_w_s, gmlp_b_s, gmlp_w_out, ffn_w_up, ffn_w_down, ple_w_gate, ple_w_proj)))
    M1 = dict(zip(WEIGHTS, (m_norm_mix, m_norm_ffn, m_norm_ple, m_mla_w_down, m_mla_q_lora_g, m_mla_kv_lora_g, m_mla_w_uq, m_mla_w_ukv, m_mla_q_nope_g, m_mla_q_rope_g, m_mla_k_nope_g, m_mla_k_rope_g, m_mla_w_out, m_gmlp_w_in, m_gmlp_ln_g, m_gmlp_ln_b, m_gmlp_w_s, m_gmlp_b_s, m_gmlp_w_out, m_ffn_w_up, m_ffn_w_down, m_ple_w_gate, m_ple_w_proj)))
    M2 = dict(zip(WEIGHTS, (v_norm_mix, v_norm_ffn, v_norm_ple, v_mla_w_down, v_mla_q_lora_g, v_mla_kv_lora_g, v_mla_w_uq, v_mla_w_ukv, v_mla_q_nope_g, v_mla_q_rope_g, v_mla_k_nope_g, v_mla_k_rope_g, v_mla_w_out, v_gmlp_w_in, v_gmlp_ln_g, v_gmlp_ln_b, v_gmlp_w_s, v_gmlp_b_s, v_gmlp_w_out, v_ffn_w_up, v_ffn_w_down, v_ple_w_gate, v_ple_w_proj)))

    nb, seq, d_model = x.shape
    T = nb * seq
    depth = norm_mix.shape[0]
    h = x.reshape(T, d_model)
    target = loss_target.reshape(T, d_model)
    p_bf = p.reshape(depth, T, p.shape[-1]).astype(BF16)

    def stage_keys(st):
        i, second = divmod(st, 2)
        if second:
            return [(n, i) for n in ('ffn_w_up', 'ffn_w_down', 'ple_w_gate', 'ple_w_proj')]
        mix = (['mla_w_down', 'mla_w_uq', 'mla_w_ukv', 'mla_w_out'] if i % 2 == 0 else
               ['gmlp_w_in', 'gmlp_ln_g', 'gmlp_ln_b', 'gmlp_w_out'])
        return [(n, i // 2) for n in mix]

    FW = {n: {} for n in SHARDED}

    def start_weights(st, after):
        keys = stage_keys(st)
        srcs = [W[n][l] if n in F32_PAYLOAD else W[n][l].astype(BF16) for n, l in keys]
        handle, token = _xchg_start(srcs, False, after, "weights_start%d" % st)
        return (keys, handle), token[0, 0]

    def wait_weights(pending, st, after):
        keys, handle = pending
        landed = _xchg_wait(handle, after, "weights_wait%d" % st)
        for (n, l), full in zip(keys, landed):
            if SHARD_AXIS[n] == 1:
                FW[n][l] = full.reshape((-1,) + full.shape[2:])
            elif n in ('mla_w_uq', 'mla_w_ukv'):
                FW[n][l] = full
            else:
                FW[n][l] = jnp.transpose(full, (1, 0, 2)).reshape(full.shape[1], -1)
        return landed[0]

    row = lambda a: a.reshape(1, -1)
    cos, sin = _rope_tables(positions.reshape(T, 1), "rope_tables")
    rmat = _rot_matrix()

    saved = []
    pending, _ = start_weights(0, h)
    for i in range(depth):
        j = i // 2
        landed = wait_weights(pending, 2 * i, h)
        pending, token = start_weights(2 * i + 1, landed)
        s = {}
        s['h0'] = h
        hn = _rms_fwd(h, row(W['norm_mix'][i]) + token, "rms_fwd")
        s['hn'] = hn
        if i % 2 == 0:
            gains = [row(W['mla_q_nope_g'][j]), row(W['mla_q_rope_g'][j]), row(W['mla_k_nope_g'][j]),
                     row(W['mla_k_rope_g'][j])]
            lat = _mm(hn, FW['mla_w_down'][j], name="mla_down")
            cq, ckv = _prep1_fwd(lat, row(W['mla_q_lora_g'][j]), row(W['mla_kv_lora_g'][j]), "mla_prep1")
            q_raw = _mm(cq, FW['mla_w_uq'][j], out_blocks=HEADS, name="mla_uq")
            kv_raw = _mm(ckv, FW['mla_w_ukv'][j], out_blocks=HEADS, name="mla_ukv")
            q, k, v = _prep2_fwd(q_raw, kv_raw, lat, cos, sin, rmat, gains, "mla_prep2")
            o, lse = _attn_fwd(q, k, v, seq, "attn_fwd")
            h = _mm(o, FW['mla_w_out'][j], extras=(h,), epilogue=lambda acc, res: (res + acc,), name="mla_out")
            s.update(lat=lat, cq=cq, ckv=ckv, q_raw=q_raw, kv_raw=kv_raw, q=q, k=k, v=v, o=o, lse=lse, gains=gains)
        else:
            z = _mm(hn, FW['gmlp_w_in'][j], name="gmlp_in")
            u, vn = _gelu_ln_fwd(z, row(FW['gmlp_ln_g'][j]), row(FW['gmlp_ln_b'][j]), "gmlp_gelu_ln")
            bs3 = W['gmlp_b_s'][j][:, :, None]
            y = _sgu_fwd(u, vn, W['gmlp_w_s'][j], bs3, "gmlp_sgu")
            h = _mm(y, FW['gmlp_w_out'][j], extras=(h,), epilogue=lambda acc, res: (res + acc,), name="gmlp_out")
            s.update(z=z, u=u, vn=vn, y=y, bs3=bs3)
        s['h1'] = h
        landed = wait_weights(pending, 2 * i + 1, h)
        token = 0.0
        if i + 1 < depth:
            pending, token = start_weights(2 * i + 2, landed)
        hn2 = _rms_fwd(h, row(W['norm_ffn'][i]) + token, "rms_fwd")
        a, r = _mm(hn2, FW['ffn_w_up'][i], epilogue=lambda acc: (acc, jnp.square(jnp.maximum(acc, 0.0))),
                   out_dtypes=(BF16, BF16), name="ffn_up")
        h = _mm(r, FW['ffn_w_down'][i], extras=(h,), epilogue=lambda acc, res: (res + acc,), name="ffn_down")
        s.update(hn2=hn2, a=a, r=r, h2=h)
        hn3 = _rms_fwd(h, row(W['norm_ple'][i]), "rms_fwd")
        gt = _mm(hn3, FW['ple_w_gate'][i], name="ple_gate")
        pp, h = _mm(p_bf[i], FW['ple_w_proj'][i], extras=(gt, h),
                    epilogue=lambda acc, g_, res: (acc, res + _sigmoid(g_) * acc), out_dtypes=(F32, F32),
                    name="ple_proj")
        s.update(hn3=hn3, gt=gt, pp=pp)
        saved.append(s)

    dh, loss_part = _loss_head(h, target, "loss_head")
    loss = lax.psum(loss_part, MESH_AXES)

    G = {n: [None] * W[n].shape[0] for n in REPLICATED}
    res = {}
    flying = []

    def shard3(n):
        shp = W[n].shape
        return shp[0], int(np.prod(shp[1:-1])), shp[-1]

    def by_owner(g):
        return g.reshape((N_DEV, g.shape[0] // N_DEV) + g.shape[1:])

    def send_grads(tag, grads):
        handle, token = _xchg_start([g for _, g in grads], True, grads[-1][1], "grads_start_" + tag)
        flying.append((tag, [key for key, _ in grads], handle))
        return token[0, 0]

    def land_grads(after):
        tag, keys, handle = flying.pop(0)
        done = 0.0
        for (n, l), full in zip(keys, _xchg_wait(handle, after, "grads_wait_" + tag)):
            dims = shard3(n)
            res[n], token = _adamw(full.reshape((N_DEV,) + dims[1:]), W[n].reshape(dims), M1[n].reshape(dims),
                                   M2[n].reshape(dims), res.get(n), l, "adamw_" + n)
            done = done + token
        return done

    def start_small(names, tag, after):
        handle, token = _xchg_start([_pack([jnp.stack(G[n]) for n in names])], False, after, "small_start_" + tag)
        return (names, handle), token[0, 0]

    def land_small(pending_small, tag, after):
        names, handle = pending_small
        (parts,) = _xchg_wait(handle, after, "small_wait_" + tag)
        like = [W[n] for n in names]
        outs, _ = _adamw(parts, _pack(like)[None], _pack([M1[n] for n in names])[None],
                         _pack([M2[n] for n in names])[None], None, 0, "adamw_small_" + tag)
        unpacked = [_unpack(o, like) for o in outs]
        for idx, n in enumerate(names):
            res[n] = [unpacked[q][idx] for q in range(4)]

    spatial = ['gmlp_w_s', 'gmlp_b_s']
    token = 0.0
    for i in reversed(range(depth)):
        j = i // 2
        s = saved[i]
        def ple_elem(d, g_, pq):
            sg = _sigmoid(g_)
            return d * sg, d * pq * sg * (1.0 - sg)

        d_pp, d_gt = _rowcall(ple_elem, [dh, s['gt'], s['pp']], [], [(d_model, BF16), (d_model, BF16)], name="ple_bwd")
        g_proj = _mm(p_bf[i], d_pp, ta=True, out_dtypes=(BF16,), name="ple_proj_dw")
        g_proj = jnp.transpose(g_proj.reshape(g_proj.shape[0], N_DEV, -1), (1, 0, 2))
        g_gate = _mm(s['hn3'], d_gt, ta=True, out_dtypes=(BF16,), name="ple_gate_dw")
        d_hn3 = _mm(d_gt, FW['ple_w_gate'][i], tb=True, name="ple_gate_dx")
        dh, dh_bf, dg = _rms_bwd(s['h2'], d_hn3, dh, row(W['norm_ple'][i]) + token, "rms_bwd")
        G['norm_ple'][i] = dg[0]
        d_a = _mm(dh_bf, FW['ffn_w_down'][i], tb=True, extras=(s['a'],),
                  epilogue=lambda acc, a_: (acc * (2.0 * jnp.maximum(a_.astype(F32), 0.0)),), out_dtypes=(BF16,),
                  name="ffn_down_dx")
        g_down = _mm(s['r'], dh_bf, ta=True, out_dtypes=(BF16,), name="ffn_down_dw")
        g_up = _mm(s['hn2'], d_a, ta=True, out_blocks=N_DEV, out_dtypes=(BF16,), name="ffn_up_dw")
        token = send_grads("mlp%d" % i, [(('ple_w_proj', i), g_proj), (('ple_w_gate', i), by_owner(g_gate)),
                                         (('ffn_w_down', i), by_owner(g_down)), (('ffn_w_up', i), g_up)])
        if len(flying) > 1:
            token = token + land_grads(g_up)
        d_hn2 = _mm(d_a, FW['ffn_w_up'][i], tb=True, name="ffn_up_dx")
        dh, dh_bf, dg = _rms_bwd(s['h1'], d_hn2, dh, row(W['norm_ffn'][i]) + token, "rms_bwd")
        G['norm_ffn'][i] = dg[0]
        if i % 2 == 0:
            d_o = _mm(dh_bf, FW['mla_w_out'][j], tb=True, name="mla_out_dx")
            g_out = _mm(s['o'], dh_bf, ta=True, out_dtypes=(BF16,), name="mla_out_dw")
            dq, dk, dv = _attn_bwd(s['q'], s['k'], s['v'], s['o'], d_o, s['lse'], seq, "attn_bwd")
            d_q_raw, d_kv_raw, d_kr, g1, g2, g3, g4 = _prep2_bwd(
                s['q_raw'], s['kv_raw'], s['lat'], cos, sin, rmat, s['gains'], dq, dk, dv, "mla_prep2_bwd")
            G['mla_q_nope_g'][j], G['mla_q_rope_g'][j] = g1[0], g2[0]
            G['mla_k_nope_g'][j], G['mla_k_rope_g'][j] = g3[0], g4[0]
            g_uq = _mm(s['cq'], d_q_raw, ta=True, out_blocks=N_DEV, out_dtypes=(BF16,), name="mla_uq_dw")
            g_ukv = _mm(s['ckv'], d_kv_raw, ta=True, out_blocks=N_DEV, out_dtypes=(BF16,), name="mla_ukv_dw")
            d_cq = _mm(d_q_raw, FW['mla_w_uq'][j], tb=True, name="mla_uq_dx")
            d_ckv = _mm(d_kv_raw, FW['mla_w_ukv'][j], tb=True, name="mla_ukv_dx")
            d_lat, dga, dgb = _prep1_bwd(s['lat'], d_cq, d_ckv, d_kr, row(W['mla_q_lora_g'][j]),
                                         row(W['mla_kv_lora_g'][j]), "mla_prep1_bwd")
            G['mla_q_lora_g'][j], G['mla_kv_lora_g'][j] = dga[0], dgb[0]
            g_down = _mm(s['hn'], d_lat, ta=True, out_dtypes=(BF16,), name="mla_down_dw")
            grads = [(('mla_w_out', j), by_owner(g_out)), (('mla_w_uq', j), g_uq), (('mla_w_ukv', j), g_ukv),
                     (('mla_w_down', j), by_owner(g_down))]
            d_hn = _mm(d_lat, FW['mla_w_down'][j], tb=True, name="mla_down_dx")
        else:
            d_y = _mm(dh_bf, FW['gmlp_w_out'][j], tb=True, name="gmlp_out_dx")
            g_out = _mm(s['y'], dh_bf, ta=True, out_dtypes=(BF16,), name="gmlp_out_dw")
            d_u, d_vn, d_ws, d_bs = _sgu_bwd(s['u'], s['vn'], d_y, W['gmlp_w_s'][j], s['bs3'], "gmlp_sgu_bwd")
            G['gmlp_w_s'][j], G['gmlp_b_s'][j] = d_ws, d_bs[:, :, 0]
            d_z, d_lg, d_lb = _gelu_ln_bwd(s['z'], d_u, d_vn, row(FW['gmlp_ln_g'][j]), row(FW['gmlp_ln_b'][j]),
                                           "gmlp_gelu_ln_bwd")
            g_in = _mm(s['hn'], d_z, ta=True, out_blocks=N_DEV, out_dtypes=(BF16,), name="gmlp_in_dw")
            grads = [(('gmlp_w_out', j), by_owner(g_out)), (('gmlp_ln_g', j), by_owner(d_lg[0])),
                     (('gmlp_ln_b', j), by_owner(d_lb[0])), (('gmlp_w_in', j), g_in)]
            d_hn = _mm(d_z, FW['gmlp_w_in'][j], tb=True, name="gmlp_in_dx")
        token = send_grads("mix%d" % i, grads)
        if len(flying) > 1:
            token = token + land_grads(grads[-1][1])
        dh, _, dg = _rms_bwd(s['h0'], d_hn, dh, row(W['norm_mix'][i]) + token, "rms_bwd")
        G['norm_mix'][i] = dg[0]
        token = 0.0
        if i == 1:
            small_a, token = start_small(spatial, "spatial", dh)
    grad_x = dh.reshape(x.shape)

    small_b, _ = start_small([n for n in REPLICATED if n not in spatial], "gains", dh)
    while flying:
        land_grads(dh)
    land_small(small_a, "spatial", dh)
    land_small(small_b, "gains", dh)

    out = lambda q: [res[n][q].reshape(W[n].shape) for n in WEIGHTS]
    return (loss, grad_x, *out(0), *out(1), *out(2), *out(3))
```

```python
import math

import numpy as np
import jax
import jax.numpy as jnp
from jax import lax
from jax.experimental import pallas as pl
from jax.experimental.pallas import tpu as pltpu

F32 = jnp.float32
BF16 = jnp.bfloat16

N_DEV = 8
MESH_AXES = ("x", "y", "c")
HEADS = 8
NOPE = 128
ROPE = 64
VDIM = 128
QK = NOPE + ROPE
Q_LORA = 384
KV_LORA = 256
ROPE_BASE = 10000.0
CHUNK = 128
GROUPS = 8
EPS = 1e-6
LR, B1, B2, ADAM_EPS, WD, STEP = 0.001, 0.9, 0.999, 1e-08, 0.01, 10
LANE = 128
VMEM_LIMIT = 56 * 1024 * 1024
MM_VMEM_BUDGET = 40 * 1024 * 1024

WEIGHTS = ['norm_mix', 'norm_ffn', 'norm_ple', 'mla_w_down', 'mla_q_lora_g', 'mla_kv_lora_g', 'mla_w_uq',
           'mla_w_ukv', 'mla_q_nope_g', 'mla_q_rope_g', 'mla_k_nope_g', 'mla_k_rope_g', 'mla_w_out', 'gmlp_w_in',
           'gmlp_ln_g', 'gmlp_ln_b', 'gmlp_w_s', 'gmlp_b_s', 'gmlp_w_out', 'ffn_w_up', 'ffn_w_down', 'ple_w_gate',
           'ple_w_proj']
SHARD_AXIS = {'mla_w_down': 1, 'mla_w_uq': 2, 'mla_w_ukv': 2, 'mla_w_out': 1, 'gmlp_w_in': 2, 'gmlp_ln_g': 1,
              'gmlp_ln_b': 1, 'gmlp_w_out': 1, 'ffn_w_up': 2, 'ffn_w_down': 1, 'ple_w_gate': 1, 'ple_w_proj': 2}
SHARDED = list(SHARD_AXIS)
REPLICATED = [n for n in WEIGHTS if n not in SHARD_AXIS]
F32_PAYLOAD = ('gmlp_ln_g', 'gmlp_ln_b')


def _pick(dim, pref, align=LANE):
    if dim <= pref:
        return dim
    b = (pref // align) * align
    while b >= align:
        if dim % b == 0:
            return b
        b -= align
    return dim


def _params(sem):
    return pltpu.CompilerParams(dimension_semantics=sem, vmem_limit_bytes=VMEM_LIMIT)


def _mm(a, b, *, ta=False, tb=False, extras=(), epilogue=None, out_dtypes=(F32,), out_blocks=None, name,
        bn=1024):
    a3, b3 = a.ndim == 3, b.ndim == 3
    assert not (ta and a3)
    if ta:
        K, M = a.shape
        ka = K
    elif a3:
        M, ka = a.shape[1:]
        K = a.shape[0] * ka
    else:
        M, K = a.shape
        ka = K
    if tb:
        N, kb = b.shape[-2:]
        nb = N
        K2 = b.shape[0] * kb if b3 else kb
    else:
        kb, nb = b.shape[-2:]
        K2 = kb
        N = b.shape[0] * nb if b3 else nb
    assert K == K2, (a.shape, b.shape, ta, tb)
    no_ = N // out_blocks if out_blocks else N
    assert not (out_blocks and extras)
    size = lambda t: jnp.dtype(t).itemsize
    per_out = sum(size(e.dtype) for e in extras) + sum(size(t) for t in out_dtypes)
    bn = _pick(min(nb, no_), bn)
    k_lim = min(ka, kb)
    fits = lambda m, k: 2 * (m * k * size(a.dtype) + k * bn * size(b.dtype) + m * bn * per_out) + 4 * m * bn
    ms = [m for m in sorted({min(M, c) for c in (2048, 1024, 512, 256)}, reverse=True) if M % m == 0]
    ks = [k for k in dict.fromkeys((k_lim, 2048, 1024, 512, 256)) if k <= k_lim and k_lim % k == 0]
    bm, bk = next(((m, k) for k in ks for m in ms if fits(m, k) <= MM_VMEM_BUDGET), (ms[-1], ks[-1]))
    nk = K // bk
    ne, no = len(extras), len(out_dtypes)
    dims = (((0,) if ta else (1,), (1,) if tb else (0,)), ((), ()))

    def finish(r, e_refs, o_refs):
        outs = epilogue(r, *[e[...] for e in e_refs]) if epilogue is not None else (r,)
        for o, v in zip(o_refs, outs):
            o[...] = v.astype(o.dtype)

    def body(*refs):
        a_ref, b_ref = refs[0], refs[1]
        e_refs = refs[2:2 + ne]
        o_refs = refs[2 + ne:2 + ne + no]
        part = lax.dot_general(a_ref[...].astype(BF16), b_ref[...].astype(BF16), dims, preferred_element_type=F32)
        if nk == 1:
            finish(part, e_refs, o_refs)
            return
        acc = refs[-1]
        k = pl.program_id(2)

        @pl.when(k == 0)
        def _():
            acc[...] = part

        @pl.when(k > 0)
        def _():
            acc[...] += part

        @pl.when(k == nk - 1)
        def _():
            finish(acc[...], e_refs, o_refs)

    ka_t, kb_t, nb_t, no_t = ka // bk, kb // bk, nb // bn, no_ // bn
    if ta:
        a_spec = pl.BlockSpec((bk, bm), lambda i, j, k: (k, i))
    elif a3:
        a_spec = pl.BlockSpec((None, bm, bk), lambda i, j, k: (k // ka_t, i, k % ka_t))
    else:
        a_spec = pl.BlockSpec((bm, bk), lambda i, j, k: (i, k))
    if tb:
        b_spec = (pl.BlockSpec((None, bn, bk), lambda i, j, k: (k // kb_t, j, k % kb_t)) if b3 else
                  pl.BlockSpec((bn, bk), lambda i, j, k: (j, k)))
    else:
        b_spec = (pl.BlockSpec((None, bk, bn), lambda i, j, k: (j // nb_t, k, j % nb_t)) if b3 else
                  pl.BlockSpec((bk, bn), lambda i, j, k: (k, j)))
    if out_blocks:
        o_spec = lambda: pl.BlockSpec((None, bm, bn), lambda i, j, k: (j // no_t, i, j % no_t))
        o_shape = (out_blocks, M, no_)
    else:
        o_spec = lambda: pl.BlockSpec((bm, bn), lambda i, j, k: (i, j))
        o_shape = (M, N)
    outs = pl.pallas_call(
        body, name=name,
        grid=(M // bm, N // bn, nk),
        in_specs=[a_spec, b_spec] + [pl.BlockSpec((bm, bn), lambda i, j, k: (i, j)) for _ in extras],
        out_specs=[o_spec() for _ in out_dtypes],
        out_shape=[jax.ShapeDtypeStruct(o_shape, dt) for dt in out_dtypes],
        scratch_shapes=[pltpu.VMEM((bm, bn), F32)] if nk > 1 else [],
        compiler_params=_params(("parallel", "parallel", "arbitrary")),
    )(a, b, *extras)
    return outs[0] if no == 1 else outs


def _rowcall(fn, rows, params, row_outs, acc_outs=(), *, bm=256, deps=(), name):
    T = rows[0].shape[0]
    bm = _pick(T, bm, 8)
    nr, npar, nro, nao = len(rows), len(params), len(row_outs), len(acc_outs)
    first_out = nr + npar + len(deps)

    def body(*refs):
        vals = [r[...] for r in refs[:nr + npar]]
        res = fn(*vals)
        ro = refs[first_out:first_out + nro]
        ao = refs[first_out + nro:]
        for r, v in zip(ro, res[:nro]):
            r[...] = v.astype(r.dtype)
        if nao:
            @pl.when(pl.program_id(0) == 0)
            def _():
                for r in ao:
                    r[...] = jnp.zeros_like(r)

            for r, v in zip(ao, res[nro:]):
                r[...] += v

    def whole(shape):
        nd = len(shape)
        return pl.BlockSpec(tuple(shape), lambda i: (0,) * nd)

    outs = pl.pallas_call(
        body, name=name,
        grid=(T // bm,),
        in_specs=[pl.BlockSpec((bm, r.shape[1]), lambda i: (i, 0)) for r in rows] + [whole(q.shape) for q in params]
        + [pl.BlockSpec(memory_space=pl.ANY)] * len(deps),
        out_specs=[pl.BlockSpec((bm, c), lambda i: (i, 0)) for c, _ in row_outs] + [whole(s) for s in acc_outs],
        out_shape=[jax.ShapeDtypeStruct((T, c), dt) for c, dt in row_outs]
        + [jax.ShapeDtypeStruct(tuple(s), F32) for s in acc_outs],
        compiler_params=_params(("arbitrary",) if nao else ("parallel",)),
    )(*rows, *params, *deps)
    return outs


def _rmsn(x, g):
    return x * lax.rsqrt(jnp.mean(x * x, axis=-1, keepdims=True) + EPS) * g


def _gelu(x):
    return 0.5 * x * (1.0 + jnp.tanh(math.sqrt(2.0 / math.pi) * (x + 0.044715 * (x * x * x))))


def _layer_norm(x, g, b):
    mu = jnp.mean(x, axis=-1, keepdims=True)
    xc = x - mu
    return xc * lax.rsqrt(jnp.mean(xc * xc, axis=-1, keepdims=True) + EPS) * g + b


def _sigmoid(x):
    return 1.0 / (1.0 + jnp.exp(-x))


def _rot(x, cos, sin, rmat):
    return x * cos + jnp.dot(x, rmat, precision=lax.Precision.HIGHEST, preferred_element_type=F32) * sin


def _rms_fwd(h, g, name, deps=()):
    return _rowcall(lambda x, gg: (_rmsn(x, gg),), [h], [g], [(h.shape[1], BF16)], bm=512, deps=deps, name=name)[0]


def _rms_bwd(h, d_hn, dh_in, g, name, deps=()):
    def fn(x, dy, dres, gg):
        _, vjp = jax.vjp(_rmsn, x, gg)
        dx, dg = vjp(dy.astype(F32))
        dh = dres + dx
        return dh, dh, dg

    d = h.shape[1]
    return _rowcall(fn, [h, d_hn, dh_in], [g], [(d, F32), (d, BF16)], [g.shape], bm=512, deps=deps, name=name)


def _rope_tables(pos, name):
    inv = np.float32(ROPE_BASE) ** (-(np.arange(0, ROPE, 2, dtype=np.float32) / np.float32(ROPE)))
    inv = jnp.asarray(np.concatenate([inv, inv])[None, :].astype(np.float32))

    def fn(pp, iv):
        ang = pp.astype(F32) * iv
        return jnp.cos(ang), jnp.sin(ang)

    return _rowcall(fn, [pos], [inv], [(ROPE, F32), (ROPE, F32)], name=name)


def _rot_matrix():
    r = np.zeros((ROPE, ROPE), np.float32)
    half = ROPE // 2
    for j in range(half):
        r[j + half, j] = -1.0
        r[j, j + half] = 1.0
    return jnp.asarray(r)


def _prep1_fwd(lat, gq, gkv, name):
    def fn(l, a, b):
        return _rmsn(l[:, :Q_LORA], a), _rmsn(l[:, Q_LORA:Q_LORA + KV_LORA], b)

    return _rowcall(fn, [lat], [gq, gkv], [(Q_LORA, BF16), (KV_LORA, BF16)], name=name)


def _prep1_bwd(lat, d_cq, d_ckv, d_kr, gq, gkv, name):
    def fn(l, dq, dkv, dkr, a, b):
        _, vq = jax.vjp(_rmsn, l[:, :Q_LORA], a)
        _, vkv = jax.vjp(_rmsn, l[:, Q_LORA:Q_LORA + KV_LORA], b)
        dxq, dga = vq(dq.astype(F32))
        dxkv, dgb = vkv(dkv.astype(F32))
        return jnp.concatenate([dxq, dxkv, dkr], axis=1), dga, dgb

    return _rowcall(fn, [lat, d_cq, d_ckv, d_kr], [gq, gkv], [(lat.shape[1], BF16)], [gq.shape, gkv.shape], name=name)


def _qk_fn(qn_raw, qr_raw, kn_raw, kr_raw, gqn, gqr, gkn, gkr, cos, sin, rmat):
    return (_rmsn(qn_raw, gqn), _rot(_rmsn(qr_raw, gqr), cos, sin, rmat),
            _rmsn(kn_raw, gkn), _rot(_rmsn(kr_raw, gkr), cos, sin, rmat))


def _prep2_fwd(q_raw, kv_raw, lat, cos, sin, rmat, gains, name, bm=1024):
    H, T, _ = q_raw.shape
    bm = _pick(T, bm, 8)
    kr0 = Q_LORA + KV_LORA

    def body(q_ref, kv_ref, lat_ref, cos_ref, sin_ref, r_ref, gqn, gqr, gkn, gkr, qo, ko, vo):
        qr, kvr = q_ref[...], kv_ref[...]
        qn, qro, kn, kro = _qk_fn(qr[:, :NOPE], qr[:, NOPE:], kvr[:, :NOPE], lat_ref[:, kr0:kr0 + ROPE],
                                  gqn[...], gqr[...], gkn[...], gkr[...], cos_ref[...], sin_ref[...], r_ref[...])
        qo[:, :NOPE] = qn.astype(BF16)
        qo[:, NOPE:] = qro.astype(BF16)
        ko[:, :NOPE] = kn.astype(BF16)
        ko[:, NOPE:] = kro.astype(BF16)
        vo[...] = kvr[:, NOPE:].astype(BF16)

    hb = lambda c: pl.BlockSpec((None, bm, c), lambda m, h: (h, m, 0))
    rb = lambda c: pl.BlockSpec((bm, c), lambda m, h: (m, 0))
    wb = lambda s: pl.BlockSpec(tuple(s), lambda m, h: (0, 0))
    return pl.pallas_call(
        body, name=name, grid=(T // bm, H),
        in_specs=[hb(QK), hb(NOPE + VDIM), rb(lat.shape[1]), rb(ROPE), rb(ROPE), wb(rmat.shape)]
        + [wb(g.shape) for g in gains],
        out_specs=[hb(QK), hb(QK), hb(VDIM)],
        out_shape=[jax.ShapeDtypeStruct((H, T, QK), BF16), jax.ShapeDtypeStruct((H, T, QK), BF16),
                   jax.ShapeDtypeStruct((H, T, VDIM), BF16)],
        compiler_params=_params(("parallel", "parallel")),
    )(q_raw, kv_raw, lat, cos, sin, rmat, *gains)


def _prep2_bwd(q_raw, kv_raw, lat, cos, sin, rmat, gains, dq, dk, dv, name, bm=512):
    H, T, _ = q_raw.shape
    bm = _pick(T, bm, 8)
    kr0 = Q_LORA + KV_LORA

    def body(q_ref, kv_ref, lat_ref, cos_ref, sin_ref, r_ref, gqn, gqr, gkn, gkr, dq_ref, dk_ref, dv_ref,
             dqo, dkvo, dkro, o_gqn, o_gqr, o_gkn, o_gkr):
        m, h = pl.program_id(0), pl.program_id(1)
        qr, kvr = q_ref[...], kv_ref[...]
        cos_v, sin_v, r_v = cos_ref[...], sin_ref[...], r_ref[...]
        f = lambda a, b, c, d, g1, g2, g3, g4: _qk_fn(a, b, c, d, g1, g2, g3, g4, cos_v, sin_v, r_v)
        _, vjp = jax.vjp(f, qr[:, :NOPE], qr[:, NOPE:], kvr[:, :NOPE], lat_ref[:, kr0:kr0 + ROPE],
                         gqn[...], gqr[...], gkn[...], gkr[...])
        dqv, dkv_ = dq_ref[...], dk_ref[...]
        d_qn, d_qr, d_kn, d_kr, g1, g2, g3, g4 = vjp((dqv[:, :NOPE], dqv[:, NOPE:], dkv_[:, :NOPE], dkv_[:, NOPE:]))
        dqo[:, :NOPE] = d_qn.astype(BF16)
        dqo[:, NOPE:] = d_qr.astype(BF16)
        dkvo[:, :NOPE] = d_kn.astype(BF16)
        dkvo[:, NOPE:] = dv_ref[...].astype(BF16)

        @pl.when(h == 0)
        def _():
            dkro[...] = jnp.zeros_like(dkro)

        dkro[...] += d_kr

        @pl.when((h == 0) & (m == 0))
        def _():
            for o in (o_gqn, o_gqr, o_gkn, o_gkr):
                o[...] = jnp.zeros_like(o)

        for o, g in zip((o_gqn, o_gqr, o_gkn, o_gkr), (g1, g2, g3, g4)):
            o[...] += g

    hb = lambda c: pl.BlockSpec((None, bm, c), lambda m, h: (h, m, 0))
    rb = lambda c: pl.BlockSpec((bm, c), lambda m, h: (m, 0))
    wb = lambda s: pl.BlockSpec(tuple(s), lambda m, h: (0, 0))
    return pl.pallas_call(
        body, name=name, grid=(T // bm, H),
        in_specs=[hb(QK), hb(NOPE + VDIM), rb(lat.shape[1]), rb(ROPE), rb(ROPE), wb(rmat.shape)]
        + [wb(g.shape) for g in gains] + [hb(QK), hb(QK), hb(VDIM)],
        out_specs=[hb(QK), hb(NOPE + VDIM), rb(ROPE)] + [wb(g.shape) for g in gains],
        out_shape=[jax.ShapeDtypeStruct((H, T, QK), BF16), jax.ShapeDtypeStruct((H, T, NOPE + VDIM), BF16),
                   jax.ShapeDtypeStruct((T, ROPE), F32)] + [jax.ShapeDtypeStruct(g.shape, F32) for g in gains],
        compiler_params=_params(("arbitrary", "arbitrary")),
    )(q_raw, kv_raw, lat, cos, sin, rmat, *gains, dq, dk, dv)


_NT = (((1,), (1,)), ((), ()))
_TN = (((0,), (0,)), ((), ()))


def _causal(blk):
    return lax.broadcasted_iota(jnp.int32, (blk, blk), 1) <= lax.broadcasted_iota(jnp.int32, (blk, blk), 0)


def _attn_fwd(q, k, v, seq, name, blk=512):
    H, T, _ = q.shape
    nb = T // seq
    blk = _pick(seq, blk)
    nq = seq // blk
    scale = float(QK) ** -0.5

    def body(q_ref, k_ref, v_ref, o_ref, lse_ref):
        qi = pl.program_id(2)
        qb = q_ref[...]

        def step(j, carry, diagonal):
            m, l, acc = carry
            ks = pl.ds(pl.multiple_of(j * blk, blk), blk)
            s = lax.dot_general(qb, k_ref[ks, :], _NT, preferred_element_type=F32) * scale
            if diagonal:
                s = jnp.where(_causal(blk), s, -jnp.inf)
            m_new = jnp.maximum(m, jnp.max(s, axis=1, keepdims=True))
            pr = jnp.exp(s - m_new)
            alpha = jnp.exp(m - m_new)
            l = alpha * l + jnp.sum(pr, axis=1, keepdims=True)
            acc = alpha * acc + jnp.dot(pr.astype(BF16), v_ref[ks, :], preferred_element_type=F32)
            return m_new, l, acc

        init = (jnp.full((blk, 1), -jnp.inf, F32), jnp.zeros((blk, 1), F32), jnp.zeros((blk, VDIM), F32))
        below = lax.fori_loop(0, qi, lambda j, c: step(j, c, False), init)
        m, l, acc = step(qi, below, True)
        o_ref[...] = (acc / l).astype(o_ref.dtype)
        lse_ref[...] = m + jnp.log(l)

    return pl.pallas_call(
        body, name=name, grid=(H, nb, nq),
        in_specs=[pl.BlockSpec((None, blk, QK), lambda h, b, i: (h, b * nq + i, 0)),
                  pl.BlockSpec((None, seq, QK), lambda h, b, i: (h, b, 0)),
                  pl.BlockSpec((None, seq, VDIM), lambda h, b, i: (h, b, 0))],
        out_specs=[pl.BlockSpec((blk, VDIM), lambda h, b, i: (b * nq + i, h)),
                   pl.BlockSpec((None, blk, 1), lambda h, b, i: (h, b * nq + i, 0))],
        out_shape=[jax.ShapeDtypeStruct((T, H * VDIM), BF16), jax.ShapeDtypeStruct((H, T, 1), F32)],
        compiler_params=_params(("parallel", "parallel", "parallel")),
    )(q, k, v)


def _attn_bwd(q, k, v, o, do, lse, seq, name, blk=512):
    H, T, _ = q.shape
    nb = T // seq
    blk = _pick(seq, blk)
    nq = seq // blk
    scale = float(QK) ** -0.5

    def body(q_ref, k_ref, v_ref, o_ref, do_ref, lse_ref, dq_ref, dk_ref, dv_ref):
        dk_ref[...] = jnp.zeros_like(dk_ref)
        dv_ref[...] = jnp.zeros_like(dv_ref)

        def qloop(i, carry):
            qs = pl.ds(pl.multiple_of(i * blk, blk), blk)
            qb = q_ref[qs, :]
            dob = do_ref[qs, :]
            dof = dob.astype(F32)
            lse_b = lse_ref[qs, :]
            delta = jnp.sum(dof * o_ref[qs, :].astype(F32), axis=1, keepdims=True)

            def kstep(j, dq_acc, diagonal):
                ks = pl.ds(pl.multiple_of(j * blk, blk), blk)
                kb = k_ref[ks, :]
                vb = v_ref[ks, :]
                s = lax.dot_general(qb, kb, _NT, preferred_element_type=F32) * scale
                pr = jnp.exp(s - lse_b)
                if diagonal:
                    pr = jnp.where(_causal(blk), pr, 0.0)
                dp = lax.dot_general(dob, vb, _NT, preferred_element_type=F32)
                ds = (pr * (dp - delta) * scale).astype(BF16)
                prb = pr.astype(BF16)
                dv_ref[ks, :] += lax.dot_general(prb, dob, _TN, preferred_element_type=F32)
                dk_ref[ks, :] += lax.dot_general(ds, qb, _TN, preferred_element_type=F32)
                return dq_acc + jnp.dot(ds, kb, preferred_element_type=F32)

            below = lax.fori_loop(0, i, lambda j, c: kstep(j, c, False), jnp.zeros((blk, QK), F32))
            dq_ref[qs, :] = kstep(i, below, True)
            return carry

        lax.fori_loop(0, nq, qloop, 0)

    hb = lambda c: pl.BlockSpec((None, seq, c), lambda h, b: (h, b, 0))
    cb = lambda: pl.BlockSpec((seq, VDIM), lambda h, b: (b, h))
    return pl.pallas_call(
        body, name=name, grid=(H, nb),
        in_specs=[hb(QK), hb(QK), hb(VDIM), cb(), cb(), hb(1)],
        out_specs=[hb(QK), hb(QK), hb(VDIM)],
        out_shape=[jax.ShapeDtypeStruct((H, T, QK), F32), jax.ShapeDtypeStruct((H, T, QK), F32),
                   jax.ShapeDtypeStruct((H, T, VDIM), F32)],
        compiler_params=_params(("parallel", "parallel")),
    )(q, k, v, o, do, lse)


def _gelu_ln_fwd(z, g, b, name):
    half = z.shape[1] // 2

    def fn(zz, gg, bb):
        return _gelu(zz[:, :half].astype(F32)), _layer_norm(_gelu(zz[:, half:].astype(F32)), gg, bb)

    return _rowcall(fn, [z], [g, b], [(half, BF16), (half, BF16)], name=name)


def _gelu_ln_bwd(z, d_u, d_vn, g, b, name):
    half = z.shape[1] // 2

    def gelu_and_slope(x):
        c, a = math.sqrt(2.0 / math.pi), 0.044715
        x2 = x * x
        t = jnp.tanh(c * x * (1.0 + a * x2))
        return 0.5 * x * (1.0 + t), 0.5 * (1.0 + t) + 0.5 * x * (1.0 - t * t) * (c * (1.0 + 3.0 * a * x2))

    def fn(zz, du, dvn, gg, bb):
        du, dvn = du.astype(F32), dvn.astype(F32)
        _, su = gelu_and_slope(zz[:, :half].astype(F32))
        v, sv = gelu_and_slope(zz[:, half:].astype(F32))
        xc = v - jnp.mean(v, axis=-1, keepdims=True)
        rstd = lax.rsqrt(jnp.mean(xc * xc, axis=-1, keepdims=True) + EPS)
        y = xc * rstd
        dy = dvn * gg
        dv = rstd * (dy - jnp.mean(dy, axis=-1, keepdims=True) - y * jnp.mean(dy * y, axis=-1, keepdims=True))
        dg = jnp.sum(dvn * y, axis=0, keepdims=True)
        db = jnp.sum(dvn, axis=0, keepdims=True)
        return jnp.concatenate([du * su, dv * sv], axis=1), dg, db

    return _rowcall(fn, [z, d_u, d_vn], [g, b], [(z.shape[1], BF16)], [g.shape, b.shape], bm=128, name=name)


def _tril_bf16(ws):
    t = lax.broadcasted_iota(jnp.int32, ws.shape, 0)
    s = lax.broadcasted_iota(jnp.int32, ws.shape, 1)
    return jnp.where(s <= t, ws, 0.0).astype(BF16)


def _sgu_fwd(u, vn, ws, bs, name, bm=2048):
    T, half = u.shape
    gd = half // GROUPS
    bm = _pick(T, bm, CHUNK)
    nc = bm // CHUNK

    def body(u_ref, vn_ref, ws_ref, bs_ref, y_ref):
        wm = _tril_bf16(ws_ref[...])
        bias = bs_ref[...]
        for c in range(nc):
            rs = slice(c * CHUNK, (c + 1) * CHUNK)
            sv = jnp.dot(wm, vn_ref[rs, :], preferred_element_type=F32) + bias
            y_ref[rs, :] = (u_ref[rs, :].astype(F32) * sv).astype(y_ref.dtype)

    tb = lambda: pl.BlockSpec((bm, gd), lambda g, i: (i, g))
    return pl.pallas_call(
        body, name=name, grid=(GROUPS, T // bm),
        in_specs=[tb(), tb(), pl.BlockSpec((None, CHUNK, CHUNK), lambda g, i: (g, 0, 0)),
                  pl.BlockSpec((None, CHUNK, 1), lambda g, i: (g, 0, 0))],
        out_specs=tb(),
        out_shape=jax.ShapeDtypeStruct((T, half), BF16),
        compiler_params=_params(("parallel", "parallel")),
    )(u, vn, ws, bs)


def _sgu_bwd(u, vn, dy, ws, bs, name, bm=1024):
    T, half = u.shape
    gd = half // GROUPS
    bm = _pick(T, bm, CHUNK)
    nc = bm // CHUNK

    def body(u_ref, vn_ref, dy_ref, ws_ref, bs_ref, du_ref, dvn_ref, dws_ref, dbs_ref):
        @pl.when(pl.program_id(1) == 0)
        def _():
            dws_ref[...] = jnp.zeros_like(dws_ref)
            dbs_ref[...] = jnp.zeros_like(dbs_ref)

        wm = _tril_bf16(ws_ref[...])
        bias = bs_ref[...]
        dws = jnp.zeros((CHUNK, CHUNK), F32)
        dbs = jnp.zeros((CHUNK, 1), F32)
        for c in range(nc):
            rs = slice(c * CHUNK, (c + 1) * CHUNK)
            vb = vn_ref[rs, :]
            dyb = dy_ref[rs, :].astype(F32)
            sv = jnp.dot(wm, vb, preferred_element_type=F32) + bias
            du_ref[rs, :] = (dyb * sv).astype(du_ref.dtype)
            dsv = dyb * u_ref[rs, :].astype(F32)
            dsb = dsv.astype(BF16)
            dvn_ref[rs, :] = lax.dot_general(wm, dsb, _TN, preferred_element_type=F32).astype(dvn_ref.dtype)
            dws = dws + lax.dot_general(dsb, vb, _NT, preferred_element_type=F32)
            dbs = dbs + jnp.sum(dsv, axis=1, keepdims=True)
        t = lax.broadcasted_iota(jnp.int32, (CHUNK, CHUNK), 0)
        s = lax.broadcasted_iota(jnp.int32, (CHUNK, CHUNK), 1)
        dws_ref[...] += jnp.where(s <= t, dws, 0.0)
        dbs_ref[...] += dbs

    tb = lambda: pl.BlockSpec((bm, gd), lambda g, i: (i, g))
    wsb = lambda: pl.BlockSpec((None, CHUNK, CHUNK), lambda g, i: (g, 0, 0))
    bsb = lambda: pl.BlockSpec((None, CHUNK, 1), lambda g, i: (g, 0, 0))
    return pl.pallas_call(
        body, name=name, grid=(GROUPS, T // bm),
        in_specs=[tb(), tb(), tb(), wsb(), bsb()],
        out_specs=[tb(), tb(), wsb(), bsb()],
        out_shape=[jax.ShapeDtypeStruct((T, half), BF16), jax.ShapeDtypeStruct((T, half), BF16),
                   jax.ShapeDtypeStruct(ws.shape, F32), jax.ShapeDtypeStruct(bs.shape, F32)],
        compiler_params=_params(("parallel", "arbitrary")),
    )(u, vn, dy, ws, bs)


def _loss_head(y, t, name):
    d_model = y.shape[1]

    def fn(yy, tt):
        d = yy - tt
        part = 0.5 * jnp.sum(jnp.mean(d * d, axis=-1, keepdims=True), axis=0, keepdims=True)
        return d / d_model, jnp.zeros((1, LANE), F32) + part

    dy, part = _rowcall(fn, [y, t], [], [(d_model, F32)], [(1, LANE)], name=name)
    return dy, part[0, 0]


def _adamw(parts, w, m, v, prev, layer, name):
    L, R, C = w.shape
    br = _pick(R, max(8, (128 * 1024) // C // 8 * 8), 8)
    c1 = 1.0 - B1 ** STEP
    c2 = 1.0 - B2 ** STEP
    if prev is None:
        prev = [lax.empty(w.shape, F32) for _ in range(4)]

    def body(p_ref, w_ref, m_ref, v_ref, a0, a1, a2, a3, g_o, d_o, m_o, v_o, token):
        g = p_ref[0].astype(F32)
        for d in range(1, N_DEV):
            g = g + p_ref[d].astype(F32)
        mn = B1 * m_ref[...] + (1.0 - B1) * g
        vn = B2 * v_ref[...] + (1.0 - B2) * (g * g)
        g_o[...] = g
        m_o[...] = mn
        v_o[...] = vn
        d_o[...] = -LR * ((mn / c1) / (jnp.sqrt(vn / c2) + ADAM_EPS) + WD * w_ref[...])
        token[...] = jnp.zeros_like(token)

    blk = lambda: pl.BlockSpec((None, br, C), lambda i: (layer, i, 0))
    anywhere = pl.BlockSpec(memory_space=pl.ANY)
    outs = pl.pallas_call(
        body, name=name, grid=(R // br,),
        in_specs=[pl.BlockSpec((N_DEV, br, C), lambda i: (0, i, 0)), blk(), blk(), blk()] + [anywhere] * 4,
        out_specs=[blk(), blk(), blk(), blk(), pl.BlockSpec((8, LANE), lambda i: (0, 0))],
        out_shape=[jax.ShapeDtypeStruct((L, R, C), F32)] * 4 + [jax.ShapeDtypeStruct((8, LANE), F32)],
        input_output_aliases={4: 0, 5: 1, 6: 2, 7: 3},
        compiler_params=_params(("arbitrary",)),
    )(parts, w, m, v, *prev)
    return list(outs[:4]), outs[4]


def _mesh_pos():
    return lax.axis_index("x"), lax.axis_index("y"), lax.axis_index("c")


def _flip(pos, k):
    x, y, c = pos
    px = 1 - x if k & 4 else x
    py = 1 - y if k & 2 else y
    pc = 1 - c if k & 1 else c
    return px, py, pc


HBM_SPEC = pl.BlockSpec(memory_space=pltpu.HBM)
SEM_SPEC = pl.BlockSpec(memory_space=pltpu.SEMAPHORE)
EFFECT = pltpu.SideEffectType.DATAFLOW_SIDE_EFFECTING


def _hbm(a):
    return pltpu.with_memory_space_constraint(a, pltpu.HBM)


def _device_index():
    x, y, c = _mesh_pos()
    return 4 * x + 2 * y + c


def _peer_copy(src, land, send, recv, a, k, pos, scatter):
    peer = _flip(pos, k)
    me = 4 * pos[0] + 2 * pos[1] + pos[2]
    piece = src.at[4 * peer[0] + 2 * peer[1] + peer[2]] if scatter else src
    return pltpu.make_async_remote_copy(
        src_ref=piece, dst_ref=land.at[me], send_sem=send.at[7 * a + k - 1], recv_sem=recv.at[7 * a + k - 1],
        device_id=peer, device_id_type=pl.DeviceIdType.MESH)


def _xchg_start(srcs, scatter, after, name):
    n = len(srcs)

    def body(*refs):
        src, land = refs[:n], refs[n:2 * n]
        send, recv, token = refs[2 * n + 1], refs[2 * n + 2], refs[-1]
        pos = _mesh_pos()
        for k in range(1, N_DEV):
            for a in range(n):
                _peer_copy(src[a], land[a], send, recv, a, k, pos, scatter).start()
        token[...] = jnp.zeros_like(token)

    lands = [lax.empty(s.shape if scatter else (N_DEV,) + s.shape, s.dtype) for s in srcs]
    outs = pl.pallas_call(
        body, name=name,
        out_shape=(pltpu.SemaphoreType.DMA((7 * n,)), pltpu.SemaphoreType.DMA((7 * n,)),
                   *[pltpu.HBM(s.shape, s.dtype) for s in srcs], *[pltpu.HBM(l.shape, l.dtype) for l in lands],
                   jax.ShapeDtypeStruct((8, LANE), F32)),
        in_specs=[HBM_SPEC] * (2 * n) + [pl.BlockSpec(memory_space=pl.ANY)],
        out_specs=(SEM_SPEC, SEM_SPEC, *[HBM_SPEC] * (2 * n), pl.BlockSpec(memory_space=pltpu.VMEM)),
        input_output_aliases={q: 2 + q for q in range(2 * n)},
        compiler_params=pltpu.CompilerParams(has_side_effects=EFFECT),
    )(*[_hbm(s) for s in srcs], *[_hbm(l) for l in lands], after)
    handle = dict(send=outs[0], recv=outs[1], srcs=list(outs[2:2 + n]), lands=list(outs[2 + n:2 + 2 * n]),
                  scatter=scatter)
    return handle, outs[-1]


def _xchg_wait(handle, after, name):
    srcs, lands, scatter = handle['srcs'], handle['lands'], handle['scatter']
    n = len(srcs)

    def body(*refs):
        src, land = refs[:n], refs[n:2 * n]
        send, recv = refs[2 * n], refs[2 * n + 1]
        pos = _mesh_pos()
        for k in range(1, N_DEV):
            for a in range(n):
                cp = _peer_copy(src[a], land[a], send, recv, a, k, pos, scatter)
                cp.wait_send()
                cp.wait_recv()

    outs = pl.pallas_call(
        body, name=name,
        out_shape=[pltpu.HBM(s.shape, s.dtype) for s in srcs] + [pltpu.HBM(l.shape, l.dtype) for l in lands],
        in_specs=[HBM_SPEC] * (2 * n) + [SEM_SPEC, SEM_SPEC, pl.BlockSpec(memory_space=pl.ANY)],
        out_specs=[HBM_SPEC] * (2 * n),
        input_output_aliases={q: q for q in range(2 * n)},
        compiler_params=pltpu.CompilerParams(has_side_effects=EFFECT),
    )(*srcs, *lands, handle['send'], handle['recv'], after)
    me = _device_index()
    full = []
    for src, land in zip(outs[:n], outs[n:]):
        mine = lax.dynamic_index_in_dim(src, me, 0, keepdims=False) if scatter else src
        full.append(lax.dynamic_update_index_in_dim(land, mine, me, 0))
    return full


def _pack(parts):
    flat = jnp.concatenate([q.reshape(-1) for q in parts])
    pad = (-flat.shape[0]) % (8 * LANE)
    return jnp.pad(flat, (0, pad)).reshape(-1, LANE)


def _unpack(packed, like):
    flat = packed.reshape(-1)
    out, o = [], 0
    for q in like:
        out.append(flat[o:o + q.size].reshape(q.shape))
        o += q.size
    return out


def kernel(x, p, positions, norm_mix, norm_ffn, norm_ple, mla_w_down, mla_q_lora_g, mla_kv_lora_g, mla_w_uq, mla_w_ukv, mla_q_nope_g, mla_q_rope_g, mla_k_nope_g, mla_k_rope_g, mla_w_out, gmlp_w_in, gmlp_ln_g, gmlp_ln_b, gmlp_w_s, gmlp_b_s, gmlp_w_out, ffn_w_up, ffn_w_down, ple_w_gate, ple_w_proj, loss_target, m_norm_mix, m_norm_ffn, m_norm_ple, m_mla_w_down, m_mla_q_lora_g, m_mla_kv_lora_g, m_mla_w_uq, m_mla_w_ukv, m_mla_q_nope_g, m_mla_q_rope_g, m_mla_k_nope_g, m_mla_k_rope_g, m_mla_w_out, m_gmlp_w_in, m_gmlp_ln_g, m_gmlp_ln_b, m_gmlp_w_s, m_gmlp_b_s, m_gmlp_w_out, m_ffn_w_up, m_ffn_w_down, m_ple_w_gate, m_ple_w_proj, v_norm_mix, v_norm_ffn, v_norm_ple, v_mla_w_down, v_mla_q_lora_g, v_mla_kv_lora_g, v_mla_w_uq, v_mla_w_ukv, v_mla_q_nope_g, v_mla_q_rope_g, v_mla_k_nope_g, v_mla_k_rope_g, v_mla_w_out, v_gmlp_w_in, v_gmlp_ln_g, v_gmlp_ln_b, v_gmlp_w_s, v_gmlp_b_s, v_gmlp_w_out, v_ffn_w_up, v_ffn_w_down, v_ple_w_gate, v_ple_w_proj):
    W = dict(zip(WEIGHTS, (norm_mix, norm_ffn, norm_ple, mla_w_down, mla_q_lora_g, mla_kv_lora_g, mla_w_uq, mla_w_ukv, mla_q_nope_g, mla_q_rope_g, mla_k_nope_g, mla_k_rope_g, mla_w_out, gmlp_w_in, gmlp_ln_g, gmlp_ln_b, gmlp_w_s, gmlp_b_s, gmlp_w_out, ffn_w_up, ffn_w_down, ple_w_gate, ple_w_proj)))
    M1 = dict(zip(WEIGHTS, (m_norm_mix, m_norm_ffn, m_norm_ple, m_mla_w_down, m_mla_q_lora_g, m_mla_kv_lora_g, m_mla_w_uq, m_mla_w_ukv, m_mla_q_nope_g, m_mla_q_rope_g, m_mla_k_nope_g, m_mla_k_rope_g, m_mla_w_out, m_gmlp_w_in, m_gmlp_ln_g, m_gmlp_ln_b, m_gmlp_w_s, m_gmlp_b_s, m_gmlp_w_out, m_ffn_w_up, m_ffn_w_down, m_ple_w_gate, m_ple_w_proj)))
    M2 = dict(zip(WEIGHTS, (v_norm_mix, v_norm_ffn, v_norm_ple, v_mla_w_down, v_mla_q_lora_g, v_mla_kv_lora_g, v_mla_w_uq, v_mla_w_ukv, v_mla_q_nope_g, v_mla_q_rope_g, v_mla_k_nope_g, v_mla_k_rope_g, v_mla_w_out, v_gmlp_w_in, v_gmlp_ln_g, v_gmlp_ln_b, v_gmlp_w_s, v_gmlp_b_s, v_gmlp_w_out, v_ffn_w_up, v_ffn_w_down, v_ple_w_gate, v_ple_w_proj)))

    nb, seq, d_model = x.shape
    T = nb * seq
    depth = norm_mix.shape[0]
    h = x.reshape(T, d_model)
    target = loss_target.reshape(T, d_model)
    p_bf = p.reshape(depth, T, p.shape[-1]).astype(BF16)

    def stage_keys(st):
        i, second = divmod(st, 2)
        if second:
            return [(n, i) for n in ('ffn_w_up', 'ffn_w_down', 'ple_w_gate', 'ple_w_proj')]
        mix = (['mla_w_down', 'mla_w_uq', 'mla_w_ukv', 'mla_w_out'] if i % 2 == 0 else
               ['gmlp_w_in', 'gmlp_ln_g', 'gmlp_ln_b', 'gmlp_w_out'])
        return [(n, i // 2) for n in mix]

    FW = {n: {} for n in SHARDED}

    def start_weights(st, after):
        keys = stage_keys(st)
        srcs = [W[n][l] if n in F32_PAYLOAD else W[n][l].astype(BF16) for n, l in keys]
        handle, token = _xchg_start(srcs, False, after, "weights_start%d" % st)
        return (keys, handle), [token]

    def wait_weights(pending, st, after):
        keys, handle = pending
        landed = _xchg_wait(handle, after, "weights_wait%d" % st)
        for (n, l), full in zip(keys, landed):
            if SHARD_AXIS[n] == 1:
                FW[n][l] = full.reshape((-1,) + full.shape[2:])
            elif n in ('mla_w_uq', 'mla_w_ukv'):
                FW[n][l] = full
            else:
                FW[n][l] = jnp.transpose(full, (1, 0, 2)).reshape(full.shape[1], -1)
        return landed[0]

    row = lambda a: a.reshape(1, -1)
    cos, sin = _rope_tables(positions.reshape(T, 1), "rope_tables")
    rmat = _rot_matrix()

    saved = []
    pending, _ = start_weights(0, h)
    for i in range(depth):
        j = i // 2
        landed = wait_weights(pending, 2 * i, h)
        pending, token = start_weights(2 * i + 1, landed)
        s = {}
        s['h0'] = h
        hn = _rms_fwd(h, row(W['norm_mix'][i]), "rms_fwd", deps=token)
        s['hn'] = hn
        if i % 2 == 0:
            gains = [row(W['mla_q_nope_g'][j]), row(W['mla_q_rope_g'][j]), row(W['mla_k_nope_g'][j]),
                     row(W['mla_k_rope_g'][j])]
            lat = _mm(hn, FW['mla_w_down'][j], name="mla_down")
            cq, ckv = _prep1_fwd(lat, row(W['mla_q_lora_g'][j]), row(W['mla_kv_lora_g'][j]), "mla_prep1")
            q_raw = _mm(cq, FW['mla_w_uq'][j], out_blocks=HEADS, name="mla_uq")
            kv_raw = _mm(ckv, FW['mla_w_ukv'][j], out_blocks=HEADS, name="mla_ukv")
            q, k, v = _prep2_fwd(q_raw, kv_raw, lat, cos, sin, rmat, gains, "mla_prep2")
            o, lse = _attn_fwd(q, k, v, seq, "attn_fwd")
            h = _mm(o, FW['mla_w_out'][j], extras=(h,), epilogue=lambda acc, res: (res + acc,), name="mla_out")
            s.update(lat=lat, cq=cq, ckv=ckv, q_raw=q_raw, kv_raw=kv_raw, q=q, k=k, v=v, o=o, lse=lse, gains=gains)
        else:
            z = _mm(hn, FW['gmlp_w_in'][j], out_dtypes=(BF16,), name="gmlp_in")
            u, vn = _gelu_ln_fwd(z, row(FW['gmlp_ln_g'][j]), row(FW['gmlp_ln_b'][j]), "gmlp_gelu_ln")
            bs3 = W['gmlp_b_s'][j][:, :, None]
            y = _sgu_fwd(u, vn, W['gmlp_w_s'][j], bs3, "gmlp_sgu")
            h = _mm(y, FW['gmlp_w_out'][j], extras=(h,), epilogue=lambda acc, res: (res + acc,), name="gmlp_out")
            s.update(z=z, u=u, vn=vn, y=y, bs3=bs3)
        s['h1'] = h
        landed = wait_weights(pending, 2 * i + 1, h)
        token = []
        if i + 1 < depth:
            pending, token = start_weights(2 * i + 2, landed)
        hn2 = _rms_fwd(h, row(W['norm_ffn'][i]), "rms_fwd", deps=token)
        a, r = _mm(hn2, FW['ffn_w_up'][i], epilogue=lambda acc: (acc, jnp.square(jnp.maximum(acc, 0.0))),
                   out_dtypes=(BF16, BF16), name="ffn_up")
        h = _mm(r, FW['ffn_w_down'][i], extras=(h,), epilogue=lambda acc, res: (res + acc,), name="ffn_down")
        s.update(hn2=hn2, a=a, r=r, h2=h)
        hn3 = _rms_fwd(h, row(W['norm_ple'][i]), "rms_fwd")
        gt = _mm(hn3, FW['ple_w_gate'][i], name="ple_gate")
        pp, h = _mm(p_bf[i], FW['ple_w_proj'][i], extras=(gt, h),
                    epilogue=lambda acc, g_, res: (acc, res + _sigmoid(g_) * acc), out_dtypes=(F32, F32),
                    name="ple_proj")
        s.update(hn3=hn3, gt=gt, pp=pp)
        saved.append(s)

    dh, loss_part = _loss_head(h, target, "loss_head")
    loss = lax.psum(loss_part, MESH_AXES)

    G = {n: [None] * W[n].shape[0] for n in REPLICATED}
    res = {}
    flying = []

    def shard3(n):
        shp = W[n].shape
        return shp[0], int(np.prod(shp[1:-1])), shp[-1]

    def by_owner(g):
        return g.reshape((N_DEV, g.shape[0] // N_DEV) + g.shape[1:])

    def send_grads(tag, grads):
        handle, token = _xchg_start([g for _, g in grads], True, grads[-1][1], "grads_start_" + tag)
        flying.append((tag, [key for key, _ in grads], handle))
        return [token]

    def land_grads(after):
        tag, keys, handle = flying.pop(0)
        done = []
        for (n, l), full in zip(keys, _xchg_wait(handle, after, "grads_wait_" + tag)):
            dims = shard3(n)
            res[n], token = _adamw(full.reshape((N_DEV,) + dims[1:]), W[n].reshape(dims), M1[n].reshape(dims),
                                   M2[n].reshape(dims), res.get(n), l, "adamw_" + n)
            done.append(token)
        return done

    def start_small(names, tag, after):
        handle, token = _xchg_start([_pack([jnp.stack(G[n]) for n in names])], False, after, "small_start_" + tag)
        return (names, handle), [token]

    def land_small(pending_small, tag, after):
        names, handle = pending_small
        (parts,) = _xchg_wait(handle, after, "small_wait_" + tag)
        like = [W[n] for n in names]
        outs, _ = _adamw(parts, _pack(like)[None], _pack([M1[n] for n in names])[None],
                         _pack([M2[n] for n in names])[None], None, 0, "adamw_small_" + tag)
        unpacked = [_unpack(o, like) for o in outs]
        for idx, n in enumerate(names):
            res[n] = [unpacked[q][idx] for q in range(4)]

    spatial = ['gmlp_w_s', 'gmlp_b_s']
    token = []
    for i in reversed(range(depth)):
        j = i // 2
        s = saved[i]
        def ple_elem(d, g_, pq):
            sg = _sigmoid(g_)
            return d * sg, d * pq * sg * (1.0 - sg)

        d_pp, d_gt = _rowcall(ple_elem, [dh, s['gt'], s['pp']], [], [(d_model, BF16), (d_model, BF16)], name="ple_bwd")
        g_proj = _mm(p_bf[i], d_pp, ta=True, out_dtypes=(BF16,), name="ple_proj_dw")
        g_proj = jnp.transpose(g_proj.reshape(g_proj.shape[0], N_DEV, -1), (1, 0, 2))
        g_gate = _mm(s['hn3'], d_gt, ta=True, out_dtypes=(BF16,), name="ple_gate_dw")
        d_hn3 = _mm(d_gt, FW['ple_w_gate'][i], tb=True, out_dtypes=(BF16,), name="ple_gate_dx")
        dh, dh_bf, dg = _rms_bwd(s['h2'], d_hn3, dh, row(W['norm_ple'][i]), "rms_bwd", deps=token)
        G['norm_ple'][i] = dg[0]
        d_a = _mm(dh_bf, FW['ffn_w_down'][i], tb=True, extras=(s['a'],),
                  epilogue=lambda acc, a_: (acc * (2.0 * jnp.maximum(a_.astype(F32), 0.0)),), out_dtypes=(BF16,),
                  name="ffn_down_dx")
        g_down = _mm(s['r'], dh_bf, ta=True, out_dtypes=(BF16,), name="ffn_down_dw")
        g_up = _mm(s['hn2'], d_a, ta=True, out_blocks=N_DEV, out_dtypes=(BF16,), name="ffn_up_dw")
        token = send_grads("mlp%d" % i, [(('ple_w_proj', i), g_proj), (('ple_w_gate', i), by_owner(g_gate)),
                                         (('ffn_w_down', i), by_owner(g_down)), (('ffn_w_up', i), g_up)])
        if len(flying) > 1:
            token = token + land_grads(g_up)
        d_hn2 = _mm(d_a, FW['ffn_w_up'][i], tb=True, out_dtypes=(BF16,), name="ffn_up_dx")
        dh, dh_bf, dg = _rms_bwd(s['h1'], d_hn2, dh, row(W['norm_ffn'][i]), "rms_bwd", deps=token)
        G['norm_ffn'][i] = dg[0]
        if i % 2 == 0:
            d_o = _mm(dh_bf, FW['mla_w_out'][j], tb=True, out_dtypes=(BF16,), name="mla_out_dx")
            g_out = _mm(s['o'], dh_bf, ta=True, out_dtypes=(BF16,), name="mla_out_dw")
            dq, dk, dv = _attn_bwd(s['q'], s['k'], s['v'], s['o'], d_o, s['lse'], seq, "attn_bwd")
            d_q_raw, d_kv_raw, d_kr, g1, g2, g3, g4 = _prep2_bwd(
                s['q_raw'], s['kv_raw'], s['lat'], cos, sin, rmat, s['gains'], dq, dk, dv, "mla_prep2_bwd")
            G['mla_q_nope_g'][j], G['mla_q_rope_g'][j] = g1[0], g2[0]
            G['mla_k_nope_g'][j], G['mla_k_rope_g'][j] = g3[0], g4[0]
            g_uq = _mm(s['cq'], d_q_raw, ta=True, out_blocks=N_DEV, out_dtypes=(BF16,), name="mla_uq_dw")
            g_ukv = _mm(s['ckv'], d_kv_raw, ta=True, out_blocks=N_DEV, out_dtypes=(BF16,), name="mla_ukv_dw")
            d_cq = _mm(d_q_raw, FW['mla_w_uq'][j], tb=True, out_dtypes=(BF16,), name="mla_uq_dx")
            d_ckv = _mm(d_kv_raw, FW['mla_w_ukv'][j], tb=True, out_dtypes=(BF16,), name="mla_ukv_dx")
            d_lat, dga, dgb = _prep1_bwd(s['lat'], d_cq, d_ckv, d_kr, row(W['mla_q_lora_g'][j]),
                                         row(W['mla_kv_lora_g'][j]), "mla_prep1_bwd")
            G['mla_q_lora_g'][j], G['mla_kv_lora_g'][j] = dga[0], dgb[0]
            g_down = _mm(s['hn'], d_lat, ta=True, out_dtypes=(BF16,), name="mla_down_dw")
            grads = [(('mla_w_out', j), by_owner(g_out)), (('mla_w_uq', j), g_uq), (('mla_w_ukv', j), g_ukv),
                     (('mla_w_down', j), by_owner(g_down))]
            d_hn = _mm(d_lat, FW['mla_w_down'][j], tb=True, out_dtypes=(BF16,), name="mla_down_dx")
        else:
            d_y = _mm(dh_bf, FW['gmlp_w_out'][j], tb=True, out_dtypes=(BF16,), name="gmlp_out_dx")
            g_out = _mm(s['y'], dh_bf, ta=True, out_dtypes=(BF16,), name="gmlp_out_dw")
            d_u, d_vn, d_ws, d_bs = _sgu_bwd(s['u'], s['vn'], d_y, W['gmlp_w_s'][j], s['bs3'], "gmlp_sgu_bwd")
            G['gmlp_w_s'][j], G['gmlp_b_s'][j] = d_ws, d_bs[:, :, 0]
            d_z, d_lg, d_lb = _gelu_ln_bwd(s['z'], d_u, d_vn, row(FW['gmlp_ln_g'][j]), row(FW['gmlp_ln_b'][j]),
                                           "gmlp_gelu_ln_bwd")
            g_in = _mm(s['hn'], d_z, ta=True, out_blocks=N_DEV, out_dtypes=(BF16,), name="gmlp_in_dw")
            grads = [(('gmlp_w_out', j), by_owner(g_out)), (('gmlp_ln_g', j), by_owner(d_lg[0])),
                     (('gmlp_ln_b', j), by_owner(d_lb[0])), (('gmlp_w_in', j), g_in)]
            d_hn = _mm(d_z, FW['gmlp_w_in'][j], tb=True, out_dtypes=(BF16,), name="gmlp_in_dx")
        token = send_grads("mix%d" % i, grads)
        if len(flying) > 1:
            token = token + land_grads(grads[-1][1])
        dh, _, dg = _rms_bwd(s['h0'], d_hn, dh, row(W['norm_mix'][i]), "rms_bwd", deps=token)
        G['norm_mix'][i] = dg[0]
        token = []
        if i == 1:
            small_a, token = start_small(spatial, "spatial", dh)
    grad_x = dh.reshape(x.shape)

    small_b, _ = start_small([n for n in REPLICATED if n not in spatial], "gains", dh)
    while flying:
        land_grads(dh)
    land_small(small_a, "spatial", dh)
    land_small(small_b, "gains", dh)

    out = lambda q: [res[n][q].reshape(W[n].shape) for n in WEIGHTS]
    return (loss, grad_x, *out(0), *out(1), *out(2), *out(3))
```

```python
import math

import numpy as np
import jax
import jax.numpy as jnp
from jax import lax
from jax.experimental import pallas as pl
from jax.experimental.pallas import tpu as pltpu

F32 = jnp.float32
BF16 = jnp.bfloat16

N_DEV = 8
MESH_AXES = ("x", "y", "c")
HEADS = 8
NOPE = 128
ROPE = 64
VDIM = 128
QK = NOPE + ROPE
Q_LORA = 384
KV_LORA = 256
ROPE_BASE = 10000.0
CHUNK = 128
GROUPS = 8
EPS = 1e-6
LR, B1, B2, ADAM_EPS, WD, STEP = 0.001, 0.9, 0.999, 1e-08, 0.01, 10
LANE = 128
VMEM_LIMIT = 56 * 1024 * 1024
MM_VMEM_BUDGET = 40 * 1024 * 1024

WEIGHTS = ['norm_mix', 'norm_ffn', 'norm_ple', 'mla_w_down', 'mla_q_lora_g', 'mla_kv_lora_g', 'mla_w_uq',
           'mla_w_ukv', 'mla_q_nope_g', 'mla_q_rope_g', 'mla_k_nope_g', 'mla_k_rope_g', 'mla_w_out', 'gmlp_w_in',
           'gmlp_ln_g', 'gmlp_ln_b', 'gmlp_w_s', 'gmlp_b_s', 'gmlp_w_out', 'ffn_w_up', 'ffn_w_down', 'ple_w_gate',
           'ple_w_proj']
SHARD_AXIS = {'mla_w_down': 1, 'mla_w_uq': 2, 'mla_w_ukv': 2, 'mla_w_out': 1, 'gmlp_w_in': 2, 'gmlp_ln_g': 1,
              'gmlp_ln_b': 1, 'gmlp_w_out': 1, 'ffn_w_up': 2, 'ffn_w_down': 1, 'ple_w_gate': 1, 'ple_w_proj': 2}
SHARDED = list(SHARD_AXIS)
REPLICATED = [n for n in WEIGHTS if n not in SHARD_AXIS]
F32_PAYLOAD = ('gmlp_ln_g', 'gmlp_ln_b')


def _pick(dim, pref, align=LANE):
    if dim <= pref:
        return dim
    b = (pref // align) * align
    while b >= align:
        if dim % b == 0:
            return b
        b -= align
    return dim


def _params(sem):
    return pltpu.CompilerParams(dimension_semantics=sem, vmem_limit_bytes=VMEM_LIMIT)


def _mm(a, b, *, ta=False, tb=False, extras=(), rows=(), epilogue=None, out_dtypes=(F32,), out_blocks=None,
        deps=(), name, bn=1024):
    a3, b3 = a.ndim == 3, b.ndim == 3
    assert not (ta and a3)
    if ta:
        K, M = a.shape
        ka = K
    elif a3:
        M, ka = a.shape[1:]
        K = a.shape[0] * ka
    else:
        M, K = a.shape
        ka = K
    if tb:
        N, kb = b.shape[-2:]
        nb = N
        K2 = b.shape[0] * kb if b3 else kb
    else:
        kb, nb = b.shape[-2:]
        K2 = kb
        N = b.shape[0] * nb if b3 else nb
    assert K == K2, (a.shape, b.shape, ta, tb)
    no_ = N // out_blocks if out_blocks else N
    assert not (out_blocks and extras)
    size = lambda t: jnp.dtype(t).itemsize
    per_out = sum(size(e.dtype) for e in extras) + sum(size(t) for t in out_dtypes)
    bn = _pick(min(nb, no_), bn)
    k_lim = min(ka, kb)
    fits = lambda m, k: 2 * (m * k * size(a.dtype) + k * bn * size(b.dtype) + m * bn * per_out) + 4 * m * bn
    ms = [m for m in sorted({min(M, c) for c in (2048, 1024, 512, 256)}, reverse=True) if M % m == 0]
    ks = [k for k in dict.fromkeys((k_lim, 2048, 1024, 512, 256)) if k <= k_lim and k_lim % k == 0]
    bm, bk = next(((m, k) for k in ks for m in ms if fits(m, k) <= MM_VMEM_BUDGET), (ms[-1], ks[-1]))
    nk = K // bk
    ne, no = len(extras) + len(rows), len(out_dtypes)
    first_out = 2 + ne + len(deps)
    dims = (((0,) if ta else (1,), (1,) if tb else (0,)), ((), ()))

    def finish(r, e_refs, o_refs):
        outs = epilogue(r, *[e[...] for e in e_refs]) if epilogue is not None else (r,)
        for o, v in zip(o_refs, outs):
            o[...] = v.astype(o.dtype)

    def body(*refs):
        a_ref, b_ref = refs[0], refs[1]
        e_refs = refs[2:2 + ne]
        o_refs = refs[first_out:first_out + no]
        part = lax.dot_general(a_ref[...].astype(BF16), b_ref[...].astype(BF16), dims, preferred_element_type=F32)
        if nk == 1:
            finish(part, e_refs, o_refs)
            return
        acc = refs[-1]
        k = pl.program_id(2)

        @pl.when(k == 0)
        def _():
            acc[...] = part

        @pl.when(k > 0)
        def _():
            acc[...] += part

        @pl.when(k == nk - 1)
        def _():
            finish(acc[...], e_refs, o_refs)

    ka_t, kb_t, nb_t, no_t = ka // bk, kb // bk, nb // bn, no_ // bn
    if ta:
        a_spec = pl.BlockSpec((bk, bm), lambda i, j, k: (k, i))
    elif a3:
        a_spec = pl.BlockSpec((None, bm, bk), lambda i, j, k: (k // ka_t, i, k % ka_t))
    else:
        a_spec = pl.BlockSpec((bm, bk), lambda i, j, k: (i, k))
    if tb:
        b_spec = (pl.BlockSpec((None, bn, bk), lambda i, j, k: (k // kb_t, j, k % kb_t)) if b3 else
                  pl.BlockSpec((bn, bk), lambda i, j, k: (j, k)))
    else:
        b_spec = (pl.BlockSpec((None, bk, bn), lambda i, j, k: (j // nb_t, k, j % nb_t)) if b3 else
                  pl.BlockSpec((bk, bn), lambda i, j, k: (k, j)))
    if out_blocks:
        o_spec = lambda: pl.BlockSpec((None, bm, bn), lambda i, j, k: (j // no_t, i, j % no_t))
        o_shape = (out_blocks, M, no_)
    else:
        o_spec = lambda: pl.BlockSpec((bm, bn), lambda i, j, k: (i, j))
        o_shape = (M, N)
    outs = pl.pallas_call(
        body, name=name,
        grid=(M // bm, N // bn, nk),
        in_specs=[a_spec, b_spec] + [pl.BlockSpec((bm, bn), lambda i, j, k: (i, j)) for _ in extras]
        + [pl.BlockSpec((1, bn), lambda i, j, k: (0, j)) for _ in rows]
        + [pl.BlockSpec(memory_space=pl.ANY)] * len(deps),
        out_specs=[o_spec() for _ in out_dtypes],
        out_shape=[jax.ShapeDtypeStruct(o_shape, dt) for dt in out_dtypes],
        scratch_shapes=[pltpu.VMEM((bm, bn), F32)] if nk > 1 else [],
        compiler_params=_params(("parallel", "parallel", "arbitrary")),
    )(a, b, *extras, *rows, *deps)
    return outs[0] if no == 1 else outs


def _rowcall(fn, rows, params, row_outs, acc_outs=(), *, bm=256, deps=(), name):
    T = rows[0].shape[0]
    bm = _pick(T, bm, 8)
    nr, npar, nro, nao = len(rows), len(params), len(row_outs), len(acc_outs)
    first_out = nr + npar + len(deps)

    def body(*refs):
        vals = [r[...] for r in refs[:nr + npar]]
        res = fn(*vals)
        ro = refs[first_out:first_out + nro]
        ao = refs[first_out + nro:]
        for r, v in zip(ro, res[:nro]):
            r[...] = v.astype(r.dtype)
        if nao:
            @pl.when(pl.program_id(0) == 0)
            def _():
                for r in ao:
                    r[...] = jnp.zeros_like(r)

            for r, v in zip(ao, res[nro:]):
                r[...] += v

    def whole(shape):
        nd = len(shape)
        return pl.BlockSpec(tuple(shape), lambda i: (0,) * nd)

    outs = pl.pallas_call(
        body, name=name,
        grid=(T // bm,),
        in_specs=[pl.BlockSpec((bm, r.shape[1]), lambda i: (i, 0)) for r in rows] + [whole(q.shape) for q in params]
        + [pl.BlockSpec(memory_space=pl.ANY)] * len(deps),
        out_specs=[pl.BlockSpec((bm, c), lambda i: (i, 0)) for c, _ in row_outs] + [whole(s) for s in acc_outs],
        out_shape=[jax.ShapeDtypeStruct((T, c), dt) for c, dt in row_outs]
        + [jax.ShapeDtypeStruct(tuple(s), F32) for s in acc_outs],
        compiler_params=_params(("arbitrary",) if nao else ("parallel",)),
    )(*rows, *params, *deps)
    return outs


def _rmsn(x, g):
    return x * lax.rsqrt(jnp.mean(x * x, axis=-1, keepdims=True) + EPS) * g


def _gelu(x):
    return 0.5 * x * (1.0 + jnp.tanh(math.sqrt(2.0 / math.pi) * (x + 0.044715 * (x * x * x))))


def _layer_norm(x, g, b):
    mu = jnp.mean(x, axis=-1, keepdims=True)
    xc = x - mu
    return xc * lax.rsqrt(jnp.mean(xc * xc, axis=-1, keepdims=True) + EPS) * g + b


def _sigmoid(x):
    return 1.0 / (1.0 + jnp.exp(-x))


def _rot(x, cos, sin, rmat):
    return x * cos + jnp.dot(x, rmat, precision=lax.Precision.HIGHEST, preferred_element_type=F32) * sin


def _rms_fwd(h, g, name, deps=()):
    return _rowcall(lambda x, gg: (_rmsn(x, gg),), [h], [g], [(h.shape[1], BF16)], bm=512, deps=deps, name=name)[0]


def _rms_bwd(h, d_hn, dh_in, g, name, deps=()):
    def fn(x, dy, dres, gg):
        _, vjp = jax.vjp(_rmsn, x, gg)
        dx, dg = vjp(dy.astype(F32))
        dh = dres + dx
        return dh, dh, dg

    d = h.shape[1]
    return _rowcall(fn, [h, d_hn, dh_in], [g], [(d, F32), (d, BF16)], [g.shape], bm=512, deps=deps, name=name)


def _rope_tables(pos, name):
    inv = np.float32(ROPE_BASE) ** (-(np.arange(0, ROPE, 2, dtype=np.float32) / np.float32(ROPE)))
    inv = jnp.asarray(np.concatenate([inv, inv])[None, :].astype(np.float32))

    def fn(pp, iv):
        ang = pp.astype(F32) * iv
        return jnp.cos(ang), jnp.sin(ang)

    return _rowcall(fn, [pos], [inv], [(ROPE, F32), (ROPE, F32)], name=name)


def _rot_matrix():
    r = np.zeros((ROPE, ROPE), np.float32)
    half = ROPE // 2
    for j in range(half):
        r[j + half, j] = -1.0
        r[j, j + half] = 1.0
    return jnp.asarray(r)


def _prep1_fwd(lat, gq, gkv, name):
    def fn(l, a, b):
        return _rmsn(l[:, :Q_LORA], a), _rmsn(l[:, Q_LORA:Q_LORA + KV_LORA], b)

    return _rowcall(fn, [lat], [gq, gkv], [(Q_LORA, BF16), (KV_LORA, BF16)], name=name)


def _prep1_bwd(lat, d_cq, d_ckv, d_kr, gq, gkv, name):
    def fn(l, dq, dkv, dkr, a, b):
        _, vq = jax.vjp(_rmsn, l[:, :Q_LORA], a)
        _, vkv = jax.vjp(_rmsn, l[:, Q_LORA:Q_LORA + KV_LORA], b)
        dxq, dga = vq(dq.astype(F32))
        dxkv, dgb = vkv(dkv.astype(F32))
        return jnp.concatenate([dxq, dxkv, dkr], axis=1), dga, dgb

    return _rowcall(fn, [lat, d_cq, d_ckv, d_kr], [gq, gkv], [(lat.shape[1], BF16)], [gq.shape, gkv.shape], name=name)


def _qk_fn(qn_raw, qr_raw, kn_raw, kr_raw, gqn, gqr, gkn, gkr, cos, sin, rmat):
    return (_rmsn(qn_raw, gqn), _rot(_rmsn(qr_raw, gqr), cos, sin, rmat),
            _rmsn(kn_raw, gkn), _rot(_rmsn(kr_raw, gkr), cos, sin, rmat))


def _prep2_fwd(q_raw, kv_raw, lat, cos, sin, rmat, gains, name, bm=1024):
    H, T, _ = q_raw.shape
    bm = _pick(T, bm, 8)
    kr0 = Q_LORA + KV_LORA

    def body(q_ref, kv_ref, lat_ref, cos_ref, sin_ref, r_ref, gqn, gqr, gkn, gkr, qo, ko, vo):
        qr, kvr = q_ref[...], kv_ref[...]
        qn, qro, kn, kro = _qk_fn(qr[:, :NOPE], qr[:, NOPE:], kvr[:, :NOPE], lat_ref[:, kr0:kr0 + ROPE],
                                  gqn[...], gqr[...], gkn[...], gkr[...], cos_ref[...], sin_ref[...], r_ref[...])
        qo[:, :NOPE] = qn.astype(BF16)
        qo[:, NOPE:] = qro.astype(BF16)
        ko[:, :NOPE] = kn.astype(BF16)
        ko[:, NOPE:] = kro.astype(BF16)
        vo[...] = kvr[:, NOPE:].astype(BF16)

    hb = lambda c: pl.BlockSpec((None, bm, c), lambda m, h: (h, m, 0))
    rb = lambda c: pl.BlockSpec((bm, c), lambda m, h: (m, 0))
    wb = lambda s: pl.BlockSpec(tuple(s), lambda m, h: (0, 0))
    return pl.pallas_call(
        body, name=name, grid=(T // bm, H),
        in_specs=[hb(QK), hb(NOPE + VDIM), rb(lat.shape[1]), rb(ROPE), rb(ROPE), wb(rmat.shape)]
        + [wb(g.shape) for g in gains],
        out_specs=[hb(QK), hb(QK), hb(VDIM)],
        out_shape=[jax.ShapeDtypeStruct((H, T, QK), BF16), jax.ShapeDtypeStruct((H, T, QK), BF16),
                   jax.ShapeDtypeStruct((H, T, VDIM), BF16)],
        compiler_params=_params(("parallel", "parallel")),
    )(q_raw, kv_raw, lat, cos, sin, rmat, *gains)


def _prep2_bwd(q_raw, kv_raw, lat, cos, sin, rmat, gains, dq, dk, dv, name, bm=512):
    H, T, _ = q_raw.shape
    bm = _pick(T, bm, 8)
    kr0 = Q_LORA + KV_LORA

    def body(q_ref, kv_ref, lat_ref, cos_ref, sin_ref, r_ref, gqn, gqr, gkn, gkr, dq_ref, dk_ref, dv_ref,
             dqo, dkvo, dkro, o_gqn, o_gqr, o_gkn, o_gkr):
        m, h = pl.program_id(0), pl.program_id(1)
        qr, kvr = q_ref[...], kv_ref[...]
        cos_v, sin_v, r_v = cos_ref[...], sin_ref[...], r_ref[...]
        f = lambda a, b, c, d, g1, g2, g3, g4: _qk_fn(a, b, c, d, g1, g2, g3, g4, cos_v, sin_v, r_v)
        _, vjp = jax.vjp(f, qr[:, :NOPE], qr[:, NOPE:], kvr[:, :NOPE], lat_ref[:, kr0:kr0 + ROPE],
                         gqn[...], gqr[...], gkn[...], gkr[...])
        dqv, dkv_ = dq_ref[...], dk_ref[...]
        d_qn, d_qr, d_kn, d_kr, g1, g2, g3, g4 = vjp((dqv[:, :NOPE], dqv[:, NOPE:], dkv_[:, :NOPE], dkv_[:, NOPE:]))
        dqo[:, :NOPE] = d_qn.astype(BF16)
        dqo[:, NOPE:] = d_qr.astype(BF16)
        dkvo[:, :NOPE] = d_kn.astype(BF16)
        dkvo[:, NOPE:] = dv_ref[...].astype(BF16)

        @pl.when(h == 0)
        def _():
            dkro[...] = jnp.zeros_like(dkro)

        dkro[...] += d_kr

        @pl.when((h == 0) & (m == 0))
        def _():
            for o in (o_gqn, o_gqr, o_gkn, o_gkr):
                o[...] = jnp.zeros_like(o)

        for o, g in zip((o_gqn, o_gqr, o_gkn, o_gkr), (g1, g2, g3, g4)):
            o[...] += g

    hb = lambda c: pl.BlockSpec((None, bm, c), lambda m, h: (h, m, 0))
    rb = lambda c: pl.BlockSpec((bm, c), lambda m, h: (m, 0))
    wb = lambda s: pl.BlockSpec(tuple(s), lambda m, h: (0, 0))
    return pl.pallas_call(
        body, name=name, grid=(T // bm, H),
        in_specs=[hb(QK), hb(NOPE + VDIM), rb(lat.shape[1]), rb(ROPE), rb(ROPE), wb(rmat.shape)]
        + [wb(g.shape) for g in gains] + [hb(QK), hb(QK), hb(VDIM)],
        out_specs=[hb(QK), hb(NOPE + VDIM), rb(ROPE)] + [wb(g.shape) for g in gains],
        out_shape=[jax.ShapeDtypeStruct((H, T, QK), BF16), jax.ShapeDtypeStruct((H, T, NOPE + VDIM), BF16),
                   jax.ShapeDtypeStruct((T, ROPE), F32)] + [jax.ShapeDtypeStruct(g.shape, F32) for g in gains],
        compiler_params=_params(("arbitrary", "arbitrary")),
    )(q_raw, kv_raw, lat, cos, sin, rmat, *gains, dq, dk, dv)


_NT = (((1,), (1,)), ((), ()))
_TN = (((0,), (0,)), ((), ()))


def _causal(blk):
    return lax.broadcasted_iota(jnp.int32, (blk, blk), 1) <= lax.broadcasted_iota(jnp.int32, (blk, blk), 0)


def _attn_fwd(q, k, v, seq, name, blk=512):
    H, T, _ = q.shape
    nb = T // seq
    blk = _pick(seq, blk)
    nq = seq // blk
    scale = float(QK) ** -0.5

    def body(q_ref, k_ref, v_ref, o_ref, lse_ref):
        qi = pl.program_id(2)
        qb = q_ref[...]

        def step(j, carry, diagonal):
            m, l, acc = carry
            ks = pl.ds(pl.multiple_of(j * blk, blk), blk)
            s = lax.dot_general(qb, k_ref[ks, :], _NT, preferred_element_type=F32) * scale
            if diagonal:
                s = jnp.where(_causal(blk), s, -jnp.inf)
            m_new = jnp.maximum(m, jnp.max(s, axis=1, keepdims=True))
            pr = jnp.exp(s - m_new)
            alpha = jnp.exp(m - m_new)
            l = alpha * l + jnp.sum(pr, axis=1, keepdims=True)
            acc = alpha * acc + jnp.dot(pr.astype(BF16), v_ref[ks, :], preferred_element_type=F32)
            return m_new, l, acc

        init = (jnp.full((blk, 1), -jnp.inf, F32), jnp.zeros((blk, 1), F32), jnp.zeros((blk, VDIM), F32))
        below = lax.fori_loop(0, qi, lambda j, c: step(j, c, False), init)
        m, l, acc = step(qi, below, True)
        o_ref[...] = (acc / l).astype(o_ref.dtype)
        lse_ref[...] = m + jnp.log(l)

    return pl.pallas_call(
        body, name=name, grid=(H, nb, nq),
        in_specs=[pl.BlockSpec((None, blk, QK), lambda h, b, i: (h, b * nq + i, 0)),
                  pl.BlockSpec((None, seq, QK), lambda h, b, i: (h, b, 0)),
                  pl.BlockSpec((None, seq, VDIM), lambda h, b, i: (h, b, 0))],
        out_specs=[pl.BlockSpec((blk, VDIM), lambda h, b, i: (b * nq + i, h)),
                   pl.BlockSpec((None, blk, 1), lambda h, b, i: (h, b * nq + i, 0))],
        out_shape=[jax.ShapeDtypeStruct((T, H * VDIM), BF16), jax.ShapeDtypeStruct((H, T, 1), F32)],
        compiler_params=_params(("parallel", "parallel", "parallel")),
    )(q, k, v)


def _attn_bwd(q, k, v, o, do, lse, seq, name, blk=512):
    H, T, _ = q.shape
    nb = T // seq
    blk = _pick(seq, blk)
    nq = seq // blk
    scale = float(QK) ** -0.5

    def body(q_ref, k_ref, v_ref, o_ref, do_ref, lse_ref, dq_ref, dk_ref, dv_ref):
        dk_ref[...] = jnp.zeros_like(dk_ref)
        dv_ref[...] = jnp.zeros_like(dv_ref)

        def qloop(i, carry):
            qs = pl.ds(pl.multiple_of(i * blk, blk), blk)
            qb = q_ref[qs, :]
            dob = do_ref[qs, :]
            dof = dob.astype(F32)
            lse_b = lse_ref[qs, :]
            delta = jnp.sum(dof * o_ref[qs, :].astype(F32), axis=1, keepdims=True)

            def kstep(j, dq_acc, diagonal):
                ks = pl.ds(pl.multiple_of(j * blk, blk), blk)
                kb = k_ref[ks, :]
                vb = v_ref[ks, :]
                s = lax.dot_general(qb, kb, _NT, preferred_element_type=F32) * scale
                pr = jnp.exp(s - lse_b)
                if diagonal:
                    pr = jnp.where(_causal(blk), pr, 0.0)
                dp = lax.dot_general(dob, vb, _NT, preferred_element_type=F32)
                ds = (pr * (dp - delta) * scale).astype(BF16)
                prb = pr.astype(BF16)
                dv_ref[ks, :] += lax.dot_general(prb, dob, _TN, preferred_element_type=F32)
                dk_ref[ks, :] += lax.dot_general(ds, qb, _TN, preferred_element_type=F32)
                return dq_acc + jnp.dot(ds, kb, preferred_element_type=F32)

            below = lax.fori_loop(0, i, lambda j, c: kstep(j, c, False), jnp.zeros((blk, QK), F32))
            dq_ref[qs, :] = kstep(i, below, True)
            return carry

        lax.fori_loop(0, nq, qloop, 0)

    hb = lambda c: pl.BlockSpec((None, seq, c), lambda h, b: (h, b, 0))
    cb = lambda: pl.BlockSpec((seq, VDIM), lambda h, b: (b, h))
    return pl.pallas_call(
        body, name=name, grid=(H, nb),
        in_specs=[hb(QK), hb(QK), hb(VDIM), cb(), cb(), hb(1)],
        out_specs=[hb(QK), hb(QK), hb(VDIM)],
        out_shape=[jax.ShapeDtypeStruct((H, T, QK), F32), jax.ShapeDtypeStruct((H, T, QK), F32),
                   jax.ShapeDtypeStruct((H, T, VDIM), F32)],
        compiler_params=_params(("parallel", "parallel")),
    )(q, k, v, o, do, lse)


def _gelu_ln_fwd(z, g, b, name):
    half = z.shape[1] // 2

    def fn(zz, gg, bb):
        return _gelu(zz[:, :half].astype(F32)), _layer_norm(_gelu(zz[:, half:].astype(F32)), gg, bb)

    return _rowcall(fn, [z], [g, b], [(half, BF16), (half, BF16)], name=name)


def _gelu_ln_bwd(z, d_u, d_vn, g, b, name):
    half = z.shape[1] // 2

    def gelu_and_slope(x):
        c, a = math.sqrt(2.0 / math.pi), 0.044715
        x2 = x * x
        t = jnp.tanh(c * x * (1.0 + a * x2))
        return 0.5 * x * (1.0 + t), 0.5 * (1.0 + t) + 0.5 * x * (1.0 - t * t) * (c * (1.0 + 3.0 * a * x2))

    def fn(zz, du, dvn, gg, bb):
        du, dvn = du.astype(F32), dvn.astype(F32)
        _, su = gelu_and_slope(zz[:, :half].astype(F32))
        v, sv = gelu_and_slope(zz[:, half:].astype(F32))
        xc = v - jnp.mean(v, axis=-1, keepdims=True)
        rstd = lax.rsqrt(jnp.mean(xc * xc, axis=-1, keepdims=True) + EPS)
        y = xc * rstd
        dy = dvn * gg
        dv = rstd * (dy - jnp.mean(dy, axis=-1, keepdims=True) - y * jnp.mean(dy * y, axis=-1, keepdims=True))
        dg = jnp.sum(dvn * y, axis=0, keepdims=True)
        db = jnp.sum(dvn, axis=0, keepdims=True)
        return jnp.concatenate([du * su, dv * sv], axis=1), dg, db

    return _rowcall(fn, [z, d_u, d_vn], [g, b], [(z.shape[1], BF16)], [g.shape, b.shape], bm=128, name=name)


def _tril_bf16(ws):
    t = lax.broadcasted_iota(jnp.int32, ws.shape, 0)
    s = lax.broadcasted_iota(jnp.int32, ws.shape, 1)
    return jnp.where(s <= t, ws, 0.0).astype(BF16)


def _sgu_fwd(u, vn, ws, bs, name, bm=2048):
    T, half = u.shape
    gd = half // GROUPS
    bm = _pick(T, bm, CHUNK)
    nc = bm // CHUNK

    def body(u_ref, vn_ref, ws_ref, bs_ref, y_ref):
        wm = _tril_bf16(ws_ref[...])
        bias = bs_ref[...]
        for c in range(nc):
            rs = slice(c * CHUNK, (c + 1) * CHUNK)
            sv = jnp.dot(wm, vn_ref[rs, :], preferred_element_type=F32) + bias
            y_ref[rs, :] = (u_ref[rs, :].astype(F32) * sv).astype(y_ref.dtype)

    tb = lambda: pl.BlockSpec((bm, gd), lambda g, i: (i, g))
    return pl.pallas_call(
        body, name=name, grid=(GROUPS, T // bm),
        in_specs=[tb(), tb(), pl.BlockSpec((None, CHUNK, CHUNK), lambda g, i: (g, 0, 0)),
                  pl.BlockSpec((None, CHUNK, 1), lambda g, i: (g, 0, 0))],
        out_specs=tb(),
        out_shape=jax.ShapeDtypeStruct((T, half), BF16),
        compiler_params=_params(("parallel", "parallel")),
    )(u, vn, ws, bs)


def _sgu_bwd(u, vn, dy, ws, bs, name, bm=1024):
    T, half = u.shape
    gd = half // GROUPS
    bm = _pick(T, bm, CHUNK)
    nc = bm // CHUNK

    def body(u_ref, vn_ref, dy_ref, ws_ref, bs_ref, du_ref, dvn_ref, dws_ref, dbs_ref):
        @pl.when(pl.program_id(1) == 0)
        def _():
            dws_ref[...] = jnp.zeros_like(dws_ref)
            dbs_ref[...] = jnp.zeros_like(dbs_ref)

        wm = _tril_bf16(ws_ref[...])
        bias = bs_ref[...]
        dws = jnp.zeros((CHUNK, CHUNK), F32)
        dbs = jnp.zeros((CHUNK, 1), F32)
        for c in range(nc):
            rs = slice(c * CHUNK, (c + 1) * CHUNK)
            vb = vn_ref[rs, :]
            dyb = dy_ref[rs, :].astype(F32)
            sv = jnp.dot(wm, vb, preferred_element_type=F32) + bias
            du_ref[rs, :] = (dyb * sv).astype(du_ref.dtype)
            dsv = dyb * u_ref[rs, :].astype(F32)
            dsb = dsv.astype(BF16)
            dvn_ref[rs, :] = lax.dot_general(wm, dsb, _TN, preferred_element_type=F32).astype(dvn_ref.dtype)
            dws = dws + lax.dot_general(dsb, vb, _NT, preferred_element_type=F32)
            dbs = dbs + jnp.sum(dsv, axis=1, keepdims=True)
        t = lax.broadcasted_iota(jnp.int32, (CHUNK, CHUNK), 0)
        s = lax.broadcasted_iota(jnp.int32, (CHUNK, CHUNK), 1)
        dws_ref[...] += jnp.where(s <= t, dws, 0.0)
        dbs_ref[...] += dbs

    tb = lambda: pl.BlockSpec((bm, gd), lambda g, i: (i, g))
    wsb = lambda: pl.BlockSpec((None, CHUNK, CHUNK), lambda g, i: (g, 0, 0))
    bsb = lambda: pl.BlockSpec((None, CHUNK, 1), lambda g, i: (g, 0, 0))
    return pl.pallas_call(
        body, name=name, grid=(GROUPS, T // bm),
        in_specs=[tb(), tb(), tb(), wsb(), bsb()],
        out_specs=[tb(), tb(), wsb(), bsb()],
        out_shape=[jax.ShapeDtypeStruct((T, half), BF16), jax.ShapeDtypeStruct((T, half), BF16),
                   jax.ShapeDtypeStruct(ws.shape, F32), jax.ShapeDtypeStruct(bs.shape, F32)],
        compiler_params=_params(("parallel", "arbitrary")),
    )(u, vn, dy, ws, bs)


def _loss_head(y, t, name):
    d_model = y.shape[1]

    def fn(yy, tt):
        d = yy - tt
        part = 0.5 * jnp.sum(jnp.mean(d * d, axis=-1, keepdims=True), axis=0, keepdims=True)
        return d / d_model, jnp.zeros((1, LANE), F32) + part

    dy, part = _rowcall(fn, [y, t], [], [(d_model, F32)], [(1, LANE)], name=name)
    return dy, part[0, 0]


def _adamw(parts, w, m, v, prev, layer, name):
    L, R, C = w.shape
    br = _pick(R, max(8, (128 * 1024) // C // 8 * 8), 8)
    c1 = 1.0 - B1 ** STEP
    c2 = 1.0 - B2 ** STEP
    if prev is None:
        prev = [lax.empty(w.shape, F32) for _ in range(4)]

    def body(p_ref, w_ref, m_ref, v_ref, a0, a1, a2, a3, g_o, d_o, m_o, v_o, token):
        g = p_ref[0].astype(F32)
        for d in range(1, N_DEV):
            g = g + p_ref[d].astype(F32)
        mn = B1 * m_ref[...] + (1.0 - B1) * g
        vn = B2 * v_ref[...] + (1.0 - B2) * (g * g)
        g_o[...] = g
        m_o[...] = mn
        v_o[...] = vn
        d_o[...] = -LR * ((mn / c1) / (jnp.sqrt(vn / c2) + ADAM_EPS) + WD * w_ref[...])
        token[...] = jnp.zeros_like(token)

    blk = lambda: pl.BlockSpec((None, br, C), lambda i: (layer, i, 0))
    anywhere = pl.BlockSpec(memory_space=pl.ANY)
    outs = pl.pallas_call(
        body, name=name, grid=(R // br,),
        in_specs=[pl.BlockSpec((N_DEV, br, C), lambda i: (0, i, 0)), blk(), blk(), blk()] + [anywhere] * 4,
        out_specs=[blk(), blk(), blk(), blk(), pl.BlockSpec((8, LANE), lambda i: (0, 0))],
        out_shape=[jax.ShapeDtypeStruct((L, R, C), F32)] * 4 + [jax.ShapeDtypeStruct((8, LANE), F32)],
        input_output_aliases={4: 0, 5: 1, 6: 2, 7: 3},
        compiler_params=_params(("arbitrary",)),
    )(parts, w, m, v, *prev)
    return list(outs[:4]), outs[4]


def _mesh_pos():
    return lax.axis_index("x"), lax.axis_index("y"), lax.axis_index("c")


def _flip(pos, k):
    x, y, c = pos
    px = 1 - x if k & 4 else x
    py = 1 - y if k & 2 else y
    pc = 1 - c if k & 1 else c
    return px, py, pc


HBM_SPEC = pl.BlockSpec(memory_space=pltpu.HBM)
SEM_SPEC = pl.BlockSpec(memory_space=pltpu.SEMAPHORE)
EFFECT = pltpu.SideEffectType.DATAFLOW_SIDE_EFFECTING


def _hbm(a):
    return pltpu.with_memory_space_constraint(a, pltpu.HBM)


def _device_index():
    x, y, c = _mesh_pos()
    return 4 * x + 2 * y + c


def _peer_copy(src, land, send, recv, a, k, pos, scatter):
    peer = _flip(pos, k)
    me = 4 * pos[0] + 2 * pos[1] + pos[2]
    piece = src.at[4 * peer[0] + 2 * peer[1] + peer[2]] if scatter else src
    return pltpu.make_async_remote_copy(
        src_ref=piece, dst_ref=land.at[me], send_sem=send.at[7 * a + k - 1], recv_sem=recv.at[7 * a + k - 1],
        device_id=peer, device_id_type=pl.DeviceIdType.MESH)


def _xchg_start(srcs, scatter, after, name):
    n = len(srcs)

    def body(*refs):
        src, land = refs[:n], refs[n:2 * n]
        send, recv, token = refs[2 * n + 1], refs[2 * n + 2], refs[-1]
        pos = _mesh_pos()
        for k in range(1, N_DEV):
            for a in range(n):
                _peer_copy(src[a], land[a], send, recv, a, k, pos, scatter).start()
        token[...] = jnp.zeros_like(token)

    lands = [lax.empty(s.shape if scatter else (N_DEV,) + s.shape, s.dtype) for s in srcs]
    outs = pl.pallas_call(
        body, name=name,
        out_shape=(pltpu.SemaphoreType.DMA((7 * n,)), pltpu.SemaphoreType.DMA((7 * n,)),
                   *[pltpu.HBM(s.shape, s.dtype) for s in srcs], *[pltpu.HBM(l.shape, l.dtype) for l in lands],
                   jax.ShapeDtypeStruct((8, LANE), F32)),
        in_specs=[HBM_SPEC] * (2 * n) + [pl.BlockSpec(memory_space=pl.ANY)],
        out_specs=(SEM_SPEC, SEM_SPEC, *[HBM_SPEC] * (2 * n), pl.BlockSpec(memory_space=pltpu.VMEM)),
        input_output_aliases={q: 2 + q for q in range(2 * n)},
        compiler_params=pltpu.CompilerParams(has_side_effects=EFFECT),
    )(*[_hbm(s) for s in srcs], *[_hbm(l) for l in lands], after)
    handle = dict(send=outs[0], recv=outs[1], srcs=list(outs[2:2 + n]), lands=list(outs[2 + n:2 + 2 * n]),
                  scatter=scatter)
    return handle, outs[-1]


def _xchg_wait(handle, after, name):
    srcs, lands, scatter = handle['srcs'], handle['lands'], handle['scatter']
    n = len(srcs)

    def body(*refs):
        src, land = refs[:n], refs[n:2 * n]
        send, recv = refs[2 * n], refs[2 * n + 1]
        pos = _mesh_pos()
        for k in range(1, N_DEV):
            for a in range(n):
                cp = _peer_copy(src[a], land[a], send, recv, a, k, pos, scatter)
                cp.wait_send()
                cp.wait_recv()

    outs = pl.pallas_call(
        body, name=name,
        out_shape=[pltpu.HBM(s.shape, s.dtype) for s in srcs] + [pltpu.HBM(l.shape, l.dtype) for l in lands],
        in_specs=[HBM_SPEC] * (2 * n) + [SEM_SPEC, SEM_SPEC, pl.BlockSpec(memory_space=pl.ANY)],
        out_specs=[HBM_SPEC] * (2 * n),
        input_output_aliases={q: q for q in range(2 * n)},
        compiler_params=pltpu.CompilerParams(has_side_effects=EFFECT),
    )(*srcs, *lands, handle['send'], handle['recv'], after)
    me = _device_index()
    full = []
    for src, land in zip(outs[:n], outs[n:]):
        mine = lax.dynamic_index_in_dim(src, me, 0, keepdims=False) if scatter else src
        full.append(lax.dynamic_update_index_in_dim(land, mine, me, 0))
    return full


def _pack(parts):
    flat = jnp.concatenate([q.reshape(-1) for q in parts])
    pad = (-flat.shape[0]) % (8 * LANE)
    return jnp.pad(flat, (0, pad)).reshape(-1, LANE)


def _unpack(packed, like):
    flat = packed.reshape(-1)
    out, o = [], 0
    for q in like:
        out.append(flat[o:o + q.size].reshape(q.shape))
        o += q.size
    return out


def kernel(x, p, positions, norm_mix, norm_ffn, norm_ple, mla_w_down, mla_q_lora_g, mla_kv_lora_g, mla_w_uq, mla_w_ukv, mla_q_nope_g, mla_q_rope_g, mla_k_nope_g, mla_k_rope_g, mla_w_out, gmlp_w_in, gmlp_ln_g, gmlp_ln_b, gmlp_w_s, gmlp_b_s, gmlp_w_out, ffn_w_up, ffn_w_down, ple_w_gate, ple_w_proj, loss_target, m_norm_mix, m_norm_ffn, m_norm_ple, m_mla_w_down, m_mla_q_lora_g, m_mla_kv_lora_g, m_mla_w_uq, m_mla_w_ukv, m_mla_q_nope_g, m_mla_q_rope_g, m_mla_k_nope_g, m_mla_k_rope_g, m_mla_w_out, m_gmlp_w_in, m_gmlp_ln_g, m_gmlp_ln_b, m_gmlp_w_s, m_gmlp_b_s, m_gmlp_w_out, m_ffn_w_up, m_ffn_w_down, m_ple_w_gate, m_ple_w_proj, v_norm_mix, v_norm_ffn, v_norm_ple, v_mla_w_down, v_mla_q_lora_g, v_mla_kv_lora_g, v_mla_w_uq, v_mla_w_ukv, v_mla_q_nope_g, v_mla_q_rope_g, v_mla_k_nope_g, v_mla_k_rope_g, v_mla_w_out, v_gmlp_w_in, v_gmlp_ln_g, v_gmlp_ln_b, v_gmlp_w_s, v_gmlp_b_s, v_gmlp_w_out, v_ffn_w_up, v_ffn_w_down, v_ple_w_gate, v_ple_w_proj):
    W = dict(zip(WEIGHTS, (norm_mix, norm_ffn, norm_ple, mla_w_down, mla_q_lora_g, mla_kv_lora_g, mla_w_uq, mla_w_ukv, mla_q_nope_g, mla_q_rope_g, mla_k_nope_g, mla_k_rope_g, mla_w_out, gmlp_w_in, gmlp_ln_g, gmlp_ln_b, gmlp_w_s, gmlp_b_s, gmlp_w_out, ffn_w_up, ffn_w_down, ple_w_gate, ple_w_proj)))
    M1 = dict(zip(WEIGHTS, (m_norm_mix, m_norm_ffn, m_norm_ple, m_mla_w_down, m_mla_q_lora_g, m_mla_kv_lora_g, m_mla_w_uq, m_mla_w_ukv, m_mla_q_nope_g, m_mla_q_rope_g, m_mla_k_nope_g, m_mla_k_rope_g, m_mla_w_out, m_gmlp_w_in, m_gmlp_ln_g, m_gmlp_ln_b, m_gmlp_w_s, m_gmlp_b_s, m_gmlp_w_out, m_ffn_w_up, m_ffn_w_down, m_ple_w_gate, m_ple_w_proj)))
    M2 = dict(zip(WEIGHTS, (v_norm_mix, v_norm_ffn, v_norm_ple, v_mla_w_down, v_mla_q_lora_g, v_mla_kv_lora_g, v_mla_w_uq, v_mla_w_ukv, v_mla_q_nope_g, v_mla_q_rope_g, v_mla_k_nope_g, v_mla_k_rope_g, v_mla_w_out, v_gmlp_w_in, v_gmlp_ln_g, v_gmlp_ln_b, v_gmlp_w_s, v_gmlp_b_s, v_gmlp_w_out, v_ffn_w_up, v_ffn_w_down, v_ple_w_gate, v_ple_w_proj)))

    nb, seq, d_model = x.shape
    assert d_model <= 1024, "the rms norms fused into matmul epilogues need whole rows in one output tile"
    T = nb * seq
    depth = norm_mix.shape[0]
    h = x.reshape(T, d_model)
    target = loss_target.reshape(T, d_model)
    p_bf = p.reshape(depth, T, p.shape[-1]).astype(BF16)

    def stage_keys(st):
        i, second = divmod(st, 2)
        if second:
            return [(n, i) for n in ('ffn_w_up', 'ffn_w_down', 'ple_w_gate', 'ple_w_proj')]
        mix = (['mla_w_down', 'mla_w_uq', 'mla_w_ukv', 'mla_w_out'] if i % 2 == 0 else
               ['gmlp_w_in', 'gmlp_ln_g', 'gmlp_ln_b', 'gmlp_w_out'])
        return [(n, i // 2) for n in mix]

    FW = {n: {} for n in SHARDED}

    def start_weights(st, after):
        keys = stage_keys(st)
        srcs = [W[n][l] if n in F32_PAYLOAD else W[n][l].astype(BF16) for n, l in keys]
        handle, token = _xchg_start(srcs, False, after, "weights_start%d" % st)
        return (keys, handle), [token]

    def wait_weights(pending, st, after):
        keys, handle = pending
        landed = _xchg_wait(handle, after, "weights_wait%d" % st)
        for (n, l), full in zip(keys, landed):
            if SHARD_AXIS[n] == 1:
                FW[n][l] = full.reshape((-1,) + full.shape[2:])
            elif n in ('mla_w_uq', 'mla_w_ukv'):
                FW[n][l] = full
            else:
                FW[n][l] = jnp.transpose(full, (1, 0, 2)).reshape(full.shape[1], -1)
        return landed[0]

    row = lambda a: a.reshape(1, -1)
    cos, sin = _rope_tables(positions.reshape(T, 1), "rope_tables")
    rmat = _rot_matrix()

    def add_and_norm(acc, res, g):
        hh = res + acc
        return hh, _rmsn(hh, g)

    saved = []
    pending = {}
    pending[0], _ = start_weights(0, h)
    hn = _rms_fwd(h, row(W['norm_mix'][0]), "rms_fwd")
    for i in range(depth):
        j = i // 2
        landed = wait_weights(pending.pop(2 * i), 2 * i, h)
        token = []
        for st in ([2 * i + 1, 2 * i + 2, 2 * i + 3] if i % 2 == 0 else [2 * i + 2]):
            if st < 2 * depth and st not in pending:
                pending[st], started = start_weights(st, landed)
                token = token + started
        s = {}
        s['h0'] = h
        s['hn'] = hn
        if i % 2 == 0:
            gains = [row(W['mla_q_nope_g'][j]), row(W['mla_q_rope_g'][j]), row(W['mla_k_nope_g'][j]),
                     row(W['mla_k_rope_g'][j])]
            lat = _mm(hn, FW['mla_w_down'][j], deps=token, name="mla_down")
            cq, ckv = _prep1_fwd(lat, row(W['mla_q_lora_g'][j]), row(W['mla_kv_lora_g'][j]), "mla_prep1")
            q_raw = _mm(cq, FW['mla_w_uq'][j], out_blocks=HEADS, name="mla_uq")
            kv_raw = _mm(ckv, FW['mla_w_ukv'][j], out_blocks=HEADS, name="mla_ukv")
            q, k, v = _prep2_fwd(q_raw, kv_raw, lat, cos, sin, rmat, gains, "mla_prep2")
            o, lse = _attn_fwd(q, k, v, seq, "attn_fwd")
            h, hn2 = _mm(o, FW['mla_w_out'][j], extras=(h,), rows=(row(W['norm_ffn'][i]),), epilogue=add_and_norm,
                         out_dtypes=(F32, BF16), name="mla_out")
            s.update(lat=lat, cq=cq, ckv=ckv, q_raw=q_raw, kv_raw=kv_raw, q=q, k=k, v=v, o=o, lse=lse, gains=gains)
        else:
            z = _mm(hn, FW['gmlp_w_in'][j], out_dtypes=(BF16,), deps=token, name="gmlp_in")
            u, vn = _gelu_ln_fwd(z, row(FW['gmlp_ln_g'][j]), row(FW['gmlp_ln_b'][j]), "gmlp_gelu_ln")
            bs3 = W['gmlp_b_s'][j][:, :, None]
            y = _sgu_fwd(u, vn, W['gmlp_w_s'][j], bs3, "gmlp_sgu")
            h, hn2 = _mm(y, FW['gmlp_w_out'][j], extras=(h,), rows=(row(W['norm_ffn'][i]),), epilogue=add_and_norm,
                         out_dtypes=(F32, BF16), name="gmlp_out")
            s.update(z=z, u=u, vn=vn, y=y, bs3=bs3)
        s['h1'] = h
        wait_weights(pending.pop(2 * i + 1), 2 * i + 1, h)
        a, r = _mm(hn2, FW['ffn_w_up'][i], epilogue=lambda acc: (acc, jnp.square(jnp.maximum(acc, 0.0))),
                   out_dtypes=(BF16, BF16), name="ffn_up")
        h, hn3 = _mm(r, FW['ffn_w_down'][i], extras=(h,), rows=(row(W['norm_ple'][i]),), epilogue=add_and_norm,
                     out_dtypes=(F32, BF16), name="ffn_down")
        s.update(hn2=hn2, a=a, r=r, h2=h)
        gt = _mm(hn3, FW['ple_w_gate'][i], name="ple_gate")
        if i + 1 < depth:
            def gate_and_norm(acc, g_, res, gain):
                hh = res + _sigmoid(g_) * acc
                return acc, hh, _rmsn(hh, gain)

            pp, h, hn = _mm(p_bf[i], FW['ple_w_proj'][i], extras=(gt, h), rows=(row(W['norm_mix'][i + 1]),),
                            epilogue=gate_and_norm, out_dtypes=(F32, F32, BF16), name="ple_proj")
        else:
            pp, h = _mm(p_bf[i], FW['ple_w_proj'][i], extras=(gt, h),
                        epilogue=lambda acc, g_, res: (acc, res + _sigmoid(g_) * acc), out_dtypes=(F32, F32),
                        name="ple_proj_last")
        s.update(hn3=hn3, gt=gt, pp=pp)
        saved.append(s)

    dh, loss_part = _loss_head(h, target, "loss_head")
    loss = lax.psum(loss_part, MESH_AXES)

    G = {n: [None] * W[n].shape[0] for n in REPLICATED}
    res = {}
    flying = []

    def shard3(n):
        shp = W[n].shape
        return shp[0], int(np.prod(shp[1:-1])), shp[-1]

    def by_owner(g):
        return g.reshape((N_DEV, g.shape[0] // N_DEV) + g.shape[1:])

    def send_grads(tag, grads):
        handle, token = _xchg_start([g for _, g in grads], True, grads[-1][1], "grads_start_" + tag)
        flying.append((tag, [key for key, _ in grads], handle))
        return [token]

    def land_grads(after):
        tag, keys, handle = flying.pop(0)
        done = []
        for (n, l), full in zip(keys, _xchg_wait(handle, after, "grads_wait_" + tag)):
            dims = shard3(n)
            res[n], token = _adamw(full.reshape((N_DEV,) + dims[1:]), W[n].reshape(dims), M1[n].reshape(dims),
                                   M2[n].reshape(dims), res.get(n), l, "adamw_" + n)
            done.append(token)
        return done

    def start_small(names, tag, after):
        handle, token = _xchg_start([_pack([jnp.stack(G[n]) for n in names])], False, after, "small_start_" + tag)
        return (names, handle), [token]

    def land_small(pending_small, tag, after):
        names, handle = pending_small
        (parts,) = _xchg_wait(handle, after, "small_wait_" + tag)
        like = [W[n] for n in names]
        outs, _ = _adamw(parts, _pack(like)[None], _pack([M1[n] for n in names])[None],
                         _pack([M2[n] for n in names])[None], None, 0, "adamw_small_" + tag)
        unpacked = [_unpack(o, like) for o in outs]
        for idx, n in enumerate(names):
            res[n] = [unpacked[q][idx] for q in range(4)]

    spatial = ['gmlp_w_s', 'gmlp_b_s']
    token = []
    for i in reversed(range(depth)):
        j = i // 2
        s = saved[i]
        def ple_elem(d, g_, pq):
            sg = _sigmoid(g_)
            return d * sg, d * pq * sg * (1.0 - sg)

        d_pp, d_gt = _rowcall(ple_elem, [dh, s['gt'], s['pp']], [], [(d_model, BF16), (d_model, BF16)], name="ple_bwd")
        g_proj = _mm(p_bf[i], d_pp, ta=True, out_dtypes=(BF16,), name="ple_proj_dw")
        g_proj = jnp.transpose(g_proj.reshape(g_proj.shape[0], N_DEV, -1), (1, 0, 2))
        g_gate = _mm(s['hn3'], d_gt, ta=True, out_dtypes=(BF16,), name="ple_gate_dw")
        d_hn3 = _mm(d_gt, FW['ple_w_gate'][i], tb=True, out_dtypes=(BF16,), name="ple_gate_dx")
        dh, dh_bf, dg = _rms_bwd(s['h2'], d_hn3, dh, row(W['norm_ple'][i]), "rms_bwd", deps=token)
        G['norm_ple'][i] = dg[0]
        d_a = _mm(dh_bf, FW['ffn_w_down'][i], tb=True, extras=(s['a'],),
                  epilogue=lambda acc, a_: (acc * (2.0 * jnp.maximum(a_.astype(F32), 0.0)),), out_dtypes=(BF16,),
                  name="ffn_down_dx")
        g_down = _mm(s['r'], dh_bf, ta=True, out_dtypes=(BF16,), name="ffn_down_dw")
        g_up = _mm(s['hn2'], d_a, ta=True, out_blocks=N_DEV, out_dtypes=(BF16,), name="ffn_up_dw")
        token = send_grads("mlp%d" % i, [(('ple_w_proj', i), g_proj), (('ple_w_gate', i), by_owner(g_gate)),
                                         (('ffn_w_down', i), by_owner(g_down)), (('ffn_w_up', i), g_up)])
        if len(flying) > 1:
            token = token + land_grads(g_up)
        d_hn2 = _mm(d_a, FW['ffn_w_up'][i], tb=True, out_dtypes=(BF16,), name="ffn_up_dx")
        dh, dh_bf, dg = _rms_bwd(s['h1'], d_hn2, dh, row(W['norm_ffn'][i]), "rms_bwd", deps=token)
        G['norm_ffn'][i] = dg[0]
        if i % 2 == 0:
            d_o = _mm(dh_bf, FW['mla_w_out'][j], tb=True, out_dtypes=(BF16,), name="mla_out_dx")
            g_out = _mm(s['o'], dh_bf, ta=True, out_dtypes=(BF16,), name="mla_out_dw")
            dq, dk, dv = _attn_bwd(s['q'], s['k'], s['v'], s['o'], d_o, s['lse'], seq, "attn_bwd")
            d_q_raw, d_kv_raw, d_kr, g1, g2, g3, g4 = _prep2_bwd(
                s['q_raw'], s['kv_raw'], s['lat'], cos, sin, rmat, s['gains'], dq, dk, dv, "mla_prep2_bwd")
            G['mla_q_nope_g'][j], G['mla_q_rope_g'][j] = g1[0], g2[0]
            G['mla_k_nope_g'][j], G['mla_k_rope_g'][j] = g3[0], g4[0]
            g_uq = _mm(s['cq'], d_q_raw, ta=True, out_blocks=N_DEV, out_dtypes=(BF16,), name="mla_uq_dw")
            g_ukv = _mm(s['ckv'], d_kv_raw, ta=True, out_blocks=N_DEV, out_dtypes=(BF16,), name="mla_ukv_dw")
            d_cq = _mm(d_q_raw, FW['mla_w_uq'][j], tb=True, out_dtypes=(BF16,), name="mla_uq_dx")
            d_ckv = _mm(d_kv_raw, FW['mla_w_ukv'][j], tb=True, out_dtypes=(BF16,), name="mla_ukv_dx")
            d_lat, dga, dgb = _prep1_bwd(s['lat'], d_cq, d_ckv, d_kr, row(W['mla_q_lora_g'][j]),
                                         row(W['mla_kv_lora_g'][j]), "mla_prep1_bwd")
            G['mla_q_lora_g'][j], G['mla_kv_lora_g'][j] = dga[0], dgb[0]
            g_down = _mm(s['hn'], d_lat, ta=True, out_dtypes=(BF16,), name="mla_down_dw")
            grads = [(('mla_w_out', j), by_owner(g_out)), (('mla_w_uq', j), g_uq), (('mla_w_ukv', j), g_ukv),
                     (('mla_w_down', j), by_owner(g_down))]
            d_hn = _mm(d_lat, FW['mla_w_down'][j], tb=True, out_dtypes=(BF16,), name="mla_down_dx")
        else:
            d_y = _mm(dh_bf, FW['gmlp_w_out'][j], tb=True, out_dtypes=(BF16,), name="gmlp_out_dx")
            g_out = _mm(s['y'], dh_bf, ta=True, out_dtypes=(BF16,), name="gmlp_out_dw")
            d_u, d_vn, d_ws, d_bs = _sgu_bwd(s['u'], s['vn'], d_y, W['gmlp_w_s'][j], s['bs3'], "gmlp_sgu_bwd")
            G['gmlp_w_s'][j], G['gmlp_b_s'][j] = d_ws, d_bs[:, :, 0]
            d_z, d_lg, d_lb = _gelu_ln_bwd(s['z'], d_u, d_vn, row(FW['gmlp_ln_g'][j]), row(FW['gmlp_ln_b'][j]),
                                           "gmlp_gelu_ln_bwd")
            g_in = _mm(s['hn'], d_z, ta=True, out_blocks=N_DEV, out_dtypes=(BF16,), name="gmlp_in_dw")
            grads = [(('gmlp_w_out', j), by_owner(g_out)), (('gmlp_ln_g', j), by_owner(d_lg[0])),
                     (('gmlp_ln_b', j), by_owner(d_lb[0])), (('gmlp_w_in', j), g_in)]
            d_hn = _mm(d_z, FW['gmlp_w_in'][j], tb=True, out_dtypes=(BF16,), name="gmlp_in_dx")
        token = send_grads("mix%d" % i, grads)
        if len(flying) > 1:
            token = token + land_grads(grads[-1][1])
        dh, _, dg = _rms_bwd(s['h0'], d_hn, dh, row(W['norm_mix'][i]), "rms_bwd", deps=token)
        G['norm_mix'][i] = dg[0]
        token = []
        if i == 1:
            small_a, token = start_small(spatial, "spatial", dh)
    grad_x = dh.reshape(x.shape)

    small_b, _ = start_small([n for n in REPLICATED if n not in spatial], "gains", dh)
    while flying:
        land_grads(dh)
    land_small(small_a, "spatial", dh)
    land_small(small_b, "gains", dh)

    out = lambda q: [res[n][q].reshape(W[n].shape) for n in WEIGHTS]
    return (loss, grad_x, *out(0), *out(1), *out(2), *out(3))
```

```python
import math

import numpy as np
import jax
import jax.numpy as jnp
from jax import lax
from jax.experimental import pallas as pl
from jax.experimental.pallas import tpu as pltpu

F32 = jnp.float32
BF16 = jnp.bfloat16

N_DEV = 8
MESH_AXES = ("x", "y", "c")
HEADS = 8
NOPE = 128
ROPE = 64
VDIM = 128
QK = NOPE + ROPE
Q_LORA = 384
KV_LORA = 256
ROPE_BASE = 10000.0
CHUNK = 128
GROUPS = 8
EPS = 1e-6
LR, B1, B2, ADAM_EPS, WD, STEP = 0.001, 0.9, 0.999, 1e-08, 0.01, 10
LANE = 128
VMEM_LIMIT = 56 * 1024 * 1024
MM_VMEM_BUDGET = 40 * 1024 * 1024

WEIGHTS = ['norm_mix', 'norm_ffn', 'norm_ple', 'mla_w_down', 'mla_q_lora_g', 'mla_kv_lora_g', 'mla_w_uq',
           'mla_w_ukv', 'mla_q_nope_g', 'mla_q_rope_g', 'mla_k_nope_g', 'mla_k_rope_g', 'mla_w_out', 'gmlp_w_in',
           'gmlp_ln_g', 'gmlp_ln_b', 'gmlp_w_s', 'gmlp_b_s', 'gmlp_w_out', 'ffn_w_up', 'ffn_w_down', 'ple_w_gate',
           'ple_w_proj']
SHARD_AXIS = {'mla_w_down': 1, 'mla_w_uq': 2, 'mla_w_ukv': 2, 'mla_w_out': 1, 'gmlp_w_in': 2, 'gmlp_ln_g': 1,
              'gmlp_ln_b': 1, 'gmlp_w_out': 1, 'ffn_w_up': 2, 'ffn_w_down': 1, 'ple_w_gate': 1, 'ple_w_proj': 2}
SHARDED = list(SHARD_AXIS)
REPLICATED = [n for n in WEIGHTS if n not in SHARD_AXIS]
F32_PAYLOAD = ('gmlp_ln_g', 'gmlp_ln_b')


def _pick(dim, pref, align=LANE):
    if dim <= pref:
        return dim
    b = (pref // align) * align
    while b >= align:
        if dim % b == 0:
            return b
        b -= align
    return dim


def _params(sem):
    return pltpu.CompilerParams(dimension_semantics=sem, vmem_limit_bytes=VMEM_LIMIT)


def _mm(a, b, *, ta=False, tb=False, extras=(), rows=(), epilogue=None, out_dtypes=(F32,), out_blocks=None,
        deps=(), name, bn=1024):
    a3, b3 = a.ndim == 3, b.ndim == 3
    assert not (ta and a3)
    if ta:
        K, M = a.shape
        ka = K
    elif a3:
        M, ka = a.shape[1:]
        K = a.shape[0] * ka
    else:
        M, K = a.shape
        ka = K
    if tb:
        N, kb = b.shape[-2:]
        nb = N
        K2 = b.shape[0] * kb if b3 else kb
    else:
        kb, nb = b.shape[-2:]
        K2 = kb
        N = b.shape[0] * nb if b3 else nb
    assert K == K2, (a.shape, b.shape, ta, tb)
    no_ = N // out_blocks if out_blocks else N
    assert not (out_blocks and extras)
    size = lambda t: jnp.dtype(t).itemsize
    per_out = sum(size(e.dtype) for e in extras) + sum(size(t) for t in out_dtypes)
    bn = _pick(min(nb, no_), bn)
    k_lim = min(ka, kb)
    fits = lambda m, k: 2 * (m * k * size(a.dtype) + k * bn * size(b.dtype) + m * bn * per_out) + 4 * m * bn
    ms = [m for m in sorted({min(M, c) for c in (2048, 1024, 512, 256)}, reverse=True) if M % m == 0]
    ks = [k for k in dict.fromkeys((k_lim, 2048, 1024, 512, 256)) if k <= k_lim and k_lim % k == 0]
    bm, bk = next(((m, k) for k in ks for m in ms if fits(m, k) <= MM_VMEM_BUDGET), (ms[-1], ks[-1]))
    nk = K // bk
    ne, no = len(extras) + len(rows), len(out_dtypes)
    first_out = 2 + ne + len(deps)
    dims = (((0,) if ta else (1,), (1,) if tb else (0,)), ((), ()))

    def finish(r, e_refs, o_refs):
        outs = epilogue(r, *[e[...] for e in e_refs]) if epilogue is not None else (r,)
        for o, v in zip(o_refs, outs):
            o[...] = v.astype(o.dtype)

    def body(*refs):
        a_ref, b_ref = refs[0], refs[1]
        e_refs = refs[2:2 + ne]
        o_refs = refs[first_out:first_out + no]
        part = lax.dot_general(a_ref[...].astype(BF16), b_ref[...].astype(BF16), dims, preferred_element_type=F32)
        if nk == 1:
            finish(part, e_refs, o_refs)
            return
        acc = refs[-1]
        k = pl.program_id(2)

        @pl.when(k == 0)
        def _():
            acc[...] = part

        @pl.when(k > 0)
        def _():
            acc[...] += part

        @pl.when(k == nk - 1)
        def _():
            finish(acc[...], e_refs, o_refs)

    ka_t, kb_t, nb_t, no_t = ka // bk, kb // bk, nb // bn, no_ // bn
    if ta:
        a_spec = pl.BlockSpec((bk, bm), lambda i, j, k: (k, i))
    elif a3:
        a_spec = pl.BlockSpec((None, bm, bk), lambda i, j, k: (k // ka_t, i, k % ka_t))
    else:
        a_spec = pl.BlockSpec((bm, bk), lambda i, j, k: (i, k))
    if tb:
        b_spec = (pl.BlockSpec((None, bn, bk), lambda i, j, k: (k // kb_t, j, k % kb_t)) if b3 else
                  pl.BlockSpec((bn, bk), lambda i, j, k: (j, k)))
    else:
        b_spec = (pl.BlockSpec((None, bk, bn), lambda i, j, k: (j // nb_t, k, j % nb_t)) if b3 else
                  pl.BlockSpec((bk, bn), lambda i, j, k: (k, j)))
    if out_blocks:
        o_spec = lambda: pl.BlockSpec((None, bm, bn), lambda i, j, k: (j // no_t, i, j % no_t))
        o_shape = (out_blocks, M, no_)
    else:
        o_spec = lambda: pl.BlockSpec((bm, bn), lambda i, j, k: (i, j))
        o_shape = (M, N)
    outs = pl.pallas_call(
        body, name=name,
        grid=(M // bm, N // bn, nk),
        in_specs=[a_spec, b_spec] + [pl.BlockSpec((bm, bn), lambda i, j, k: (i, j)) for _ in extras]
        + [pl.BlockSpec((1, bn), lambda i, j, k: (0, j)) for _ in rows]
        + [pl.BlockSpec(memory_space=pl.ANY)] * len(deps),
        out_specs=[o_spec() for _ in out_dtypes],
        out_shape=[jax.ShapeDtypeStruct(o_shape, dt) for dt in out_dtypes],
        scratch_shapes=[pltpu.VMEM((bm, bn), F32)] if nk > 1 else [],
        compiler_params=_params(("parallel", "parallel", "arbitrary")),
    )(a, b, *extras, *rows, *deps)
    return outs[0] if no == 1 else outs


def _rowcall(fn, rows, params, row_outs, acc_outs=(), *, bm=256, deps=(), name):
    T = rows[0].shape[0]
    bm = _pick(T, bm, 8)
    nr, npar, nro, nao = len(rows), len(params), len(row_outs), len(acc_outs)
    first_out = nr + npar + len(deps)

    def body(*refs):
        vals = [r[...] for r in refs[:nr + npar]]
        res = fn(*vals)
        ro = refs[first_out:first_out + nro]
        ao = refs[first_out + nro:]
        for r, v in zip(ro, res[:nro]):
            r[...] = v.astype(r.dtype)
        if nao:
            @pl.when(pl.program_id(0) == 0)
            def _():
                for r in ao:
                    r[...] = jnp.zeros_like(r)

            for r, v in zip(ao, res[nro:]):
                r[...] += v

    def whole(shape):
        nd = len(shape)
        return pl.BlockSpec(tuple(shape), lambda i: (0,) * nd)

    outs = pl.pallas_call(
        body, name=name,
        grid=(T // bm,),
        in_specs=[pl.BlockSpec((bm, r.shape[1]), lambda i: (i, 0)) for r in rows] + [whole(q.shape) for q in params]
        + [pl.BlockSpec(memory_space=pl.ANY)] * len(deps),
        out_specs=[pl.BlockSpec((bm, c), lambda i: (i, 0)) for c, _ in row_outs] + [whole(s) for s in acc_outs],
        out_shape=[jax.ShapeDtypeStruct((T, c), dt) for c, dt in row_outs]
        + [jax.ShapeDtypeStruct(tuple(s), F32) for s in acc_outs],
        compiler_params=_params(("arbitrary",) if nao else ("parallel",)),
    )(*rows, *params, *deps)
    return outs


def _rmsn(x, g):
    return x * lax.rsqrt(jnp.mean(x * x, axis=-1, keepdims=True) + EPS) * g


def _gelu(x):
    return 0.5 * x * (1.0 + jnp.tanh(math.sqrt(2.0 / math.pi) * (x + 0.044715 * (x * x * x))))


def _layer_norm(x, g, b):
    mu = jnp.mean(x, axis=-1, keepdims=True)
    xc = x - mu
    return xc * lax.rsqrt(jnp.mean(xc * xc, axis=-1, keepdims=True) + EPS) * g + b


def _sigmoid(x):
    return 1.0 / (1.0 + jnp.exp(-x))


def _rot(x, cos, sin, rmat):
    return x * cos + jnp.dot(x, rmat, precision=lax.Precision.HIGHEST, preferred_element_type=F32) * sin


def _rms_fwd(h, g, name, deps=()):
    return _rowcall(lambda x, gg: (_rmsn(x, gg),), [h], [g], [(h.shape[1], BF16)], bm=512, deps=deps, name=name)[0]


def _rms_bwd(h, d_hn, dh_in, g, name, deps=()):
    def fn(x, dy, dres, gg):
        _, vjp = jax.vjp(_rmsn, x, gg)
        dx, dg = vjp(dy.astype(F32))
        dh = dres + dx
        return dh, dh, dg

    d = h.shape[1]
    return _rowcall(fn, [h, d_hn, dh_in], [g], [(d, F32), (d, BF16)], [g.shape], bm=512, deps=deps, name=name)


def _rope_tables(pos, name):
    inv = np.float32(ROPE_BASE) ** (-(np.arange(0, ROPE, 2, dtype=np.float32) / np.float32(ROPE)))
    inv = jnp.asarray(np.concatenate([inv, inv])[None, :].astype(np.float32))

    def fn(pp, iv):
        ang = pp.astype(F32) * iv
        return jnp.cos(ang), jnp.sin(ang)

    return _rowcall(fn, [pos], [inv], [(ROPE, F32), (ROPE, F32)], name=name)


def _rot_matrix():
    r = np.zeros((ROPE, ROPE), np.float32)
    half = ROPE // 2
    for j in range(half):
        r[j + half, j] = -1.0
        r[j, j + half] = 1.0
    return jnp.asarray(r)


def _prep1_fwd(lat, gq, gkv, name):
    def fn(l, a, b):
        return _rmsn(l[:, :Q_LORA], a), _rmsn(l[:, Q_LORA:Q_LORA + KV_LORA], b)

    return _rowcall(fn, [lat], [gq, gkv], [(Q_LORA, BF16), (KV_LORA, BF16)], name=name)


def _prep1_bwd(lat, d_cq, d_ckv, d_kr, gq, gkv, name):
    def fn(l, dq, dkv, dkr, a, b):
        _, vq = jax.vjp(_rmsn, l[:, :Q_LORA], a)
        _, vkv = jax.vjp(_rmsn, l[:, Q_LORA:Q_LORA + KV_LORA], b)
        dxq, dga = vq(dq.astype(F32))
        dxkv, dgb = vkv(dkv.astype(F32))
        return jnp.concatenate([dxq, dxkv, dkr], axis=1), dga, dgb

    return _rowcall(fn, [lat, d_cq, d_ckv, d_kr], [gq, gkv], [(lat.shape[1], BF16)], [gq.shape, gkv.shape], name=name)


def _qk_fn(qn_raw, qr_raw, kn_raw, kr_raw, gqn, gqr, gkn, gkr, cos, sin, rmat):
    return (_rmsn(qn_raw, gqn), _rot(_rmsn(qr_raw, gqr), cos, sin, rmat),
            _rmsn(kn_raw, gkn), _rot(_rmsn(kr_raw, gkr), cos, sin, rmat))


def _prep2_fwd(q_raw, kv_raw, lat, cos, sin, rmat, gains, name, bm=1024):
    H, T, _ = q_raw.shape
    bm = _pick(T, bm, 8)
    kr0 = Q_LORA + KV_LORA

    def body(q_ref, kv_ref, lat_ref, cos_ref, sin_ref, r_ref, gqn, gqr, gkn, gkr, qo, ko, vo):
        qr, kvr = q_ref[...], kv_ref[...]
        qn, qro, kn, kro = _qk_fn(qr[:, :NOPE], qr[:, NOPE:], kvr[:, :NOPE], lat_ref[:, kr0:kr0 + ROPE],
                                  gqn[...], gqr[...], gkn[...], gkr[...], cos_ref[...], sin_ref[...], r_ref[...])
        qo[:, :NOPE] = qn.astype(BF16)
        qo[:, NOPE:] = qro.astype(BF16)
        ko[:, :NOPE] = kn.astype(BF16)
        ko[:, NOPE:] = kro.astype(BF16)
        vo[...] = kvr[:, NOPE:].astype(BF16)

    hb = lambda c: pl.BlockSpec((None, bm, c), lambda m, h: (h, m, 0))
    rb = lambda c: pl.BlockSpec((bm, c), lambda m, h: (m, 0))
    wb = lambda s: pl.BlockSpec(tuple(s), lambda m, h: (0, 0))
    return pl.pallas_call(
        body, name=name, grid=(T // bm, H),
        in_specs=[hb(QK), hb(NOPE + VDIM), rb(lat.shape[1]), rb(ROPE), rb(ROPE), wb(rmat.shape)]
        + [wb(g.shape) for g in gains],
        out_specs=[hb(QK), hb(QK), hb(VDIM)],
        out_shape=[jax.ShapeDtypeStruct((H, T, QK), BF16), jax.ShapeDtypeStruct((H, T, QK), BF16),
                   jax.ShapeDtypeStruct((H, T, VDIM), BF16)],
        compiler_params=_params(("parallel", "parallel")),
    )(q_raw, kv_raw, lat, cos, sin, rmat, *gains)


def _prep2_bwd(q_raw, kv_raw, lat, cos, sin, rmat, gains, dq, dk, dv, name, bm=512):
    H, T, _ = q_raw.shape
    bm = _pick(T, bm, 8)
    kr0 = Q_LORA + KV_LORA

    def body(q_ref, kv_ref, lat_ref, cos_ref, sin_ref, r_ref, gqn, gqr, gkn, gkr, dq_ref, dk_ref, dv_ref,
             dqo, dkvo, dkro, o_gqn, o_gqr, o_gkn, o_gkr):
        m, h = pl.program_id(0), pl.program_id(1)
        qr, kvr = q_ref[...], kv_ref[...]
        cos_v, sin_v, r_v = cos_ref[...], sin_ref[...], r_ref[...]
        f = lambda a, b, c, d, g1, g2, g3, g4: _qk_fn(a, b, c, d, g1, g2, g3, g4, cos_v, sin_v, r_v)
        _, vjp = jax.vjp(f, qr[:, :NOPE], qr[:, NOPE:], kvr[:, :NOPE], lat_ref[:, kr0:kr0 + ROPE],
                         gqn[...], gqr[...], gkn[...], gkr[...])
        dqv, dkv_ = dq_ref[...], dk_ref[...]
        d_qn, d_qr, d_kn, d_kr, g1, g2, g3, g4 = vjp((dqv[:, :NOPE], dqv[:, NOPE:], dkv_[:, :NOPE], dkv_[:, NOPE:]))
        dqo[:, :NOPE] = d_qn.astype(BF16)
        dqo[:, NOPE:] = d_qr.astype(BF16)
        dkvo[:, :NOPE] = d_kn.astype(BF16)
        dkvo[:, NOPE:] = dv_ref[...].astype(BF16)

        @pl.when(h == 0)
        def _():
            dkro[...] = jnp.zeros_like(dkro)

        dkro[...] += d_kr

        @pl.when((h == 0) & (m == 0))
        def _():
            for o in (o_gqn, o_gqr, o_gkn, o_gkr):
                o[...] = jnp.zeros_like(o)

        for o, g in zip((o_gqn, o_gqr, o_gkn, o_gkr), (g1, g2, g3, g4)):
            o[...] += g

    hb = lambda c: pl.BlockSpec((None, bm, c), lambda m, h: (h, m, 0))
    rb = lambda c: pl.BlockSpec((bm, c), lambda m, h: (m, 0))
    wb = lambda s: pl.BlockSpec(tuple(s), lambda m, h: (0, 0))
    return pl.pallas_call(
        body, name=name, grid=(T // bm, H),
        in_specs=[hb(QK), hb(NOPE + VDIM), rb(lat.shape[1]), rb(ROPE), rb(ROPE), wb(rmat.shape)]
        + [wb(g.shape) for g in gains] + [hb(QK), hb(QK), hb(VDIM)],
        out_specs=[hb(QK), hb(NOPE + VDIM), rb(ROPE)] + [wb(g.shape) for g in gains],
        out_shape=[jax.ShapeDtypeStruct((H, T, QK), BF16), jax.ShapeDtypeStruct((H, T, NOPE + VDIM), BF16),
                   jax.ShapeDtypeStruct((T, ROPE), F32)] + [jax.ShapeDtypeStruct(g.shape, F32) for g in gains],
        compiler_params=_params(("arbitrary", "arbitrary")),
    )(q_raw, kv_raw, lat, cos, sin, rmat, *gains, dq, dk, dv)


_NT = (((1,), (1,)), ((), ()))
_TN = (((0,), (0,)), ((), ()))


def _causal(blk):
    return lax.broadcasted_iota(jnp.int32, (blk, blk), 1) <= lax.broadcasted_iota(jnp.int32, (blk, blk), 0)


def _attn_fwd(q, k, v, seq, name, blk=512):
    H, T, _ = q.shape
    nb = T // seq
    blk = _pick(seq, blk)
    nq = seq // blk
    scale = float(QK) ** -0.5

    def body(q_ref, k_ref, v_ref, o_ref, lse_ref):
        qi = pl.program_id(2)
        qb = q_ref[...]

        def step(j, carry, diagonal):
            m, l, acc = carry
            ks = pl.ds(pl.multiple_of(j * blk, blk), blk)
            s = lax.dot_general(qb, k_ref[ks, :], _NT, preferred_element_type=F32) * scale
            if diagonal:
                s = jnp.where(_causal(blk), s, -jnp.inf)
            m_new = jnp.maximum(m, jnp.max(s, axis=1, keepdims=True))
            pr = jnp.exp(s - m_new)
            alpha = jnp.exp(m - m_new)
            l = alpha * l + jnp.sum(pr, axis=1, keepdims=True)
            acc = alpha * acc + jnp.dot(pr.astype(BF16), v_ref[ks, :], preferred_element_type=F32)
            return m_new, l, acc

        init = (jnp.full((blk, 1), -jnp.inf, F32), jnp.zeros((blk, 1), F32), jnp.zeros((blk, VDIM), F32))
        below = lax.fori_loop(0, qi, lambda j, c: step(j, c, False), init)
        m, l, acc = step(qi, below, True)
        o_ref[...] = (acc / l).astype(o_ref.dtype)
        lse_ref[...] = m + jnp.log(l)

    return pl.pallas_call(
        body, name=name, grid=(H, nb, nq),
        in_specs=[pl.BlockSpec((None, blk, QK), lambda h, b, i: (h, b * nq + i, 0)),
                  pl.BlockSpec((None, seq, QK), lambda h, b, i: (h, b, 0)),
                  pl.BlockSpec((None, seq, VDIM), lambda h, b, i: (h, b, 0))],
        out_specs=[pl.BlockSpec((blk, VDIM), lambda h, b, i: (b * nq + i, h)),
                   pl.BlockSpec((None, blk, 1), lambda h, b, i: (h, b * nq + i, 0))],
        out_shape=[jax.ShapeDtypeStruct((T, H * VDIM), BF16), jax.ShapeDtypeStruct((H, T, 1), F32)],
        compiler_params=_params(("parallel", "parallel", "parallel")),
    )(q, k, v)


def _attn_bwd(q, k, v, o, do, lse, seq, name, blk=512):
    H, T, _ = q.shape
    nb = T // seq
    blk = _pick(seq, blk)
    nq = seq // blk
    scale = float(QK) ** -0.5

    def body(q_ref, k_ref, v_ref, o_ref, do_ref, lse_ref, dq_ref, dk_ref, dv_ref):
        dk_ref[...] = jnp.zeros_like(dk_ref)
        dv_ref[...] = jnp.zeros_like(dv_ref)

        def qloop(i, carry):
            qs = pl.ds(pl.multiple_of(i * blk, blk), blk)
            qb = q_ref[qs, :]
            dob = do_ref[qs, :]
            dof = dob.astype(F32)
            lse_b = lse_ref[qs, :]
            delta = jnp.sum(dof * o_ref[qs, :].astype(F32), axis=1, keepdims=True)

            def kstep(j, dq_acc, diagonal):
                ks = pl.ds(pl.multiple_of(j * blk, blk), blk)
                kb = k_ref[ks, :]
                vb = v_ref[ks, :]
                s = lax.dot_general(qb, kb, _NT, preferred_element_type=F32) * scale
                pr = jnp.exp(s - lse_b)
                if diagonal:
                    pr = jnp.where(_causal(blk), pr, 0.0)
                dp = lax.dot_general(dob, vb, _NT, preferred_element_type=F32)
                ds = (pr * (dp - delta) * scale).astype(BF16)
                prb = pr.astype(BF16)
                dv_ref[ks, :] += lax.dot_general(prb, dob, _TN, preferred_element_type=F32)
                dk_ref[ks, :] += lax.dot_general(ds, qb, _TN, preferred_element_type=F32)
                return dq_acc + jnp.dot(ds, kb, preferred_element_type=F32)

            below = lax.fori_loop(0, i, lambda j, c: kstep(j, c, False), jnp.zeros((blk, QK), F32))
            dq_ref[qs, :] = kstep(i, below, True)
            return carry

        lax.fori_loop(0, nq, qloop, 0)

    hb = lambda c: pl.BlockSpec((None, seq, c), lambda h, b: (h, b, 0))
    cb = lambda: pl.BlockSpec((seq, VDIM), lambda h, b: (b, h))
    return pl.pallas_call(
        body, name=name, grid=(H, nb),
        in_specs=[hb(QK), hb(QK), hb(VDIM), cb(), cb(), hb(1)],
        out_specs=[hb(QK), hb(QK), hb(VDIM)],
        out_shape=[jax.ShapeDtypeStruct((H, T, QK), F32), jax.ShapeDtypeStruct((H, T, QK), F32),
                   jax.ShapeDtypeStruct((H, T, VDIM), F32)],
        compiler_params=_params(("parallel", "parallel")),
    )(q, k, v, o, do, lse)


def _gelu_ln_fwd(z, g, b, name):
    half = z.shape[1] // 2

    def fn(zz, gg, bb):
        return _gelu(zz[:, :half].astype(F32)), _layer_norm(_gelu(zz[:, half:].astype(F32)), gg, bb)

    return _rowcall(fn, [z], [g, b], [(half, BF16), (half, BF16)], name=name)


def _gelu_ln_bwd(z, d_u, d_vn, g, b, name):
    half = z.shape[1] // 2

    def gelu_and_slope(x):
        c, a = math.sqrt(2.0 / math.pi), 0.044715
        x2 = x * x
        t = jnp.tanh(c * x * (1.0 + a * x2))
        return 0.5 * x * (1.0 + t), 0.5 * (1.0 + t) + 0.5 * x * (1.0 - t * t) * (c * (1.0 + 3.0 * a * x2))

    def fn(zz, du, dvn, gg, bb):
        du, dvn = du.astype(F32), dvn.astype(F32)
        _, su = gelu_and_slope(zz[:, :half].astype(F32))
        v, sv = gelu_and_slope(zz[:, half:].astype(F32))
        xc = v - jnp.mean(v, axis=-1, keepdims=True)
        rstd = lax.rsqrt(jnp.mean(xc * xc, axis=-1, keepdims=True) + EPS)
        y = xc * rstd
        dy = dvn * gg
        dv = rstd * (dy - jnp.mean(dy, axis=-1, keepdims=True) - y * jnp.mean(dy * y, axis=-1, keepdims=True))
        dg = jnp.sum(dvn * y, axis=0, keepdims=True)
        db = jnp.sum(dvn, axis=0, keepdims=True)
        return jnp.concatenate([du * su, dv * sv], axis=1), dg, db

    return _rowcall(fn, [z, d_u, d_vn], [g, b], [(z.shape[1], BF16)], [g.shape, b.shape], bm=128, name=name)


def _tril_bf16(ws):
    t = lax.broadcasted_iota(jnp.int32, ws.shape, 0)
    s = lax.broadcasted_iota(jnp.int32, ws.shape, 1)
    return jnp.where(s <= t, ws, 0.0).astype(BF16)


def _sgu_fwd(u, vn, ws, bs, name, bm=2048):
    T, half = u.shape
    gd = half // GROUPS
    bm = _pick(T, bm, CHUNK)
    nc = bm // CHUNK

    def body(u_ref, vn_ref, ws_ref, bs_ref, y_ref):
        wm = _tril_bf16(ws_ref[...])
        bias = bs_ref[...]
        for c in range(nc):
            rs = slice(c * CHUNK, (c + 1) * CHUNK)
            sv = jnp.dot(wm, vn_ref[rs, :], preferred_element_type=F32) + bias
            y_ref[rs, :] = (u_ref[rs, :].astype(F32) * sv).astype(y_ref.dtype)

    tb = lambda: pl.BlockSpec((bm, gd), lambda g, i: (i, g))
    return pl.pallas_call(
        body, name=name, grid=(GROUPS, T // bm),
        in_specs=[tb(), tb(), pl.BlockSpec((None, CHUNK, CHUNK), lambda g, i: (g, 0, 0)),
                  pl.BlockSpec((None, CHUNK, 1), lambda g, i: (g, 0, 0))],
        out_specs=tb(),
        out_shape=jax.ShapeDtypeStruct((T, half), BF16),
        compiler_params=_params(("parallel", "parallel")),
    )(u, vn, ws, bs)


def _sgu_bwd(u, vn, dy, ws, bs, name, bm=1024):
    T, half = u.shape
    gd = half // GROUPS
    bm = _pick(T, bm, CHUNK)
    nc = bm // CHUNK

    def body(u_ref, vn_ref, dy_ref, ws_ref, bs_ref, du_ref, dvn_ref, dws_ref, dbs_ref):
        @pl.when(pl.program_id(1) == 0)
        def _():
            dws_ref[...] = jnp.zeros_like(dws_ref)
            dbs_ref[...] = jnp.zeros_like(dbs_ref)

        wm = _tril_bf16(ws_ref[...])
        bias = bs_ref[...]
        dws = jnp.zeros((CHUNK, CHUNK), F32)
        dbs = jnp.zeros((CHUNK, 1), F32)
        for c in range(nc):
            rs = slice(c * CHUNK, (c + 1) * CHUNK)
            vb = vn_ref[rs, :]
            dyb = dy_ref[rs, :].astype(F32)
            sv = jnp.dot(wm, vb, preferred_element_type=F32) + bias
            du_ref[rs, :] = (dyb * sv).astype(du_ref.dtype)
            dsv = dyb * u_ref[rs, :].astype(F32)
            dsb = dsv.astype(BF16)
            dvn_ref[rs, :] = lax.dot_general(wm, dsb, _TN, preferred_element_type=F32).astype(dvn_ref.dtype)
            dws = dws + lax.dot_general(dsb, vb, _NT, preferred_element_type=F32)
            dbs = dbs + jnp.sum(dsv, axis=1, keepdims=True)
        t = lax.broadcasted_iota(jnp.int32, (CHUNK, CHUNK), 0)
        s = lax.broadcasted_iota(jnp.int32, (CHUNK, CHUNK), 1)
        dws_ref[...] += jnp.where(s <= t, dws, 0.0)
        dbs_ref[...] += dbs

    tb = lambda: pl.BlockSpec((bm, gd), lambda g, i: (i, g))
    wsb = lambda: pl.BlockSpec((None, CHUNK, CHUNK), lambda g, i: (g, 0, 0))
    bsb = lambda: pl.BlockSpec((None, CHUNK, 1), lambda g, i: (g, 0, 0))
    return pl.pallas_call(
        body, name=name, grid=(GROUPS, T // bm),
        in_specs=[tb(), tb(), tb(), wsb(), bsb()],
        out_specs=[tb(), tb(), wsb(), bsb()],
        out_shape=[jax.ShapeDtypeStruct((T, half), BF16), jax.ShapeDtypeStruct((T, half), BF16),
                   jax.ShapeDtypeStruct(ws.shape, F32), jax.ShapeDtypeStruct(bs.shape, F32)],
        compiler_params=_params(("parallel", "arbitrary")),
    )(u, vn, dy, ws, bs)


def _loss_head(y, t, name):
    d_model = y.shape[1]

    def fn(yy, tt):
        d = yy - tt
        part = 0.5 * jnp.sum(jnp.mean(d * d, axis=-1, keepdims=True), axis=0, keepdims=True)
        return d / d_model, jnp.zeros((1, LANE), F32) + part

    dy, part = _rowcall(fn, [y, t], [], [(d_model, F32)], [(1, LANE)], name=name)
    return dy, part[0, 0]


def _adamw(parts, w, m, v, prev, layer, name):
    L, R, C = w.shape
    br = _pick(R, max(8, (128 * 1024) // C // 8 * 8), 8)
    c1 = 1.0 - B1 ** STEP
    c2 = 1.0 - B2 ** STEP
    if prev is None:
        prev = [lax.empty(w.shape, F32) for _ in range(4)]

    def body(p_ref, w_ref, m_ref, v_ref, a0, a1, a2, a3, g_o, d_o, m_o, v_o, token):
        g = p_ref[0].astype(F32)
        for d in range(1, N_DEV):
            g = g + p_ref[d].astype(F32)
        mn = B1 * m_ref[...] + (1.0 - B1) * g
        vn = B2 * v_ref[...] + (1.0 - B2) * (g * g)
        g_o[...] = g
        m_o[...] = mn
        v_o[...] = vn
        d_o[...] = -LR * ((mn / c1) / (jnp.sqrt(vn / c2) + ADAM_EPS) + WD * w_ref[...])
        token[...] = jnp.zeros_like(token)

    blk = lambda: pl.BlockSpec((None, br, C), lambda i: (layer, i, 0))
    anywhere = pl.BlockSpec(memory_space=pl.ANY)
    outs = pl.pallas_call(
        body, name=name, grid=(R // br,),
        in_specs=[pl.BlockSpec((N_DEV, br, C), lambda i: (0, i, 0)), blk(), blk(), blk()] + [anywhere] * 4,
        out_specs=[blk(), blk(), blk(), blk(), pl.BlockSpec((8, LANE), lambda i: (0, 0))],
        out_shape=[jax.ShapeDtypeStruct((L, R, C), F32)] * 4 + [jax.ShapeDtypeStruct((8, LANE), F32)],
        input_output_aliases={4: 0, 5: 1, 6: 2, 7: 3},
        compiler_params=_params(("arbitrary",)),
    )(parts, w, m, v, *prev)
    return list(outs[:4]), outs[4]


def _mesh_pos():
    return lax.axis_index("x"), lax.axis_index("y"), lax.axis_index("c")


def _flip(pos, k):
    x, y, c = pos
    px = 1 - x if k & 4 else x
    py = 1 - y if k & 2 else y
    pc = 1 - c if k & 1 else c
    return px, py, pc


HBM_SPEC = pl.BlockSpec(memory_space=pltpu.HBM)
SEM_SPEC = pl.BlockSpec(memory_space=pltpu.SEMAPHORE)
EFFECT = pltpu.SideEffectType.DATAFLOW_SIDE_EFFECTING


def _hbm(a):
    return pltpu.with_memory_space_constraint(a, pltpu.HBM)


def _device_index():
    x, y, c = _mesh_pos()
    return 4 * x + 2 * y + c


def _peer_copy(src, land, send, recv, a, k, pos, scatter):
    peer = _flip(pos, k)
    me = 4 * pos[0] + 2 * pos[1] + pos[2]
    piece = src.at[4 * peer[0] + 2 * peer[1] + peer[2]] if scatter else src
    return pltpu.make_async_remote_copy(
        src_ref=piece, dst_ref=land.at[me], send_sem=send.at[7 * a + k - 1], recv_sem=recv.at[7 * a + k - 1],
        device_id=peer, device_id_type=pl.DeviceIdType.MESH)


def _xchg_start(srcs, scatter, after, name):
    n = len(srcs)

    def body(*refs):
        src, land = refs[:n], refs[n:2 * n]
        send, recv, token = refs[2 * n + 1], refs[2 * n + 2], refs[-1]
        pos = _mesh_pos()
        for k in range(1, N_DEV):
            for a in range(n):
                _peer_copy(src[a], land[a], send, recv, a, k, pos, scatter).start()
        token[...] = jnp.zeros_like(token)

    lands = [lax.empty(s.shape if scatter else (N_DEV,) + s.shape, s.dtype) for s in srcs]
    outs = pl.pallas_call(
        body, name=name,
        out_shape=(pltpu.SemaphoreType.DMA((7 * n,)), pltpu.SemaphoreType.DMA((7 * n,)),
                   *[pltpu.HBM(s.shape, s.dtype) for s in srcs], *[pltpu.HBM(l.shape, l.dtype) for l in lands],
                   jax.ShapeDtypeStruct((8, LANE), F32)),
        in_specs=[HBM_SPEC] * (2 * n) + [pl.BlockSpec(memory_space=pl.ANY)],
        out_specs=(SEM_SPEC, SEM_SPEC, *[HBM_SPEC] * (2 * n), pl.BlockSpec(memory_space=pltpu.VMEM)),
        input_output_aliases={q: 2 + q for q in range(2 * n)},
        compiler_params=pltpu.CompilerParams(has_side_effects=EFFECT),
    )(*[_hbm(s) for s in srcs], *[_hbm(l) for l in lands], after)
    handle = dict(send=outs[0], recv=outs[1], srcs=list(outs[2:2 + n]), lands=list(outs[2 + n:2 + 2 * n]),
                  scatter=scatter)
    return handle, outs[-1]


def _xchg_wait(handle, after, name):
    srcs, lands, scatter = handle['srcs'], handle['lands'], handle['scatter']
    n = len(srcs)

    def body(*refs):
        src, land = refs[:n], refs[n:2 * n]
        send, recv = refs[2 * n], refs[2 * n + 1]
        pos = _mesh_pos()
        for k in range(1, N_DEV):
            for a in range(n):
                cp = _peer_copy(src[a], land[a], send, recv, a, k, pos, scatter)
                cp.wait_send()
                cp.wait_recv()

    outs = pl.pallas_call(
        body, name=name,
        out_shape=[pltpu.HBM(s.shape, s.dtype) for s in srcs] + [pltpu.HBM(l.shape, l.dtype) for l in lands],
        in_specs=[HBM_SPEC] * (2 * n) + [SEM_SPEC, SEM_SPEC, pl.BlockSpec(memory_space=pl.ANY)],
        out_specs=[HBM_SPEC] * (2 * n),
        input_output_aliases={q: q for q in range(2 * n)},
        compiler_params=pltpu.CompilerParams(has_side_effects=EFFECT),
    )(*srcs, *lands, handle['send'], handle['recv'], after)
    me = _device_index()
    full = []
    for src, land in zip(outs[:n], outs[n:]):
        mine = lax.dynamic_index_in_dim(src, me, 0, keepdims=False) if scatter else src
        full.append(lax.dynamic_update_index_in_dim(land, mine, me, 0))
    return full


def _pack(parts):
    flat = jnp.concatenate([q.reshape(-1) for q in parts])
    pad = (-flat.shape[0]) % (8 * LANE)
    return jnp.pad(flat, (0, pad)).reshape(-1, LANE)


def _unpack(packed, like):
    flat = packed.reshape(-1)
    out, o = [], 0
    for q in like:
        out.append(flat[o:o + q.size].reshape(q.shape))
        o += q.size
    return out


def kernel(x, p, positions, norm_mix, norm_ffn, norm_ple, mla_w_down, mla_q_lora_g, mla_kv_lora_g, mla_w_uq, mla_w_ukv, mla_q_nope_g, mla_q_rope_g, mla_k_nope_g, mla_k_rope_g, mla_w_out, gmlp_w_in, gmlp_ln_g, gmlp_ln_b, gmlp_w_s, gmlp_b_s, gmlp_w_out, ffn_w_up, ffn_w_down, ple_w_gate, ple_w_proj, loss_target, m_norm_mix, m_norm_ffn, m_norm_ple, m_mla_w_down, m_mla_q_lora_g, m_mla_kv_lora_g, m_mla_w_uq, m_mla_w_ukv, m_mla_q_nope_g, m_mla_q_rope_g, m_mla_k_nope_g, m_mla_k_rope_g, m_mla_w_out, m_gmlp_w_in, m_gmlp_ln_g, m_gmlp_ln_b, m_gmlp_w_s, m_gmlp_b_s, m_gmlp_w_out, m_ffn_w_up, m_ffn_w_down, m_ple_w_gate, m_ple_w_proj, v_norm_mix, v_norm_ffn, v_norm_ple, v_mla_w_down, v_mla_q_lora_g, v_mla_kv_lora_g, v_mla_w_uq, v_mla_w_ukv, v_mla_q_nope_g, v_mla_q_rope_g, v_mla_k_nope_g, v_mla_k_rope_g, v_mla_w_out, v_gmlp_w_in, v_gmlp_ln_g, v_gmlp_ln_b, v_gmlp_w_s, v_gmlp_b_s, v_gmlp_w_out, v_ffn_w_up, v_ffn_w_down, v_ple_w_gate, v_ple_w_proj):
    W = dict(zip(WEIGHTS, (norm_mix, norm_ffn, norm_ple, mla_w_down, mla_q_lora_g, mla_kv_lora_g, mla_w_uq, mla_w_ukv, mla_q_nope_g, mla_q_rope_g, mla_k_nope_g, mla_k_rope_g, mla_w_out, gmlp_w_in, gmlp_ln_g, gmlp_ln_b, gmlp_w_s, gmlp_b_s, gmlp_w_out, ffn_w_up, ffn_w_down, ple_w_gate, ple_w_proj)))
    M1 = dict(zip(WEIGHTS, (m_norm_mix, m_norm_ffn, m_norm_ple, m_mla_w_down, m_mla_q_lora_g, m_mla_kv_lora_g, m_mla_w_uq, m_mla_w_ukv, m_mla_q_nope_g, m_mla_q_rope_g, m_mla_k_nope_g, m_mla_k_rope_g, m_mla_w_out, m_gmlp_w_in, m_gmlp_ln_g, m_gmlp_ln_b, m_gmlp_w_s, m_gmlp_b_s, m_gmlp_w_out, m_ffn_w_up, m_ffn_w_down, m_ple_w_gate, m_ple_w_proj)))
    M2 = dict(zip(WEIGHTS, (v_norm_mix, v_norm_ffn, v_norm_ple, v_mla_w_down, v_mla_q_lora_g, v_mla_kv_lora_g, v_mla_w_uq, v_mla_w_ukv, v_mla_q_nope_g, v_mla_q_rope_g, v_mla_k_nope_g, v_mla_k_rope_g, v_mla_w_out, v_gmlp_w_in, v_gmlp_ln_g, v_gmlp_ln_b, v_gmlp_w_s, v_gmlp_b_s, v_gmlp_w_out, v_ffn_w_up, v_ffn_w_down, v_ple_w_gate, v_ple_w_proj)))

    nb, seq, d_model = x.shape
    assert d_model <= 1024, "the rms norms fused into matmul epilogues need whole rows in one output tile"
    T = nb * seq
    depth = norm_mix.shape[0]
    h = x.reshape(T, d_model)
    target = loss_target.reshape(T, d_model)
    p_bf = p.reshape(depth, T, p.shape[-1]).astype(BF16)

    def stage_keys(st):
        i, second = divmod(st, 2)
        if second:
            return [(n, i) for n in ('ffn_w_up', 'ffn_w_down', 'ple_w_gate', 'ple_w_proj')]
        mix = (['mla_w_down', 'mla_w_uq', 'mla_w_ukv', 'mla_w_out'] if i % 2 == 0 else
               ['gmlp_w_in', 'gmlp_ln_g', 'gmlp_ln_b', 'gmlp_w_out'])
        return [(n, i // 2) for n in mix]

    FW = {n: {} for n in SHARDED}

    def start_weights(st, after):
        keys = stage_keys(st)
        srcs = [W[n][l] if n in F32_PAYLOAD else W[n][l].astype(BF16) for n, l in keys]
        handle, token = _xchg_start(srcs, False, after, "weights_start%d" % st)
        return (keys, handle), [token]

    def wait_weights(pending, st, after):
        keys, handle = pending
        landed = _xchg_wait(handle, after, "weights_wait%d" % st)
        for (n, l), full in zip(keys, landed):
            if SHARD_AXIS[n] == 1:
                FW[n][l] = full.reshape((-1,) + full.shape[2:])
            elif n in ('mla_w_uq', 'mla_w_ukv'):
                FW[n][l] = full
            else:
                FW[n][l] = jnp.transpose(full, (1, 0, 2)).reshape(full.shape[1], -1)
        return landed[0]

    row = lambda a: a.reshape(1, -1)
    cos, sin = _rope_tables(positions.reshape(T, 1), "rope_tables")
    rmat = _rot_matrix()

    def add_and_norm(acc, res, g):
        hh = res + acc
        return hh, _rmsn(hh, g)

    saved = []
    chain = {'stage': 0}
    chain['pending'], _ = start_weights(0, h)

    def advance(after):
        st = chain['stage']
        if st >= 2 * depth:
            return []
        landed = wait_weights(chain['pending'], st, after)
        chain['stage'] = st + 1
        if st + 1 >= 2 * depth:
            return []
        chain['pending'], token = start_weights(st + 1, landed)
        return token

    hn = _rms_fwd(h, row(W['norm_mix'][0]), "rms_fwd")
    token = advance(hn)
    for i in range(depth):
        j = i // 2
        s = {}
        s['h0'] = h
        s['hn'] = hn
        if i % 2 == 0:
            gains = [row(W['mla_q_nope_g'][j]), row(W['mla_q_rope_g'][j]), row(W['mla_k_nope_g'][j]),
                     row(W['mla_k_rope_g'][j])]
            lat = _mm(hn, FW['mla_w_down'][j], deps=token, name="mla_down")
            cq, ckv = _prep1_fwd(lat, row(W['mla_q_lora_g'][j]), row(W['mla_kv_lora_g'][j]), "mla_prep1")
            q_raw = _mm(cq, FW['mla_w_uq'][j], out_blocks=HEADS, name="mla_uq")
            kv_raw = _mm(ckv, FW['mla_w_ukv'][j], out_blocks=HEADS, name="mla_ukv")
            q, k, v = _prep2_fwd(q_raw, kv_raw, lat, cos, sin, rmat, gains, "mla_prep2")
            o, lse = _attn_fwd(q, k, v, seq, "attn_fwd")
            token = advance(o)
            h, hn2 = _mm(o, FW['mla_w_out'][j], extras=(h,), rows=(row(W['norm_ffn'][i]),), epilogue=add_and_norm,
                         out_dtypes=(F32, BF16), deps=token, name="mla_out")
            s.update(lat=lat, cq=cq, ckv=ckv, q_raw=q_raw, kv_raw=kv_raw, q=q, k=k, v=v, o=o, lse=lse, gains=gains)
        else:
            z = _mm(hn, FW['gmlp_w_in'][j], out_dtypes=(BF16,), name="gmlp_in")
            u, vn = _gelu_ln_fwd(z, row(FW['gmlp_ln_g'][j]), row(FW['gmlp_ln_b'][j]), "gmlp_gelu_ln")
            bs3 = W['gmlp_b_s'][j][:, :, None]
            y = _sgu_fwd(u, vn, W['gmlp_w_s'][j], bs3, "gmlp_sgu")
            token = advance(y)
            h, hn2 = _mm(y, FW['gmlp_w_out'][j], extras=(h,), rows=(row(W['norm_ffn'][i]),), epilogue=add_and_norm,
                         out_dtypes=(F32, BF16), deps=token, name="gmlp_out")
            s.update(z=z, u=u, vn=vn, y=y, bs3=bs3)
        s['h1'] = h
        a, r = _mm(hn2, FW['ffn_w_up'][i], epilogue=lambda acc: (acc, jnp.square(jnp.maximum(acc, 0.0))),
                   out_dtypes=(BF16, BF16), name="ffn_up")
        h, hn3 = _mm(r, FW['ffn_w_down'][i], extras=(h,), rows=(row(W['norm_ple'][i]),), epilogue=add_and_norm,
                     out_dtypes=(F32, BF16), name="ffn_down")
        s.update(hn2=hn2, a=a, r=r, h2=h)
        token = advance(hn3)
        gt = _mm(hn3, FW['ple_w_gate'][i], deps=token, name="ple_gate")
        if i + 1 < depth:
            def gate_and_norm(acc, g_, res, gain):
                hh = res + _sigmoid(g_) * acc
                return acc, hh, _rmsn(hh, gain)

            pp, h, hn = _mm(p_bf[i], FW['ple_w_proj'][i], extras=(gt, h), rows=(row(W['norm_mix'][i + 1]),),
                            epilogue=gate_and_norm, out_dtypes=(F32, F32, BF16), name="ple_proj")
        else:
            pp, h = _mm(p_bf[i], FW['ple_w_proj'][i], extras=(gt, h),
                        epilogue=lambda acc, g_, res: (acc, res + _sigmoid(g_) * acc), out_dtypes=(F32, F32),
                        name="ple_proj_last")
        s.update(hn3=hn3, gt=gt, pp=pp)
        saved.append(s)

    dh, loss_part = _loss_head(h, target, "loss_head")
    loss = lax.psum(loss_part, MESH_AXES)

    G = {n: [None] * W[n].shape[0] for n in REPLICATED}
    res = {}
    flying = []

    def shard3(n):
        shp = W[n].shape
        return shp[0], int(np.prod(shp[1:-1])), shp[-1]

    def by_owner(g):
        return g.reshape((N_DEV, g.shape[0] // N_DEV) + g.shape[1:])

    def send_grads(tag, grads):
        handle, token = _xchg_start([g for _, g in grads], True, grads[-1][1], "grads_start_" + tag)
        flying.append((tag, [key for key, _ in grads], handle))
        return [token]

    def land_grads(after):
        tag, keys, handle = flying.pop(0)
        done = []
        for (n, l), full in zip(keys, _xchg_wait(handle, after, "grads_wait_" + tag)):
            dims = shard3(n)
            res[n], token = _adamw(full.reshape((N_DEV,) + dims[1:]), W[n].reshape(dims), M1[n].reshape(dims),
                                   M2[n].reshape(dims), res.get(n), l, "adamw_" + n)
            done.append(token)
        return done

    def start_small(names, tag, after):
        handle, token = _xchg_start([_pack([jnp.stack(G[n]) for n in names])], False, after, "small_start_" + tag)
        return (names, handle), [token]

    def land_small(pending_small, tag, after):
        names, handle = pending_small
        (parts,) = _xchg_wait(handle, after, "small_wait_" + tag)
        like = [W[n] for n in names]
        outs, _ = _adamw(parts, _pack(like)[None], _pack([M1[n] for n in names])[None],
                         _pack([M2[n] for n in names])[None], None, 0, "adamw_small_" + tag)
        unpacked = [_unpack(o, like) for o in outs]
        for idx, n in enumerate(names):
            res[n] = [unpacked[q][idx] for q in range(4)]

    spatial = ['gmlp_w_s', 'gmlp_b_s']
    token = []
    for i in reversed(range(depth)):
        j = i // 2
        s = saved[i]
        def ple_elem(d, g_, pq):
            sg = _sigmoid(g_)
            return d * sg, d * pq * sg * (1.0 - sg)

        d_pp, d_gt = _rowcall(ple_elem, [dh, s['gt'], s['pp']], [], [(d_model, BF16), (d_model, BF16)], name="ple_bwd")
        g_proj = _mm(p_bf[i], d_pp, ta=True, out_dtypes=(BF16,), name="ple_proj_dw")
        g_proj = jnp.transpose(g_proj.reshape(g_proj.shape[0], N_DEV, -1), (1, 0, 2))
        g_gate = _mm(s['hn3'], d_gt, ta=True, out_dtypes=(BF16,), name="ple_gate_dw")
        d_hn3 = _mm(d_gt, FW['ple_w_gate'][i], tb=True, out_dtypes=(BF16,), name="ple_gate_dx")
        dh, dh_bf, dg = _rms_bwd(s['h2'], d_hn3, dh, row(W['norm_ple'][i]), "rms_bwd", deps=token)
        G['norm_ple'][i] = dg[0]
        d_a = _mm(dh_bf, FW['ffn_w_down'][i], tb=True, extras=(s['a'],),
                  epilogue=lambda acc, a_: (acc * (2.0 * jnp.maximum(a_.astype(F32), 0.0)),), out_dtypes=(BF16,),
                  name="ffn_down_dx")
        g_down = _mm(s['r'], dh_bf, ta=True, out_dtypes=(BF16,), name="ffn_down_dw")
        g_up = _mm(s['hn2'], d_a, ta=True, out_blocks=N_DEV, out_dtypes=(BF16,), name="ffn_up_dw")
        token = send_grads("mlp%d" % i, [(('ple_w_proj', i), g_proj), (('ple_w_gate', i), by_owner(g_gate)),
                                         (('ffn_w_down', i), by_owner(g_down)), (('ffn_w_up', i), g_up)])
        if len(flying) > 1:
            token = token + land_grads(g_up)
        d_hn2 = _mm(d_a, FW['ffn_w_up'][i], tb=True, out_dtypes=(BF16,), name="ffn_up_dx")
        dh, dh_bf, dg = _rms_bwd(s['h1'], d_hn2, dh, row(W['norm_ffn'][i]), "rms_bwd", deps=token)
        G['norm_ffn'][i] = dg[0]
        if i % 2 == 0:
            d_o = _mm(dh_bf, FW['mla_w_out'][j], tb=True, out_dtypes=(BF16,), name="mla_out_dx")
            g_out = _mm(s['o'], dh_bf, ta=True, out_dtypes=(BF16,), name="mla_out_dw")
            dq, dk, dv = _attn_bwd(s['q'], s['k'], s['v'], s['o'], d_o, s['lse'], seq, "attn_bwd")
            d_q_raw, d_kv_raw, d_kr, g1, g2, g3, g4 = _prep2_bwd(
                s['q_raw'], s['kv_raw'], s['lat'], cos, sin, rmat, s['gains'], dq, dk, dv, "mla_prep2_bwd")
            G['mla_q_nope_g'][j], G['mla_q_rope_g'][j] = g1[0], g2[0]
            G['mla_k_nope_g'][j], G['mla_k_rope_g'][j] = g3[0], g4[0]
            g_uq = _mm(s['cq'], d_q_raw, ta=True, out_blocks=N_DEV, out_dtypes=(BF16,), name="mla_uq_dw")
            g_ukv = _mm(s['ckv'], d_kv_raw, ta=True, out_blocks=N_DEV, out_dtypes=(BF16,), name="mla_ukv_dw")
            d_cq = _mm(d_q_raw, FW['mla_w_uq'][j], tb=True, out_dtypes=(BF16,), name="mla_uq_dx")
            d_ckv = _mm(d_kv_raw, FW['mla_w_ukv'][j], tb=True, out_dtypes=(BF16,), name="mla_ukv_dx")
            d_lat, dga, dgb = _prep1_bwd(s['lat'], d_cq, d_ckv, d_kr, row(W['mla_q_lora_g'][j]),
                                         row(W['mla_kv_lora_g'][j]), "mla_prep1_bwd")
            G['mla_q_lora_g'][j], G['mla_kv_lora_g'][j] = dga[0], dgb[0]
            g_down = _mm(s['hn'], d_lat, ta=True, out_dtypes=(BF16,), name="mla_down_dw")
            grads = [(('mla_w_out', j), by_owner(g_out)), (('mla_w_uq', j), g_uq), (('mla_w_ukv', j), g_ukv),
                     (('mla_w_down', j), by_owner(g_down))]
            d_hn = _mm(d_lat, FW['mla_w_down'][j], tb=True, out_dtypes=(BF16,), name="mla_down_dx")
        else:
            d_y = _mm(dh_bf, FW['gmlp_w_out'][j], tb=True, out_dtypes=(BF16,), name="gmlp_out_dx")
            g_out = _mm(s['y'], dh_bf, ta=True, out_dtypes=(BF16,), name="gmlp_out_dw")
            d_u, d_vn, d_ws, d_bs = _sgu_bwd(s['u'], s['vn'], d_y, W['gmlp_w_s'][j], s['bs3'], "gmlp_sgu_bwd")
            G['gmlp_w_s'][j], G['gmlp_b_s'][j] = d_ws, d_bs[:, :, 0]
            d_z, d_lg, d_lb = _gelu_ln_bwd(s['z'], d_u, d_vn, row(FW['gmlp_ln_g'][j]), row(FW['gmlp_ln_b'][j]),
                                           "gmlp_gelu_ln_bwd")
            g_in = _mm(s['hn'], d_z, ta=True, out_blocks=N_DEV, out_dtypes=(BF16,), name="gmlp_in_dw")
            grads = [(('gmlp_w_out', j), by_owner(g_out)), (('gmlp_ln_g', j), by_owner(d_lg[0])),
                     (('gmlp_ln_b', j), by_owner(d_lb[0])), (('gmlp_w_in', j), g_in)]
            d_hn = _mm(d_z, FW['gmlp_w_in'][j], tb=True, out_dtypes=(BF16,), name="gmlp_in_dx")
        token = send_grads("mix%d" % i, grads)
        if len(flying) > 1:
            token = token + land_grads(grads[-1][1])
        dh, _, dg = _rms_bwd(s['h0'], d_hn, dh, row(W['norm_mix'][i]), "rms_bwd", deps=token)
        G['norm_mix'][i] = dg[0]
        token = []
        if i == 1:
            small_a, token = start_small(spatial, "spatial", dh)
    grad_x = dh.reshape(x.shape)

    small_b, _ = start_small([n for n in REPLICATED if n not in spatial], "gains", dh)
    while flying:
        land_grads(dh)
    land_small(small_a, "spatial", dh)
    land_small(small_b, "gains", dh)

    out = lambda q: [res[n][q].reshape(W[n].shape) for n in WEIGHTS]
    return (loss, grad_x, *out(0), *out(1), *out(2), *out(3))
```

```python
import math

import numpy as np
import jax
import jax.numpy as jnp
from jax import lax
from jax.experimental import pallas as pl
from jax.experimental.pallas import tpu as pltpu

F32 = jnp.float32
BF16 = jnp.bfloat16

N_DEV = 8
MESH_AXES = ("x", "y", "c")
HEADS = 8
NOPE = 128
ROPE = 64
VDIM = 128
QK = NOPE + ROPE
Q_LORA = 384
KV_LORA = 256
ROPE_BASE = 10000.0
CHUNK = 128
GROUPS = 8
EPS = 1e-6
LR, B1, B2, ADAM_EPS, WD, STEP = 0.001, 0.9, 0.999, 1e-08, 0.01, 10
LANE = 128
VMEM_LIMIT = 56 * 1024 * 1024
MM_VMEM_BUDGET = 40 * 1024 * 1024

WEIGHTS = ['norm_mix', 'norm_ffn', 'norm_ple', 'mla_w_down', 'mla_q_lora_g', 'mla_kv_lora_g', 'mla_w_uq',
           'mla_w_ukv', 'mla_q_nope_g', 'mla_q_rope_g', 'mla_k_nope_g', 'mla_k_rope_g', 'mla_w_out', 'gmlp_w_in',
           'gmlp_ln_g', 'gmlp_ln_b', 'gmlp_w_s', 'gmlp_b_s', 'gmlp_w_out', 'ffn_w_up', 'ffn_w_down', 'ple_w_gate',
           'ple_w_proj']
SHARD_AXIS = {'mla_w_down': 1, 'mla_w_uq': 2, 'mla_w_ukv': 2, 'mla_w_out': 1, 'gmlp_w_in': 2, 'gmlp_ln_g': 1,
              'gmlp_ln_b': 1, 'gmlp_w_out': 1, 'ffn_w_up': 2, 'ffn_w_down': 1, 'ple_w_gate': 1, 'ple_w_proj': 2}
SHARDED = list(SHARD_AXIS)
REPLICATED = [n for n in WEIGHTS if n not in SHARD_AXIS]
F32_PAYLOAD = ('gmlp_ln_g', 'gmlp_ln_b')


def _pick(dim, pref, align=LANE):
    if dim <= pref:
        return dim
    b = (pref // align) * align
    while b >= align:
        if dim % b == 0:
            return b
        b -= align
    return dim


def _params(sem):
    return pltpu.CompilerParams(dimension_semantics=sem, vmem_limit_bytes=VMEM_LIMIT)


def _mm(a, b, *, ta=False, tb=False, extras=(), rows=(), epilogue=None, out_dtypes=(F32,), out_blocks=None,
        deps=(), name, bn=1024):
    a3, b3 = a.ndim == 3, b.ndim == 3
    assert not (ta and a3)
    if ta:
        K, M = a.shape
        ka = K
    elif a3:
        M, ka = a.shape[1:]
        K = a.shape[0] * ka
    else:
        M, K = a.shape
        ka = K
    if tb:
        N, kb = b.shape[-2:]
        nb = N
        K2 = b.shape[0] * kb if b3 else kb
    else:
        kb, nb = b.shape[-2:]
        K2 = kb
        N = b.shape[0] * nb if b3 else nb
    assert K == K2, (a.shape, b.shape, ta, tb)
    no_ = N // out_blocks if out_blocks else N
    assert not (out_blocks and extras)
    size = lambda t: jnp.dtype(t).itemsize
    per_out = sum(size(e.dtype) for e in extras) + sum(size(t) for t in out_dtypes)
    bn = _pick(min(nb, no_), bn)
    k_lim = min(ka, kb)
    fits = lambda m, k: 2 * (m * k * size(a.dtype) + k * bn * size(b.dtype) + m * bn * per_out) + 4 * m * bn
    ms = [m for m in sorted({min(M, c) for c in (2048, 1024, 512, 256)}, reverse=True) if M % m == 0]
    ks = [k for k in dict.fromkeys((k_lim, 2048, 1024, 512, 256)) if k <= k_lim and k_lim % k == 0]
    bm, bk = next(((m, k) for k in ks for m in ms if fits(m, k) <= MM_VMEM_BUDGET), (ms[-1], ks[-1]))
    nk = K // bk
    ne, no = len(extras) + len(rows), len(out_dtypes)
    first_out = 2 + ne + len(deps)
    dims = (((0,) if ta else (1,), (1,) if tb else (0,)), ((), ()))

    def finish(r, e_refs, o_refs):
        outs = epilogue(r, *[e[...] for e in e_refs]) if epilogue is not None else (r,)
        for o, v in zip(o_refs, outs):
            o[...] = v.astype(o.dtype)

    def body(*refs):
        a_ref, b_ref = refs[0], refs[1]
        e_refs = refs[2:2 + ne]
        o_refs = refs[first_out:first_out + no]
        part = lax.dot_general(a_ref[...].astype(BF16), b_ref[...].astype(BF16), dims, preferred_element_type=F32)
        if nk == 1:
            finish(part, e_refs, o_refs)
            return
        acc = refs[-1]
        k = pl.program_id(2)

        @pl.when(k == 0)
        def _():
            acc[...] = part

        @pl.when(k > 0)
        def _():
            acc[...] += part

        @pl.when(k == nk - 1)
        def _():
            finish(acc[...], e_refs, o_refs)

    ka_t, kb_t, nb_t, no_t = ka // bk, kb // bk, nb // bn, no_ // bn
    if ta:
        a_spec = pl.BlockSpec((bk, bm), lambda i, j, k: (k, i))
    elif a3:
        a_spec = pl.BlockSpec((None, bm, bk), lambda i, j, k: (k // ka_t, i, k % ka_t))
    else:
        a_spec = pl.BlockSpec((bm, bk), lambda i, j, k: (i, k))
    if tb:
        b_spec = (pl.BlockSpec((None, bn, bk), lambda i, j, k: (k // kb_t, j, k % kb_t)) if b3 else
                  pl.BlockSpec((bn, bk), lambda i, j, k: (j, k)))
    else:
        b_spec = (pl.BlockSpec((None, bk, bn), lambda i, j, k: (j // nb_t, k, j % nb_t)) if b3 else
                  pl.BlockSpec((bk, bn), lambda i, j, k: (k, j)))
    if out_blocks:
        o_spec = lambda: pl.BlockSpec((None, bm, bn), lambda i, j, k: (j // no_t, i, j % no_t))
        o_shape = (out_blocks, M, no_)
    else:
        o_spec = lambda: pl.BlockSpec((bm, bn), lambda i, j, k: (i, j))
        o_shape = (M, N)
    outs = pl.pallas_call(
        body, name=name,
        grid=(M // bm, N // bn, nk),
        in_specs=[a_spec, b_spec] + [pl.BlockSpec((bm, bn), lambda i, j, k: (i, j)) for _ in extras]
        + [pl.BlockSpec((1, bn), lambda i, j, k: (0, j)) for _ in rows]
        + [pl.BlockSpec(memory_space=pl.ANY)] * len(deps),
        out_specs=[o_spec() for _ in out_dtypes],
        out_shape=[jax.ShapeDtypeStruct(o_shape, dt) for dt in out_dtypes],
        scratch_shapes=[pltpu.VMEM((bm, bn), F32)] if nk > 1 else [],
        compiler_params=_params(("parallel", "parallel", "arbitrary")),
    )(a, b, *extras, *rows, *deps)
    return outs[0] if no == 1 else outs


def _rowcall(fn, rows, params, row_outs, acc_outs=(), *, bm=256, deps=(), name):
    T = rows[0].shape[0]
    bm = _pick(T, bm, 8)
    nr, npar, nro, nao = len(rows), len(params), len(row_outs), len(acc_outs)
    first_out = nr + npar + len(deps)

    def body(*refs):
        vals = [r[...] for r in refs[:nr + npar]]
        res = fn(*vals)
        ro = refs[first_out:first_out + nro]
        ao = refs[first_out + nro:]
        for r, v in zip(ro, res[:nro]):
            r[...] = v.astype(r.dtype)
        if nao:
            @pl.when(pl.program_id(0) == 0)
            def _():
                for r in ao:
                    r[...] = jnp.zeros_like(r)

            for r, v in zip(ao, res[nro:]):
                r[...] += v

    def whole(shape):
        nd = len(shape)
        return pl.BlockSpec(tuple(shape), lambda i: (0,) * nd)

    outs = pl.pallas_call(
        body, name=name,
        grid=(T // bm,),
        in_specs=[pl.BlockSpec((bm, r.shape[1]), lambda i: (i, 0)) for r in rows] + [whole(q.shape) for q in params]
        + [pl.BlockSpec(memory_space=pl.ANY)] * len(deps),
        out_specs=[pl.BlockSpec((bm, c), lambda i: (i, 0)) for c, _ in row_outs] + [whole(s) for s in acc_outs],
        out_shape=[jax.ShapeDtypeStruct((T, c), dt) for c, dt in row_outs]
        + [jax.ShapeDtypeStruct(tuple(s), F32) for s in acc_outs],
        compiler_params=_params(("arbitrary",) if nao else ("parallel",)),
    )(*rows, *params, *deps)
    return outs


def _rmsn(x, g):
    return x * lax.rsqrt(jnp.mean(x * x, axis=-1, keepdims=True) + EPS) * g


def _gelu(x):
    return 0.5 * x * (1.0 + jnp.tanh(math.sqrt(2.0 / math.pi) * (x + 0.044715 * (x * x * x))))


def _layer_norm(x, g, b):
    mu = jnp.mean(x, axis=-1, keepdims=True)
    xc = x - mu
    return xc * lax.rsqrt(jnp.mean(xc * xc, axis=-1, keepdims=True) + EPS) * g + b


def _sigmoid(x):
    return 1.0 / (1.0 + jnp.exp(-x))


def _rot(x, cos, sin, rmat):
    return x * cos + jnp.dot(x, rmat, precision=lax.Precision.HIGHEST, preferred_element_type=F32) * sin


def _rms_fwd(h, g, name, deps=()):
    return _rowcall(lambda x, gg: (_rmsn(x, gg),), [h], [g], [(h.shape[1], BF16)], bm=512, deps=deps, name=name)[0]


def _rms_bwd(h, d_hn, dh_in, g, name, deps=()):
    def fn(x, dy, dres, gg):
        _, vjp = jax.vjp(_rmsn, x, gg)
        dx, dg = vjp(dy.astype(F32))
        dh = dres + dx
        return dh, dh, dg

    d = h.shape[1]
    return _rowcall(fn, [h, d_hn, dh_in], [g], [(d, F32), (d, BF16)], [g.shape], bm=512, deps=deps, name=name)


def _rope_tables(pos, name):
    inv = np.float32(ROPE_BASE) ** (-(np.arange(0, ROPE, 2, dtype=np.float32) / np.float32(ROPE)))
    inv = jnp.asarray(np.concatenate([inv, inv])[None, :].astype(np.float32))

    def fn(pp, iv):
        ang = pp.astype(F32) * iv
        return jnp.cos(ang), jnp.sin(ang)

    return _rowcall(fn, [pos], [inv], [(ROPE, F32), (ROPE, F32)], name=name)


def _rot_matrix():
    r = np.zeros((ROPE, ROPE), np.float32)
    half = ROPE // 2
    for j in range(half):
        r[j + half, j] = -1.0
        r[j, j + half] = 1.0
    return jnp.asarray(r)


def _prep1_fwd(lat, gq, gkv, name):
    def fn(l, a, b):
        return _rmsn(l[:, :Q_LORA], a), _rmsn(l[:, Q_LORA:Q_LORA + KV_LORA], b)

    return _rowcall(fn, [lat], [gq, gkv], [(Q_LORA, BF16), (KV_LORA, BF16)], name=name)


def _prep1_bwd(lat, d_cq, d_ckv, d_kr, gq, gkv, name):
    def fn(l, dq, dkv, dkr, a, b):
        _, vq = jax.vjp(_rmsn, l[:, :Q_LORA], a)
        _, vkv = jax.vjp(_rmsn, l[:, Q_LORA:Q_LORA + KV_LORA], b)
        dxq, dga = vq(dq.astype(F32))
        dxkv, dgb = vkv(dkv.astype(F32))
        return jnp.concatenate([dxq, dxkv, dkr], axis=1), dga, dgb

    return _rowcall(fn, [lat, d_cq, d_ckv, d_kr], [gq, gkv], [(lat.shape[1], BF16)], [gq.shape, gkv.shape], name=name)


def _qk_fn(qn_raw, qr_raw, kn_raw, kr_raw, gqn, gqr, gkn, gkr, cos, sin, rmat):
    return (_rmsn(qn_raw, gqn), _rot(_rmsn(qr_raw, gqr), cos, sin, rmat),
            _rmsn(kn_raw, gkn), _rot(_rmsn(kr_raw, gkr), cos, sin, rmat))


def _prep2_fwd(q_raw, kv_raw, lat, cos, sin, rmat, gains, name, bm=1024):
    H, T, _ = q_raw.shape
    bm = _pick(T, bm, 8)
    kr0 = Q_LORA + KV_LORA

    def body(q_ref, kv_ref, lat_ref, cos_ref, sin_ref, r_ref, gqn, gqr, gkn, gkr, qo, ko, vo):
        qr, kvr = q_ref[...], kv_ref[...]
        qn, qro, kn, kro = _qk_fn(qr[:, :NOPE], qr[:, NOPE:], kvr[:, :NOPE], lat_ref[:, kr0:kr0 + ROPE],
                                  gqn[...], gqr[...], gkn[...], gkr[...], cos_ref[...], sin_ref[...], r_ref[...])
        qo[:, :NOPE] = qn.astype(BF16)
        qo[:, NOPE:] = qro.astype(BF16)
        ko[:, :NOPE] = kn.astype(BF16)
        ko[:, NOPE:] = kro.astype(BF16)
        vo[...] = kvr[:, NOPE:].astype(BF16)

    hb = lambda c: pl.BlockSpec((None, bm, c), lambda m, h: (h, m, 0))
    rb = lambda c: pl.BlockSpec((bm, c), lambda m, h: (m, 0))
    wb = lambda s: pl.BlockSpec(tuple(s), lambda m, h: (0, 0))
    return pl.pallas_call(
        body, name=name, grid=(T // bm, H),
        in_specs=[hb(QK), hb(NOPE + VDIM), rb(lat.shape[1]), rb(ROPE), rb(ROPE), wb(rmat.shape)]
        + [wb(g.shape) for g in gains],
        out_specs=[hb(QK), hb(QK), hb(VDIM)],
        out_shape=[jax.ShapeDtypeStruct((H, T, QK), BF16), jax.ShapeDtypeStruct((H, T, QK), BF16),
                   jax.ShapeDtypeStruct((H, T, VDIM), BF16)],
        compiler_params=_params(("parallel", "parallel")),
    )(q_raw, kv_raw, lat, cos, sin, rmat, *gains)


def _prep2_bwd(q_raw, kv_raw, lat, cos, sin, rmat, gains, dq, dk, dv, name, bm=512):
    H, T, _ = q_raw.shape
    bm = _pick(T, bm, 8)
    kr0 = Q_LORA + KV_LORA

    def body(q_ref, kv_ref, lat_ref, cos_ref, sin_ref, r_ref, gqn, gqr, gkn, gkr, dq_ref, dk_ref, dv_ref,
             dqo, dkvo, dkro, o_gqn, o_gqr, o_gkn, o_gkr):
        m, h = pl.program_id(0), pl.program_id(1)
        qr, kvr = q_ref[...], kv_ref[...]
        cos_v, sin_v, r_v = cos_ref[...], sin_ref[...], r_ref[...]
        f = lambda a, b, c, d, g1, g2, g3, g4: _qk_fn(a, b, c, d, g1, g2, g3, g4, cos_v, sin_v, r_v)
        _, vjp = jax.vjp(f, qr[:, :NOPE], qr[:, NOPE:], kvr[:, :NOPE], lat_ref[:, kr0:kr0 + ROPE],
                         gqn[...], gqr[...], gkn[...], gkr[...])
        dqv, dkv_ = dq_ref[...], dk_ref[...]
        d_qn, d_qr, d_kn, d_kr, g1, g2, g3, g4 = vjp((dqv[:, :NOPE], dqv[:, NOPE:], dkv_[:, :NOPE], dkv_[:, NOPE:]))
        dqo[:, :NOPE] = d_qn.astype(BF16)
        dqo[:, NOPE:] = d_qr.astype(BF16)
        dkvo[:, :NOPE] = d_kn.astype(BF16)
        dkvo[:, NOPE:] = dv_ref[...].astype(BF16)

        @pl.when(h == 0)
        def _():
            dkro[...] = jnp.zeros_like(dkro)

        dkro[...] += d_kr

        @pl.when((h == 0) & (m == 0))
        def _():
            for o in (o_gqn, o_gqr, o_gkn, o_gkr):
                o[...] = jnp.zeros_like(o)

        for o, g in zip((o_gqn, o_gqr, o_gkn, o_gkr), (g1, g2, g3, g4)):
            o[...] += g

    hb = lambda c: pl.BlockSpec((None, bm, c), lambda m, h: (h, m, 0))
    rb = lambda c: pl.BlockSpec((bm, c), lambda m, h: (m, 0))
    wb = lambda s: pl.BlockSpec(tuple(s), lambda m, h: (0, 0))
    return pl.pallas_call(
        body, name=name, grid=(T // bm, H),
        in_specs=[hb(QK), hb(NOPE + VDIM), rb(lat.shape[1]), rb(ROPE), rb(ROPE), wb(rmat.shape)]
        + [wb(g.shape) for g in gains] + [hb(QK), hb(QK), hb(VDIM)],
        out_specs=[hb(QK), hb(NOPE + VDIM), rb(ROPE)] + [wb(g.shape) for g in gains],
        out_shape=[jax.ShapeDtypeStruct((H, T, QK), BF16), jax.ShapeDtypeStruct((H, T, NOPE + VDIM), BF16),
                   jax.ShapeDtypeStruct((T, ROPE), F32)] + [jax.ShapeDtypeStruct(g.shape, F32) for g in gains],
        compiler_params=_params(("arbitrary", "arbitrary")),
    )(q_raw, kv_raw, lat, cos, sin, rmat, *gains, dq, dk, dv)


_NT = (((1,), (1,)), ((), ()))
_TN = (((0,), (0,)), ((), ()))


def _causal(blk):
    return lax.broadcasted_iota(jnp.int32, (blk, blk), 1) <= lax.broadcasted_iota(jnp.int32, (blk, blk), 0)


def _attn_fwd(q, k, v, seq, name, blk=512):
    H, T, _ = q.shape
    nb = T // seq
    blk = _pick(seq, blk)
    nq = seq // blk
    scale = float(QK) ** -0.5

    def body(q_ref, k_ref, v_ref, o_ref, lse_ref):
        qi = pl.program_id(2)
        qb = q_ref[...]

        def step(j, carry, diagonal):
            m, l, acc = carry
            ks = pl.ds(pl.multiple_of(j * blk, blk), blk)
            s = lax.dot_general(qb, k_ref[ks, :], _NT, preferred_element_type=F32) * scale
            if diagonal:
                s = jnp.where(_causal(blk), s, -jnp.inf)
            m_new = jnp.maximum(m, jnp.max(s, axis=1, keepdims=True))
            pr = jnp.exp(s - m_new)
            alpha = jnp.exp(m - m_new)
            l = alpha * l + jnp.sum(pr, axis=1, keepdims=True)
            acc = alpha * acc + jnp.dot(pr.astype(BF16), v_ref[ks, :], preferred_element_type=F32)
            return m_new, l, acc

        init = (jnp.full((blk, 1), -jnp.inf, F32), jnp.zeros((blk, 1), F32), jnp.zeros((blk, VDIM), F32))
        below = lax.fori_loop(0, qi, lambda j, c: step(j, c, False), init)
        m, l, acc = step(qi, below, True)
        o_ref[...] = (acc / l).astype(o_ref.dtype)
        lse_ref[...] = m + jnp.log(l)

    return pl.pallas_call(
        body, name=name, grid=(H, nb, nq),
        in_specs=[pl.BlockSpec((None, blk, QK), lambda h, b, i: (h, b * nq + i, 0)),
                  pl.BlockSpec((None, seq, QK), lambda h, b, i: (h, b, 0)),
                  pl.BlockSpec((None, seq, VDIM), lambda h, b, i: (h, b, 0))],
        out_specs=[pl.BlockSpec((blk, VDIM), lambda h, b, i: (b * nq + i, h)),
                   pl.BlockSpec((None, blk, 1), lambda h, b, i: (h, b * nq + i, 0))],
        out_shape=[jax.ShapeDtypeStruct((T, H * VDIM), BF16), jax.ShapeDtypeStruct((H, T, 1), F32)],
        compiler_params=_params(("parallel", "parallel", "parallel")),
    )(q, k, v)


def _attn_bwd(q, k, v, o, do, lse, seq, name, blk=512):
    H, T, _ = q.shape
    nb = T // seq
    blk = _pick(seq, blk)
    nq = seq // blk
    scale = float(QK) ** -0.5

    def body(q_ref, k_ref, v_ref, o_ref, do_ref, lse_ref, dq_ref, dk_ref, dv_ref):
        dk_ref[...] = jnp.zeros_like(dk_ref)
        dv_ref[...] = jnp.zeros_like(dv_ref)

        def qloop(i, carry):
            qs = pl.ds(pl.multiple_of(i * blk, blk), blk)
            qb = q_ref[qs, :]
            dob = do_ref[qs, :]
            dof = dob.astype(F32)
            lse_b = lse_ref[qs, :]
            delta = jnp.sum(dof * o_ref[qs, :].astype(F32), axis=1, keepdims=True)

            def kstep(j, dq_acc, diagonal):
                ks = pl.ds(pl.multiple_of(j * blk, blk), blk)
                kb = k_ref[ks, :]
                vb = v_ref[ks, :]
                s = lax.dot_general(qb, kb, _NT, preferred_element_type=F32) * scale
                pr = jnp.exp(s - lse_b)
                if diagonal:
                    pr = jnp.where(_causal(blk), pr, 0.0)
                dp = lax.dot_general(dob, vb, _NT, preferred_element_type=F32)
                ds = (pr * (dp - delta) * scale).astype(BF16)
                prb = pr.astype(BF16)
                dv_ref[ks, :] += lax.dot_general(prb, dob, _TN, preferred_element_type=F32)
                dk_ref[ks, :] += lax.dot_general(ds, qb, _TN, preferred_element_type=F32)
                return dq_acc + jnp.dot(ds, kb, preferred_element_type=F32)

            below = lax.fori_loop(0, i, lambda j, c: kstep(j, c, False), jnp.zeros((blk, QK), F32))
            dq_ref[qs, :] = kstep(i, below, True)
            return carry

        lax.fori_loop(0, nq, qloop, 0)

    hb = lambda c: pl.BlockSpec((None, seq, c), lambda h, b: (h, b, 0))
    cb = lambda: pl.BlockSpec((seq, VDIM), lambda h, b: (b, h))
    return pl.pallas_call(
        body, name=name, grid=(H, nb),
        in_specs=[hb(QK), hb(QK), hb(VDIM), cb(), cb(), hb(1)],
        out_specs=[hb(QK), hb(QK), hb(VDIM)],
        out_shape=[jax.ShapeDtypeStruct((H, T, QK), F32), jax.ShapeDtypeStruct((H, T, QK), F32),
                   jax.ShapeDtypeStruct((H, T, VDIM), F32)],
        compiler_params=_params(("parallel", "parallel")),
    )(q, k, v, o, do, lse)


def _gelu_ln_fwd(z, g, b, name):
    half = z.shape[1] // 2

    def fn(zz, gg, bb):
        return _gelu(zz[:, :half].astype(F32)), _layer_norm(_gelu(zz[:, half:].astype(F32)), gg, bb)

    return _rowcall(fn, [z], [g, b], [(half, BF16), (half, BF16)], name=name)


def _gelu_ln_bwd(z, d_u, d_vn, g, b, name):
    half = z.shape[1] // 2

    def gelu_and_slope(x):
        c, a = math.sqrt(2.0 / math.pi), 0.044715
        x2 = x * x
        t = jnp.tanh(c * x * (1.0 + a * x2))
        return 0.5 * x * (1.0 + t), 0.5 * (1.0 + t) + 0.5 * x * (1.0 - t * t) * (c * (1.0 + 3.0 * a * x2))

    def fn(zz, du, dvn, gg, bb):
        du, dvn = du.astype(F32), dvn.astype(F32)
        _, su = gelu_and_slope(zz[:, :half].astype(F32))
        v, sv = gelu_and_slope(zz[:, half:].astype(F32))
        xc = v - jnp.mean(v, axis=-1, keepdims=True)
        rstd = lax.rsqrt(jnp.mean(xc * xc, axis=-1, keepdims=True) + EPS)
        y = xc * rstd
        dy = dvn * gg
        dv = rstd * (dy - jnp.mean(dy, axis=-1, keepdims=True) - y * jnp.mean(dy * y, axis=-1, keepdims=True))
        dg = jnp.sum(dvn * y, axis=0, keepdims=True)
        db = jnp.sum(dvn, axis=0, keepdims=True)
        return jnp.concatenate([du * su, dv * sv], axis=1), dg, db

    return _rowcall(fn, [z, d_u, d_vn], [g, b], [(z.shape[1], BF16)], [g.shape, b.shape], bm=128, name=name)


def _tril_bf16(ws):
    t = lax.broadcasted_iota(jnp.int32, ws.shape, 0)
    s = lax.broadcasted_iota(jnp.int32, ws.shape, 1)
    return jnp.where(s <= t, ws, 0.0).astype(BF16)


def _sgu_fwd(u, vn, ws, bs, name, bm=2048):
    T, half = u.shape
    gd = half // GROUPS
    bm = _pick(T, bm, CHUNK)
    nc = bm // CHUNK

    def body(u_ref, vn_ref, ws_ref, bs_ref, y_ref):
        wm = _tril_bf16(ws_ref[...])
        bias = bs_ref[...]
        for c in range(nc):
            rs = slice(c * CHUNK, (c + 1) * CHUNK)
            sv = jnp.dot(wm, vn_ref[rs, :], preferred_element_type=F32) + bias
            y_ref[rs, :] = (u_ref[rs, :].astype(F32) * sv).astype(y_ref.dtype)

    tb = lambda: pl.BlockSpec((bm, gd), lambda g, i: (i, g))
    return pl.pallas_call(
        body, name=name, grid=(GROUPS, T // bm),
        in_specs=[tb(), tb(), pl.BlockSpec((None, CHUNK, CHUNK), lambda g, i: (g, 0, 0)),
                  pl.BlockSpec((None, CHUNK, 1), lambda g, i: (g, 0, 0))],
        out_specs=tb(),
        out_shape=jax.ShapeDtypeStruct((T, half), BF16),
        compiler_params=_params(("parallel", "parallel")),
    )(u, vn, ws, bs)


def _sgu_bwd(u, vn, dy, ws, bs, name, bm=1024):
    T, half = u.shape
    gd = half // GROUPS
    bm = _pick(T, bm, CHUNK)
    nc = bm // CHUNK

    def body(u_ref, vn_ref, dy_ref, ws_ref, bs_ref, du_ref, dvn_ref, dws_ref, dbs_ref):
        @pl.when(pl.program_id(1) == 0)
        def _():
            dws_ref[...] = jnp.zeros_like(dws_ref)
            dbs_ref[...] = jnp.zeros_like(dbs_ref)

        wm = _tril_bf16(ws_ref[...])
        bias = bs_ref[...]
        dws = jnp.zeros((CHUNK, CHUNK), F32)
        dbs = jnp.zeros((CHUNK, 1), F32)
        for c in range(nc):
            rs = slice(c * CHUNK, (c + 1) * CHUNK)
            vb = vn_ref[rs, :]
            dyb = dy_ref[rs, :].astype(F32)
            sv = jnp.dot(wm, vb, preferred_element_type=F32) + bias
            du_ref[rs, :] = (dyb * sv).astype(du_ref.dtype)
            dsv = dyb * u_ref[rs, :].astype(F32)
            dsb = dsv.astype(BF16)
            dvn_ref[rs, :] = lax.dot_general(wm, dsb, _TN, preferred_element_type=F32).astype(dvn_ref.dtype)
            dws = dws + lax.dot_general(dsb, vb, _NT, preferred_element_type=F32)
            dbs = dbs + jnp.sum(dsv, axis=1, keepdims=True)
        t = lax.broadcasted_iota(jnp.int32, (CHUNK, CHUNK), 0)
        s = lax.broadcasted_iota(jnp.int32, (CHUNK, CHUNK), 1)
        dws_ref[...] += jnp.where(s <= t, dws, 0.0)
        dbs_ref[...] += dbs

    tb = lambda: pl.BlockSpec((bm, gd), lambda g, i: (i, g))
    wsb = lambda: pl.BlockSpec((None, CHUNK, CHUNK), lambda g, i: (g, 0, 0))
    bsb = lambda: pl.BlockSpec((None, CHUNK, 1), lambda g, i: (g, 0, 0))
    return pl.pallas_call(
        body, name=name, grid=(GROUPS, T // bm),
        in_specs=[tb(), tb(), tb(), wsb(), bsb()],
        out_specs=[tb(), tb(), wsb(), bsb()],
        out_shape=[jax.ShapeDtypeStruct((T, half), BF16), jax.ShapeDtypeStruct((T, half), BF16),
                   jax.ShapeDtypeStruct(ws.shape, F32), jax.ShapeDtypeStruct(bs.shape, F32)],
        compiler_params=_params(("parallel", "arbitrary")),
    )(u, vn, dy, ws, bs)


def _loss_head(y, t, name):
    d_model = y.shape[1]

    def fn(yy, tt):
        d = yy - tt
        part = 0.5 * jnp.sum(jnp.mean(d * d, axis=-1, keepdims=True), axis=0, keepdims=True)
        return d / d_model, jnp.zeros((1, LANE), F32) + part

    dy, part = _rowcall(fn, [y, t], [], [(d_model, F32)], [(1, LANE)], name=name)
    return dy, part[0, 0]


def _adamw(parts, w, m, v, prev, layer, name):
    L, R, C = w.shape
    br = _pick(R, max(8, (128 * 1024) // C // 8 * 8), 8)
    c1 = 1.0 - B1 ** STEP
    c2 = 1.0 - B2 ** STEP
    if prev is None:
        prev = [lax.empty(w.shape, F32) for _ in range(4)]

    def body(p_ref, w_ref, m_ref, v_ref, a0, a1, a2, a3, g_o, d_o, m_o, v_o, token):
        g = p_ref[0].astype(F32)
        for d in range(1, N_DEV):
            g = g + p_ref[d].astype(F32)
        mn = B1 * m_ref[...] + (1.0 - B1) * g
        vn = B2 * v_ref[...] + (1.0 - B2) * (g * g)
        g_o[...] = g
        m_o[...] = mn
        v_o[...] = vn
        d_o[...] = -LR * ((mn / c1) / (jnp.sqrt(vn / c2) + ADAM_EPS) + WD * w_ref[...])
        token[...] = jnp.zeros_like(token)

    blk = lambda: pl.BlockSpec((None, br, C), lambda i: (layer, i, 0))
    anywhere = pl.BlockSpec(memory_space=pl.ANY)
    outs = pl.pallas_call(
        body, name=name, grid=(R // br,),
        in_specs=[pl.BlockSpec((N_DEV, br, C), lambda i: (0, i, 0)), blk(), blk(), blk()] + [anywhere] * 4,
        out_specs=[blk(), blk(), blk(), blk(), pl.BlockSpec((8, LANE), lambda i: (0, 0))],
        out_shape=[jax.ShapeDtypeStruct((L, R, C), F32)] * 4 + [jax.ShapeDtypeStruct((8, LANE), F32)],
        input_output_aliases={4: 0, 5: 1, 6: 2, 7: 3},
        compiler_params=_params(("arbitrary",)),
    )(parts, w, m, v, *prev)
    return list(outs[:4]), outs[4]


def _mesh_pos():
    return lax.axis_index("x"), lax.axis_index("y"), lax.axis_index("c")


def _flip(pos, k):
    x, y, c = pos
    px = 1 - x if k & 4 else x
    py = 1 - y if k & 2 else y
    pc = 1 - c if k & 1 else c
    return px, py, pc


HBM_SPEC = pl.BlockSpec(memory_space=pltpu.HBM)
SEM_SPEC = pl.BlockSpec(memory_space=pltpu.SEMAPHORE)
EFFECT = pltpu.SideEffectType.DATAFLOW_SIDE_EFFECTING


def _hbm(a):
    return pltpu.with_memory_space_constraint(a, pltpu.HBM)


def _device_index():
    x, y, c = _mesh_pos()
    return 4 * x + 2 * y + c


def _peer_copy(src, land, send, recv, a, k, pos, scatter):
    peer = _flip(pos, k)
    me = 4 * pos[0] + 2 * pos[1] + pos[2]
    piece = src.at[4 * peer[0] + 2 * peer[1] + peer[2]] if scatter else src
    return pltpu.make_async_remote_copy(
        src_ref=piece, dst_ref=land.at[me], send_sem=send.at[7 * a + k - 1], recv_sem=recv.at[7 * a + k - 1],
        device_id=peer, device_id_type=pl.DeviceIdType.MESH)


def _xchg_start(srcs, scatter, after, name):
    n = len(srcs)

    def body(*refs):
        src, land = refs[:n], refs[n:2 * n]
        send, recv, token = refs[2 * n + 1], refs[2 * n + 2], refs[-1]
        pos = _mesh_pos()
        for k in range(1, N_DEV):
            for a in range(n):
                _peer_copy(src[a], land[a], send, recv, a, k, pos, scatter).start()
        token[...] = jnp.zeros_like(token)

    lands = [lax.empty(s.shape if scatter else (N_DEV,) + s.shape, s.dtype) for s in srcs]
    outs = pl.pallas_call(
        body, name=name,
        out_shape=(pltpu.SemaphoreType.DMA((7 * n,)), pltpu.SemaphoreType.DMA((7 * n,)),
                   *[pltpu.HBM(s.shape, s.dtype) for s in srcs], *[pltpu.HBM(l.shape, l.dtype) for l in lands],
                   jax.ShapeDtypeStruct((8, LANE), F32)),
        in_specs=[HBM_SPEC] * (2 * n) + [pl.BlockSpec(memory_space=pl.ANY)],
        out_specs=(SEM_SPEC, SEM_SPEC, *[HBM_SPEC] * (2 * n), pl.BlockSpec(memory_space=pltpu.VMEM)),
        input_output_aliases={q: 2 + q for q in range(2 * n)},
        compiler_params=pltpu.CompilerParams(has_side_effects=EFFECT),
    )(*[_hbm(s) for s in srcs], *[_hbm(l) for l in lands], after)
    handle = dict(send=outs[0], recv=outs[1], srcs=list(outs[2:2 + n]), lands=list(outs[2 + n:2 + 2 * n]),
                  scatter=scatter)
    return handle, outs[-1]


def _xchg_wait(handle, after, name):
    srcs, lands, scatter = handle['srcs'], handle['lands'], handle['scatter']
    n = len(srcs)

    def body(*refs):
        src, land = refs[:n], refs[n:2 * n]
        send, recv = refs[2 * n], refs[2 * n + 1]
        pos = _mesh_pos()
        for k in range(1, N_DEV):
            for a in range(n):
                cp = _peer_copy(src[a], land[a], send, recv, a, k, pos, scatter)
                cp.wait_send()
                cp.wait_recv()

    outs = pl.pallas_call(
        body, name=name,
        out_shape=[pltpu.HBM(s.shape, s.dtype) for s in srcs] + [pltpu.HBM(l.shape, l.dtype) for l in lands],
        in_specs=[HBM_SPEC] * (2 * n) + [SEM_SPEC, SEM_SPEC, pl.BlockSpec(memory_space=pl.ANY)],
        out_specs=[HBM_SPEC] * (2 * n),
        input_output_aliases={q: q for q in range(2 * n)},
        compiler_params=pltpu.CompilerParams(has_side_effects=EFFECT),
    )(*srcs, *lands, handle['send'], handle['recv'], after)
    me = _device_index()
    full = []
    for src, land in zip(outs[:n], outs[n:]):
        mine = lax.dynamic_index_in_dim(src, me, 0, keepdims=False) if scatter else src
        full.append(lax.dynamic_update_index_in_dim(land, mine, me, 0))
    return full


def _pack(parts):
    flat = jnp.concatenate([q.reshape(-1) for q in parts])
    pad = (-flat.shape[0]) % (8 * LANE)
    return jnp.pad(flat, (0, pad)).reshape(-1, LANE)


def _unpack(packed, like):
    flat = packed.reshape(-1)
    out, o = [], 0
    for q in like:
        out.append(flat[o:o + q.size].reshape(q.shape))
        o += q.size
    return out


def kernel(x, p, positions, norm_mix, norm_ffn, norm_ple, mla_w_down, mla_q_lora_g, mla_kv_lora_g, mla_w_uq, mla_w_ukv, mla_q_nope_g, mla_q_rope_g, mla_k_nope_g, mla_k_rope_g, mla_w_out, gmlp_w_in, gmlp_ln_g, gmlp_ln_b, gmlp_w_s, gmlp_b_s, gmlp_w_out, ffn_w_up, ffn_w_down, ple_w_gate, ple_w_proj, loss_target, m_norm_mix, m_norm_ffn, m_norm_ple, m_mla_w_down, m_mla_q_lora_g, m_mla_kv_lora_g, m_mla_w_uq, m_mla_w_ukv, m_mla_q_nope_g, m_mla_q_rope_g, m_mla_k_nope_g, m_mla_k_rope_g, m_mla_w_out, m_gmlp_w_in, m_gmlp_ln_g, m_gmlp_ln_b, m_gmlp_w_s, m_gmlp_b_s, m_gmlp_w_out, m_ffn_w_up, m_ffn_w_down, m_ple_w_gate, m_ple_w_proj, v_norm_mix, v_norm_ffn, v_norm_ple, v_mla_w_down, v_mla_q_lora_g, v_mla_kv_lora_g, v_mla_w_uq, v_mla_w_ukv, v_mla_q_nope_g, v_mla_q_rope_g, v_mla_k_nope_g, v_mla_k_rope_g, v_mla_w_out, v_gmlp_w_in, v_gmlp_ln_g, v_gmlp_ln_b, v_gmlp_w_s, v_gmlp_b_s, v_gmlp_w_out, v_ffn_w_up, v_ffn_w_down, v_ple_w_gate, v_ple_w_proj):
    W = dict(zip(WEIGHTS, (norm_mix, norm_ffn, norm_ple, mla_w_down, mla_q_lora_g, mla_kv_lora_g, mla_w_uq, mla_w_ukv, mla_q_nope_g, mla_q_rope_g, mla_k_nope_g, mla_k_rope_g, mla_w_out, gmlp_w_in, gmlp_ln_g, gmlp_ln_b, gmlp_w_s, gmlp_b_s, gmlp_w_out, ffn_w_up, ffn_w_down, ple_w_gate, ple_w_proj)))
    M1 = dict(zip(WEIGHTS, (m_norm_mix, m_norm_ffn, m_norm_ple, m_mla_w_down, m_mla_q_lora_g, m_mla_kv_lora_g, m_mla_w_uq, m_mla_w_ukv, m_mla_q_nope_g, m_mla_q_rope_g, m_mla_k_nope_g, m_mla_k_rope_g, m_mla_w_out, m_gmlp_w_in, m_gmlp_ln_g, m_gmlp_ln_b, m_gmlp_w_s, m_gmlp_b_s, m_gmlp_w_out, m_ffn_w_up, m_ffn_w_down, m_ple_w_gate, m_ple_w_proj)))
    M2 = dict(zip(WEIGHTS, (v_norm_mix, v_norm_ffn, v_norm_ple, v_mla_w_down, v_mla_q_lora_g, v_mla_kv_lora_g, v_mla_w_uq, v_mla_w_ukv, v_mla_q_nope_g, v_mla_q_rope_g, v_mla_k_nope_g, v_mla_k_rope_g, v_mla_w_out, v_gmlp_w_in, v_gmlp_ln_g, v_gmlp_ln_b, v_gmlp_w_s, v_gmlp_b_s, v_gmlp_w_out, v_ffn_w_up, v_ffn_w_down, v_ple_w_gate, v_ple_w_proj)))

    nb, seq, d_model = x.shape
    assert d_model <= 1024, "the rms norms fused into matmul epilogues need whole rows in one output tile"
    T = nb * seq
    depth = norm_mix.shape[0]
    h = x.reshape(T, d_model)
    target = loss_target.reshape(T, d_model)
    p_bf = p.reshape(depth, T, p.shape[-1]).astype(BF16)

    def stage_keys(st):
        i, second = divmod(st, 2)
        late_gate = lambda l: l % 2 == 1 and l + 1 < depth
        gate = lambda l: [(n, l) for n in ('ple_w_gate', 'ple_w_proj')]
        if second:
            return [(n, i) for n in ('ffn_w_up', 'ffn_w_down')] + ([] if late_gate(i) else gate(i))
        mix = (['mla_w_down', 'mla_w_uq', 'mla_w_ukv', 'mla_w_out'] if i % 2 == 0 else
               ['gmlp_w_in', 'gmlp_ln_g', 'gmlp_ln_b', 'gmlp_w_out'])
        return (gate(i - 1) if i > 0 and late_gate(i - 1) else []) + [(n, i // 2) for n in mix]

    FW = {n: {} for n in SHARDED}

    def start_weights(st, after):
        keys = stage_keys(st)
        srcs = [W[n][l] if n in F32_PAYLOAD else W[n][l].astype(BF16) for n, l in keys]
        handle, token = _xchg_start(srcs, False, after, "weights_start%d" % st)
        return (keys, handle), [token]

    def wait_weights(pending, st, after):
        keys, handle = pending
        landed = _xchg_wait(handle, after, "weights_wait%d" % st)
        for (n, l), full in zip(keys, landed):
            if SHARD_AXIS[n] == 1:
                FW[n][l] = full.reshape((-1,) + full.shape[2:])
            elif n in ('mla_w_uq', 'mla_w_ukv'):
                FW[n][l] = full
            else:
                FW[n][l] = jnp.transpose(full, (1, 0, 2)).reshape(full.shape[1], -1)
        return landed[0]

    row = lambda a: a.reshape(1, -1)
    cos, sin = _rope_tables(positions.reshape(T, 1), "rope_tables")
    rmat = _rot_matrix()

    def add_and_norm(acc, res, g):
        hh = res + acc
        return hh, _rmsn(hh, g)

    saved = []
    chain = {'stage': 0}
    chain['pending'], _ = start_weights(0, h)

    def advance(after):
        st = chain['stage']
        if st >= 2 * depth:
            return []
        landed = wait_weights(chain['pending'], st, after)
        chain['stage'] = st + 1
        if st + 1 >= 2 * depth:
            return []
        chain['pending'], token = start_weights(st + 1, landed)
        return token

    hn = _rms_fwd(h, row(W['norm_mix'][0]), "rms_fwd")
    token = advance(hn)
    for i in range(depth):
        j = i // 2
        s = {}
        s['h0'] = h
        s['hn'] = hn
        if i % 2 == 0:
            gains = [row(W['mla_q_nope_g'][j]), row(W['mla_q_rope_g'][j]), row(W['mla_k_nope_g'][j]),
                     row(W['mla_k_rope_g'][j])]
            lat = _mm(hn, FW['mla_w_down'][j], deps=token, name="mla_down")
            cq, ckv = _prep1_fwd(lat, row(W['mla_q_lora_g'][j]), row(W['mla_kv_lora_g'][j]), "mla_prep1")
            q_raw = _mm(cq, FW['mla_w_uq'][j], out_blocks=HEADS, name="mla_uq")
            kv_raw = _mm(ckv, FW['mla_w_ukv'][j], out_blocks=HEADS, name="mla_ukv")
            q, k, v = _prep2_fwd(q_raw, kv_raw, lat, cos, sin, rmat, gains, "mla_prep2")
            o, lse = _attn_fwd(q, k, v, seq, "attn_fwd")
            token = advance(o)
            h, hn2 = _mm(o, FW['mla_w_out'][j], extras=(h,), rows=(row(W['norm_ffn'][i]),), epilogue=add_and_norm,
                         out_dtypes=(F32, BF16), deps=token, name="mla_out")
            s.update(lat=lat, cq=cq, ckv=ckv, q_raw=q_raw, kv_raw=kv_raw, q=q, k=k, v=v, o=o, lse=lse, gains=gains)
        else:
            z = _mm(hn, FW['gmlp_w_in'][j], out_dtypes=(BF16,), name="gmlp_in")
            u, vn = _gelu_ln_fwd(z, row(FW['gmlp_ln_g'][j]), row(FW['gmlp_ln_b'][j]), "gmlp_gelu_ln")
            bs3 = W['gmlp_b_s'][j][:, :, None]
            y = _sgu_fwd(u, vn, W['gmlp_w_s'][j], bs3, "gmlp_sgu")
            h, hn2 = _mm(y, FW['gmlp_w_out'][j], extras=(h,), rows=(row(W['norm_ffn'][i]),), epilogue=add_and_norm,
                         out_dtypes=(F32, BF16), name="gmlp_out")
            token = advance(hn2)
            s.update(z=z, u=u, vn=vn, y=y, bs3=bs3)
        s['h1'] = h
        a, r = _mm(hn2, FW['ffn_w_up'][i], epilogue=lambda acc: (acc, jnp.square(jnp.maximum(acc, 0.0))),
                   out_dtypes=(BF16, BF16), deps=token, name="ffn_up")
        h, hn3 = _mm(r, FW['ffn_w_down'][i], extras=(h,), rows=(row(W['norm_ple'][i]),), epilogue=add_and_norm,
                     out_dtypes=(F32, BF16), name="ffn_down")
        s.update(hn2=hn2, a=a, r=r, h2=h)
        if i % 2 == 1:
            token = advance(hn3)
        gt = _mm(hn3, FW['ple_w_gate'][i], deps=token, name="ple_gate")
        if i % 2 == 0:
            token = advance(gt)
        if i + 1 < depth:
            def gate_and_norm(acc, g_, res, gain):
                hh = res + _sigmoid(g_) * acc
                return acc, hh, _rmsn(hh, gain)

            pp, h, hn = _mm(p_bf[i], FW['ple_w_proj'][i], extras=(gt, h), rows=(row(W['norm_mix'][i + 1]),),
                            epilogue=gate_and_norm, out_dtypes=(F32, F32, BF16), deps=token, name="ple_proj")
        else:
            pp, h = _mm(p_bf[i], FW['ple_w_proj'][i], extras=(gt, h),
                        epilogue=lambda acc, g_, res: (acc, res + _sigmoid(g_) * acc), out_dtypes=(F32, F32),
                        name="ple_proj_last")
        s.update(hn3=hn3, gt=gt, pp=pp)
        saved.append(s)

    dh, loss_part = _loss_head(h, target, "loss_head")
    loss = lax.psum(loss_part, MESH_AXES)

    G = {n: [None] * W[n].shape[0] for n in REPLICATED}
    res = {}
    flying = []

    def shard3(n):
        shp = W[n].shape
        return shp[0], int(np.prod(shp[1:-1])), shp[-1]

    def by_owner(g):
        return g.reshape((N_DEV, g.shape[0] // N_DEV) + g.shape[1:])

    def send_grads(tag, grads):
        handle, token = _xchg_start([g for _, g in grads], True, cos, "grads_start_" + tag)
        flying.append((tag, [key for key, _ in grads], handle))
        return [token]

    def land_grads(after):
        tag, keys, handle = flying.pop(0)
        done = []
        for (n, l), full in zip(keys, _xchg_wait(handle, after, "grads_wait_" + tag)):
            dims = shard3(n)
            res[n], token = _adamw(full.reshape((N_DEV,) + dims[1:]), W[n].reshape(dims), M1[n].reshape(dims),
                                   M2[n].reshape(dims), res.get(n), l, "adamw_" + n)
            done.append(token)
        return done

    def start_small(names, tag, after):
        handle, token = _xchg_start([_pack([jnp.stack(G[n]) for n in names])], False, after, "small_start_" + tag)
        return (names, handle), [token]

    def land_small(pending_small, tag, after):
        names, handle = pending_small
        (parts,) = _xchg_wait(handle, after, "small_wait_" + tag)
        like = [W[n] for n in names]
        outs, _ = _adamw(parts, _pack(like)[None], _pack([M1[n] for n in names])[None],
                         _pack([M2[n] for n in names])[None], None, 0, "adamw_small_" + tag)
        unpacked = [_unpack(o, like) for o in outs]
        for idx, n in enumerate(names):
            res[n] = [unpacked[q][idx] for q in range(4)]

    spatial = ['gmlp_w_s', 'gmlp_b_s']
    token = []
    for i in reversed(range(depth)):
        j = i // 2
        s = saved[i]
        def ple_elem(d, g_, pq):
            sg = _sigmoid(g_)
            return d * sg, d * pq * sg * (1.0 - sg)

        d_pp, d_gt = _rowcall(ple_elem, [dh, s['gt'], s['pp']], [], [(d_model, BF16), (d_model, BF16)], name="ple_bwd")
        g_proj = _mm(p_bf[i], d_pp, ta=True, out_dtypes=(BF16,), name="ple_proj_dw")
        g_proj = jnp.transpose(g_proj.reshape(g_proj.shape[0], N_DEV, -1), (1, 0, 2))
        g_gate = _mm(s['hn3'], d_gt, ta=True, out_dtypes=(BF16,), name="ple_gate_dw")
        d_hn3 = _mm(d_gt, FW['ple_w_gate'][i], tb=True, out_dtypes=(BF16,), name="ple_gate_dx")
        dh, dh_bf, dg = _rms_bwd(s['h2'], d_hn3, dh, row(W['norm_ple'][i]), "rms_bwd", deps=token)
        G['norm_ple'][i] = dg[0]
        d_a = _mm(dh_bf, FW['ffn_w_down'][i], tb=True, extras=(s['a'],),
                  epilogue=lambda acc, a_: (acc * (2.0 * jnp.maximum(a_.astype(F32), 0.0)),), out_dtypes=(BF16,),
                  name="ffn_down_dx")
        g_down = _mm(s['r'], dh_bf, ta=True, out_dtypes=(BF16,), name="ffn_down_dw")
        g_up = _mm(s['hn2'], d_a, ta=True, out_blocks=N_DEV, out_dtypes=(BF16,), name="ffn_up_dw")
        token = send_grads("mlp%d" % i, [(('ple_w_proj', i), g_proj), (('ple_w_gate', i), by_owner(g_gate)),
                                         (('ffn_w_down', i), by_owner(g_down)), (('ffn_w_up', i), g_up)])
        if len(flying) > 1:
            token = token + land_grads(token[0])
        d_hn2 = _mm(d_a, FW['ffn_w_up'][i], tb=True, out_dtypes=(BF16,), name="ffn_up_dx")
        dh, dh_bf, dg = _rms_bwd(s['h1'], d_hn2, dh, row(W['norm_ffn'][i]), "rms_bwd", deps=token)
        G['norm_ffn'][i] = dg[0]
        if i % 2 == 0:
            d_o = _mm(dh_bf, FW['mla_w_out'][j], tb=True, out_dtypes=(BF16,), name="mla_out_dx")
            g_out = _mm(s['o'], dh_bf, ta=True, out_dtypes=(BF16,), name="mla_out_dw")
            dq, dk, dv = _attn_bwd(s['q'], s['k'], s['v'], s['o'], d_o, s['lse'], seq, "attn_bwd")
            d_q_raw, d_kv_raw, d_kr, g1, g2, g3, g4 = _prep2_bwd(
                s['q_raw'], s['kv_raw'], s['lat'], cos, sin, rmat, s['gains'], dq, dk, dv, "mla_prep2_bwd")
            G['mla_q_nope_g'][j], G['mla_q_rope_g'][j] = g1[0], g2[0]
            G['mla_k_nope_g'][j], G['mla_k_rope_g'][j] = g3[0], g4[0]
            g_uq = _mm(s['cq'], d_q_raw, ta=True, out_blocks=N_DEV, out_dtypes=(BF16,), name="mla_uq_dw")
            g_ukv = _mm(s['ckv'], d_kv_raw, ta=True, out_blocks=N_DEV, out_dtypes=(BF16,), name="mla_ukv_dw")
            d_cq = _mm(d_q_raw, FW['mla_w_uq'][j], tb=True, out_dtypes=(BF16,), name="mla_uq_dx")
            d_ckv = _mm(d_kv_raw, FW['mla_w_ukv'][j], tb=True, out_dtypes=(BF16,), name="mla_ukv_dx")
            d_lat, dga, dgb = _prep1_bwd(s['lat'], d_cq, d_ckv, d_kr, row(W['mla_q_lora_g'][j]),
                                         row(W['mla_kv_lora_g'][j]), "mla_prep1_bwd")
            G['mla_q_lora_g'][j], G['mla_kv_lora_g'][j] = dga[0], dgb[0]
            g_down = _mm(s['hn'], d_lat, ta=True, out_dtypes=(BF16,), name="mla_down_dw")
            grads = [(('mla_w_out', j), by_owner(g_out)), (('mla_w_uq', j), g_uq), (('mla_w_ukv', j), g_ukv),
                     (('mla_w_down', j), by_owner(g_down))]
            d_hn = _mm(d_lat, FW['mla_w_down'][j], tb=True, out_dtypes=(BF16,), name="mla_down_dx")
        else:
            d_y = _mm(dh_bf, FW['gmlp_w_out'][j], tb=True, out_dtypes=(BF16,), name="gmlp_out_dx")
            g_out = _mm(s['y'], dh_bf, ta=True, out_dtypes=(BF16,), name="gmlp_out_dw")
            d_u, d_vn, d_ws, d_bs = _sgu_bwd(s['u'], s['vn'], d_y, W['gmlp_w_s'][j], s['bs3'], "gmlp_sgu_bwd")
            G['gmlp_w_s'][j], G['gmlp_b_s'][j] = d_ws, d_bs[:, :, 0]
            d_z, d_lg, d_lb = _gelu_ln_bwd(s['z'], d_u, d_vn, row(FW['gmlp_ln_g'][j]), row(FW['gmlp_ln_b'][j]),
                                           "gmlp_gelu_ln_bwd")
            g_in = _mm(s['hn'], d_z, ta=True, out_blocks=N_DEV, out_dtypes=(BF16,), name="gmlp_in_dw")
            grads = [(('gmlp_w_out', j), by_owner(g_out)), (('gmlp_ln_g', j), by_owner(d_lg[0])),
                     (('gmlp_ln_b', j), by_owner(d_lb[0])), (('gmlp_w_in', j), g_in)]
            d_hn = _mm(d_z, FW['gmlp_w_in'][j], tb=True, out_dtypes=(BF16,), name="gmlp_in_dx")
        token = send_grads("mix%d" % i, grads)
        if len(flying) > 1:
            token = token + land_grads(token[0])
        dh, _, dg = _rms_bwd(s['h0'], d_hn, dh, row(W['norm_mix'][i]), "rms_bwd", deps=token)
        G['norm_mix'][i] = dg[0]
        token = []
        if i == 1:
            small_a, token = start_small(spatial, "spatial", dh)
    grad_x = dh.reshape(x.shape)

    small_b, _ = start_small([n for n in REPLICATED if n not in spatial], "gains", dh)
    while flying:
        land_grads(dh)
    land_small(small_a, "spatial", dh)
    land_small(small_b, "gains", dh)

    out = lambda q: [res[n][q].reshape(W[n].shape) for n in WEIGHTS]
    return (loss, grad_x, *out(0), *out(1), *out(2), *out(3))
```

```python
import math

import numpy as np
import jax
import jax.numpy as jnp
from jax import lax
from jax.experimental import pallas as pl
from jax.experimental.pallas import tpu as pltpu

F32 = jnp.float32
BF16 = jnp.bfloat16

N_DEV = 8
MESH_AXES = ("x", "y", "c")
HEADS = 8
NOPE = 128
ROPE = 64
VDIM = 128
QK = NOPE + ROPE
Q_LORA = 384
KV_LORA = 256
ROPE_BASE = 10000.0
CHUNK = 128
GROUPS = 8
EPS = 1e-6
LR, B1, B2, ADAM_EPS, WD, STEP = 0.001, 0.9, 0.999, 1e-08, 0.01, 10
LANE = 128
VMEM_LIMIT = 56 * 1024 * 1024
MM_VMEM_BUDGET = 40 * 1024 * 1024

WEIGHTS = ['norm_mix', 'norm_ffn', 'norm_ple', 'mla_w_down', 'mla_q_lora_g', 'mla_kv_lora_g', 'mla_w_uq',
           'mla_w_ukv', 'mla_q_nope_g', 'mla_q_rope_g', 'mla_k_nope_g', 'mla_k_rope_g', 'mla_w_out', 'gmlp_w_in',
           'gmlp_ln_g', 'gmlp_ln_b', 'gmlp_w_s', 'gmlp_b_s', 'gmlp_w_out', 'ffn_w_up', 'ffn_w_down', 'ple_w_gate',
           'ple_w_proj']
SHARD_AXIS = {'mla_w_down': 1, 'mla_w_uq': 2, 'mla_w_ukv': 2, 'mla_w_out': 1, 'gmlp_w_in': 2, 'gmlp_ln_g': 1,
              'gmlp_ln_b': 1, 'gmlp_w_out': 1, 'ffn_w_up': 2, 'ffn_w_down': 1, 'ple_w_gate': 1, 'ple_w_proj': 2}
SHARDED = list(SHARD_AXIS)
REPLICATED = [n for n in WEIGHTS if n not in SHARD_AXIS]
F32_PAYLOAD = ('gmlp_ln_g', 'gmlp_ln_b')


def _pick(dim, pref, align=LANE):
    if dim <= pref:
        return dim
    b = (pref // align) * align
    while b >= align:
        if dim % b == 0:
            return b
        b -= align
    return dim


def _params(sem):
    return pltpu.CompilerParams(dimension_semantics=sem, vmem_limit_bytes=VMEM_LIMIT)


def _mm(a, b, *, ta=False, tb=False, extras=(), rows=(), epilogue=None, out_dtypes=(F32,), out_blocks=None,
        deps=(), name, bn=1024):
    a3, b3 = a.ndim == 3, b.ndim == 3
    assert not (ta and a3)
    if ta:
        K, M = a.shape
        ka = K
    elif a3:
        M, ka = a.shape[1:]
        K = a.shape[0] * ka
    else:
        M, K = a.shape
        ka = K
    if tb:
        N, kb = b.shape[-2:]
        nb = N
        K2 = b.shape[0] * kb if b3 else kb
    else:
        kb, nb = b.shape[-2:]
        K2 = kb
        N = b.shape[0] * nb if b3 else nb
    assert K == K2, (a.shape, b.shape, ta, tb)
    no_ = N // out_blocks if out_blocks else N
    assert not (out_blocks and extras)
    size = lambda t: jnp.dtype(t).itemsize
    per_out = sum(size(e.dtype) for e in extras) + sum(size(t) for t in out_dtypes)
    bn = _pick(min(nb, no_), bn)
    k_lim = min(ka, kb)
    fits = lambda m, k: 2 * (m * k * size(a.dtype) + k * bn * size(b.dtype) + m * bn * per_out) + 4 * m * bn
    ms = [m for m in sorted({min(M, c) for c in (2048, 1024, 512, 256)}, reverse=True) if M % m == 0]
    ks = [k for k in dict.fromkeys((k_lim, 2048, 1024, 512, 256)) if k <= k_lim and k_lim % k == 0]
    bm, bk = next(((m, k) for k in ks for m in ms if fits(m, k) <= MM_VMEM_BUDGET), (ms[-1], ks[-1]))
    nk = K // bk
    ne, no = len(extras) + len(rows), len(out_dtypes)
    first_out = 2 + ne + len(deps)
    dims = (((0,) if ta else (1,), (1,) if tb else (0,)), ((), ()))

    def finish(r, e_refs, o_refs):
        outs = epilogue(r, *[e[...] for e in e_refs]) if epilogue is not None else (r,)
        for o, v in zip(o_refs, outs):
            o[...] = v.astype(o.dtype)

    def body(*refs):
        a_ref, b_ref = refs[0], refs[1]
        e_refs = refs[2:2 + ne]
        o_refs = refs[first_out:first_out + no]
        part = lax.dot_general(a_ref[...].astype(BF16), b_ref[...].astype(BF16), dims, preferred_element_type=F32)
        if nk == 1:
            finish(part, e_refs, o_refs)
            return
        acc = refs[-1]
        k = pl.program_id(2)

        @pl.when(k == 0)
        def _():
            acc[...] = part

        @pl.when(k > 0)
        def _():
            acc[...] += part

        @pl.when(k == nk - 1)
        def _():
            finish(acc[...], e_refs, o_refs)

    ka_t, kb_t, nb_t, no_t = ka // bk, kb // bk, nb // bn, no_ // bn
    if ta:
        a_spec = pl.BlockSpec((bk, bm), lambda i, j, k: (k, i))
    elif a3:
        a_spec = pl.BlockSpec((None, bm, bk), lambda i, j, k: (k // ka_t, i, k % ka_t))
    else:
        a_spec = pl.BlockSpec((bm, bk), lambda i, j, k: (i, k))
    if tb:
        b_spec = (pl.BlockSpec((None, bn, bk), lambda i, j, k: (k // kb_t, j, k % kb_t)) if b3 else
                  pl.BlockSpec((bn, bk), lambda i, j, k: (j, k)))
    else:
        b_spec = (pl.BlockSpec((None, bk, bn), lambda i, j, k: (j // nb_t, k, j % nb_t)) if b3 else
                  pl.BlockSpec((bk, bn), lambda i, j, k: (k, j)))
    if out_blocks:
        o_spec = lambda: pl.BlockSpec((None, bm, bn), lambda i, j, k: (j // no_t, i, j % no_t))
        o_shape = (out_blocks, M, no_)
    else:
        o_spec = lambda: pl.BlockSpec((bm, bn), lambda i, j, k: (i, j))
        o_shape = (M, N)
    outs = pl.pallas_call(
        body, name=name,
        grid=(M // bm, N // bn, nk),
        in_specs=[a_spec, b_spec] + [pl.BlockSpec((bm, bn), lambda i, j, k: (i, j)) for _ in extras]
        + [pl.BlockSpec((1, bn), lambda i, j, k: (0, j)) for _ in rows]
        + [pl.BlockSpec(memory_space=pl.ANY)] * len(deps),
        out_specs=[o_spec() for _ in out_dtypes],
        out_shape=[jax.ShapeDtypeStruct(o_shape, dt) for dt in out_dtypes],
        scratch_shapes=[pltpu.VMEM((bm, bn), F32)] if nk > 1 else [],
        compiler_params=_params(("parallel", "parallel", "arbitrary")),
    )(a, b, *extras, *rows, *deps)
    return outs[0] if no == 1 else outs


def _rowcall(fn, rows, params, row_outs, acc_outs=(), *, bm=256, deps=(), name):
    T = rows[0].shape[0]
    bm = _pick(T, bm, 8)
    nr, npar, nro, nao = len(rows), len(params), len(row_outs), len(acc_outs)
    first_out = nr + npar + len(deps)

    def body(*refs):
        vals = [r[...] for r in refs[:nr + npar]]
        res = fn(*vals)
        ro = refs[first_out:first_out + nro]
        ao = refs[first_out + nro:]
        for r, v in zip(ro, res[:nro]):
            r[...] = v.astype(r.dtype)
        if nao:
            @pl.when(pl.program_id(0) == 0)
            def _():
                for r in ao:
                    r[...] = jnp.zeros_like(r)

            for r, v in zip(ao, res[nro:]):
                r[...] += v

    def whole(shape):
        nd = len(shape)
        return pl.BlockSpec(tuple(shape), lambda i: (0,) * nd)

    outs = pl.pallas_call(
        body, name=name,
        grid=(T // bm,),
        in_specs=[pl.BlockSpec((bm, r.shape[1]), lambda i: (i, 0)) for r in rows] + [whole(q.shape) for q in params]
        + [pl.BlockSpec(memory_space=pl.ANY)] * len(deps),
        out_specs=[pl.BlockSpec((bm, c), lambda i: (i, 0)) for c, _ in row_outs] + [whole(s) for s in acc_outs],
        out_shape=[jax.ShapeDtypeStruct((T, c), dt) for c, dt in row_outs]
        + [jax.ShapeDtypeStruct(tuple(s), F32) for s in acc_outs],
        compiler_params=_params(("arbitrary",) if nao else ("parallel",)),
    )(*rows, *params, *deps)
    return outs


def _rmsn(x, g):
    return x * lax.rsqrt(jnp.mean(x * x, axis=-1, keepdims=True) + EPS) * g


def _gelu(x):
    return 0.5 * x * (1.0 + jnp.tanh(math.sqrt(2.0 / math.pi) * (x + 0.044715 * (x * x * x))))


def _layer_norm(x, g, b):
    mu = jnp.mean(x, axis=-1, keepdims=True)
    xc = x - mu
    return xc * lax.rsqrt(jnp.mean(xc * xc, axis=-1, keepdims=True) + EPS) * g + b


def _sigmoid(x):
    return 1.0 / (1.0 + jnp.exp(-x))


def _rot(x, cos, sin, rmat):
    return x * cos + jnp.dot(x, rmat, precision=lax.Precision.HIGHEST, preferred_element_type=F32) * sin


def _rms_fwd(h, g, name, deps=()):
    return _rowcall(lambda x, gg: (_rmsn(x, gg),), [h], [g], [(h.shape[1], BF16)], bm=512, deps=deps, name=name)[0]


def _rms_bwd(h, d_hn, dh_in, g, name, deps=()):
    def fn(x, dy, dres, gg):
        _, vjp = jax.vjp(_rmsn, x, gg)
        dx, dg = vjp(dy.astype(F32))
        dh = dres + dx
        return dh, dh, dg

    d = h.shape[1]
    return _rowcall(fn, [h, d_hn, dh_in], [g], [(d, F32), (d, BF16)], [g.shape], bm=512, deps=deps, name=name)


def _rope_tables(pos, name):
    inv = np.float32(ROPE_BASE) ** (-(np.arange(0, ROPE, 2, dtype=np.float32) / np.float32(ROPE)))
    inv = jnp.asarray(np.concatenate([inv, inv])[None, :].astype(np.float32))

    def fn(pp, iv):
        ang = pp.astype(F32) * iv
        return jnp.cos(ang), jnp.sin(ang)

    return _rowcall(fn, [pos], [inv], [(ROPE, F32), (ROPE, F32)], name=name)


def _rot_matrix():
    r = np.zeros((ROPE, ROPE), np.float32)
    half = ROPE // 2
    for j in range(half):
        r[j + half, j] = -1.0
        r[j, j + half] = 1.0
    return jnp.asarray(r)


def _prep1_fwd(lat, gq, gkv, name):
    def fn(l, a, b):
        return _rmsn(l[:, :Q_LORA], a), _rmsn(l[:, Q_LORA:Q_LORA + KV_LORA], b)

    return _rowcall(fn, [lat], [gq, gkv], [(Q_LORA, BF16), (KV_LORA, BF16)], name=name)


def _prep1_bwd(lat, d_cq, d_ckv, d_kr, gq, gkv, name):
    def fn(l, dq, dkv, dkr, a, b):
        _, vq = jax.vjp(_rmsn, l[:, :Q_LORA], a)
        _, vkv = jax.vjp(_rmsn, l[:, Q_LORA:Q_LORA + KV_LORA], b)
        dxq, dga = vq(dq.astype(F32))
        dxkv, dgb = vkv(dkv.astype(F32))
        return jnp.concatenate([dxq, dxkv, dkr], axis=1), dga, dgb

    return _rowcall(fn, [lat, d_cq, d_ckv, d_kr], [gq, gkv], [(lat.shape[1], BF16)], [gq.shape, gkv.shape], name=name)


def _qk_fn(qn_raw, qr_raw, kn_raw, kr_raw, gqn, gqr, gkn, gkr, cos, sin, rmat):
    return (_rmsn(qn_raw, gqn), _rot(_rmsn(qr_raw, gqr), cos, sin, rmat),
            _rmsn(kn_raw, gkn), _rot(_rmsn(kr_raw, gkr), cos, sin, rmat))


def _prep2_fwd(q_raw, kv_raw, lat, cos, sin, rmat, gains, name, bm=1024):
    H, T, _ = q_raw.shape
    bm = _pick(T, bm, 8)
    kr0 = Q_LORA + KV_LORA

    def body(q_ref, kv_ref, lat_ref, cos_ref, sin_ref, r_ref, gqn, gqr, gkn, gkr, qo, ko, vo):
        qr, kvr = q_ref[...], kv_ref[...]
        qn, qro, kn, kro = _qk_fn(qr[:, :NOPE], qr[:, NOPE:], kvr[:, :NOPE], lat_ref[:, kr0:kr0 + ROPE],
                                  gqn[...], gqr[...], gkn[...], gkr[...], cos_ref[...], sin_ref[...], r_ref[...])
        qo[:, :NOPE] = qn.astype(BF16)
        qo[:, NOPE:] = qro.astype(BF16)
        ko[:, :NOPE] = kn.astype(BF16)
        ko[:, NOPE:] = kro.astype(BF16)
        vo[...] = kvr[:, NOPE:].astype(BF16)

    hb = lambda c: pl.BlockSpec((None, bm, c), lambda m, h: (h, m, 0))
    rb = lambda c: pl.BlockSpec((bm, c), lambda m, h: (m, 0))
    wb = lambda s: pl.BlockSpec(tuple(s), lambda m, h: (0, 0))
    return pl.pallas_call(
        body, name=name, grid=(T // bm, H),
        in_specs=[hb(QK), hb(NOPE + VDIM), rb(lat.shape[1]), rb(ROPE), rb(ROPE), wb(rmat.shape)]
        + [wb(g.shape) for g in gains],
        out_specs=[hb(QK), hb(QK), hb(VDIM)],
        out_shape=[jax.ShapeDtypeStruct((H, T, QK), BF16), jax.ShapeDtypeStruct((H, T, QK), BF16),
                   jax.ShapeDtypeStruct((H, T, VDIM), BF16)],
        compiler_params=_params(("parallel", "parallel")),
    )(q_raw, kv_raw, lat, cos, sin, rmat, *gains)


def _prep2_bwd(q_raw, kv_raw, lat, cos, sin, rmat, gains, dq, dk, dv, name, bm=512):
    H, T, _ = q_raw.shape
    bm = _pick(T, bm, 8)
    kr0 = Q_LORA + KV_LORA

    def body(q_ref, kv_ref, lat_ref, cos_ref, sin_ref, r_ref, gqn, gqr, gkn, gkr, dq_ref, dk_ref, dv_ref,
             dqo, dkvo, dkro, o_gqn, o_gqr, o_gkn, o_gkr):
        m, h = pl.program_id(0), pl.program_id(1)
        qr, kvr = q_ref[...], kv_ref[...]
        cos_v, sin_v, r_v = cos_ref[...], sin_ref[...], r_ref[...]
        f = lambda a, b, c, d, g1, g2, g3, g4: _qk_fn(a, b, c, d, g1, g2, g3, g4, cos_v, sin_v, r_v)
        _, vjp = jax.vjp(f, qr[:, :NOPE], qr[:, NOPE:], kvr[:, :NOPE], lat_ref[:, kr0:kr0 + ROPE],
                         gqn[...], gqr[...], gkn[...], gkr[...])
        dqv, dkv_ = dq_ref[...], dk_ref[...]
        d_qn, d_qr, d_kn, d_kr, g1, g2, g3, g4 = vjp((dqv[:, :NOPE], dqv[:, NOPE:], dkv_[:, :NOPE], dkv_[:, NOPE:]))
        dqo[:, :NOPE] = d_qn.astype(BF16)
        dqo[:, NOPE:] = d_qr.astype(BF16)
        dkvo[:, :NOPE] = d_kn.astype(BF16)
        dkvo[:, NOPE:] = dv_ref[...].astype(BF16)

        @pl.when(h == 0)
        def _():
            dkro[...] = jnp.zeros_like(dkro)

        dkro[...] += d_kr

        @pl.when((h == 0) & (m == 0))
        def _():
            for o in (o_gqn, o_gqr, o_gkn, o_gkr):
                o[...] = jnp.zeros_like(o)

        for o, g in zip((o_gqn, o_gqr, o_gkn, o_gkr), (g1, g2, g3, g4)):
            o[...] += g

    hb = lambda c: pl.BlockSpec((None, bm, c), lambda m, h: (h, m, 0))
    rb = lambda c: pl.BlockSpec((bm, c), lambda m, h: (m, 0))
    wb = lambda s: pl.BlockSpec(tuple(s), lambda m, h: (0, 0))
    return pl.pallas_call(
        body, name=name, grid=(T // bm, H),
        in_specs=[hb(QK), hb(NOPE + VDIM), rb(lat.shape[1]), rb(ROPE), rb(ROPE), wb(rmat.shape)]
        + [wb(g.shape) for g in gains] + [hb(QK), hb(QK), hb(VDIM)],
        out_specs=[hb(QK), hb(NOPE + VDIM), rb(ROPE)] + [wb(g.shape) for g in gains],
        out_shape=[jax.ShapeDtypeStruct((H, T, QK), BF16), jax.ShapeDtypeStruct((H, T, NOPE + VDIM), BF16),
                   jax.ShapeDtypeStruct((T, ROPE), F32)] + [jax.ShapeDtypeStruct(g.shape, F32) for g in gains],
        compiler_params=_params(("arbitrary", "arbitrary")),
    )(q_raw, kv_raw, lat, cos, sin, rmat, *gains, dq, dk, dv)


_NT = (((1,), (1,)), ((), ()))
_TN = (((0,), (0,)), ((), ()))


def _causal(blk):
    return lax.broadcasted_iota(jnp.int32, (blk, blk), 1) <= lax.broadcasted_iota(jnp.int32, (blk, blk), 0)


def _attn_fwd(q, k, v, seq, name, blk=512):
    H, T, _ = q.shape
    nb = T // seq
    blk = _pick(seq, blk)
    nq = seq // blk
    scale = float(QK) ** -0.5

    def body(q_ref, k_ref, v_ref, o_ref, lse_ref):
        qi = pl.program_id(2)
        qb = q_ref[...]

        def step(j, carry, diagonal):
            m, l, acc = carry
            ks = pl.ds(pl.multiple_of(j * blk, blk), blk)
            s = lax.dot_general(qb, k_ref[ks, :], _NT, preferred_element_type=F32) * scale
            if diagonal:
                s = jnp.where(_causal(blk), s, -jnp.inf)
            m_new = jnp.maximum(m, jnp.max(s, axis=1, keepdims=True))
            pr = jnp.exp(s - m_new)
            alpha = jnp.exp(m - m_new)
            l = alpha * l + jnp.sum(pr, axis=1, keepdims=True)
            acc = alpha * acc + jnp.dot(pr.astype(BF16), v_ref[ks, :], preferred_element_type=F32)
            return m_new, l, acc

        init = (jnp.full((blk, 1), -jnp.inf, F32), jnp.zeros((blk, 1), F32), jnp.zeros((blk, VDIM), F32))
        below = lax.fori_loop(0, qi, lambda j, c: step(j, c, False), init)
        m, l, acc = step(qi, below, True)
        o_ref[...] = (acc / l).astype(o_ref.dtype)
        lse_ref[...] = m + jnp.log(l)

    return pl.pallas_call(
        body, name=name, grid=(H, nb, nq),
        in_specs=[pl.BlockSpec((None, blk, QK), lambda h, b, i: (h, b * nq + i, 0)),
                  pl.BlockSpec((None, seq, QK), lambda h, b, i: (h, b, 0)),
                  pl.BlockSpec((None, seq, VDIM), lambda h, b, i: (h, b, 0))],
        out_specs=[pl.BlockSpec((blk, VDIM), lambda h, b, i: (b * nq + i, h)),
                   pl.BlockSpec((None, blk, 1), lambda h, b, i: (h, b * nq + i, 0))],
        out_shape=[jax.ShapeDtypeStruct((T, H * VDIM), BF16), jax.ShapeDtypeStruct((H, T, 1), F32)],
        compiler_params=_params(("parallel", "parallel", "parallel")),
    )(q, k, v)


def _attn_bwd(q, k, v, o, do, lse, seq, name, blk=512):
    H, T, _ = q.shape
    nb = T // seq
    blk = _pick(seq, blk)
    nq = seq // blk
    scale = float(QK) ** -0.5

    def body(q_ref, k_ref, v_ref, o_ref, do_ref, lse_ref, dq_ref, dk_ref, dv_ref):
        dk_ref[...] = jnp.zeros_like(dk_ref)
        dv_ref[...] = jnp.zeros_like(dv_ref)

        def qloop(i, carry):
            qs = pl.ds(pl.multiple_of(i * blk, blk), blk)
            qb = q_ref[qs, :]
            dob = do_ref[qs, :]
            dof = dob.astype(F32)
            lse_b = lse_ref[qs, :]
            delta = jnp.sum(dof * o_ref[qs, :].astype(F32), axis=1, keepdims=True)

            def kstep(j, dq_acc, diagonal):
                ks = pl.ds(pl.multiple_of(j * blk, blk), blk)
                kb = k_ref[ks, :]
                vb = v_ref[ks, :]
                s = lax.dot_general(qb, kb, _NT, preferred_element_type=F32) * scale
                pr = jnp.exp(s - lse_b)
                if diagonal:
                    pr = jnp.where(_causal(blk), pr, 0.0)
                dp = lax.dot_general(dob, vb, _NT, preferred_element_type=F32)
                ds = (pr * (dp - delta) * scale).astype(BF16)
                prb = pr.astype(BF16)
                dv_ref[ks, :] += lax.dot_general(prb, dob, _TN, preferred_element_type=F32)
                dk_ref[ks, :] += lax.dot_general(ds, qb, _TN, preferred_element_type=F32)
                return dq_acc + jnp.dot(ds, kb, preferred_element_type=F32)

            below = lax.fori_loop(0, i, lambda j, c: kstep(j, c, False), jnp.zeros((blk, QK), F32))
            dq_ref[qs, :] = kstep(i, below, True)
            return carry

        lax.fori_loop(0, nq, qloop, 0)

    hb = lambda c: pl.BlockSpec((None, seq, c), lambda h, b: (h, b, 0))
    cb = lambda: pl.BlockSpec((seq, VDIM), lambda h, b: (b, h))
    return pl.pallas_call(
        body, name=name, grid=(H, nb),
        in_specs=[hb(QK), hb(QK), hb(VDIM), cb(), cb(), hb(1)],
        out_specs=[hb(QK), hb(QK), hb(VDIM)],
        out_shape=[jax.ShapeDtypeStruct((H, T, QK), F32), jax.ShapeDtypeStruct((H, T, QK), F32),
                   jax.ShapeDtypeStruct((H, T, VDIM), F32)],
        compiler_params=_params(("parallel", "parallel")),
    )(q, k, v, o, do, lse)


def _gelu_ln_fwd(z, g, b, name):
    half = z.shape[1] // 2

    def fn(zz, gg, bb):
        return _gelu(zz[:, :half].astype(F32)), _layer_norm(_gelu(zz[:, half:].astype(F32)), gg, bb)

    return _rowcall(fn, [z], [g, b], [(half, BF16), (half, BF16)], name=name)


def _gelu_ln_bwd(z, d_u, d_vn, g, b, name):
    half = z.shape[1] // 2

    def gelu_and_slope(x):
        c, a = math.sqrt(2.0 / math.pi), 0.044715
        x2 = x * x
        t = jnp.tanh(c * x * (1.0 + a * x2))
        return 0.5 * x * (1.0 + t), 0.5 * (1.0 + t) + 0.5 * x * (1.0 - t * t) * (c * (1.0 + 3.0 * a * x2))

    def fn(zz, du, dvn, gg, bb):
        du, dvn = du.astype(F32), dvn.astype(F32)
        _, su = gelu_and_slope(zz[:, :half].astype(F32))
        v, sv = gelu_and_slope(zz[:, half:].astype(F32))
        xc = v - jnp.mean(v, axis=-1, keepdims=True)
        rstd = lax.rsqrt(jnp.mean(xc * xc, axis=-1, keepdims=True) + EPS)
        y = xc * rstd
        dy = dvn * gg
        dv = rstd * (dy - jnp.mean(dy, axis=-1, keepdims=True) - y * jnp.mean(dy * y, axis=-1, keepdims=True))
        dg = jnp.sum(dvn * y, axis=0, keepdims=True)
        db = jnp.sum(dvn, axis=0, keepdims=True)
        return jnp.concatenate([du * su, dv * sv], axis=1), dg, db

    return _rowcall(fn, [z, d_u, d_vn], [g, b], [(z.shape[1], BF16)], [g.shape, b.shape], bm=128, name=name)


def _tril_bf16(ws):
    t = lax.broadcasted_iota(jnp.int32, ws.shape, 0)
    s = lax.broadcasted_iota(jnp.int32, ws.shape, 1)
    return jnp.where(s <= t, ws, 0.0).astype(BF16)


def _sgu_fwd(u, vn, ws, bs, name, bm=2048):
    T, half = u.shape
    gd = half // GROUPS
    bm = _pick(T, bm, CHUNK)
    nc = bm // CHUNK

    def body(u_ref, vn_ref, ws_ref, bs_ref, y_ref):
        wm = _tril_bf16(ws_ref[...])
        bias = bs_ref[...]
        for c in range(nc):
            rs = slice(c * CHUNK, (c + 1) * CHUNK)
            sv = jnp.dot(wm, vn_ref[rs, :], preferred_element_type=F32) + bias
            y_ref[rs, :] = (u_ref[rs, :].astype(F32) * sv).astype(y_ref.dtype)

    tb = lambda: pl.BlockSpec((bm, gd), lambda g, i: (i, g))
    return pl.pallas_call(
        body, name=name, grid=(GROUPS, T // bm),
        in_specs=[tb(), tb(), pl.BlockSpec((None, CHUNK, CHUNK), lambda g, i: (g, 0, 0)),
                  pl.BlockSpec((None, CHUNK, 1), lambda g, i: (g, 0, 0))],
        out_specs=tb(),
        out_shape=jax.ShapeDtypeStruct((T, half), BF16),
        compiler_params=_params(("parallel", "parallel")),
    )(u, vn, ws, bs)


def _sgu_bwd(u, vn, dy, ws, bs, name, bm=1024):
    T, half = u.shape
    gd = half // GROUPS
    bm = _pick(T, bm, CHUNK)
    nc = bm // CHUNK

    def body(u_ref, vn_ref, dy_ref, ws_ref, bs_ref, du_ref, dvn_ref, dws_ref, dbs_ref):
        @pl.when(pl.program_id(1) == 0)
        def _():
            dws_ref[...] = jnp.zeros_like(dws_ref)
            dbs_ref[...] = jnp.zeros_like(dbs_ref)

        wm = _tril_bf16(ws_ref[...])
        bias = bs_ref[...]
        dws = jnp.zeros((CHUNK, CHUNK), F32)
        dbs = jnp.zeros((CHUNK, 1), F32)
        for c in range(nc):
            rs = slice(c * CHUNK, (c + 1) * CHUNK)
            vb = vn_ref[rs, :]
            dyb = dy_ref[rs, :].astype(F32)
            sv = jnp.dot(wm, vb, preferred_element_type=F32) + bias
            du_ref[rs, :] = (dyb * sv).astype(du_ref.dtype)
            dsv = dyb * u_ref[rs, :].astype(F32)
            dsb = dsv.astype(BF16)
            dvn_ref[rs, :] = lax.dot_general(wm, dsb, _TN, preferred_element_type=F32).astype(dvn_ref.dtype)
            dws = dws + lax.dot_general(dsb, vb, _NT, preferred_element_type=F32)
            dbs = dbs + jnp.sum(dsv, axis=1, keepdims=True)
        t = lax.broadcasted_iota(jnp.int32, (CHUNK, CHUNK), 0)
        s = lax.broadcasted_iota(jnp.int32, (CHUNK, CHUNK), 1)
        dws_ref[...] += jnp.where(s <= t, dws, 0.0)
        dbs_ref[...] += dbs

    tb = lambda: pl.BlockSpec((bm, gd), lambda g, i: (i, g))
    wsb = lambda: pl.BlockSpec((None, CHUNK, CHUNK), lambda g, i: (g, 0, 0))
    bsb = lambda: pl.BlockSpec((None, CHUNK, 1), lambda g, i: (g, 0, 0))
    return pl.pallas_call(
        body, name=name, grid=(GROUPS, T // bm),
        in_specs=[tb(), tb(), tb(), wsb(), bsb()],
        out_specs=[tb(), tb(), wsb(), bsb()],
        out_shape=[jax.ShapeDtypeStruct((T, half), BF16), jax.ShapeDtypeStruct((T, half), BF16),
                   jax.ShapeDtypeStruct(ws.shape, F32), jax.ShapeDtypeStruct(bs.shape, F32)],
        compiler_params=_params(("parallel", "arbitrary")),
    )(u, vn, dy, ws, bs)


def _loss_head(y, t, name):
    d_model = y.shape[1]

    def fn(yy, tt):
        d = yy - tt
        part = 0.5 * jnp.sum(jnp.mean(d * d, axis=-1, keepdims=True), axis=0, keepdims=True)
        return d / d_model, jnp.zeros((1, LANE), F32) + part

    dy, part = _rowcall(fn, [y, t], [], [(d_model, F32)], [(1, LANE)], name=name)
    return dy, part[0, 0]


def _adamw(parts, w, m, v, prev, layer, name):
    L, R, C = w.shape
    br = _pick(R, max(8, (128 * 1024) // C // 8 * 8), 8)
    c1 = 1.0 - B1 ** STEP
    c2 = 1.0 - B2 ** STEP
    if prev is None:
        prev = [lax.empty(w.shape, F32) for _ in range(4)]

    def body(p_ref, w_ref, m_ref, v_ref, a0, a1, a2, a3, g_o, d_o, m_o, v_o, token):
        g = p_ref[0].astype(F32)
        for d in range(1, N_DEV):
            g = g + p_ref[d].astype(F32)
        mn = B1 * m_ref[...] + (1.0 - B1) * g
        vn = B2 * v_ref[...] + (1.0 - B2) * (g * g)
        g_o[...] = g
        m_o[...] = mn
        v_o[...] = vn
        d_o[...] = -LR * ((mn / c1) / (jnp.sqrt(vn / c2) + ADAM_EPS) + WD * w_ref[...])
        token[...] = jnp.zeros_like(token)

    blk = lambda: pl.BlockSpec((None, br, C), lambda i: (layer, i, 0))
    anywhere = pl.BlockSpec(memory_space=pl.ANY)
    outs = pl.pallas_call(
        body, name=name, grid=(R // br,),
        in_specs=[pl.BlockSpec((N_DEV, br, C), lambda i: (0, i, 0)), blk(), blk(), blk()] + [anywhere] * 4,
        out_specs=[blk(), blk(), blk(), blk(), pl.BlockSpec((8, LANE), lambda i: (0, 0))],
        out_shape=[jax.ShapeDtypeStruct((L, R, C), F32)] * 4 + [jax.ShapeDtypeStruct((8, LANE), F32)],
        input_output_aliases={4: 0, 5: 1, 6: 2, 7: 3},
        compiler_params=_params(("arbitrary",)),
    )(parts, w, m, v, *prev)
    return list(outs[:4]), outs[4]


def _mesh_pos():
    return lax.axis_index("x"), lax.axis_index("y"), lax.axis_index("c")


def _flip(pos, k):
    x, y, c = pos
    px = 1 - x if k & 4 else x
    py = 1 - y if k & 2 else y
    pc = 1 - c if k & 1 else c
    return px, py, pc


HBM_SPEC = pl.BlockSpec(memory_space=pltpu.HBM)
SEM_SPEC = pl.BlockSpec(memory_space=pltpu.SEMAPHORE)
EFFECT = pltpu.SideEffectType.DATAFLOW_SIDE_EFFECTING


def _hbm(a):
    return pltpu.with_memory_space_constraint(a, pltpu.HBM)


def _device_index():
    x, y, c = _mesh_pos()
    return 4 * x + 2 * y + c


def _peer_copy(src, land, send, recv, a, k, pos, scatter):
    peer = _flip(pos, k)
    me = 4 * pos[0] + 2 * pos[1] + pos[2]
    piece = src.at[4 * peer[0] + 2 * peer[1] + peer[2]] if scatter else src
    return pltpu.make_async_remote_copy(
        src_ref=piece, dst_ref=land.at[me], send_sem=send.at[7 * a + k - 1], recv_sem=recv.at[7 * a + k - 1],
        device_id=peer, device_id_type=pl.DeviceIdType.MESH)


def _own_copy(src, land, own, a, pos, scatter):
    me = 4 * pos[0] + 2 * pos[1] + pos[2]
    return pltpu.make_async_copy(src.at[me] if scatter else src, land.at[me], own.at[a])


def _xchg_start(srcs, scatter, after, name):
    n = len(srcs)

    def body(*refs):
        src, land = refs[:n], refs[n:2 * n]
        send, recv, own, token = refs[2 * n + 1], refs[2 * n + 2], refs[2 * n + 3], refs[-1]
        pos = _mesh_pos()
        for k in range(1, N_DEV):
            for a in range(n):
                _peer_copy(src[a], land[a], send, recv, a, k, pos, scatter).start()
        for a in range(n):
            _own_copy(src[a], land[a], own, a, pos, scatter).start()
        token[...] = jnp.zeros_like(token)

    lands = [lax.empty(s.shape if scatter else (N_DEV,) + s.shape, s.dtype) for s in srcs]
    outs = pl.pallas_call(
        body, name=name,
        out_shape=(pltpu.SemaphoreType.DMA((7 * n,)), pltpu.SemaphoreType.DMA((7 * n,)), pltpu.SemaphoreType.DMA((n,)),
                   *[pltpu.HBM(s.shape, s.dtype) for s in srcs], *[pltpu.HBM(l.shape, l.dtype) for l in lands],
                   jax.ShapeDtypeStruct((8, LANE), F32)),
        in_specs=[HBM_SPEC] * (2 * n) + [pl.BlockSpec(memory_space=pl.ANY)],
        out_specs=(SEM_SPEC, SEM_SPEC, SEM_SPEC, *[HBM_SPEC] * (2 * n), pl.BlockSpec(memory_space=pltpu.VMEM)),
        input_output_aliases={q: 3 + q for q in range(2 * n)},
        compiler_params=pltpu.CompilerParams(has_side_effects=EFFECT),
    )(*[_hbm(s) for s in srcs], *[_hbm(l) for l in lands], after)
    handle = dict(send=outs[0], recv=outs[1], own=outs[2], srcs=list(outs[3:3 + n]),
                  lands=list(outs[3 + n:3 + 2 * n]), scatter=scatter)
    return handle, outs[-1]


def _xchg_wait(handle, after, name):
    srcs, lands, scatter = handle['srcs'], handle['lands'], handle['scatter']
    n = len(srcs)

    def body(*refs):
        src, land = refs[:n], refs[n:2 * n]
        send, recv, own = refs[2 * n], refs[2 * n + 1], refs[2 * n + 2]
        pos = _mesh_pos()
        for k in range(1, N_DEV):
            for a in range(n):
                cp = _peer_copy(src[a], land[a], send, recv, a, k, pos, scatter)
                cp.wait_send()
                cp.wait_recv()
        for a in range(n):
            _own_copy(src[a], land[a], own, a, pos, scatter).wait()

    outs = pl.pallas_call(
        body, name=name,
        out_shape=[pltpu.HBM(s.shape, s.dtype) for s in srcs] + [pltpu.HBM(l.shape, l.dtype) for l in lands],
        in_specs=[HBM_SPEC] * (2 * n) + [SEM_SPEC, SEM_SPEC, SEM_SPEC, pl.BlockSpec(memory_space=pl.ANY)],
        out_specs=[HBM_SPEC] * (2 * n),
        input_output_aliases={q: q for q in range(2 * n)},
        compiler_params=pltpu.CompilerParams(has_side_effects=EFFECT),
    )(*srcs, *lands, handle['send'], handle['recv'], handle['own'], after)
    return list(outs[n:])


def _pack(parts):
    flat = jnp.concatenate([q.reshape(-1) for q in parts])
    pad = (-flat.shape[0]) % (8 * LANE)
    return jnp.pad(flat, (0, pad)).reshape(-1, LANE)


def _unpack(packed, like):
    flat = packed.reshape(-1)
    out, o = [], 0
    for q in like:
        out.append(flat[o:o + q.size].reshape(q.shape))
        o += q.size
    return out


def kernel(x, p, positions, norm_mix, norm_ffn, norm_ple, mla_w_down, mla_q_lora_g, mla_kv_lora_g, mla_w_uq, mla_w_ukv, mla_q_nope_g, mla_q_rope_g, mla_k_nope_g, mla_k_rope_g, mla_w_out, gmlp_w_in, gmlp_ln_g, gmlp_ln_b, gmlp_w_s, gmlp_b_s, gmlp_w_out, ffn_w_up, ffn_w_down, ple_w_gate, ple_w_proj, loss_target, m_norm_mix, m_norm_ffn, m_norm_ple, m_mla_w_down, m_mla_q_lora_g, m_mla_kv_lora_g, m_mla_w_uq, m_mla_w_ukv, m_mla_q_nope_g, m_mla_q_rope_g, m_mla_k_nope_g, m_mla_k_rope_g, m_mla_w_out, m_gmlp_w_in, m_gmlp_ln_g, m_gmlp_ln_b, m_gmlp_w_s, m_gmlp_b_s, m_gmlp_w_out, m_ffn_w_up, m_ffn_w_down, m_ple_w_gate, m_ple_w_proj, v_norm_mix, v_norm_ffn, v_norm_ple, v_mla_w_down, v_mla_q_lora_g, v_mla_kv_lora_g, v_mla_w_uq, v_mla_w_ukv, v_mla_q_nope_g, v_mla_q_rope_g, v_mla_k_nope_g, v_mla_k_rope_g, v_mla_w_out, v_gmlp_w_in, v_gmlp_ln_g, v_gmlp_ln_b, v_gmlp_w_s, v_gmlp_b_s, v_gmlp_w_out, v_ffn_w_up, v_ffn_w_down, v_ple_w_gate, v_ple_w_proj):
    W = dict(zip(WEIGHTS, (norm_mix, norm_ffn, norm_ple, mla_w_down, mla_q_lora_g, mla_kv_lora_g, mla_w_uq, mla_w_ukv, mla_q_nope_g, mla_q_rope_g, mla_k_nope_g, mla_k_rope_g, mla_w_out, gmlp_w_in, gmlp_ln_g, gmlp_ln_b, gmlp_w_s, gmlp_b_s, gmlp_w_out, ffn_w_up, ffn_w_down, ple_w_gate, ple_w_proj)))
    M1 = dict(zip(WEIGHTS, (m_norm_mix, m_norm_ffn, m_norm_ple, m_mla_w_down, m_mla_q_lora_g, m_mla_kv_lora_g, m_mla_w_uq, m_mla_w_ukv, m_mla_q_nope_g, m_mla_q_rope_g, m_mla_k_nope_g, m_mla_k_rope_g, m_mla_w_out, m_gmlp_w_in, m_gmlp_ln_g, m_gmlp_ln_b, m_gmlp_w_s, m_gmlp_b_s, m_gmlp_w_out, m_ffn_w_up, m_ffn_w_down, m_ple_w_gate, m_ple_w_proj)))
    M2 = dict(zip(WEIGHTS, (v_norm_mix, v_norm_ffn, v_norm_ple, v_mla_w_down, v_mla_q_lora_g, v_mla_kv_lora_g, v_mla_w_uq, v_mla_w_ukv, v_mla_q_nope_g, v_mla_q_rope_g, v_mla_k_nope_g, v_mla_k_rope_g, v_mla_w_out, v_gmlp_w_in, v_gmlp_ln_g, v_gmlp_ln_b, v_gmlp_w_s, v_gmlp_b_s, v_gmlp_w_out, v_ffn_w_up, v_ffn_w_down, v_ple_w_gate, v_ple_w_proj)))

    nb, seq, d_model = x.shape
    assert d_model <= 1024, "the rms norms fused into matmul epilogues need whole rows in one output tile"
    T = nb * seq
    depth = norm_mix.shape[0]
    h = x.reshape(T, d_model)
    target = loss_target.reshape(T, d_model)
    p_bf = p.reshape(depth, T, p.shape[-1]).astype(BF16)

    stages, carried = [], []
    for i in range(depth):
        gate = [(n, i) for n in ('ple_w_gate', 'ple_w_proj')]
        mlp = [(n, i) for n in ('ffn_w_up', 'ffn_w_down')]
        if i % 2 == 0:
            mixer = [(n, i // 2) for n in ('mla_w_down', 'mla_w_uq', 'mla_w_ukv', 'mla_w_out')]
            stages += [mixer[:1], mixer[1:]] if i == 0 else [carried + mixer]
            stages.append(mlp + gate)
            carried = []
        else:
            stages.append(carried + [(n, i // 2) for n in ('gmlp_w_in', 'gmlp_ln_g', 'gmlp_ln_b', 'gmlp_w_out')])
            stages.append(mlp)
            carried = gate
    if carried:
        stages.append(carried)

    FW = {n: {} for n in SHARDED}

    def start_weights(st, after):
        keys = stages[st]
        srcs = [W[n][l] if n in F32_PAYLOAD else W[n][l].astype(BF16) for n, l in keys]
        handle, token = _xchg_start(srcs, False, after, "weights_start%d" % st)
        return (keys, handle), [token]

    def wait_weights(pending, st, after):
        keys, handle = pending
        landed = _xchg_wait(handle, after, "weights_wait%d" % st)
        for (n, l), full in zip(keys, landed):
            if SHARD_AXIS[n] == 1:
                FW[n][l] = full.reshape((-1,) + full.shape[2:])
            elif n in ('mla_w_uq', 'mla_w_ukv'):
                FW[n][l] = full
            else:
                FW[n][l] = jnp.transpose(full, (1, 0, 2)).reshape(full.shape[1], -1)
        return landed[0]

    row = lambda a: a.reshape(1, -1)
    cos, sin = _rope_tables(positions.reshape(T, 1), "rope_tables")
    rmat = _rot_matrix()

    def add_and_norm(acc, res, g):
        hh = res + acc
        return hh, _rmsn(hh, g)

    saved = []
    chain = {'stage': 0}
    chain['pending'], _ = start_weights(0, h)

    def advance(after):
        st = chain['stage']
        if st >= len(stages):
            return []
        landed = wait_weights(chain['pending'], st, after)
        chain['stage'] = st + 1
        if st + 1 >= len(stages):
            return []
        chain['pending'], token = start_weights(st + 1, landed)
        return token

    hn = _rms_fwd(h, row(W['norm_mix'][0]), "rms_fwd")
    token = advance(hn)
    for i in range(depth):
        j = i // 2
        s = {}
        s['h0'] = h
        s['hn'] = hn
        if i % 2 == 0:
            gains = [row(W['mla_q_nope_g'][j]), row(W['mla_q_rope_g'][j]), row(W['mla_k_nope_g'][j]),
                     row(W['mla_k_rope_g'][j])]
            lat = _mm(hn, FW['mla_w_down'][j], deps=token, name="mla_down")
            if i == 0:
                token = advance(lat)
            cq, ckv = _prep1_fwd(lat, row(W['mla_q_lora_g'][j]), row(W['mla_kv_lora_g'][j]), "mla_prep1")
            q_raw = _mm(cq, FW['mla_w_uq'][j], out_blocks=HEADS, deps=token, name="mla_uq")
            kv_raw = _mm(ckv, FW['mla_w_ukv'][j], out_blocks=HEADS, name="mla_ukv")
            q, k, v = _prep2_fwd(q_raw, kv_raw, lat, cos, sin, rmat, gains, "mla_prep2")
            o, lse = _attn_fwd(q, k, v, seq, "attn_fwd")
            token = advance(o)
            h, hn2 = _mm(o, FW['mla_w_out'][j], extras=(h,), rows=(row(W['norm_ffn'][i]),), epilogue=add_and_norm,
                         out_dtypes=(F32, BF16), deps=token, name="mla_out")
            s.update(lat=lat, cq=cq, ckv=ckv, q_raw=q_raw, kv_raw=kv_raw, q=q, k=k, v=v, o=o, lse=lse, gains=gains)
        else:
            z = _mm(hn, FW['gmlp_w_in'][j], out_dtypes=(BF16,), name="gmlp_in")
            u, vn = _gelu_ln_fwd(z, row(FW['gmlp_ln_g'][j]), row(FW['gmlp_ln_b'][j]), "gmlp_gelu_ln")
            bs3 = W['gmlp_b_s'][j][:, :, None]
            y = _sgu_fwd(u, vn, W['gmlp_w_s'][j], bs3, "gmlp_sgu")
            h, hn2 = _mm(y, FW['gmlp_w_out'][j], extras=(h,), rows=(row(W['norm_ffn'][i]),), epilogue=add_and_norm,
                         out_dtypes=(F32, BF16), name="gmlp_out")
            token = advance(hn2)
            s.update(z=z, u=u, vn=vn, y=y, bs3=bs3)
        s['h1'] = h
        a, r = _mm(hn2, FW['ffn_w_up'][i], epilogue=lambda acc: (acc, jnp.square(jnp.maximum(acc, 0.0))),
                   out_dtypes=(BF16, BF16), deps=token, name="ffn_up")
        h, hn3 = _mm(r, FW['ffn_w_down'][i], extras=(h,), rows=(row(W['norm_ple'][i]),), epilogue=add_and_norm,
                     out_dtypes=(F32, BF16), name="ffn_down")
        s.update(hn2=hn2, a=a, r=r, h2=h)
        if i % 2 == 1:
            token = advance(hn3)
        gt = _mm(hn3, FW['ple_w_gate'][i], deps=token, name="ple_gate")
        if i % 2 == 0:
            token = advance(gt)
        if i + 1 < depth:
            def gate_and_norm(acc, g_, res, gain):
                hh = res + _sigmoid(g_) * acc
                return acc, hh, _rmsn(hh, gain)

            pp, h, hn = _mm(p_bf[i], FW['ple_w_proj'][i], extras=(gt, h), rows=(row(W['norm_mix'][i + 1]),),
                            epilogue=gate_and_norm, out_dtypes=(F32, F32, BF16), deps=token, name="ple_proj")
        else:
            pp, h = _mm(p_bf[i], FW['ple_w_proj'][i], extras=(gt, h),
                        epilogue=lambda acc, g_, res: (acc, res + _sigmoid(g_) * acc), out_dtypes=(F32, F32),
                        name="ple_proj_last")
        s.update(hn3=hn3, gt=gt, pp=pp)
        saved.append(s)

    dh, loss_part = _loss_head(h, target, "loss_head")
    loss = lax.psum(loss_part, MESH_AXES)

    G = {n: [None] * W[n].shape[0] for n in REPLICATED}
    res = {}
    flying = []

    def shard3(n):
        shp = W[n].shape
        return shp[0], int(np.prod(shp[1:-1])), shp[-1]

    def by_owner(g):
        return g.reshape((N_DEV, g.shape[0] // N_DEV) + g.shape[1:])

    def send_grads(tag, grads):
        handle, token = _xchg_start([g for _, g in grads], True, cos, "grads_start_" + tag)
        flying.append((tag, [key for key, _ in grads], handle))
        return [token]

    def land_grads(after):
        tag, keys, handle = flying.pop(0)
        done = []
        for (n, l), full in zip(keys, _xchg_wait(handle, after, "grads_wait_" + tag)):
            dims = shard3(n)
            res[n], token = _adamw(full.reshape((N_DEV,) + dims[1:]), W[n].reshape(dims), M1[n].reshape(dims),
                                   M2[n].reshape(dims), res.get(n), l, "adamw_" + n)
            done.append(token)
        return done

    def start_small(names, tag, after):
        handle, token = _xchg_start([_pack([jnp.stack(G[n]) for n in names])], False, after, "small_start_" + tag)
        return (names, handle), [token]

    def land_small(pending_small, tag, after):
        names, handle = pending_small
        (parts,) = _xchg_wait(handle, after, "small_wait_" + tag)
        like = [W[n] for n in names]
        outs, _ = _adamw(parts, _pack(like)[None], _pack([M1[n] for n in names])[None],
                         _pack([M2[n] for n in names])[None], None, 0, "adamw_small_" + tag)
        unpacked = [_unpack(o, like) for o in outs]
        for idx, n in enumerate(names):
            res[n] = [unpacked[q][idx] for q in range(4)]

    spatial = ['gmlp_w_s', 'gmlp_b_s']
    token = []
    for i in reversed(range(depth)):
        j = i // 2
        s = saved[i]
        def ple_elem(d, g_, pq):
            sg = _sigmoid(g_)
            return d * sg, d * pq * sg * (1.0 - sg)

        d_pp, d_gt = _rowcall(ple_elem, [dh, s['gt'], s['pp']], [], [(d_model, BF16), (d_model, BF16)], name="ple_bwd")
        g_proj = _mm(p_bf[i], d_pp, ta=True, out_dtypes=(BF16,), name="ple_proj_dw")
        g_proj = jnp.transpose(g_proj.reshape(g_proj.shape[0], N_DEV, -1), (1, 0, 2))
        g_gate = _mm(s['hn3'], d_gt, ta=True, out_dtypes=(BF16,), name="ple_gate_dw")
        d_hn3 = _mm(d_gt, FW['ple_w_gate'][i], tb=True, out_dtypes=(BF16,), name="ple_gate_dx")
        dh, dh_bf, dg = _rms_bwd(s['h2'], d_hn3, dh, row(W['norm_ple'][i]), "rms_bwd", deps=token)
        G['norm_ple'][i] = dg[0]
        d_a = _mm(dh_bf, FW['ffn_w_down'][i], tb=True, extras=(s['a'],),
                  epilogue=lambda acc, a_: (acc * (2.0 * jnp.maximum(a_.astype(F32), 0.0)),), out_dtypes=(BF16,),
                  name="ffn_down_dx")
        g_down = _mm(s['r'], dh_bf, ta=True, out_dtypes=(BF16,), name="ffn_down_dw")
        g_up = _mm(s['hn2'], d_a, ta=True, out_blocks=N_DEV, out_dtypes=(BF16,), name="ffn_up_dw")
        token = send_grads("mlp%d" % i, [(('ple_w_proj', i), g_proj), (('ple_w_gate', i), by_owner(g_gate)),
                                         (('ffn_w_down', i), by_owner(g_down)), (('ffn_w_up', i), g_up)])
        if len(flying) > 1:
            token = token + land_grads(token[0])
        d_hn2 = _mm(d_a, FW['ffn_w_up'][i], tb=True, out_dtypes=(BF16,), name="ffn_up_dx")
        dh, dh_bf, dg = _rms_bwd(s['h1'], d_hn2, dh, row(W['norm_ffn'][i]), "rms_bwd", deps=token)
        G['norm_ffn'][i] = dg[0]
        if i % 2 == 0:
            d_o = _mm(dh_bf, FW['mla_w_out'][j], tb=True, out_dtypes=(BF16,), name="mla_out_dx")
            g_out = _mm(s['o'], dh_bf, ta=True, out_dtypes=(BF16,), name="mla_out_dw")
            dq, dk, dv = _attn_bwd(s['q'], s['k'], s['v'], s['o'], d_o, s['lse'], seq, "attn_bwd")
            d_q_raw, d_kv_raw, d_kr, g1, g2, g3, g4 = _prep2_bwd(
                s['q_raw'], s['kv_raw'], s['lat'], cos, sin, rmat, s['gains'], dq, dk, dv, "mla_prep2_bwd")
            G['mla_q_nope_g'][j], G['mla_q_rope_g'][j] = g1[0], g2[0]
            G['mla_k_nope_g'][j], G['mla_k_rope_g'][j] = g3[0], g4[0]
            g_uq = _mm(s['cq'], d_q_raw, ta=True, out_blocks=N_DEV, out_dtypes=(BF16,), name="mla_uq_dw")
            g_ukv = _mm(s['ckv'], d_kv_raw, ta=True, out_blocks=N_DEV, out_dtypes=(BF16,), name="mla_ukv_dw")
            d_cq = _mm(d_q_raw, FW['mla_w_uq'][j], tb=True, out_dtypes=(BF16,), name="mla_uq_dx")
            d_ckv = _mm(d_kv_raw, FW['mla_w_ukv'][j], tb=True, out_dtypes=(BF16,), name="mla_ukv_dx")
            d_lat, dga, dgb = _prep1_bwd(s['lat'], d_cq, d_ckv, d_kr, row(W['mla_q_lora_g'][j]),
                                         row(W['mla_kv_lora_g'][j]), "mla_prep1_bwd")
            G['mla_q_lora_g'][j], G['mla_kv_lora_g'][j] = dga[0], dgb[0]
            g_down = _mm(s['hn'], d_lat, ta=True, out_dtypes=(BF16,), name="mla_down_dw")
            grads = [(('mla_w_out', j), by_owner(g_out)), (('mla_w_uq', j), g_uq), (('mla_w_ukv', j), g_ukv),
                     (('mla_w_down', j), by_owner(g_down))]
            d_hn = _mm(d_lat, FW['mla_w_down'][j], tb=True, out_dtypes=(BF16,), name="mla_down_dx")
        else:
            d_y = _mm(dh_bf, FW['gmlp_w_out'][j], tb=True, out_dtypes=(BF16,), name="gmlp_out_dx")
            g_out = _mm(s['y'], dh_bf, ta=True, out_dtypes=(BF16,), name="gmlp_out_dw")
            d_u, d_vn, d_ws, d_bs = _sgu_bwd(s['u'], s['vn'], d_y, W['gmlp_w_s'][j], s['bs3'], "gmlp_sgu_bwd")
            G['gmlp_w_s'][j], G['gmlp_b_s'][j] = d_ws, d_bs[:, :, 0]
            d_z, d_lg, d_lb = _gelu_ln_bwd(s['z'], d_u, d_vn, row(FW['gmlp_ln_g'][j]), row(FW['gmlp_ln_b'][j]),
                                           "gmlp_gelu_ln_bwd")
            g_in = _mm(s['hn'], d_z, ta=True, out_blocks=N_DEV, out_dtypes=(BF16,), name="gmlp_in_dw")
            grads = [(('gmlp_w_out', j), by_owner(g_out)), (('gmlp_ln_g', j), by_owner(d_lg[0])),
                     (('gmlp_ln_b', j), by_owner(d_lb[0])), (('gmlp_w_in', j), g_in)]
            d_hn = _mm(d_z, FW['gmlp_w_in'][j], tb=True, out_dtypes=(BF16,), name="gmlp_in_dx")
        token = send_grads("mix%d" % i, grads)
        if len(flying) > 1:
            token = token + land_grads(token[0])
        dh, _, dg = _rms_bwd(s['h0'], d_hn, dh, row(W['norm_mix'][i]), "rms_bwd", deps=token)
        G['norm_mix'][i] = dg[0]
        token = []
        if i == 1:
            small_a, token = start_small(spatial, "spatial", dh)
    grad_x = dh.reshape(x.shape)

    small_b, _ = start_small([n for n in REPLICATED if n not in spatial], "gains", dh)
    while flying:
        land_grads(dh)
    land_small(small_a, "spatial", dh)
    land_small(small_b, "gains", dh)

    out = lambda q: [res[n][q].reshape(W[n].shape) for n in WEIGHTS]
    return (loss, grad_x, *out(0), *out(1), *out(2), *out(3))
```

```python
import math

import numpy as np
import jax
import jax.numpy as jnp
from jax import lax
from jax.experimental import pallas as pl
from jax.experimental.pallas import tpu as pltpu

F32 = jnp.float32
BF16 = jnp.bfloat16

N_DEV = 8
MESH_AXES = ("x", "y", "c")
HEADS = 8
NOPE = 128
ROPE = 64
VDIM = 128
QK = NOPE + ROPE
Q_LORA = 384
KV_LORA = 256
ROPE_BASE = 10000.0
CHUNK = 128
GROUPS = 8
EPS = 1e-6
LR, B1, B2, ADAM_EPS, WD, STEP = 0.001, 0.9, 0.999, 1e-08, 0.01, 10
LANE = 128
VMEM_LIMIT = 56 * 1024 * 1024
MM_VMEM_BUDGET = 40 * 1024 * 1024

WEIGHTS = ['norm_mix', 'norm_ffn', 'norm_ple', 'mla_w_down', 'mla_q_lora_g', 'mla_kv_lora_g', 'mla_w_uq',
           'mla_w_ukv', 'mla_q_nope_g', 'mla_q_rope_g', 'mla_k_nope_g', 'mla_k_rope_g', 'mla_w_out', 'gmlp_w_in',
           'gmlp_ln_g', 'gmlp_ln_b', 'gmlp_w_s', 'gmlp_b_s', 'gmlp_w_out', 'ffn_w_up', 'ffn_w_down', 'ple_w_gate',
           'ple_w_proj']
SHARD_AXIS = {'mla_w_down': 1, 'mla_w_uq': 2, 'mla_w_ukv': 2, 'mla_w_out': 1, 'gmlp_w_in': 2, 'gmlp_ln_g': 1,
              'gmlp_ln_b': 1, 'gmlp_w_out': 1, 'ffn_w_up': 2, 'ffn_w_down': 1, 'ple_w_gate': 1, 'ple_w_proj': 2}
SHARDED = list(SHARD_AXIS)
REPLICATED = [n for n in WEIGHTS if n not in SHARD_AXIS]
F32_PAYLOAD = ('gmlp_ln_g', 'gmlp_ln_b')


def _pick(dim, pref, align=LANE):
    if dim <= pref:
        return dim
    b = (pref // align) * align
    while b >= align:
        if dim % b == 0:
            return b
        b -= align
    return dim


def _params(sem):
    return pltpu.CompilerParams(dimension_semantics=sem, vmem_limit_bytes=VMEM_LIMIT)


def _mm(a, b, *, ta=False, tb=False, extras=(), rows=(), epilogue=None, out_dtypes=(F32,), out_blocks=None,
        acc_rows=0, deps=(), name, bn=1024):
    a3, b3 = a.ndim == 3, b.ndim == 3
    assert not (ta and a3)
    if ta:
        K, M = a.shape
        ka = K
    elif a3:
        M, ka = a.shape[1:]
        K = a.shape[0] * ka
    else:
        M, K = a.shape
        ka = K
    if tb:
        N, kb = b.shape[-2:]
        nb = N
        K2 = b.shape[0] * kb if b3 else kb
    else:
        kb, nb = b.shape[-2:]
        K2 = kb
        N = b.shape[0] * nb if b3 else nb
    assert K == K2, (a.shape, b.shape, ta, tb)
    no_ = N // out_blocks if out_blocks else N
    assert not (out_blocks and extras)
    size = lambda t: jnp.dtype(t).itemsize
    per_out = sum(size(e.dtype) for e in extras) + sum(size(t) for t in out_dtypes)
    bn = _pick(min(nb, no_), bn)
    assert not acc_rows or bn == N, "row sums are kept across row tiles only when one tile spans the columns"
    k_lim = min(ka, kb)
    fits = lambda m, k: 2 * (m * k * size(a.dtype) + k * bn * size(b.dtype) + m * bn * per_out) + 4 * m * bn
    ms = [m for m in sorted({min(M, c) for c in (2048, 1024, 512, 256)}, reverse=True) if M % m == 0]
    ks = [k for k in dict.fromkeys((k_lim, 2048, 1024, 512, 256)) if k <= k_lim and k_lim % k == 0]
    bm, bk = next(((m, k) for k in ks for m in ms if fits(m, k) <= MM_VMEM_BUDGET), (ms[-1], ks[-1]))
    nk = K // bk
    ne, no = len(extras) + len(rows), len(out_dtypes)
    first_out = 2 + ne + len(deps)
    dims = (((0,) if ta else (1,), (1,) if tb else (0,)), ((), ()))

    def finish(r, e_refs, o_refs):
        outs = epilogue(r, *[e[...] for e in e_refs]) if epilogue is not None else (r,)
        for o, v in zip(o_refs[:no], outs):
            o[...] = v.astype(o.dtype)
        for o, v in zip(o_refs[no:], outs[no:]):
            first = pl.program_id(0) == 0

            @pl.when(first)
            def _():
                o[...] = v

            @pl.when(jnp.logical_not(first))
            def _():
                o[...] += v

    def body(*refs):
        a_ref, b_ref = refs[0], refs[1]
        e_refs = refs[2:2 + ne]
        o_refs = refs[first_out:first_out + no + acc_rows]
        part = lax.dot_general(a_ref[...].astype(BF16), b_ref[...].astype(BF16), dims, preferred_element_type=F32)
        if nk == 1:
            finish(part, e_refs, o_refs)
            return
        acc = refs[-1]
        k = pl.program_id(2)

        @pl.when(k == 0)
        def _():
            acc[...] = part

        @pl.when(k > 0)
        def _():
            acc[...] += part

        @pl.when(k == nk - 1)
        def _():
            finish(acc[...], e_refs, o_refs)

    ka_t, kb_t, nb_t, no_t = ka // bk, kb // bk, nb // bn, no_ // bn
    if ta:
        a_spec = pl.BlockSpec((bk, bm), lambda i, j, k: (k, i))
    elif a3:
        a_spec = pl.BlockSpec((None, bm, bk), lambda i, j, k: (k // ka_t, i, k % ka_t))
    else:
        a_spec = pl.BlockSpec((bm, bk), lambda i, j, k: (i, k))
    if tb:
        b_spec = (pl.BlockSpec((None, bn, bk), lambda i, j, k: (k // kb_t, j, k % kb_t)) if b3 else
                  pl.BlockSpec((bn, bk), lambda i, j, k: (j, k)))
    else:
        b_spec = (pl.BlockSpec((None, bk, bn), lambda i, j, k: (j // nb_t, k, j % nb_t)) if b3 else
                  pl.BlockSpec((bk, bn), lambda i, j, k: (k, j)))
    if out_blocks:
        o_spec = lambda: pl.BlockSpec((None, bm, bn), lambda i, j, k: (j // no_t, i, j % no_t))
        o_shape = (out_blocks, M, no_)
    else:
        o_spec = lambda: pl.BlockSpec((bm, bn), lambda i, j, k: (i, j))
        o_shape = (M, N)
    outs = pl.pallas_call(
        body, name=name,
        grid=(M // bm, N // bn, nk),
        in_specs=[a_spec, b_spec] + [pl.BlockSpec((bm, bn), lambda i, j, k: (i, j)) for _ in extras]
        + [pl.BlockSpec((1, bn), lambda i, j, k: (0, j)) for _ in rows]
        + [pl.BlockSpec(memory_space=pl.ANY)] * len(deps),
        out_specs=[o_spec() for _ in out_dtypes] + [pl.BlockSpec((1, bn), lambda i, j, k: (0, j))] * acc_rows,
        out_shape=[jax.ShapeDtypeStruct(o_shape, dt) for dt in out_dtypes]
        + [jax.ShapeDtypeStruct((1, N), F32)] * acc_rows,
        scratch_shapes=[pltpu.VMEM((bm, bn), F32)] if nk > 1 else [],
        compiler_params=_params(("arbitrary",) * 3 if acc_rows else ("parallel", "parallel", "arbitrary")),
    )(a, b, *extras, *rows, *deps)
    return outs[0] if no + acc_rows == 1 else outs


def _rowcall(fn, rows, params, row_outs, acc_outs=(), *, bm=256, deps=(), name):
    T = rows[0].shape[0]
    bm = _pick(T, bm, 8)
    nr, npar, nro, nao = len(rows), len(params), len(row_outs), len(acc_outs)
    first_out = nr + npar + len(deps)

    def body(*refs):
        vals = [r[...] for r in refs[:nr + npar]]
        res = fn(*vals)
        ro = refs[first_out:first_out + nro]
        ao = refs[first_out + nro:]
        for r, v in zip(ro, res[:nro]):
            r[...] = v.astype(r.dtype)
        if nao:
            @pl.when(pl.program_id(0) == 0)
            def _():
                for r in ao:
                    r[...] = jnp.zeros_like(r)

            for r, v in zip(ao, res[nro:]):
                r[...] += v

    def whole(shape):
        nd = len(shape)
        return pl.BlockSpec(tuple(shape), lambda i: (0,) * nd)

    outs = pl.pallas_call(
        body, name=name,
        grid=(T // bm,),
        in_specs=[pl.BlockSpec((bm, r.shape[1]), lambda i: (i, 0)) for r in rows] + [whole(q.shape) for q in params]
        + [pl.BlockSpec(memory_space=pl.ANY)] * len(deps),
        out_specs=[pl.BlockSpec((bm, c), lambda i: (i, 0)) for c, _ in row_outs] + [whole(s) for s in acc_outs],
        out_shape=[jax.ShapeDtypeStruct((T, c), dt) for c, dt in row_outs]
        + [jax.ShapeDtypeStruct(tuple(s), F32) for s in acc_outs],
        compiler_params=_params(("arbitrary",) if nao else ("parallel",)),
    )(*rows, *params, *deps)
    return outs


def _rmsn(x, g):
    return x * lax.rsqrt(jnp.mean(x * x, axis=-1, keepdims=True) + EPS) * g


def _gelu(x):
    return 0.5 * x * (1.0 + jnp.tanh(math.sqrt(2.0 / math.pi) * (x + 0.044715 * (x * x * x))))


def _layer_norm(x, g, b):
    mu = jnp.mean(x, axis=-1, keepdims=True)
    xc = x - mu
    return xc * lax.rsqrt(jnp.mean(xc * xc, axis=-1, keepdims=True) + EPS) * g + b


def _sigmoid(x):
    return 1.0 / (1.0 + jnp.exp(-x))


def _rot(x, cos, sin, rmat):
    return x * cos + jnp.dot(x, rmat, precision=lax.Precision.HIGHEST, preferred_element_type=F32) * sin


def _rms_fwd(h, g, name, deps=()):
    return _rowcall(lambda x, gg: (_rmsn(x, gg),), [h], [g], [(h.shape[1], BF16)], bm=512, deps=deps, name=name)[0]


def _rms_bwd(h, d_hn, dh_in, g, name, deps=()):
    def fn(x, dy, dres, gg):
        _, vjp = jax.vjp(_rmsn, x, gg)
        dx, dg = vjp(dy.astype(F32))
        dh = dres + dx
        return dh, dh, dg

    d = h.shape[1]
    return _rowcall(fn, [h, d_hn, dh_in], [g], [(d, F32), (d, BF16)], [g.shape], bm=512, deps=deps, name=name)


def _rope_tables(pos, name):
    inv = np.float32(ROPE_BASE) ** (-(np.arange(0, ROPE, 2, dtype=np.float32) / np.float32(ROPE)))
    inv = jnp.asarray(np.concatenate([inv, inv])[None, :].astype(np.float32))

    def fn(pp, iv):
        ang = pp.astype(F32) * iv
        return jnp.cos(ang), jnp.sin(ang)

    return _rowcall(fn, [pos], [inv], [(ROPE, F32), (ROPE, F32)], name=name)


def _rot_matrix():
    r = np.zeros((ROPE, ROPE), np.float32)
    half = ROPE // 2
    for j in range(half):
        r[j + half, j] = -1.0
        r[j, j + half] = 1.0
    return jnp.asarray(r)


def _prep1_fwd(lat, gq, gkv, name):
    def fn(l, a, b):
        return _rmsn(l[:, :Q_LORA], a), _rmsn(l[:, Q_LORA:Q_LORA + KV_LORA], b)

    return _rowcall(fn, [lat], [gq, gkv], [(Q_LORA, BF16), (KV_LORA, BF16)], name=name)


def _prep1_bwd(lat, d_cq, d_ckv, d_kr, gq, gkv, name):
    def fn(l, dq, dkv, dkr, a, b):
        _, vq = jax.vjp(_rmsn, l[:, :Q_LORA], a)
        _, vkv = jax.vjp(_rmsn, l[:, Q_LORA:Q_LORA + KV_LORA], b)
        dxq, dga = vq(dq.astype(F32))
        dxkv, dgb = vkv(dkv.astype(F32))
        return jnp.concatenate([dxq, dxkv, dkr], axis=1), dga, dgb

    return _rowcall(fn, [lat, d_cq, d_ckv, d_kr], [gq, gkv], [(lat.shape[1], BF16)], [gq.shape, gkv.shape], name=name)


def _qk_fn(qn_raw, qr_raw, kn_raw, kr_raw, gqn, gqr, gkn, gkr, cos, sin, rmat):
    return (_rmsn(qn_raw, gqn), _rot(_rmsn(qr_raw, gqr), cos, sin, rmat),
            _rmsn(kn_raw, gkn), _rot(_rmsn(kr_raw, gkr), cos, sin, rmat))


def _prep2_fwd(q_raw, kv_raw, lat, cos, sin, rmat, gains, name, bm=1024):
    H, T, _ = q_raw.shape
    bm = _pick(T, bm, 8)
    kr0 = Q_LORA + KV_LORA

    def body(q_ref, kv_ref, lat_ref, cos_ref, sin_ref, r_ref, gqn, gqr, gkn, gkr, qo, ko, vo):
        qr, kvr = q_ref[...], kv_ref[...]
        qn, qro, kn, kro = _qk_fn(qr[:, :NOPE], qr[:, NOPE:], kvr[:, :NOPE], lat_ref[:, kr0:kr0 + ROPE],
                                  gqn[...], gqr[...], gkn[...], gkr[...], cos_ref[...], sin_ref[...], r_ref[...])
        qo[:, :NOPE] = qn.astype(BF16)
        qo[:, NOPE:] = qro.astype(BF16)
        ko[:, :NOPE] = kn.astype(BF16)
        ko[:, NOPE:] = kro.astype(BF16)
        vo[...] = kvr[:, NOPE:].astype(BF16)

    hb = lambda c: pl.BlockSpec((None, bm, c), lambda m, h: (h, m, 0))
    rb = lambda c: pl.BlockSpec((bm, c), lambda m, h: (m, 0))
    wb = lambda s: pl.BlockSpec(tuple(s), lambda m, h: (0, 0))
    return pl.pallas_call(
        body, name=name, grid=(T // bm, H),
        in_specs=[hb(QK), hb(NOPE + VDIM), rb(lat.shape[1]), rb(ROPE), rb(ROPE), wb(rmat.shape)]
        + [wb(g.shape) for g in gains],
        out_specs=[hb(QK), hb(QK), hb(VDIM)],
        out_shape=[jax.ShapeDtypeStruct((H, T, QK), BF16), jax.ShapeDtypeStruct((H, T, QK), BF16),
                   jax.ShapeDtypeStruct((H, T, VDIM), BF16)],
        compiler_params=_params(("parallel", "parallel")),
    )(q_raw, kv_raw, lat, cos, sin, rmat, *gains)


def _prep2_bwd(q_raw, kv_raw, lat, cos, sin, rmat, gains, dq, dk, dv, name, bm=512):
    H, T, _ = q_raw.shape
    bm = _pick(T, bm, 8)
    kr0 = Q_LORA + KV_LORA

    def body(q_ref, kv_ref, lat_ref, cos_ref, sin_ref, r_ref, gqn, gqr, gkn, gkr, dq_ref, dk_ref, dv_ref,
             dqo, dkvo, dkro, o_gqn, o_gqr, o_gkn, o_gkr):
        m, h = pl.program_id(0), pl.program_id(1)
        qr, kvr = q_ref[...], kv_ref[...]
        cos_v, sin_v, r_v = cos_ref[...], sin_ref[...], r_ref[...]
        f = lambda a, b, c, d, g1, g2, g3, g4: _qk_fn(a, b, c, d, g1, g2, g3, g4, cos_v, sin_v, r_v)
        _, vjp = jax.vjp(f, qr[:, :NOPE], qr[:, NOPE:], kvr[:, :NOPE], lat_ref[:, kr0:kr0 + ROPE],
                         gqn[...], gqr[...], gkn[...], gkr[...])
        dqv, dkv_ = dq_ref[...], dk_ref[...]
        d_qn, d_qr, d_kn, d_kr, g1, g2, g3, g4 = vjp((dqv[:, :NOPE], dqv[:, NOPE:], dkv_[:, :NOPE], dkv_[:, NOPE:]))
        dqo[:, :NOPE] = d_qn.astype(BF16)
        dqo[:, NOPE:] = d_qr.astype(BF16)
        dkvo[:, :NOPE] = d_kn.astype(BF16)
        dkvo[:, NOPE:] = dv_ref[...].astype(BF16)

        @pl.when(h == 0)
        def _():
            dkro[...] = jnp.zeros_like(dkro)

        dkro[...] += d_kr

        @pl.when((h == 0) & (m == 0))
        def _():
            for o in (o_gqn, o_gqr, o_gkn, o_gkr):
                o[...] = jnp.zeros_like(o)

        for o, g in zip((o_gqn, o_gqr, o_gkn, o_gkr), (g1, g2, g3, g4)):
            o[...] += g

    hb = lambda c: pl.BlockSpec((None, bm, c), lambda m, h: (h, m, 0))
    rb = lambda c: pl.BlockSpec((bm, c), lambda m, h: (m, 0))
    wb = lambda s: pl.BlockSpec(tuple(s), lambda m, h: (0, 0))
    return pl.pallas_call(
        body, name=name, grid=(T // bm, H),
        in_specs=[hb(QK), hb(NOPE + VDIM), rb(lat.shape[1]), rb(ROPE), rb(ROPE), wb(rmat.shape)]
        + [wb(g.shape) for g in gains] + [hb(QK), hb(QK), hb(VDIM)],
        out_specs=[hb(QK), hb(NOPE + VDIM), rb(ROPE)] + [wb(g.shape) for g in gains],
        out_shape=[jax.ShapeDtypeStruct((H, T, QK), BF16), jax.ShapeDtypeStruct((H, T, NOPE + VDIM), BF16),
                   jax.ShapeDtypeStruct((T, ROPE), F32)] + [jax.ShapeDtypeStruct(g.shape, F32) for g in gains],
        compiler_params=_params(("arbitrary", "arbitrary")),
    )(q_raw, kv_raw, lat, cos, sin, rmat, *gains, dq, dk, dv)


_NT = (((1,), (1,)), ((), ()))
_TN = (((0,), (0,)), ((), ()))


def _causal(blk):
    return lax.broadcasted_iota(jnp.int32, (blk, blk), 1) <= lax.broadcasted_iota(jnp.int32, (blk, blk), 0)


def _attn_fwd(q, k, v, seq, name, blk=512):
    H, T, _ = q.shape
    nb = T // seq
    blk = _pick(seq, blk)
    nq = seq // blk
    scale = float(QK) ** -0.5

    def body(q_ref, k_ref, v_ref, o_ref, lse_ref):
        qi = pl.program_id(2)
        qb = q_ref[...]

        def step(j, carry, diagonal):
            m, l, acc = carry
            ks = pl.ds(pl.multiple_of(j * blk, blk), blk)
            s = lax.dot_general(qb, k_ref[ks, :], _NT, preferred_element_type=F32) * scale
            if diagonal:
                s = jnp.where(_causal(blk), s, -jnp.inf)
            m_new = jnp.maximum(m, jnp.max(s, axis=1, keepdims=True))
            pr = jnp.exp(s - m_new)
            alpha = jnp.exp(m - m_new)
            l = alpha * l + jnp.sum(pr, axis=1, keepdims=True)
            acc = alpha * acc + jnp.dot(pr.astype(BF16), v_ref[ks, :], preferred_element_type=F32)
            return m_new, l, acc

        init = (jnp.full((blk, 1), -jnp.inf, F32), jnp.zeros((blk, 1), F32), jnp.zeros((blk, VDIM), F32))
        below = lax.fori_loop(0, qi, lambda j, c: step(j, c, False), init)
        m, l, acc = step(qi, below, True)
        o_ref[...] = (acc / l).astype(o_ref.dtype)
        lse_ref[...] = m + jnp.log(l)

    return pl.pallas_call(
        body, name=name, grid=(H, nb, nq),
        in_specs=[pl.BlockSpec((None, blk, QK), lambda h, b, i: (h, b * nq + i, 0)),
                  pl.BlockSpec((None, seq, QK), lambda h, b, i: (h, b, 0)),
                  pl.BlockSpec((None, seq, VDIM), lambda h, b, i: (h, b, 0))],
        out_specs=[pl.BlockSpec((blk, VDIM), lambda h, b, i: (b * nq + i, h)),
                   pl.BlockSpec((None, blk, 1), lambda h, b, i: (h, b * nq + i, 0))],
        out_shape=[jax.ShapeDtypeStruct((T, H * VDIM), BF16), jax.ShapeDtypeStruct((H, T, 1), F32)],
        compiler_params=_params(("parallel", "parallel", "parallel")),
    )(q, k, v)


def _attn_bwd(q, k, v, o, do, lse, seq, name, blk=512):
    H, T, _ = q.shape
    nb = T // seq
    blk = _pick(seq, blk)
    nq = seq // blk
    scale = float(QK) ** -0.5

    def body(q_ref, k_ref, v_ref, o_ref, do_ref, lse_ref, dq_ref, dk_ref, dv_ref):
        dk_ref[...] = jnp.zeros_like(dk_ref)
        dv_ref[...] = jnp.zeros_like(dv_ref)

        def qloop(i, carry):
            qs = pl.ds(pl.multiple_of(i * blk, blk), blk)
            qb = q_ref[qs, :]
            dob = do_ref[qs, :]
            dof = dob.astype(F32)
            lse_b = lse_ref[qs, :]
            delta = jnp.sum(dof * o_ref[qs, :].astype(F32), axis=1, keepdims=True)

            def kstep(j, dq_acc, diagonal):
                ks = pl.ds(pl.multiple_of(j * blk, blk), blk)
                kb = k_ref[ks, :]
                vb = v_ref[ks, :]
                s = lax.dot_general(qb, kb, _NT, preferred_element_type=F32) * scale
                pr = jnp.exp(s - lse_b)
                if diagonal:
                    pr = jnp.where(_causal(blk), pr, 0.0)
                dp = lax.dot_general(dob, vb, _NT, preferred_element_type=F32)
                ds = (pr * (dp - delta) * scale).astype(BF16)
                prb = pr.astype(BF16)
                dv_ref[ks, :] += lax.dot_general(prb, dob, _TN, preferred_element_type=F32)
                dk_ref[ks, :] += lax.dot_general(ds, qb, _TN, preferred_element_type=F32)
                return dq_acc + jnp.dot(ds, kb, preferred_element_type=F32)

            below = lax.fori_loop(0, i, lambda j, c: kstep(j, c, False), jnp.zeros((blk, QK), F32))
            dq_ref[qs, :] = kstep(i, below, True)
            return carry

        lax.fori_loop(0, nq, qloop, 0)

    hb = lambda c: pl.BlockSpec((None, seq, c), lambda h, b: (h, b, 0))
    cb = lambda: pl.BlockSpec((seq, VDIM), lambda h, b: (b, h))
    return pl.pallas_call(
        body, name=name, grid=(H, nb),
        in_specs=[hb(QK), hb(QK), hb(VDIM), cb(), cb(), hb(1)],
        out_specs=[hb(QK), hb(QK), hb(VDIM)],
        out_shape=[jax.ShapeDtypeStruct((H, T, QK), F32), jax.ShapeDtypeStruct((H, T, QK), F32),
                   jax.ShapeDtypeStruct((H, T, VDIM), F32)],
        compiler_params=_params(("parallel", "parallel")),
    )(q, k, v, o, do, lse)


def _gelu_ln_fwd(z, g, b, name):
    half = z.shape[1] // 2

    def fn(zz, gg, bb):
        return _gelu(zz[:, :half].astype(F32)), _layer_norm(_gelu(zz[:, half:].astype(F32)), gg, bb)

    return _rowcall(fn, [z], [g, b], [(half, BF16), (half, BF16)], name=name)


def _gelu_ln_bwd(z, d_u, d_vn, g, b, name):
    half = z.shape[1] // 2

    def gelu_and_slope(x):
        c, a = math.sqrt(2.0 / math.pi), 0.044715
        x2 = x * x
        t = jnp.tanh(c * x * (1.0 + a * x2))
        return 0.5 * x * (1.0 + t), 0.5 * (1.0 + t) + 0.5 * x * (1.0 - t * t) * (c * (1.0 + 3.0 * a * x2))

    def fn(zz, du, dvn, gg, bb):
        du, dvn = du.astype(F32), dvn.astype(F32)
        _, su = gelu_and_slope(zz[:, :half].astype(F32))
        v, sv = gelu_and_slope(zz[:, half:].astype(F32))
        xc = v - jnp.mean(v, axis=-1, keepdims=True)
        rstd = lax.rsqrt(jnp.mean(xc * xc, axis=-1, keepdims=True) + EPS)
        y = xc * rstd
        dy = dvn * gg
        dv = rstd * (dy - jnp.mean(dy, axis=-1, keepdims=True) - y * jnp.mean(dy * y, axis=-1, keepdims=True))
        dg = jnp.sum(dvn * y, axis=0, keepdims=True)
        db = jnp.sum(dvn, axis=0, keepdims=True)
        return jnp.concatenate([du * su, dv * sv], axis=1), dg, db

    return _rowcall(fn, [z, d_u, d_vn], [g, b], [(z.shape[1], BF16)], [g.shape, b.shape], bm=128, name=name)


def _tril_bf16(ws):
    t = lax.broadcasted_iota(jnp.int32, ws.shape, 0)
    s = lax.broadcasted_iota(jnp.int32, ws.shape, 1)
    return jnp.where(s <= t, ws, 0.0).astype(BF16)


def _sgu_fwd(u, vn, ws, bs, name, bm=2048):
    T, half = u.shape
    gd = half // GROUPS
    bm = _pick(T, bm, CHUNK)
    nc = bm // CHUNK

    def body(u_ref, vn_ref, ws_ref, bs_ref, y_ref):
        wm = _tril_bf16(ws_ref[...])
        bias = bs_ref[...]
        for c in range(nc):
            rs = slice(c * CHUNK, (c + 1) * CHUNK)
            sv = jnp.dot(wm, vn_ref[rs, :], preferred_element_type=F32) + bias
            y_ref[rs, :] = (u_ref[rs, :].astype(F32) * sv).astype(y_ref.dtype)

    tb = lambda: pl.BlockSpec((bm, gd), lambda g, i: (i, g))
    return pl.pallas_call(
        body, name=name, grid=(GROUPS, T // bm),
        in_specs=[tb(), tb(), pl.BlockSpec((None, CHUNK, CHUNK), lambda g, i: (g, 0, 0)),
                  pl.BlockSpec((None, CHUNK, 1), lambda g, i: (g, 0, 0))],
        out_specs=tb(),
        out_shape=jax.ShapeDtypeStruct((T, half), BF16),
        compiler_params=_params(("parallel", "parallel")),
    )(u, vn, ws, bs)


def _sgu_bwd(u, vn, dy, ws, bs, name, bm=1024):
    T, half = u.shape
    gd = half // GROUPS
    bm = _pick(T, bm, CHUNK)
    nc = bm // CHUNK

    def body(u_ref, vn_ref, dy_ref, ws_ref, bs_ref, du_ref, dvn_ref, dws_ref, dbs_ref):
        @pl.when(pl.program_id(1) == 0)
        def _():
            dws_ref[...] = jnp.zeros_like(dws_ref)
            dbs_ref[...] = jnp.zeros_like(dbs_ref)

        wm = _tril_bf16(ws_ref[...])
        bias = bs_ref[...]
        dws = jnp.zeros((CHUNK, CHUNK), F32)
        dbs = jnp.zeros((CHUNK, 1), F32)
        for c in range(nc):
            rs = slice(c * CHUNK, (c + 1) * CHUNK)
            vb = vn_ref[rs, :]
            dyb = dy_ref[rs, :].astype(F32)
            sv = jnp.dot(wm, vb, preferred_element_type=F32) + bias
            du_ref[rs, :] = (dyb * sv).astype(du_ref.dtype)
            dsv = dyb * u_ref[rs, :].astype(F32)
            dsb = dsv.astype(BF16)
            dvn_ref[rs, :] = lax.dot_general(wm, dsb, _TN, preferred_element_type=F32).astype(dvn_ref.dtype)
            dws = dws + lax.dot_general(dsb, vb, _NT, preferred_element_type=F32)
            dbs = dbs + jnp.sum(dsv, axis=1, keepdims=True)
        t = lax.broadcasted_iota(jnp.int32, (CHUNK, CHUNK), 0)
        s = lax.broadcasted_iota(jnp.int32, (CHUNK, CHUNK), 1)
        dws_ref[...] += jnp.where(s <= t, dws, 0.0)
        dbs_ref[...] += dbs

    tb = lambda: pl.BlockSpec((bm, gd), lambda g, i: (i, g))
    wsb = lambda: pl.BlockSpec((None, CHUNK, CHUNK), lambda g, i: (g, 0, 0))
    bsb = lambda: pl.BlockSpec((None, CHUNK, 1), lambda g, i: (g, 0, 0))
    return pl.pallas_call(
        body, name=name, grid=(GROUPS, T // bm),
        in_specs=[tb(), tb(), tb(), wsb(), bsb()],
        out_specs=[tb(), tb(), wsb(), bsb()],
        out_shape=[jax.ShapeDtypeStruct((T, half), BF16), jax.ShapeDtypeStruct((T, half), BF16),
                   jax.ShapeDtypeStruct(ws.shape, F32), jax.ShapeDtypeStruct(bs.shape, F32)],
        compiler_params=_params(("parallel", "arbitrary")),
    )(u, vn, dy, ws, bs)


def _loss_head(y, t, name):
    d_model = y.shape[1]

    def fn(yy, tt):
        d = yy - tt
        part = 0.5 * jnp.sum(jnp.mean(d * d, axis=-1, keepdims=True), axis=0, keepdims=True)
        return d / d_model, jnp.zeros((1, LANE), F32) + part

    dy, part = _rowcall(fn, [y, t], [], [(d_model, F32)], [(1, LANE)], name=name)
    return dy, part[0, 0]


def _adamw(parts, w, m, v, prev, layer, name):
    L, R, C = w.shape
    br = _pick(R, max(8, (128 * 1024) // C // 8 * 8), 8)
    c1 = 1.0 - B1 ** STEP
    c2 = 1.0 - B2 ** STEP
    if prev is None:
        prev = [lax.empty(w.shape, F32) for _ in range(4)]

    def body(p_ref, w_ref, m_ref, v_ref, a0, a1, a2, a3, g_o, d_o, m_o, v_o, token):
        g = p_ref[0].astype(F32)
        for d in range(1, N_DEV):
            g = g + p_ref[d].astype(F32)
        mn = B1 * m_ref[...] + (1.0 - B1) * g
        vn = B2 * v_ref[...] + (1.0 - B2) * (g * g)
        g_o[...] = g
        m_o[...] = mn
        v_o[...] = vn
        d_o[...] = -LR * ((mn / c1) / (jnp.sqrt(vn / c2) + ADAM_EPS) + WD * w_ref[...])
        token[...] = jnp.zeros_like(token)

    blk = lambda: pl.BlockSpec((None, br, C), lambda i: (layer, i, 0))
    anywhere = pl.BlockSpec(memory_space=pl.ANY)
    outs = pl.pallas_call(
        body, name=name, grid=(R // br,),
        in_specs=[pl.BlockSpec((N_DEV, br, C), lambda i: (0, i, 0)), blk(), blk(), blk()] + [anywhere] * 4,
        out_specs=[blk(), blk(), blk(), blk(), pl.BlockSpec((8, LANE), lambda i: (0, 0))],
        out_shape=[jax.ShapeDtypeStruct((L, R, C), F32)] * 4 + [jax.ShapeDtypeStruct((8, LANE), F32)],
        input_output_aliases={4: 0, 5: 1, 6: 2, 7: 3},
        compiler_params=_params(("arbitrary",)),
    )(parts, w, m, v, *prev)
    return list(outs[:4]), outs[4]


def _mesh_pos():
    return lax.axis_index("x"), lax.axis_index("y"), lax.axis_index("c")


def _flip(pos, k):
    x, y, c = pos
    px = 1 - x if k & 4 else x
    py = 1 - y if k & 2 else y
    pc = 1 - c if k & 1 else c
    return px, py, pc


HBM_SPEC = pl.BlockSpec(memory_space=pltpu.HBM)
SEM_SPEC = pl.BlockSpec(memory_space=pltpu.SEMAPHORE)
EFFECT = pltpu.SideEffectType.DATAFLOW_SIDE_EFFECTING


def _hbm(a):
    return pltpu.with_memory_space_constraint(a, pltpu.HBM)


def _device_index():
    x, y, c = _mesh_pos()
    return 4 * x + 2 * y + c


def _peer_copy(src, land, send, recv, a, k, pos, scatter):
    peer = _flip(pos, k)
    me = 4 * pos[0] + 2 * pos[1] + pos[2]
    piece = src.at[4 * peer[0] + 2 * peer[1] + peer[2]] if scatter else src
    return pltpu.make_async_remote_copy(
        src_ref=piece, dst_ref=land.at[me], send_sem=send.at[7 * a + k - 1], recv_sem=recv.at[7 * a + k - 1],
        device_id=peer, device_id_type=pl.DeviceIdType.MESH)


def _own_copy(src, land, own, a, pos, scatter):
    me = 4 * pos[0] + 2 * pos[1] + pos[2]
    return pltpu.make_async_copy(src.at[me] if scatter else src, land.at[me], own.at[a])


def _xchg_start(srcs, scatter, after, name):
    n = len(srcs)

    def body(*refs):
        src, land = refs[:n], refs[n:2 * n]
        send, recv, own, token = refs[2 * n + 1], refs[2 * n + 2], refs[2 * n + 3], refs[-1]
        pos = _mesh_pos()
        for k in range(1, N_DEV):
            for a in range(n):
                _peer_copy(src[a], land[a], send, recv, a, k, pos, scatter).start()
        for a in range(n):
            _own_copy(src[a], land[a], own, a, pos, scatter).start()
        token[...] = jnp.zeros_like(token)

    lands = [lax.empty(s.shape if scatter else (N_DEV,) + s.shape, s.dtype) for s in srcs]
    outs = pl.pallas_call(
        body, name=name,
        out_shape=(pltpu.SemaphoreType.DMA((7 * n,)), pltpu.SemaphoreType.DMA((7 * n,)), pltpu.SemaphoreType.DMA((n,)),
                   *[pltpu.HBM(s.shape, s.dtype) for s in srcs], *[pltpu.HBM(l.shape, l.dtype) for l in lands],
                   jax.ShapeDtypeStruct((8, LANE), F32)),
        in_specs=[HBM_SPEC] * (2 * n) + [pl.BlockSpec(memory_space=pl.ANY)],
        out_specs=(SEM_SPEC, SEM_SPEC, SEM_SPEC, *[HBM_SPEC] * (2 * n), pl.BlockSpec(memory_space=pltpu.VMEM)),
        input_output_aliases={q: 3 + q for q in range(2 * n)},
        compiler_params=pltpu.CompilerParams(has_side_effects=EFFECT),
    )(*[_hbm(s) for s in srcs], *[_hbm(l) for l in lands], after)
    handle = dict(send=outs[0], recv=outs[1], own=outs[2], srcs=list(outs[3:3 + n]),
                  lands=list(outs[3 + n:3 + 2 * n]), scatter=scatter)
    return handle, outs[-1]


def _xchg_wait(handle, after, name):
    srcs, lands, scatter = handle['srcs'], handle['lands'], handle['scatter']
    n = len(srcs)

    def body(*refs):
        src, land = refs[:n], refs[n:2 * n]
        send, recv, own = refs[2 * n], refs[2 * n + 1], refs[2 * n + 2]
        pos = _mesh_pos()
        for k in range(1, N_DEV):
            for a in range(n):
                cp = _peer_copy(src[a], land[a], send, recv, a, k, pos, scatter)
                cp.wait_send()
                cp.wait_recv()
        for a in range(n):
            _own_copy(src[a], land[a], own, a, pos, scatter).wait()

    outs = pl.pallas_call(
        body, name=name,
        out_shape=[pltpu.HBM(s.shape, s.dtype) for s in srcs] + [pltpu.HBM(l.shape, l.dtype) for l in lands],
        in_specs=[HBM_SPEC] * (2 * n) + [SEM_SPEC, SEM_SPEC, SEM_SPEC, pl.BlockSpec(memory_space=pl.ANY)],
        out_specs=[HBM_SPEC] * (2 * n),
        input_output_aliases={q: q for q in range(2 * n)},
        compiler_params=pltpu.CompilerParams(has_side_effects=EFFECT),
    )(*srcs, *lands, handle['send'], handle['recv'], handle['own'], after)
    return list(outs[n:])


def _pack(parts):
    flat = jnp.concatenate([q.reshape(-1) for q in parts])
    pad = (-flat.shape[0]) % (8 * LANE)
    return jnp.pad(flat, (0, pad)).reshape(-1, LANE)


def _unpack(packed, like):
    flat = packed.reshape(-1)
    out, o = [], 0
    for q in like:
        out.append(flat[o:o + q.size].reshape(q.shape))
        o += q.size
    return out


def kernel(x, p, positions, norm_mix, norm_ffn, norm_ple, mla_w_down, mla_q_lora_g, mla_kv_lora_g, mla_w_uq, mla_w_ukv, mla_q_nope_g, mla_q_rope_g, mla_k_nope_g, mla_k_rope_g, mla_w_out, gmlp_w_in, gmlp_ln_g, gmlp_ln_b, gmlp_w_s, gmlp_b_s, gmlp_w_out, ffn_w_up, ffn_w_down, ple_w_gate, ple_w_proj, loss_target, m_norm_mix, m_norm_ffn, m_norm_ple, m_mla_w_down, m_mla_q_lora_g, m_mla_kv_lora_g, m_mla_w_uq, m_mla_w_ukv, m_mla_q_nope_g, m_mla_q_rope_g, m_mla_k_nope_g, m_mla_k_rope_g, m_mla_w_out, m_gmlp_w_in, m_gmlp_ln_g, m_gmlp_ln_b, m_gmlp_w_s, m_gmlp_b_s, m_gmlp_w_out, m_ffn_w_up, m_ffn_w_down, m_ple_w_gate, m_ple_w_proj, v_norm_mix, v_norm_ffn, v_norm_ple, v_mla_w_down, v_mla_q_lora_g, v_mla_kv_lora_g, v_mla_w_uq, v_mla_w_ukv, v_mla_q_nope_g, v_mla_q_rope_g, v_mla_k_nope_g, v_mla_k_rope_g, v_mla_w_out, v_gmlp_w_in, v_gmlp_ln_g, v_gmlp_ln_b, v_gmlp_w_s, v_gmlp_b_s, v_gmlp_w_out, v_ffn_w_up, v_ffn_w_down, v_ple_w_gate, v_ple_w_proj):
    W = dict(zip(WEIGHTS, (norm_mix, norm_ffn, norm_ple, mla_w_down, mla_q_lora_g, mla_kv_lora_g, mla_w_uq, mla_w_ukv, mla_q_nope_g, mla_q_rope_g, mla_k_nope_g, mla_k_rope_g, mla_w_out, gmlp_w_in, gmlp_ln_g, gmlp_ln_b, gmlp_w_s, gmlp_b_s, gmlp_w_out, ffn_w_up, ffn_w_down, ple_w_gate, ple_w_proj)))
    M1 = dict(zip(WEIGHTS, (m_norm_mix, m_norm_ffn, m_norm_ple, m_mla_w_down, m_mla_q_lora_g, m_mla_kv_lora_g, m_mla_w_uq, m_mla_w_ukv, m_mla_q_nope_g, m_mla_q_rope_g, m_mla_k_nope_g, m_mla_k_rope_g, m_mla_w_out, m_gmlp_w_in, m_gmlp_ln_g, m_gmlp_ln_b, m_gmlp_w_s, m_gmlp_b_s, m_gmlp_w_out, m_ffn_w_up, m_ffn_w_down, m_ple_w_gate, m_ple_w_proj)))
    M2 = dict(zip(WEIGHTS, (v_norm_mix, v_norm_ffn, v_norm_ple, v_mla_w_down, v_mla_q_lora_g, v_mla_kv_lora_g, v_mla_w_uq, v_mla_w_ukv, v_mla_q_nope_g, v_mla_q_rope_g, v_mla_k_nope_g, v_mla_k_rope_g, v_mla_w_out, v_gmlp_w_in, v_gmlp_ln_g, v_gmlp_ln_b, v_gmlp_w_s, v_gmlp_b_s, v_gmlp_w_out, v_ffn_w_up, v_ffn_w_down, v_ple_w_gate, v_ple_w_proj)))

    nb, seq, d_model = x.shape
    assert d_model <= 1024, "the rms norms fused into matmul epilogues need whole rows in one output tile"
    T = nb * seq
    depth = norm_mix.shape[0]
    h = x.reshape(T, d_model)
    target = loss_target.reshape(T, d_model)
    p_bf = p.reshape(depth, T, p.shape[-1]).astype(BF16)

    stages, carried = [], []
    for i in range(depth):
        gate = [(n, i) for n in ('ple_w_gate', 'ple_w_proj')]
        mlp = [(n, i) for n in ('ffn_w_up', 'ffn_w_down')]
        if i % 2 == 0:
            mixer = [(n, i // 2) for n in ('mla_w_down', 'mla_w_uq', 'mla_w_ukv', 'mla_w_out')]
            stages += [mixer[:1], mixer[1:]] if i == 0 else [carried + mixer]
            stages.append(mlp + gate)
            carried = []
        else:
            stages.append(carried + [(n, i // 2) for n in ('gmlp_w_in', 'gmlp_ln_g', 'gmlp_ln_b', 'gmlp_w_out')])
            stages.append(mlp)
            carried = gate
    if carried:
        stages.append(carried)

    FW = {n: {} for n in SHARDED}

    def start_weights(st, after):
        keys = stages[st]
        srcs = [W[n][l] if n in F32_PAYLOAD else W[n][l].astype(BF16) for n, l in keys]
        handle, token = _xchg_start(srcs, False, after, "weights_start%d" % st)
        return (keys, handle), [token]

    def wait_weights(pending, st, after):
        keys, handle = pending
        landed = _xchg_wait(handle, after, "weights_wait%d" % st)
        for (n, l), full in zip(keys, landed):
            if SHARD_AXIS[n] == 1:
                FW[n][l] = full.reshape((-1,) + full.shape[2:])
            elif n in ('mla_w_uq', 'mla_w_ukv'):
                FW[n][l] = full
            else:
                FW[n][l] = jnp.transpose(full, (1, 0, 2)).reshape(full.shape[1], -1)
        return landed[0]

    row = lambda a: a.reshape(1, -1)
    cos, sin = _rope_tables(positions.reshape(T, 1), "rope_tables")
    rmat = _rot_matrix()

    def add_and_norm(acc, res, g):
        hh = res + acc
        return hh, _rmsn(hh, g)

    saved = []
    chain = {'stage': 0}
    chain['pending'], _ = start_weights(0, h)

    def advance(after):
        st = chain['stage']
        if st >= len(stages):
            return []
        landed = wait_weights(chain['pending'], st, after)
        chain['stage'] = st + 1
        if st + 1 >= len(stages):
            return []
        chain['pending'], token = start_weights(st + 1, landed)
        return token

    hn = _rms_fwd(h, row(W['norm_mix'][0]), "rms_fwd")
    token = advance(hn)
    for i in range(depth):
        j = i // 2
        s = {}
        s['h0'] = h
        s['hn'] = hn
        if i % 2 == 0:
            gains = [row(W['mla_q_nope_g'][j]), row(W['mla_q_rope_g'][j]), row(W['mla_k_nope_g'][j]),
                     row(W['mla_k_rope_g'][j])]
            lat = _mm(hn, FW['mla_w_down'][j], deps=token, name="mla_down")
            if i == 0:
                token = advance(lat)
            cq, ckv = _prep1_fwd(lat, row(W['mla_q_lora_g'][j]), row(W['mla_kv_lora_g'][j]), "mla_prep1")
            q_raw = _mm(cq, FW['mla_w_uq'][j], out_blocks=HEADS, deps=token, name="mla_uq")
            kv_raw = _mm(ckv, FW['mla_w_ukv'][j], out_blocks=HEADS, name="mla_ukv")
            q, k, v = _prep2_fwd(q_raw, kv_raw, lat, cos, sin, rmat, gains, "mla_prep2")
            o, lse = _attn_fwd(q, k, v, seq, "attn_fwd")
            token = advance(o)
            h, hn2 = _mm(o, FW['mla_w_out'][j], extras=(h,), rows=(row(W['norm_ffn'][i]),), epilogue=add_and_norm,
                         out_dtypes=(F32, BF16), deps=token, name="mla_out")
            s.update(lat=lat, cq=cq, ckv=ckv, q_raw=q_raw, kv_raw=kv_raw, q=q, k=k, v=v, o=o, lse=lse, gains=gains)
        else:
            z = _mm(hn, FW['gmlp_w_in'][j], out_dtypes=(BF16,), name="gmlp_in")
            u, vn = _gelu_ln_fwd(z, row(FW['gmlp_ln_g'][j]), row(FW['gmlp_ln_b'][j]), "gmlp_gelu_ln")
            bs3 = W['gmlp_b_s'][j][:, :, None]
            y = _sgu_fwd(u, vn, W['gmlp_w_s'][j], bs3, "gmlp_sgu")
            h, hn2 = _mm(y, FW['gmlp_w_out'][j], extras=(h,), rows=(row(W['norm_ffn'][i]),), epilogue=add_and_norm,
                         out_dtypes=(F32, BF16), name="gmlp_out")
            token = advance(hn2)
            s.update(z=z, u=u, vn=vn, y=y, bs3=bs3)
        s['h1'] = h
        a, r = _mm(hn2, FW['ffn_w_up'][i], epilogue=lambda acc: (acc, jnp.square(jnp.maximum(acc, 0.0))),
                   out_dtypes=(BF16, BF16), deps=token, name="ffn_up")
        h, hn3 = _mm(r, FW['ffn_w_down'][i], extras=(h,), rows=(row(W['norm_ple'][i]),), epilogue=add_and_norm,
                     out_dtypes=(F32, BF16), name="ffn_down")
        s.update(hn2=hn2, a=a, r=r, h2=h)
        if i % 2 == 1:
            token = advance(hn3)
        gt = _mm(hn3, FW['ple_w_gate'][i], deps=token, name="ple_gate")
        if i % 2 == 0:
            token = advance(gt)
        if i + 1 < depth:
            def gate_and_norm(acc, g_, res, gain):
                hh = res + _sigmoid(g_) * acc
                return acc, hh, _rmsn(hh, gain)

            pp, h, hn = _mm(p_bf[i], FW['ple_w_proj'][i], extras=(gt, h), rows=(row(W['norm_mix'][i + 1]),),
                            epilogue=gate_and_norm, out_dtypes=(F32, F32, BF16), deps=token, name="ple_proj")
        else:
            pp, h = _mm(p_bf[i], FW['ple_w_proj'][i], extras=(gt, h),
                        epilogue=lambda acc, g_, res: (acc, res + _sigmoid(g_) * acc), out_dtypes=(F32, F32),
                        name="ple_proj_last")
        s.update(hn3=hn3, gt=gt, pp=pp)
        saved.append(s)

    dh, loss_part = _loss_head(h, target, "loss_head")
    loss = lax.psum(loss_part, MESH_AXES)

    G = {n: [None] * W[n].shape[0] for n in REPLICATED}
    res = {}
    flying = []

    def shard3(n):
        shp = W[n].shape
        return shp[0], int(np.prod(shp[1:-1])), shp[-1]

    def by_owner(g):
        return g.reshape((N_DEV, g.shape[0] // N_DEV) + g.shape[1:])

    def norm_back(h_in, dh_in, gain):
        def epilogue(acc, x, dres, g):
            _, vjp = jax.vjp(_rmsn, x, g)
            dx, dg = vjp(acc)
            dh = dres + dx
            return dh, dh, dg

        return dict(extras=(h_in, dh_in), rows=(gain,), epilogue=epilogue, out_dtypes=(F32, BF16), acc_rows=1)

    def send_grads(tag, grads):
        handle, token = _xchg_start([g for _, g in grads], True, cos, "grads_start_" + tag)
        flying.append((tag, [key for key, _ in grads], handle))
        return [token]

    def land_grads(after):
        tag, keys, handle = flying.pop(0)
        done = []
        for (n, l), full in zip(keys, _xchg_wait(handle, after, "grads_wait_" + tag)):
            dims = shard3(n)
            res[n], token = _adamw(full.reshape((N_DEV,) + dims[1:]), W[n].reshape(dims), M1[n].reshape(dims),
                                   M2[n].reshape(dims), res.get(n), l, "adamw_" + n)
            done.append(token)
        return done

    def start_small(names, tag, after):
        handle, token = _xchg_start([_pack([jnp.stack(G[n]) for n in names])], False, after, "small_start_" + tag)
        return (names, handle), [token]

    def land_small(pending_small, tag, after):
        names, handle = pending_small
        (parts,) = _xchg_wait(handle, after, "small_wait_" + tag)
        like = [W[n] for n in names]
        outs, _ = _adamw(parts, _pack(like)[None], _pack([M1[n] for n in names])[None],
                         _pack([M2[n] for n in names])[None], None, 0, "adamw_small_" + tag)
        unpacked = [_unpack(o, like) for o in outs]
        for idx, n in enumerate(names):
            res[n] = [unpacked[q][idx] for q in range(4)]

    spatial = ['gmlp_w_s', 'gmlp_b_s']
    token = []
    for i in reversed(range(depth)):
        j = i // 2
        s = saved[i]
        def ple_elem(d, g_, pq):
            sg = _sigmoid(g_)
            return d * sg, d * pq * sg * (1.0 - sg)

        d_pp, d_gt = _rowcall(ple_elem, [dh, s['gt'], s['pp']], [], [(d_model, BF16), (d_model, BF16)], name="ple_bwd")
        g_proj = _mm(p_bf[i], d_pp, ta=True, out_dtypes=(BF16,), name="ple_proj_dw")
        g_proj = jnp.transpose(g_proj.reshape(g_proj.shape[0], N_DEV, -1), (1, 0, 2))
        g_gate = _mm(s['hn3'], d_gt, ta=True, out_dtypes=(BF16,), name="ple_gate_dw")
        dh, dh_bf, dg = _mm(d_gt, FW['ple_w_gate'][i], tb=True, deps=token, name="ple_gate_dx",
                            **norm_back(s['h2'], dh, row(W['norm_ple'][i])))
        G['norm_ple'][i] = dg[0]
        d_a = _mm(dh_bf, FW['ffn_w_down'][i], tb=True, extras=(s['a'],),
                  epilogue=lambda acc, a_: (acc * (2.0 * jnp.maximum(a_.astype(F32), 0.0)),), out_dtypes=(BF16,),
                  name="ffn_down_dx")
        g_down = _mm(s['r'], dh_bf, ta=True, out_dtypes=(BF16,), name="ffn_down_dw")
        g_up = _mm(s['hn2'], d_a, ta=True, out_blocks=N_DEV, out_dtypes=(BF16,), name="ffn_up_dw")
        token = send_grads("mlp%d" % i, [(('ple_w_proj', i), g_proj), (('ple_w_gate', i), by_owner(g_gate)),
                                         (('ffn_w_down', i), by_owner(g_down)), (('ffn_w_up', i), g_up)])
        if len(flying) > 1:
            token = token + land_grads(token[0])
        dh, dh_bf, dg = _mm(d_a, FW['ffn_w_up'][i], tb=True, deps=token, name="ffn_up_dx",
                            **norm_back(s['h1'], dh, row(W['norm_ffn'][i])))
        G['norm_ffn'][i] = dg[0]
        if i % 2 == 0:
            d_o = _mm(dh_bf, FW['mla_w_out'][j], tb=True, out_dtypes=(BF16,), name="mla_out_dx")
            g_out = _mm(s['o'], dh_bf, ta=True, out_dtypes=(BF16,), name="mla_out_dw")
            dq, dk, dv = _attn_bwd(s['q'], s['k'], s['v'], s['o'], d_o, s['lse'], seq, "attn_bwd")
            d_q_raw, d_kv_raw, d_kr, g1, g2, g3, g4 = _prep2_bwd(
                s['q_raw'], s['kv_raw'], s['lat'], cos, sin, rmat, s['gains'], dq, dk, dv, "mla_prep2_bwd")
            G['mla_q_nope_g'][j], G['mla_q_rope_g'][j] = g1[0], g2[0]
            G['mla_k_nope_g'][j], G['mla_k_rope_g'][j] = g3[0], g4[0]
            g_uq = _mm(s['cq'], d_q_raw, ta=True, out_blocks=N_DEV, out_dtypes=(BF16,), name="mla_uq_dw")
            g_ukv = _mm(s['ckv'], d_kv_raw, ta=True, out_blocks=N_DEV, out_dtypes=(BF16,), name="mla_ukv_dw")
            d_cq = _mm(d_q_raw, FW['mla_w_uq'][j], tb=True, out_dtypes=(BF16,), name="mla_uq_dx")
            d_ckv = _mm(d_kv_raw, FW['mla_w_ukv'][j], tb=True, out_dtypes=(BF16,), name="mla_ukv_dx")
            d_lat, dga, dgb = _prep1_bwd(s['lat'], d_cq, d_ckv, d_kr, row(W['mla_q_lora_g'][j]),
                                         row(W['mla_kv_lora_g'][j]), "mla_prep1_bwd")
            G['mla_q_lora_g'][j], G['mla_kv_lora_g'][j] = dga[0], dgb[0]
            g_down = _mm(s['hn'], d_lat, ta=True, out_dtypes=(BF16,), name="mla_down_dw")
            grads = [(('mla_w_out', j), by_owner(g_out)), (('mla_w_uq', j), g_uq), (('mla_w_ukv', j), g_ukv),
                     (('mla_w_down', j), by_owner(g_down))]
            last = (d_lat, FW['mla_w_down'][j], "mla_down_dx")
        else:
            d_y = _mm(dh_bf, FW['gmlp_w_out'][j], tb=True, out_dtypes=(BF16,), name="gmlp_out_dx")
            g_out = _mm(s['y'], dh_bf, ta=True, out_dtypes=(BF16,), name="gmlp_out_dw")
            d_u, d_vn, d_ws, d_bs = _sgu_bwd(s['u'], s['vn'], d_y, W['gmlp_w_s'][j], s['bs3'], "gmlp_sgu_bwd")
            G['gmlp_w_s'][j], G['gmlp_b_s'][j] = d_ws, d_bs[:, :, 0]
            d_z, d_lg, d_lb = _gelu_ln_bwd(s['z'], d_u, d_vn, row(FW['gmlp_ln_g'][j]), row(FW['gmlp_ln_b'][j]),
                                           "gmlp_gelu_ln_bwd")
            g_in = _mm(s['hn'], d_z, ta=True, out_blocks=N_DEV, out_dtypes=(BF16,), name="gmlp_in_dw")
            grads = [(('gmlp_w_out', j), by_owner(g_out)), (('gmlp_ln_g', j), by_owner(d_lg[0])),
                     (('gmlp_ln_b', j), by_owner(d_lb[0])), (('gmlp_w_in', j), g_in)]
            last = (d_z, FW['gmlp_w_in'][j], "gmlp_in_dx")
        token = send_grads("mix%d" % i, grads)
        if len(flying) > 1:
            token = token + land_grads(token[0])
        dh, _, dg = _mm(last[0], last[1], tb=True, deps=token, name=last[2],
                        **norm_back(s['h0'], dh, row(W['norm_mix'][i])))
        G['norm_mix'][i] = dg[0]
        token = []
        if i == 1:
            small_a, token = start_small(spatial, "spatial", dh)
    grad_x = dh.reshape(x.shape)

    small_b, _ = start_small([n for n in REPLICATED if n not in spatial], "gains", dh)
    while flying:
        land_grads(dh)
    land_small(small_a, "spatial", dh)
    land_small(small_b, "gains", dh)

    out = lambda q: [res[n][q].reshape(W[n].shape) for n in WEIGHTS]
    return (loss, grad_x, *out(0), *out(1), *out(2), *out(3))
```

```python
import math

import numpy as np
import jax
import jax.numpy as jnp
from jax import lax
from jax.experimental import pallas as pl
from jax.experimental.pallas import tpu as pltpu

F32 = jnp.float32
BF16 = jnp.bfloat16

N_DEV = 8
MESH_AXES = ("x", "y", "c")
HEADS = 8
NOPE = 128
ROPE = 64
VDIM = 128
QK = NOPE + ROPE
Q_LORA = 384
KV_LORA = 256
ROPE_BASE = 10000.0
CHUNK = 128
GROUPS = 8
EPS = 1e-6
LR, B1, B2, ADAM_EPS, WD, STEP = 0.001, 0.9, 0.999, 1e-08, 0.01, 10
LANE = 128
VMEM_LIMIT = 56 * 1024 * 1024
MM_VMEM_BUDGET = 40 * 1024 * 1024

WEIGHTS = ['norm_mix', 'norm_ffn', 'norm_ple', 'mla_w_down', 'mla_q_lora_g', 'mla_kv_lora_g', 'mla_w_uq',
           'mla_w_ukv', 'mla_q_nope_g', 'mla_q_rope_g', 'mla_k_nope_g', 'mla_k_rope_g', 'mla_w_out', 'gmlp_w_in',
           'gmlp_ln_g', 'gmlp_ln_b', 'gmlp_w_s', 'gmlp_b_s', 'gmlp_w_out', 'ffn_w_up', 'ffn_w_down', 'ple_w_gate',
           'ple_w_proj']
SHARD_AXIS = {'mla_w_down': 1, 'mla_w_uq': 2, 'mla_w_ukv': 2, 'mla_w_out': 1, 'gmlp_w_in': 2, 'gmlp_ln_g': 1,
              'gmlp_ln_b': 1, 'gmlp_w_out': 1, 'ffn_w_up': 2, 'ffn_w_down': 1, 'ple_w_gate': 1, 'ple_w_proj': 2}
SHARDED = list(SHARD_AXIS)
REPLICATED = [n for n in WEIGHTS if n not in SHARD_AXIS]
F32_PAYLOAD = ('gmlp_ln_g', 'gmlp_ln_b')


def _pick(dim, pref, align=LANE):
    if dim <= pref:
        return dim
    b = (pref // align) * align
    while b >= align:
        if dim % b == 0:
            return b
        b -= align
    return dim


def _params(sem):
    return pltpu.CompilerParams(dimension_semantics=sem, vmem_limit_bytes=VMEM_LIMIT)


def _mm(a, b, *, ta=False, tb=False, extras=(), rows=(), epilogue=None, out_dtypes=(F32,), out_blocks=None,
        acc_rows=0, deps=(), name, bn=1024):
    a3, b3 = a.ndim == 3, b.ndim == 3
    assert not (ta and a3)
    if ta:
        K, M = a.shape
        ka = K
    elif a3:
        M, ka = a.shape[1:]
        K = a.shape[0] * ka
    else:
        M, K = a.shape
        ka = K
    if tb:
        N, kb = b.shape[-2:]
        nb = N
        K2 = b.shape[0] * kb if b3 else kb
    else:
        kb, nb = b.shape[-2:]
        K2 = kb
        N = b.shape[0] * nb if b3 else nb
    assert K == K2, (a.shape, b.shape, ta, tb)
    no_ = N // out_blocks if out_blocks else N
    assert not (out_blocks and extras)
    size = lambda t: jnp.dtype(t).itemsize
    per_out = sum(size(e.dtype) for e in extras) + sum(size(t) for t in out_dtypes)
    bn = _pick(min(nb, no_), bn)
    assert not acc_rows or bn == N, "row sums are kept across row tiles only when one tile spans the columns"
    k_lim = min(ka, kb)
    fits = lambda m, k: 2 * (m * k * size(a.dtype) + k * bn * size(b.dtype) + m * bn * per_out) + 4 * m * bn
    ms = [m for m in sorted({min(M, c) for c in (2048, 1024, 512, 256)}, reverse=True) if M % m == 0]
    ks = [k for k in dict.fromkeys((k_lim, 2048, 1024, 512, 256)) if k <= k_lim and k_lim % k == 0]
    bm, bk = next(((m, k) for k in ks for m in ms if fits(m, k) <= MM_VMEM_BUDGET), (ms[-1], ks[-1]))
    nk = K // bk
    ne, no = len(extras) + len(rows), len(out_dtypes)
    first_out = 2 + ne + len(deps)
    dims = (((0,) if ta else (1,), (1,) if tb else (0,)), ((), ()))

    def finish(r, e_refs, o_refs):
        outs = epilogue(r, *[e[...] for e in e_refs]) if epilogue is not None else (r,)
        for o, v in zip(o_refs[:no], outs):
            o[...] = v.astype(o.dtype)
        for o, v in zip(o_refs[no:], outs[no:]):
            first = pl.program_id(0) == 0

            @pl.when(first)
            def _():
                o[...] = v

            @pl.when(jnp.logical_not(first))
            def _():
                o[...] += v

    def body(*refs):
        a_ref, b_ref = refs[0], refs[1]
        e_refs = refs[2:2 + ne]
        o_refs = refs[first_out:first_out + no + acc_rows]
        part = lax.dot_general(a_ref[...].astype(BF16), b_ref[...].astype(BF16), dims, preferred_element_type=F32)
        if nk == 1:
            finish(part, e_refs, o_refs)
            return
        acc = refs[-1]
        k = pl.program_id(2)

        @pl.when(k == 0)
        def _():
            acc[...] = part

        @pl.when(k > 0)
        def _():
            acc[...] += part

        @pl.when(k == nk - 1)
        def _():
            finish(acc[...], e_refs, o_refs)

    ka_t, kb_t, nb_t, no_t = ka // bk, kb // bk, nb // bn, no_ // bn
    if ta:
        a_spec = pl.BlockSpec((bk, bm), lambda i, j, k: (k, i))
    elif a3:
        a_spec = pl.BlockSpec((None, bm, bk), lambda i, j, k: (k // ka_t, i, k % ka_t))
    else:
        a_spec = pl.BlockSpec((bm, bk), lambda i, j, k: (i, k))
    if tb:
        b_spec = (pl.BlockSpec((None, bn, bk), lambda i, j, k: (k // kb_t, j, k % kb_t)) if b3 else
                  pl.BlockSpec((bn, bk), lambda i, j, k: (j, k)))
    else:
        b_spec = (pl.BlockSpec((None, bk, bn), lambda i, j, k: (j // nb_t, k, j % nb_t)) if b3 else
                  pl.BlockSpec((bk, bn), lambda i, j, k: (k, j)))
    if out_blocks:
        o_spec = lambda: pl.BlockSpec((None, bm, bn), lambda i, j, k: (j // no_t, i, j % no_t))
        o_shape = (out_blocks, M, no_)
    else:
        o_spec = lambda: pl.BlockSpec((bm, bn), lambda i, j, k: (i, j))
        o_shape = (M, N)
    outs = pl.pallas_call(
        body, name=name,
        grid=(M // bm, N // bn, nk),
        in_specs=[a_spec, b_spec] + [pl.BlockSpec((bm, bn), lambda i, j, k: (i, j)) for _ in extras]
        + [pl.BlockSpec((1, bn), lambda i, j, k: (0, j)) for _ in rows]
        + [pl.BlockSpec(memory_space=pl.ANY)] * len(deps),
        out_specs=[o_spec() for _ in out_dtypes] + [pl.BlockSpec((1, bn), lambda i, j, k: (0, j))] * acc_rows,
        out_shape=[jax.ShapeDtypeStruct(o_shape, dt) for dt in out_dtypes]
        + [jax.ShapeDtypeStruct((1, N), F32)] * acc_rows,
        scratch_shapes=[pltpu.VMEM((bm, bn), F32)] if nk > 1 else [],
        compiler_params=_params(("arbitrary",) * 3 if acc_rows else ("parallel", "parallel", "arbitrary")),
    )(a, b, *extras, *rows, *deps)
    return outs[0] if no + acc_rows == 1 else outs


def _rowcall(fn, rows, params, row_outs, acc_outs=(), *, bm=256, deps=(), name):
    T = rows[0].shape[0]
    bm = _pick(T, bm, 8)
    nr, npar, nro, nao = len(rows), len(params), len(row_outs), len(acc_outs)
    first_out = nr + npar + len(deps)

    def body(*refs):
        vals = [r[...] for r in refs[:nr + npar]]
        res = fn(*vals)
        ro = refs[first_out:first_out + nro]
        ao = refs[first_out + nro:]
        for r, v in zip(ro, res[:nro]):
            r[...] = v.astype(r.dtype)
        if nao:
            @pl.when(pl.program_id(0) == 0)
            def _():
                for r in ao:
                    r[...] = jnp.zeros_like(r)

            for r, v in zip(ao, res[nro:]):
                r[...] += v

    def whole(shape):
        nd = len(shape)
        return pl.BlockSpec(tuple(shape), lambda i: (0,) * nd)

    outs = pl.pallas_call(
        body, name=name,
        grid=(T // bm,),
        in_specs=[pl.BlockSpec((bm, r.shape[1]), lambda i: (i, 0)) for r in rows] + [whole(q.shape) for q in params]
        + [pl.BlockSpec(memory_space=pl.ANY)] * len(deps),
        out_specs=[pl.BlockSpec((bm, c), lambda i: (i, 0)) for c, _ in row_outs] + [whole(s) for s in acc_outs],
        out_shape=[jax.ShapeDtypeStruct((T, c), dt) for c, dt in row_outs]
        + [jax.ShapeDtypeStruct(tuple(s), F32) for s in acc_outs],
        compiler_params=_params(("arbitrary",) if nao else ("parallel",)),
    )(*rows, *params, *deps)
    return outs


def _rmsn(x, g):
    return x * lax.rsqrt(jnp.mean(x * x, axis=-1, keepdims=True) + EPS) * g


def _gelu(x):
    return 0.5 * x * (1.0 + jnp.tanh(math.sqrt(2.0 / math.pi) * (x + 0.044715 * (x * x * x))))


def _layer_norm(x, g, b):
    mu = jnp.mean(x, axis=-1, keepdims=True)
    xc = x - mu
    return xc * lax.rsqrt(jnp.mean(xc * xc, axis=-1, keepdims=True) + EPS) * g + b


def _sigmoid(x):
    return 1.0 / (1.0 + jnp.exp(-x))


def _rot(x, cos, sin, rmat):
    return x * cos + jnp.dot(x, rmat, precision=lax.Precision.HIGHEST, preferred_element_type=F32) * sin


def _rms_fwd(h, g, name, deps=()):
    return _rowcall(lambda x, gg: (_rmsn(x, gg),), [h], [g], [(h.shape[1], BF16)], bm=512, deps=deps, name=name)[0]


def _rope_tables(pos, name):
    inv = np.float32(ROPE_BASE) ** (-(np.arange(0, ROPE, 2, dtype=np.float32) / np.float32(ROPE)))
    inv = jnp.asarray(np.concatenate([inv, inv])[None, :].astype(np.float32))

    def fn(pp, iv):
        ang = pp.astype(F32) * iv
        return jnp.cos(ang), jnp.sin(ang)

    return _rowcall(fn, [pos], [inv], [(ROPE, F32), (ROPE, F32)], name=name)


def _rot_matrix():
    r = np.zeros((ROPE, ROPE), np.float32)
    half = ROPE // 2
    for j in range(half):
        r[j + half, j] = -1.0
        r[j, j + half] = 1.0
    return jnp.asarray(r)


def _prep1_fwd(lat, gq, gkv, name):
    def fn(l, a, b):
        return _rmsn(l[:, :Q_LORA], a), _rmsn(l[:, Q_LORA:Q_LORA + KV_LORA], b)

    return _rowcall(fn, [lat], [gq, gkv], [(Q_LORA, BF16), (KV_LORA, BF16)], name=name)


def _prep1_bwd(lat, d_cq, d_ckv, d_kr, gq, gkv, name):
    def fn(l, dq, dkv, dkr, a, b):
        _, vq = jax.vjp(_rmsn, l[:, :Q_LORA], a)
        _, vkv = jax.vjp(_rmsn, l[:, Q_LORA:Q_LORA + KV_LORA], b)
        dxq, dga = vq(dq.astype(F32))
        dxkv, dgb = vkv(dkv.astype(F32))
        return jnp.concatenate([dxq, dxkv, dkr], axis=1), dga, dgb

    return _rowcall(fn, [lat, d_cq, d_ckv, d_kr], [gq, gkv], [(lat.shape[1], BF16)], [gq.shape, gkv.shape], name=name)


def _qk_fn(qn_raw, qr_raw, kn_raw, kr_raw, gqn, gqr, gkn, gkr, cos, sin, rmat):
    return (_rmsn(qn_raw, gqn), _rot(_rmsn(qr_raw, gqr), cos, sin, rmat),
            _rmsn(kn_raw, gkn), _rot(_rmsn(kr_raw, gkr), cos, sin, rmat))


def _prep2_fwd(q_raw, kv_raw, lat, cos, sin, rmat, gains, name, bm=1024):
    H, T, _ = q_raw.shape
    bm = _pick(T, bm, 8)
    kr0 = Q_LORA + KV_LORA

    def body(q_ref, kv_ref, lat_ref, cos_ref, sin_ref, r_ref, gqn, gqr, gkn, gkr, qo, ko, vo):
        qr, kvr = q_ref[...], kv_ref[...]
        qn, qro, kn, kro = _qk_fn(qr[:, :NOPE], qr[:, NOPE:], kvr[:, :NOPE], lat_ref[:, kr0:kr0 + ROPE],
                                  gqn[...], gqr[...], gkn[...], gkr[...], cos_ref[...], sin_ref[...], r_ref[...])
        qo[:, :NOPE] = qn.astype(BF16)
        qo[:, NOPE:] = qro.astype(BF16)
        ko[:, :NOPE] = kn.astype(BF16)
        ko[:, NOPE:] = kro.astype(BF16)
        vo[...] = kvr[:, NOPE:].astype(BF16)

    hb = lambda c: pl.BlockSpec((None, bm, c), lambda m, h: (h, m, 0))
    rb = lambda c: pl.BlockSpec((bm, c), lambda m, h: (m, 0))
    wb = lambda s: pl.BlockSpec(tuple(s), lambda m, h: (0, 0))
    return pl.pallas_call(
        body, name=name, grid=(T // bm, H),
        in_specs=[hb(QK), hb(NOPE + VDIM), rb(lat.shape[1]), rb(ROPE), rb(ROPE), wb(rmat.shape)]
        + [wb(g.shape) for g in gains],
        out_specs=[hb(QK), hb(QK), hb(VDIM)],
        out_shape=[jax.ShapeDtypeStruct((H, T, QK), BF16), jax.ShapeDtypeStruct((H, T, QK), BF16),
                   jax.ShapeDtypeStruct((H, T, VDIM), BF16)],
        compiler_params=_params(("parallel", "parallel")),
    )(q_raw, kv_raw, lat, cos, sin, rmat, *gains)


def _prep2_bwd(q_raw, kv_raw, lat, cos, sin, rmat, gains, dq, dk, dv, name, bm=512):
    H, T, _ = q_raw.shape
    bm = _pick(T, bm, 8)
    kr0 = Q_LORA + KV_LORA

    def body(q_ref, kv_ref, lat_ref, cos_ref, sin_ref, r_ref, gqn, gqr, gkn, gkr, dq_ref, dk_ref, dv_ref,
             dqo, dkvo, dkro, o_gqn, o_gqr, o_gkn, o_gkr):
        m, h = pl.program_id(0), pl.program_id(1)
        qr, kvr = q_ref[...], kv_ref[...]
        cos_v, sin_v, r_v = cos_ref[...], sin_ref[...], r_ref[...]
        f = lambda a, b, c, d, g1, g2, g3, g4: _qk_fn(a, b, c, d, g1, g2, g3, g4, cos_v, sin_v, r_v)
        _, vjp = jax.vjp(f, qr[:, :NOPE], qr[:, NOPE:], kvr[:, :NOPE], lat_ref[:, kr0:kr0 + ROPE],
                         gqn[...], gqr[...], gkn[...], gkr[...])
        dqv, dkv_ = dq_ref[...], dk_ref[...]
        d_qn, d_qr, d_kn, d_kr, g1, g2, g3, g4 = vjp((dqv[:, :NOPE], dqv[:, NOPE:], dkv_[:, :NOPE], dkv_[:, NOPE:]))
        dqo[:, :NOPE] = d_qn.astype(BF16)
        dqo[:, NOPE:] = d_qr.astype(BF16)
        dkvo[:, :NOPE] = d_kn.astype(BF16)
        dkvo[:, NOPE:] = dv_ref[...].astype(BF16)

        @pl.when(h == 0)
        def _():
            dkro[...] = jnp.zeros_like(dkro)

        dkro[...] += d_kr

        @pl.when((h == 0) & (m == 0))
        def _():
            for o in (o_gqn, o_gqr, o_gkn, o_gkr):
                o[...] = jnp.zeros_like(o)

        for o, g in zip((o_gqn, o_gqr, o_gkn, o_gkr), (g1, g2, g3, g4)):
            o[...] += g

    hb = lambda c: pl.BlockSpec((None, bm, c), lambda m, h: (h, m, 0))
    rb = lambda c: pl.BlockSpec((bm, c), lambda m, h: (m, 0))
    wb = lambda s: pl.BlockSpec(tuple(s), lambda m, h: (0, 0))
    return pl.pallas_call(
        body, name=name, grid=(T // bm, H),
        in_specs=[hb(QK), hb(NOPE + VDIM), rb(lat.shape[1]), rb(ROPE), rb(ROPE), wb(rmat.shape)]
        + [wb(g.shape) for g in gains] + [hb(QK), hb(QK), hb(VDIM)],
        out_specs=[hb(QK), hb(NOPE + VDIM), rb(ROPE)] + [wb(g.shape) for g in gains],
        out_shape=[jax.ShapeDtypeStruct((H, T, QK), BF16), jax.ShapeDtypeStruct((H, T, NOPE + VDIM), BF16),
                   jax.ShapeDtypeStruct((T, ROPE), F32)] + [jax.ShapeDtypeStruct(g.shape, F32) for g in gains],
        compiler_params=_params(("arbitrary", "arbitrary")),
    )(q_raw, kv_raw, lat, cos, sin, rmat, *gains, dq, dk, dv)


_NT = (((1,), (1,)), ((), ()))
_TN = (((0,), (0,)), ((), ()))


def _causal(blk):
    return lax.broadcasted_iota(jnp.int32, (blk, blk), 1) <= lax.broadcasted_iota(jnp.int32, (blk, blk), 0)


def _attn_fwd(q, k, v, seq, name, blk=512):
    H, T, _ = q.shape
    nb = T // seq
    blk = _pick(seq, blk)
    nq = seq // blk
    scale = float(QK) ** -0.5

    def body(q_ref, k_ref, v_ref, o_ref, lse_ref):
        qi = pl.program_id(2)
        qb = q_ref[...]

        def step(j, carry, diagonal):
            m, l, acc = carry
            ks = pl.ds(pl.multiple_of(j * blk, blk), blk)
            s = lax.dot_general(qb, k_ref[ks, :], _NT, preferred_element_type=F32) * scale
            if diagonal:
                s = jnp.where(_causal(blk), s, -jnp.inf)
            m_new = jnp.maximum(m, jnp.max(s, axis=1, keepdims=True))
            pr = jnp.exp(s - m_new)
            alpha = jnp.exp(m - m_new)
            l = alpha * l + jnp.sum(pr, axis=1, keepdims=True)
            acc = alpha * acc + jnp.dot(pr.astype(BF16), v_ref[ks, :], preferred_element_type=F32)
            return m_new, l, acc

        init = (jnp.full((blk, 1), -jnp.inf, F32), jnp.zeros((blk, 1), F32), jnp.zeros((blk, VDIM), F32))
        below = lax.fori_loop(0, qi, lambda j, c: step(j, c, False), init)
        m, l, acc = step(qi, below, True)
        o_ref[...] = (acc / l).astype(o_ref.dtype)
        lse_ref[...] = m + jnp.log(l)

    return pl.pallas_call(
        body, name=name, grid=(H, nb, nq),
        in_specs=[pl.BlockSpec((None, blk, QK), lambda h, b, i: (h, b * nq + i, 0)),
                  pl.BlockSpec((None, seq, QK), lambda h, b, i: (h, b, 0)),
                  pl.BlockSpec((None, seq, VDIM), lambda h, b, i: (h, b, 0))],
        out_specs=[pl.BlockSpec((blk, VDIM), lambda h, b, i: (b * nq + i, h)),
                   pl.BlockSpec((None, blk, 1), lambda h, b, i: (h, b * nq + i, 0))],
        out_shape=[jax.ShapeDtypeStruct((T, H * VDIM), BF16), jax.ShapeDtypeStruct((H, T, 1), F32)],
        compiler_params=_params(("parallel", "parallel", "parallel")),
    )(q, k, v)


def _attn_bwd(q, k, v, o, do, lse, seq, name, blk=512):
    H, T, _ = q.shape
    nb = T // seq
    blk = _pick(seq, blk)
    nq = seq // blk
    scale = float(QK) ** -0.5

    def body(q_ref, k_ref, v_ref, o_ref, do_ref, lse_ref, dq_ref, dk_ref, dv_ref):
        dk_ref[...] = jnp.zeros_like(dk_ref)
        dv_ref[...] = jnp.zeros_like(dv_ref)

        def qloop(i, carry):
            qs = pl.ds(pl.multiple_of(i * blk, blk), blk)
            qb = q_ref[qs, :]
            dob = do_ref[qs, :]
            dof = dob.astype(F32)
            lse_b = lse_ref[qs, :]
            delta = jnp.sum(dof * o_ref[qs, :].astype(F32), axis=1, keepdims=True)

            def kstep(j, dq_acc, diagonal):
                ks = pl.ds(pl.multiple_of(j * blk, blk), blk)
                kb = k_ref[ks, :]
                vb = v_ref[ks, :]
                s = lax.dot_general(qb, kb, _NT, preferred_element_type=F32) * scale
                pr = jnp.exp(s - lse_b)
                if diagonal:
                    pr = jnp.where(_causal(blk), pr, 0.0)
                dp = lax.dot_general(dob, vb, _NT, preferred_element_type=F32)
                ds = (pr * (dp - delta) * scale).astype(BF16)
                prb = pr.astype(BF16)
                dv_ref[ks, :] += lax.dot_general(prb, dob, _TN, preferred_element_type=F32)
                dk_ref[ks, :] += lax.dot_general(ds, qb, _TN, preferred_element_type=F32)
                return dq_acc + jnp.dot(ds, kb, preferred_element_type=F32)

            below = lax.fori_loop(0, i, lambda j, c: kstep(j, c, False), jnp.zeros((blk, QK), F32))
            dq_ref[qs, :] = kstep(i, below, True)
            return carry

        lax.fori_loop(0, nq, qloop, 0)

    hb = lambda c: pl.BlockSpec((None, seq, c), lambda h, b: (h, b, 0))
    cb = lambda: pl.BlockSpec((seq, VDIM), lambda h, b: (b, h))
    return pl.pallas_call(
        body, name=name, grid=(H, nb),
        in_specs=[hb(QK), hb(QK), hb(VDIM), cb(), cb(), hb(1)],
        out_specs=[hb(QK), hb(QK), hb(VDIM)],
        out_shape=[jax.ShapeDtypeStruct((H, T, QK), F32), jax.ShapeDtypeStruct((H, T, QK), F32),
                   jax.ShapeDtypeStruct((H, T, VDIM), F32)],
        compiler_params=_params(("parallel", "parallel")),
    )(q, k, v, o, do, lse)


def _gelu_ln_fwd(z, g, b, name):
    half = z.shape[1] // 2

    def fn(zz, gg, bb):
        return _gelu(zz[:, :half].astype(F32)), _layer_norm(_gelu(zz[:, half:].astype(F32)), gg, bb)

    return _rowcall(fn, [z], [g, b], [(half, BF16), (half, BF16)], name=name)


def _gelu_ln_bwd(z, d_u, d_vn, g, b, name):
    half = z.shape[1] // 2

    def gelu_and_slope(x):
        c, a = math.sqrt(2.0 / math.pi), 0.044715
        x2 = x * x
        t = jnp.tanh(c * x * (1.0 + a * x2))
        return 0.5 * x * (1.0 + t), 0.5 * (1.0 + t) + 0.5 * x * (1.0 - t * t) * (c * (1.0 + 3.0 * a * x2))

    def fn(zz, du, dvn, gg, bb):
        du, dvn = du.astype(F32), dvn.astype(F32)
        _, su = gelu_and_slope(zz[:, :half].astype(F32))
        v, sv = gelu_and_slope(zz[:, half:].astype(F32))
        xc = v - jnp.mean(v, axis=-1, keepdims=True)
        rstd = lax.rsqrt(jnp.mean(xc * xc, axis=-1, keepdims=True) + EPS)
        y = xc * rstd
        dy = dvn * gg
        dv = rstd * (dy - jnp.mean(dy, axis=-1, keepdims=True) - y * jnp.mean(dy * y, axis=-1, keepdims=True))
        dg = jnp.sum(dvn * y, axis=0, keepdims=True)
        db = jnp.sum(dvn, axis=0, keepdims=True)
        return jnp.concatenate([du * su, dv * sv], axis=1), dg, db

    return _rowcall(fn, [z, d_u, d_vn], [g, b], [(z.shape[1], BF16)], [g.shape, b.shape], bm=128, name=name)


def _tril_bf16(ws):
    t = lax.broadcasted_iota(jnp.int32, ws.shape, 0)
    s = lax.broadcasted_iota(jnp.int32, ws.shape, 1)
    return jnp.where(s <= t, ws, 0.0).astype(BF16)


def _sgu_fwd(u, vn, ws, bs, name, bm=2048):
    T, half = u.shape
    gd = half // GROUPS
    bm = _pick(T, bm, CHUNK)
    nc = bm // CHUNK

    def body(u_ref, vn_ref, ws_ref, bs_ref, y_ref):
        wm = _tril_bf16(ws_ref[...])
        bias = bs_ref[...]
        for c in range(nc):
            rs = slice(c * CHUNK, (c + 1) * CHUNK)
            sv = jnp.dot(wm, vn_ref[rs, :], preferred_element_type=F32) + bias
            y_ref[rs, :] = (u_ref[rs, :].astype(F32) * sv).astype(y_ref.dtype)

    tb = lambda: pl.BlockSpec((bm, gd), lambda g, i: (i, g))
    return pl.pallas_call(
        body, name=name, grid=(GROUPS, T // bm),
        in_specs=[tb(), tb(), pl.BlockSpec((None, CHUNK, CHUNK), lambda g, i: (g, 0, 0)),
                  pl.BlockSpec((None, CHUNK, 1), lambda g, i: (g, 0, 0))],
        out_specs=tb(),
        out_shape=jax.ShapeDtypeStruct((T, half), BF16),
        compiler_params=_params(("parallel", "parallel")),
    )(u, vn, ws, bs)


def _sgu_bwd(u, vn, dy, ws, bs, name, bm=1024):
    T, half = u.shape
    gd = half // GROUPS
    bm = _pick(T, bm, CHUNK)
    nc = bm // CHUNK

    def body(u_ref, vn_ref, dy_ref, ws_ref, bs_ref, du_ref, dvn_ref, dws_ref, dbs_ref):
        @pl.when(pl.program_id(1) == 0)
        def _():
            dws_ref[...] = jnp.zeros_like(dws_ref)
            dbs_ref[...] = jnp.zeros_like(dbs_ref)

        wm = _tril_bf16(ws_ref[...])
        bias = bs_ref[...]
        dws = jnp.zeros((CHUNK, CHUNK), F32)
        dbs = jnp.zeros((CHUNK, 1), F32)
        for c in range(nc):
            rs = slice(c * CHUNK, (c + 1) * CHUNK)
            vb = vn_ref[rs, :]
            dyb = dy_ref[rs, :].astype(F32)
            sv = jnp.dot(wm, vb, preferred_element_type=F32) + bias
            du_ref[rs, :] = (dyb * sv).astype(du_ref.dtype)
            dsv = dyb * u_ref[rs, :].astype(F32)
            dsb = dsv.astype(BF16)
            dvn_ref[rs, :] = lax.dot_general(wm, dsb, _TN, preferred_element_type=F32).astype(dvn_ref.dtype)
            dws = dws + lax.dot_general(dsb, vb, _NT, preferred_element_type=F32)
            dbs = dbs + jnp.sum(dsv, axis=1, keepdims=True)
        t = lax.broadcasted_iota(jnp.int32, (CHUNK, CHUNK), 0)
        s = lax.broadcasted_iota(jnp.int32, (CHUNK, CHUNK), 1)
        dws_ref[...] += jnp.where(s <= t, dws, 0.0)
        dbs_ref[...] += dbs

    tb = lambda: pl.BlockSpec((bm, gd), lambda g, i: (i, g))
    wsb = lambda: pl.BlockSpec((None, CHUNK, CHUNK), lambda g, i: (g, 0, 0))
    bsb = lambda: pl.BlockSpec((None, CHUNK, 1), lambda g, i: (g, 0, 0))
    return pl.pallas_call(
        body, name=name, grid=(GROUPS, T // bm),
        in_specs=[tb(), tb(), tb(), wsb(), bsb()],
        out_specs=[tb(), tb(), wsb(), bsb()],
        out_shape=[jax.ShapeDtypeStruct((T, half), BF16), jax.ShapeDtypeStruct((T, half), BF16),
                   jax.ShapeDtypeStruct(ws.shape, F32), jax.ShapeDtypeStruct(bs.shape, F32)],
        compiler_params=_params(("parallel", "arbitrary")),
    )(u, vn, dy, ws, bs)


def _gate_back(d, gate, proj):
    sg = _sigmoid(gate)
    return d * sg, d * proj * sg * (1.0 - sg)


def _loss_head(y, t, gate, proj, name):
    d_model = y.shape[1]

    def fn(yy, tt, gg, pp):
        d = yy - tt
        part = 0.5 * jnp.sum(jnp.mean(d * d, axis=-1, keepdims=True), axis=0, keepdims=True)
        dy = d / d_model
        return (dy,) + _gate_back(dy, gg, pp) + (jnp.zeros((1, LANE), F32) + part,)

    dy, d_pp, d_gt, part = _rowcall(fn, [y, t, gate, proj], [], [(d_model, F32), (d_model, BF16), (d_model, BF16)],
                                    [(1, LANE)], name=name)
    return dy, d_pp, d_gt, part[0, 0]


def _adamw(parts, w, m, v, prev, layer, name):
    L, R, C = w.shape
    br = _pick(R, max(8, (128 * 1024) // C // 8 * 8), 8)
    c1 = 1.0 - B1 ** STEP
    c2 = 1.0 - B2 ** STEP
    if prev is None:
        prev = [lax.empty(w.shape, F32) for _ in range(4)]

    def body(p_ref, w_ref, m_ref, v_ref, a0, a1, a2, a3, g_o, d_o, m_o, v_o, token):
        g = p_ref[0].astype(F32)
        for d in range(1, N_DEV):
            g = g + p_ref[d].astype(F32)
        mn = B1 * m_ref[...] + (1.0 - B1) * g
        vn = B2 * v_ref[...] + (1.0 - B2) * (g * g)
        g_o[...] = g
        m_o[...] = mn
        v_o[...] = vn
        d_o[...] = -LR * ((mn / c1) / (jnp.sqrt(vn / c2) + ADAM_EPS) + WD * w_ref[...])
        token[...] = jnp.zeros_like(token)

    blk = lambda: pl.BlockSpec((None, br, C), lambda i: (layer, i, 0))
    anywhere = pl.BlockSpec(memory_space=pl.ANY)
    outs = pl.pallas_call(
        body, name=name, grid=(R // br,),
        in_specs=[pl.BlockSpec((N_DEV, br, C), lambda i: (0, i, 0)), blk(), blk(), blk()] + [anywhere] * 4,
        out_specs=[blk(), blk(), blk(), blk(), pl.BlockSpec((8, LANE), lambda i: (0, 0))],
        out_shape=[jax.ShapeDtypeStruct((L, R, C), F32)] * 4 + [jax.ShapeDtypeStruct((8, LANE), F32)],
        input_output_aliases={4: 0, 5: 1, 6: 2, 7: 3},
        compiler_params=_params(("arbitrary",)),
    )(parts, w, m, v, *prev)
    return list(outs[:4]), outs[4]


def _mesh_pos():
    return lax.axis_index("x"), lax.axis_index("y"), lax.axis_index("c")


def _flip(pos, k):
    x, y, c = pos
    px = 1 - x if k & 4 else x
    py = 1 - y if k & 2 else y
    pc = 1 - c if k & 1 else c
    return px, py, pc


HBM_SPEC = pl.BlockSpec(memory_space=pltpu.HBM)
SEM_SPEC = pl.BlockSpec(memory_space=pltpu.SEMAPHORE)
EFFECT = pltpu.SideEffectType.DATAFLOW_SIDE_EFFECTING


def _hbm(a):
    return pltpu.with_memory_space_constraint(a, pltpu.HBM)


def _peer_copy(src, land, send, recv, a, k, pos, scatter):
    peer = _flip(pos, k)
    me = 4 * pos[0] + 2 * pos[1] + pos[2]
    piece = src.at[4 * peer[0] + 2 * peer[1] + peer[2]] if scatter else src
    return pltpu.make_async_remote_copy(
        src_ref=piece, dst_ref=land.at[me], send_sem=send.at[7 * a + k - 1], recv_sem=recv.at[7 * a + k - 1],
        device_id=peer, device_id_type=pl.DeviceIdType.MESH)


def _own_copy(src, land, own, a, pos, scatter):
    me = 4 * pos[0] + 2 * pos[1] + pos[2]
    return pltpu.make_async_copy(src.at[me] if scatter else src, land.at[me], own.at[a])


def _xchg_start(srcs, scatter, after, name):
    n = len(srcs)

    def body(*refs):
        src, land = refs[:n], refs[n:2 * n]
        send, recv, own, token = refs[2 * n + 1], refs[2 * n + 2], refs[2 * n + 3], refs[-1]
        pos = _mesh_pos()
        for k in range(1, N_DEV):
            for a in range(n):
                _peer_copy(src[a], land[a], send, recv, a, k, pos, scatter).start()
        for a in range(n):
            _own_copy(src[a], land[a], own, a, pos, scatter).start()
        token[...] = jnp.zeros_like(token)

    lands = [lax.empty(s.shape if scatter else (N_DEV,) + s.shape, s.dtype) for s in srcs]
    outs = pl.pallas_call(
        body, name=name,
        out_shape=(pltpu.SemaphoreType.DMA((7 * n,)), pltpu.SemaphoreType.DMA((7 * n,)), pltpu.SemaphoreType.DMA((n,)),
                   *[pltpu.HBM(s.shape, s.dtype) for s in srcs], *[pltpu.HBM(l.shape, l.dtype) for l in lands],
                   jax.ShapeDtypeStruct((8, LANE), F32)),
        in_specs=[HBM_SPEC] * (2 * n) + [pl.BlockSpec(memory_space=pl.ANY)],
        out_specs=(SEM_SPEC, SEM_SPEC, SEM_SPEC, *[HBM_SPEC] * (2 * n), pl.BlockSpec(memory_space=pltpu.VMEM)),
        input_output_aliases={q: 3 + q for q in range(2 * n)},
        compiler_params=pltpu.CompilerParams(has_side_effects=EFFECT),
    )(*[_hbm(s) for s in srcs], *[_hbm(l) for l in lands], after)
    handle = dict(send=outs[0], recv=outs[1], own=outs[2], srcs=list(outs[3:3 + n]),
                  lands=list(outs[3 + n:3 + 2 * n]), scatter=scatter)
    return handle, outs[-1]


def _xchg_wait(handle, after, name):
    srcs, lands, scatter = handle['srcs'], handle['lands'], handle['scatter']
    n = len(srcs)

    def body(*refs):
        src, land = refs[:n], refs[n:2 * n]
        send, recv, own = refs[2 * n], refs[2 * n + 1], refs[2 * n + 2]
        pos = _mesh_pos()
        for k in range(1, N_DEV):
            for a in range(n):
                cp = _peer_copy(src[a], land[a], send, recv, a, k, pos, scatter)
                cp.wait_send()
                cp.wait_recv()
        for a in range(n):
            _own_copy(src[a], land[a], own, a, pos, scatter).wait()

    outs = pl.pallas_call(
        body, name=name,
        out_shape=[pltpu.HBM(s.shape, s.dtype) for s in srcs] + [pltpu.HBM(l.shape, l.dtype) for l in lands],
        in_specs=[HBM_SPEC] * (2 * n) + [SEM_SPEC, SEM_SPEC, SEM_SPEC, pl.BlockSpec(memory_space=pl.ANY)],
        out_specs=[HBM_SPEC] * (2 * n),
        input_output_aliases={q: q for q in range(2 * n)},
        compiler_params=pltpu.CompilerParams(has_side_effects=EFFECT),
    )(*srcs, *lands, handle['send'], handle['recv'], handle['own'], after)
    return list(outs[n:])


def _pack(parts):
    flat = jnp.concatenate([q.reshape(-1) for q in parts])
    pad = (-flat.shape[0]) % (8 * LANE)
    return jnp.pad(flat, (0, pad)).reshape(-1, LANE)


def _unpack(packed, like):
    flat = packed.reshape(-1)
    out, o = [], 0
    for q in like:
        out.append(flat[o:o + q.size].reshape(q.shape))
        o += q.size
    return out


def kernel(x, p, positions, norm_mix, norm_ffn, norm_ple, mla_w_down, mla_q_lora_g, mla_kv_lora_g, mla_w_uq, mla_w_ukv, mla_q_nope_g, mla_q_rope_g, mla_k_nope_g, mla_k_rope_g, mla_w_out, gmlp_w_in, gmlp_ln_g, gmlp_ln_b, gmlp_w_s, gmlp_b_s, gmlp_w_out, ffn_w_up, ffn_w_down, ple_w_gate, ple_w_proj, loss_target, m_norm_mix, m_norm_ffn, m_norm_ple, m_mla_w_down, m_mla_q_lora_g, m_mla_kv_lora_g, m_mla_w_uq, m_mla_w_ukv, m_mla_q_nope_g, m_mla_q_rope_g, m_mla_k_nope_g, m_mla_k_rope_g, m_mla_w_out, m_gmlp_w_in, m_gmlp_ln_g, m_gmlp_ln_b, m_gmlp_w_s, m_gmlp_b_s, m_gmlp_w_out, m_ffn_w_up, m_ffn_w_down, m_ple_w_gate, m_ple_w_proj, v_norm_mix, v_norm_ffn, v_norm_ple, v_mla_w_down, v_mla_q_lora_g, v_mla_kv_lora_g, v_mla_w_uq, v_mla_w_ukv, v_mla_q_nope_g, v_mla_q_rope_g, v_mla_k_nope_g, v_mla_k_rope_g, v_mla_w_out, v_gmlp_w_in, v_gmlp_ln_g, v_gmlp_ln_b, v_gmlp_w_s, v_gmlp_b_s, v_gmlp_w_out, v_ffn_w_up, v_ffn_w_down, v_ple_w_gate, v_ple_w_proj):
    W = dict(zip(WEIGHTS, (norm_mix, norm_ffn, norm_ple, mla_w_down, mla_q_lora_g, mla_kv_lora_g, mla_w_uq, mla_w_ukv, mla_q_nope_g, mla_q_rope_g, mla_k_nope_g, mla_k_rope_g, mla_w_out, gmlp_w_in, gmlp_ln_g, gmlp_ln_b, gmlp_w_s, gmlp_b_s, gmlp_w_out, ffn_w_up, ffn_w_down, ple_w_gate, ple_w_proj)))
    M1 = dict(zip(WEIGHTS, (m_norm_mix, m_norm_ffn, m_norm_ple, m_mla_w_down, m_mla_q_lora_g, m_mla_kv_lora_g, m_mla_w_uq, m_mla_w_ukv, m_mla_q_nope_g, m_mla_q_rope_g, m_mla_k_nope_g, m_mla_k_rope_g, m_mla_w_out, m_gmlp_w_in, m_gmlp_ln_g, m_gmlp_ln_b, m_gmlp_w_s, m_gmlp_b_s, m_gmlp_w_out, m_ffn_w_up, m_ffn_w_down, m_ple_w_gate, m_ple_w_proj)))
    M2 = dict(zip(WEIGHTS, (v_norm_mix, v_norm_ffn, v_norm_ple, v_mla_w_down, v_mla_q_lora_g, v_mla_kv_lora_g, v_mla_w_uq, v_mla_w_ukv, v_mla_q_nope_g, v_mla_q_rope_g, v_mla_k_nope_g, v_mla_k_rope_g, v_mla_w_out, v_gmlp_w_in, v_gmlp_ln_g, v_gmlp_ln_b, v_gmlp_w_s, v_gmlp_b_s, v_gmlp_w_out, v_ffn_w_up, v_ffn_w_down, v_ple_w_gate, v_ple_w_proj)))

    nb, seq, d_model = x.shape
    assert d_model <= 1024, "the rms norms fused into matmul epilogues need whole rows in one output tile"
    T = nb * seq
    depth = norm_mix.shape[0]
    h = x.reshape(T, d_model)
    target = loss_target.reshape(T, d_model)
    p_bf = p.reshape(depth, T, p.shape[-1]).astype(BF16)

    stages, carried = [], []
    for i in range(depth):
        gate = [(n, i) for n in ('ple_w_gate', 'ple_w_proj')]
        mlp = [(n, i) for n in ('ffn_w_up', 'ffn_w_down')]
        if i % 2 == 0:
            mixer = [(n, i // 2) for n in ('mla_w_down', 'mla_w_uq', 'mla_w_ukv', 'mla_w_out')]
            stages += [mixer[:1], mixer[1:]] if i == 0 else [carried + mixer]
            stages.append(mlp + gate)
            carried = []
        else:
            stages.append(carried + [(n, i // 2) for n in ('gmlp_w_in', 'gmlp_ln_g', 'gmlp_ln_b', 'gmlp_w_out')])
            stages.append(mlp)
            carried = gate
    if carried:
        stages.append(carried)

    FW = {n: {} for n in SHARDED}

    def start_weights(st, after):
        keys = stages[st]
        srcs = [W[n][l] if n in F32_PAYLOAD else W[n][l].astype(BF16) for n, l in keys]
        handle, token = _xchg_start(srcs, False, after, "weights_start%d" % st)
        return (keys, handle), [token]

    def wait_weights(pending, st, after):
        keys, handle = pending
        landed = _xchg_wait(handle, after, "weights_wait%d" % st)
        for (n, l), full in zip(keys, landed):
            if SHARD_AXIS[n] == 1:
                FW[n][l] = full.reshape((-1,) + full.shape[2:])
            elif n in ('mla_w_uq', 'mla_w_ukv'):
                FW[n][l] = full
            else:
                FW[n][l] = jnp.transpose(full, (1, 0, 2)).reshape(full.shape[1], -1)
        return landed[0]

    row = lambda a: a.reshape(1, -1)
    cos, sin = _rope_tables(positions.reshape(T, 1), "rope_tables")
    rmat = _rot_matrix()

    def add_and_norm(acc, res, g):
        hh = res + acc
        return hh, _rmsn(hh, g)

    saved = []
    chain = {'stage': 0}
    chain['pending'], _ = start_weights(0, h)

    def advance(after):
        st = chain['stage']
        if st >= len(stages):
            return []
        landed = wait_weights(chain['pending'], st, after)
        chain['stage'] = st + 1
        if st + 1 >= len(stages):
            return []
        chain['pending'], token = start_weights(st + 1, landed)
        return token

    hn = _rms_fwd(h, row(W['norm_mix'][0]), "rms_fwd")
    token = advance(hn)
    for i in range(depth):
        j = i // 2
        s = {}
        s['h0'] = h
        s['hn'] = hn
        if i % 2 == 0:
            gains = [row(W['mla_q_nope_g'][j]), row(W['mla_q_rope_g'][j]), row(W['mla_k_nope_g'][j]),
                     row(W['mla_k_rope_g'][j])]
            lat = _mm(hn, FW['mla_w_down'][j], deps=token, name="mla_down")
            if i == 0:
                token = advance(lat)
            cq, ckv = _prep1_fwd(lat, row(W['mla_q_lora_g'][j]), row(W['mla_kv_lora_g'][j]), "mla_prep1")
            q_raw = _mm(cq, FW['mla_w_uq'][j], out_blocks=HEADS, deps=token, name="mla_uq")
            kv_raw = _mm(ckv, FW['mla_w_ukv'][j], out_blocks=HEADS, name="mla_ukv")
            q, k, v = _prep2_fwd(q_raw, kv_raw, lat, cos, sin, rmat, gains, "mla_prep2")
            o, lse = _attn_fwd(q, k, v, seq, "attn_fwd")
            token = advance(o)
            h, hn2 = _mm(o, FW['mla_w_out'][j], extras=(h,), rows=(row(W['norm_ffn'][i]),), epilogue=add_and_norm,
                         out_dtypes=(F32, BF16), deps=token, name="mla_out")
            s.update(lat=lat, cq=cq, ckv=ckv, q_raw=q_raw, kv_raw=kv_raw, q=q, k=k, v=v, o=o, lse=lse, gains=gains)
        else:
            z = _mm(hn, FW['gmlp_w_in'][j], out_dtypes=(BF16,), name="gmlp_in")
            u, vn = _gelu_ln_fwd(z, row(FW['gmlp_ln_g'][j]), row(FW['gmlp_ln_b'][j]), "gmlp_gelu_ln")
            bs3 = W['gmlp_b_s'][j][:, :, None]
            y = _sgu_fwd(u, vn, W['gmlp_w_s'][j], bs3, "gmlp_sgu")
            h, hn2 = _mm(y, FW['gmlp_w_out'][j], extras=(h,), rows=(row(W['norm_ffn'][i]),), epilogue=add_and_norm,
                         out_dtypes=(F32, BF16), name="gmlp_out")
            token = advance(hn2)
            s.update(z=z, u=u, vn=vn, y=y, bs3=bs3)
        s['h1'] = h
        a, r = _mm(hn2, FW['ffn_w_up'][i], epilogue=lambda acc: (acc, jnp.square(jnp.maximum(acc, 0.0))),
                   out_dtypes=(BF16, BF16), deps=token, name="ffn_up")
        h, hn3 = _mm(r, FW['ffn_w_down'][i], extras=(h,), rows=(row(W['norm_ple'][i]),), epilogue=add_and_norm,
                     out_dtypes=(F32, BF16), name="ffn_down")
        s.update(hn2=hn2, a=a, r=r, h2=h)
        if i % 2 == 1:
            token = advance(hn3)
        gt = _mm(hn3, FW['ple_w_gate'][i], deps=token, name="ple_gate")
        if i % 2 == 0:
            token = advance(gt)
        if i + 1 < depth:
            def gate_and_norm(acc, g_, res, gain):
                hh = res + _sigmoid(g_) * acc
                return acc, hh, _rmsn(hh, gain)

            pp, h, hn = _mm(p_bf[i], FW['ple_w_proj'][i], extras=(gt, h), rows=(row(W['norm_mix'][i + 1]),),
                            epilogue=gate_and_norm, out_dtypes=(F32, F32, BF16), deps=token, name="ple_proj")
        else:
            pp, h = _mm(p_bf[i], FW['ple_w_proj'][i], extras=(gt, h),
                        epilogue=lambda acc, g_, res: (acc, res + _sigmoid(g_) * acc), out_dtypes=(F32, F32),
                        name="ple_proj_last")
        s.update(hn3=hn3, gt=gt, pp=pp)
        saved.append(s)

    dh, d_pp, d_gt, loss_part = _loss_head(h, target, saved[-1]['gt'], saved[-1]['pp'], "loss_head")
    loss = lax.psum(loss_part, MESH_AXES)

    G = {n: [None] * W[n].shape[0] for n in REPLICATED}
    res = {}
    flying = []

    def shard3(n):
        shp = W[n].shape
        return shp[0], int(np.prod(shp[1:-1])), shp[-1]

    def by_owner(g):
        return g.reshape((N_DEV, g.shape[0] // N_DEV) + g.shape[1:])

    def norm_back(h_in, dh_in, gain, below=None):
        def epilogue(acc, x, dres, *rest):
            _, vjp = jax.vjp(_rmsn, x, rest[-1])
            dx, dg = vjp(acc)
            dh = dres + dx
            return (dh,) + (_gate_back(dh, *rest[:2]) if below else (dh,)) + (dg,)

        return dict(extras=(h_in, dh_in) + tuple(below or ()), rows=(gain,), epilogue=epilogue,
                    out_dtypes=(F32, BF16, BF16) if below else (F32, BF16), acc_rows=1)

    def send_grads(tag, grads):
        handle, token = _xchg_start([g for _, g in grads], True, cos, "grads_start_" + tag)
        flying.append((tag, [key for key, _ in grads], handle))
        return [token]

    def land_grads(after):
        tag, keys, handle = flying.pop(0)
        done = []
        for (n, l), full in zip(keys, _xchg_wait(handle, after, "grads_wait_" + tag)):
            dims = shard3(n)
            res[n], token = _adamw(full.reshape((N_DEV,) + dims[1:]), W[n].reshape(dims), M1[n].reshape(dims),
                                   M2[n].reshape(dims), res.get(n), l, "adamw_" + n)
            done.append(token)
        return done

    def start_small(names, tag, after):
        handle, token = _xchg_start([_pack([jnp.stack(G[n]) for n in names])], False, after, "small_start_" + tag)
        return (names, handle), [token]

    def land_small(pending_small, tag, after):
        names, handle = pending_small
        (parts,) = _xchg_wait(handle, after, "small_wait_" + tag)
        like = [W[n] for n in names]
        outs, _ = _adamw(parts, _pack(like)[None], _pack([M1[n] for n in names])[None],
                         _pack([M2[n] for n in names])[None], None, 0, "adamw_small_" + tag)
        unpacked = [_unpack(o, like) for o in outs]
        for idx, n in enumerate(names):
            res[n] = [unpacked[q][idx] for q in range(4)]

    spatial = ['gmlp_w_s', 'gmlp_b_s']
    token = []
    for i in reversed(range(depth)):
        j = i // 2
        s = saved[i]
        g_proj = _mm(p_bf[i], d_pp, ta=True, out_dtypes=(BF16,), name="ple_proj_dw")
        g_proj = jnp.transpose(g_proj.reshape(g_proj.shape[0], N_DEV, -1), (1, 0, 2))
        g_gate = _mm(s['hn3'], d_gt, ta=True, out_dtypes=(BF16,), name="ple_gate_dw")
        dh, dh_bf, dg = _mm(d_gt, FW['ple_w_gate'][i], tb=True, deps=token, name="ple_gate_dx",
                            **norm_back(s['h2'], dh, row(W['norm_ple'][i])))
        G['norm_ple'][i] = dg[0]
        d_a = _mm(dh_bf, FW['ffn_w_down'][i], tb=True, extras=(s['a'],),
                  epilogue=lambda acc, a_: (acc * (2.0 * jnp.maximum(a_.astype(F32), 0.0)),), out_dtypes=(BF16,),
                  name="ffn_down_dx")
        g_down = _mm(s['r'], dh_bf, ta=True, out_dtypes=(BF16,), name="ffn_down_dw")
        g_up = _mm(s['hn2'], d_a, ta=True, out_blocks=N_DEV, out_dtypes=(BF16,), name="ffn_up_dw")
        token = send_grads("mlp%d" % i, [(('ple_w_proj', i), g_proj), (('ple_w_gate', i), by_owner(g_gate)),
                                         (('ffn_w_down', i), by_owner(g_down)), (('ffn_w_up', i), g_up)])
        if len(flying) > 1:
            token = token + land_grads(token[0])
        dh, dh_bf, dg = _mm(d_a, FW['ffn_w_up'][i], tb=True, deps=token, name="ffn_up_dx",
                            **norm_back(s['h1'], dh, row(W['norm_ffn'][i])))
        G['norm_ffn'][i] = dg[0]
        if i % 2 == 0:
            d_o = _mm(dh_bf, FW['mla_w_out'][j], tb=True, out_dtypes=(BF16,), name="mla_out_dx")
            g_out = _mm(s['o'], dh_bf, ta=True, out_dtypes=(BF16,), name="mla_out_dw")
            dq, dk, dv = _attn_bwd(s['q'], s['k'], s['v'], s['o'], d_o, s['lse'], seq, "attn_bwd")
            d_q_raw, d_kv_raw, d_kr, g1, g2, g3, g4 = _prep2_bwd(
                s['q_raw'], s['kv_raw'], s['lat'], cos, sin, rmat, s['gains'], dq, dk, dv, "mla_prep2_bwd")
            G['mla_q_nope_g'][j], G['mla_q_rope_g'][j] = g1[0], g2[0]
            G['mla_k_nope_g'][j], G['mla_k_rope_g'][j] = g3[0], g4[0]
            g_uq = _mm(s['cq'], d_q_raw, ta=True, out_blocks=N_DEV, out_dtypes=(BF16,), name="mla_uq_dw")
            g_ukv = _mm(s['ckv'], d_kv_raw, ta=True, out_blocks=N_DEV, out_dtypes=(BF16,), name="mla_ukv_dw")
            d_cq = _mm(d_q_raw, FW['mla_w_uq'][j], tb=True, out_dtypes=(BF16,), name="mla_uq_dx")
            d_ckv = _mm(d_kv_raw, FW['mla_w_ukv'][j], tb=True, out_dtypes=(BF16,), name="mla_ukv_dx")
            d_lat, dga, dgb = _prep1_bwd(s['lat'], d_cq, d_ckv, d_kr, row(W['mla_q_lora_g'][j]),
                                         row(W['mla_kv_lora_g'][j]), "mla_prep1_bwd")
            G['mla_q_lora_g'][j], G['mla_kv_lora_g'][j] = dga[0], dgb[0]
            g_down = _mm(s['hn'], d_lat, ta=True, out_dtypes=(BF16,), name="mla_down_dw")
            grads = [(('mla_w_out', j), by_owner(g_out)), (('mla_w_uq', j), g_uq), (('mla_w_ukv', j), g_ukv),
                     (('mla_w_down', j), by_owner(g_down))]
            last = (d_lat, FW['mla_w_down'][j], "mla_down_dx")
        else:
            d_y = _mm(dh_bf, FW['gmlp_w_out'][j], tb=True, out_dtypes=(BF16,), name="gmlp_out_dx")
            g_out = _mm(s['y'], dh_bf, ta=True, out_dtypes=(BF16,), name="gmlp_out_dw")
            d_u, d_vn, d_ws, d_bs = _sgu_bwd(s['u'], s['vn'], d_y, W['gmlp_w_s'][j], s['bs3'], "gmlp_sgu_bwd")
            G['gmlp_w_s'][j], G['gmlp_b_s'][j] = d_ws, d_bs[:, :, 0]
            d_z, d_lg, d_lb = _gelu_ln_bwd(s['z'], d_u, d_vn, row(FW['gmlp_ln_g'][j]), row(FW['gmlp_ln_b'][j]),
                                           "gmlp_gelu_ln_bwd")
            g_in = _mm(s['hn'], d_z, ta=True, out_blocks=N_DEV, out_dtypes=(BF16,), name="gmlp_in_dw")
            grads = [(('gmlp_w_out', j), by_owner(g_out)), (('gmlp_ln_g', j), by_owner(d_lg[0])),
                     (('gmlp_ln_b', j), by_owner(d_lb[0])), (('gmlp_w_in', j), g_in)]
            last = (d_z, FW['gmlp_w_in'][j], "gmlp_in_dx")
        token = send_grads("mix%d" % i, grads)
        if len(flying) > 1:
            token = token + land_grads(token[0])
        if i > 0:
            dh, d_pp, d_gt, dg = _mm(last[0], last[1], tb=True, deps=token, name=last[2], **norm_back(
                s['h0'], dh, row(W['norm_mix'][i]), (saved[i - 1]['gt'], saved[i - 1]['pp'])))
        else:
            dh, _, dg = _mm(last[0], last[1], tb=True, deps=token, name=last[2] + "_first",
                            **norm_back(s['h0'], dh, row(W['norm_mix'][i])))
        G['norm_mix'][i] = dg[0]
        token = []
        if i == 1:
            small_a, token = start_small(spatial, "spatial", dh)
    grad_x = dh.reshape(x.shape)

    small_b, _ = start_small([n for n in REPLICATED if n not in spatial], "gains", dh)
    while flying:
        land_grads(dh)
    land_small(small_a, "spatial", dh)
    land_small(small_b, "gains", dh)

    out = lambda q: [res[n][q].reshape(W[n].shape) for n in WEIGHTS]
    return (loss, grad_x, *out(0), *out(1), *out(2), *out(3))
```

```python
import math

import numpy as np
import jax
import jax.numpy as jnp
from jax import lax
from jax.experimental import pallas as pl
from jax.experimental.pallas import tpu as pltpu

F32 = jnp.float32
BF16 = jnp.bfloat16

N_DEV = 8
MESH_AXES = ("x", "y", "c")
HEADS = 8
NOPE = 128
ROPE = 64
VDIM = 128
QK = NOPE + ROPE
Q_LORA = 384
KV_LORA = 256
ROPE_BASE = 10000.0
CHUNK = 128
GROUPS = 8
EPS = 1e-6
LR, B1, B2, ADAM_EPS, WD, STEP = 0.001, 0.9, 0.999, 1e-08, 0.01, 10
LANE = 128
VMEM_LIMIT = 56 * 1024 * 1024
MM_VMEM_BUDGET = 40 * 1024 * 1024

WEIGHTS = ['norm_mix', 'norm_ffn', 'norm_ple', 'mla_w_down', 'mla_q_lora_g', 'mla_kv_lora_g', 'mla_w_uq',
           'mla_w_ukv', 'mla_q_nope_g', 'mla_q_rope_g', 'mla_k_nope_g', 'mla_k_rope_g', 'mla_w_out', 'gmlp_w_in',
           'gmlp_ln_g', 'gmlp_ln_b', 'gmlp_w_s', 'gmlp_b_s', 'gmlp_w_out', 'ffn_w_up', 'ffn_w_down', 'ple_w_gate',
           'ple_w_proj']
SHARD_AXIS = {'mla_w_down': 1, 'mla_w_uq': 2, 'mla_w_ukv': 2, 'mla_w_out': 1, 'gmlp_w_in': 2, 'gmlp_ln_g': 1,
              'gmlp_ln_b': 1, 'gmlp_w_out': 1, 'ffn_w_up': 2, 'ffn_w_down': 1, 'ple_w_gate': 1, 'ple_w_proj': 2}
SHARDED = list(SHARD_AXIS)
REPLICATED = [n for n in WEIGHTS if n not in SHARD_AXIS]
F32_PAYLOAD = ('gmlp_ln_g', 'gmlp_ln_b')


def _pick(dim, pref, align=LANE):
    if dim <= pref:
        return dim
    b = (pref // align) * align
    while b >= align:
        if dim % b == 0:
            return b
        b -= align
    return dim


def _params(sem):
    return pltpu.CompilerParams(dimension_semantics=sem, vmem_limit_bytes=VMEM_LIMIT)


def _mm(a, b, *, ta=False, tb=False, extras=(), rows=(), epilogue=None, out_dtypes=(F32,), out_blocks=None,
        acc_rows=0, deps=(), name, bn=1024):
    a3, b3 = a.ndim == 3, b.ndim == 3
    assert not (ta and a3)
    if ta:
        K, M = a.shape
        ka = K
    elif a3:
        M, ka = a.shape[1:]
        K = a.shape[0] * ka
    else:
        M, K = a.shape
        ka = K
    if tb:
        N, kb = b.shape[-2:]
        nb = N
        K2 = b.shape[0] * kb if b3 else kb
    else:
        kb, nb = b.shape[-2:]
        K2 = kb
        N = b.shape[0] * nb if b3 else nb
    assert K == K2, (a.shape, b.shape, ta, tb)
    no_ = N // out_blocks if out_blocks else N
    assert not (out_blocks and extras)
    size = lambda t: jnp.dtype(t).itemsize
    per_out = sum(size(e.dtype) for e in extras) + sum(size(t) for t in out_dtypes)
    bn = _pick(min(nb, no_), bn)
    assert not acc_rows or bn == N, "row sums are kept across row tiles only when one tile spans the columns"
    k_lim = min(ka, kb)
    fits = lambda m, k: 2 * (m * k * size(a.dtype) + k * bn * size(b.dtype) + m * bn * per_out) + 4 * m * bn
    ms = [m for m in sorted({min(M, c) for c in (2048, 1024, 512, 256)}, reverse=True) if M % m == 0]
    ks = [k for k in dict.fromkeys((k_lim, 2048, 1024, 512, 256)) if k <= k_lim and k_lim % k == 0]
    bm, bk = next(((m, k) for k in ks for m in ms if fits(m, k) <= MM_VMEM_BUDGET), (ms[-1], ks[-1]))
    nk = K // bk
    ne, no = len(extras) + len(rows), len(out_dtypes)
    first_out = 2 + ne + len(deps)
    dims = (((0,) if ta else (1,), (1,) if tb else (0,)), ((), ()))

    def finish(r, e_refs, o_refs):
        outs = epilogue(r, *[e[...] for e in e_refs]) if epilogue is not None else (r,)
        for o, v in zip(o_refs[:no], outs):
            o[...] = v.astype(o.dtype)
        for o, v in zip(o_refs[no:], outs[no:]):
            first = pl.program_id(0) == 0

            @pl.when(first)
            def _():
                o[...] = v

            @pl.when(jnp.logical_not(first))
            def _():
                o[...] += v

    def body(*refs):
        a_ref, b_ref = refs[0], refs[1]
        e_refs = refs[2:2 + ne]
        o_refs = refs[first_out:first_out + no + acc_rows]
        part = lax.dot_general(a_ref[...].astype(BF16), b_ref[...].astype(BF16), dims, preferred_element_type=F32)
        if nk == 1:
            finish(part, e_refs, o_refs)
            return
        acc = refs[-1]
        k = pl.program_id(2)

        @pl.when(k == 0)
        def _():
            acc[...] = part

        @pl.when(k > 0)
        def _():
            acc[...] += part

        @pl.when(k == nk - 1)
        def _():
            finish(acc[...], e_refs, o_refs)

    ka_t, kb_t, nb_t, no_t = ka // bk, kb // bk, nb // bn, no_ // bn
    if ta:
        a_spec = pl.BlockSpec((bk, bm), lambda i, j, k: (k, i))
    elif a3:
        a_spec = pl.BlockSpec((None, bm, bk), lambda i, j, k: (k // ka_t, i, k % ka_t))
    else:
        a_spec = pl.BlockSpec((bm, bk), lambda i, j, k: (i, k))
    if tb:
        b_spec = (pl.BlockSpec((None, bn, bk), lambda i, j, k: (k // kb_t, j, k % kb_t)) if b3 else
                  pl.BlockSpec((bn, bk), lambda i, j, k: (j, k)))
    else:
        b_spec = (pl.BlockSpec((None, bk, bn), lambda i, j, k: (j // nb_t, k, j % nb_t)) if b3 else
                  pl.BlockSpec((bk, bn), lambda i, j, k: (k, j)))
    if out_blocks:
        o_spec = lambda: pl.BlockSpec((None, bm, bn), lambda i, j, k: (j // no_t, i, j % no_t))
        o_shape = (out_blocks, M, no_)
    else:
        o_spec = lambda: pl.BlockSpec((bm, bn), lambda i, j, k: (i, j))
        o_shape = (M, N)
    outs = pl.pallas_call(
        body, name=name,
        grid=(M // bm, N // bn, nk),
        in_specs=[a_spec, b_spec] + [pl.BlockSpec((bm, bn), lambda i, j, k: (i, j)) for _ in extras]
        + [pl.BlockSpec((1, bn), lambda i, j, k: (0, j)) for _ in rows]
        + [pl.BlockSpec(memory_space=pl.ANY)] * len(deps),
        out_specs=[o_spec() for _ in out_dtypes] + [pl.BlockSpec((1, bn), lambda i, j, k: (0, j))] * acc_rows,
        out_shape=[jax.ShapeDtypeStruct(o_shape, dt) for dt in out_dtypes]
        + [jax.ShapeDtypeStruct((1, N), F32)] * acc_rows,
        scratch_shapes=[pltpu.VMEM((bm, bn), F32)] if nk > 1 else [],
        compiler_params=_params(("arbitrary",) * 3 if acc_rows else ("parallel", "parallel", "arbitrary")),
    )(a, b, *extras, *rows, *deps)
    return outs[0] if no + acc_rows == 1 else outs


def _rowcall(fn, rows, params, row_outs, acc_outs=(), *, bm=256, deps=(), name):
    T = rows[0].shape[0]
    bm = _pick(T, bm, 8)
    nr, npar, nro, nao = len(rows), len(params), len(row_outs), len(acc_outs)
    first_out = nr + npar + len(deps)

    def body(*refs):
        vals = [r[...] for r in refs[:nr + npar]]
        res = fn(*vals)
        ro = refs[first_out:first_out + nro]
        ao = refs[first_out + nro:]
        for r, v in zip(ro, res[:nro]):
            r[...] = v.astype(r.dtype)
        if nao:
            @pl.when(pl.program_id(0) == 0)
            def _():
                for r in ao:
                    r[...] = jnp.zeros_like(r)

            for r, v in zip(ao, res[nro:]):
                r[...] += v

    def whole(shape):
        nd = len(shape)
        return pl.BlockSpec(tuple(shape), lambda i: (0,) * nd)

    outs = pl.pallas_call(
        body, name=name,
        grid=(T // bm,),
        in_specs=[pl.BlockSpec((bm, r.shape[1]), lambda i: (i, 0)) for r in rows] + [whole(q.shape) for q in params]
        + [pl.BlockSpec(memory_space=pl.ANY)] * len(deps),
        out_specs=[pl.BlockSpec((bm, c), lambda i: (i, 0)) for c, _ in row_outs] + [whole(s) for s in acc_outs],
        out_shape=[jax.ShapeDtypeStruct((T, c), dt) for c, dt in row_outs]
        + [jax.ShapeDtypeStruct(tuple(s), F32) for s in acc_outs],
        compiler_params=_params(("arbitrary",) if nao else ("parallel",)),
    )(*rows, *params, *deps)
    return outs


def _rmsn(x, g):
    return x * lax.rsqrt(jnp.mean(x * x, axis=-1, keepdims=True) + EPS) * g


def _gelu(x):
    return 0.5 * x * (1.0 + jnp.tanh(math.sqrt(2.0 / math.pi) * (x + 0.044715 * (x * x * x))))


def _layer_norm(x, g, b):
    mu = jnp.mean(x, axis=-1, keepdims=True)
    xc = x - mu
    return xc * lax.rsqrt(jnp.mean(xc * xc, axis=-1, keepdims=True) + EPS) * g + b


def _sigmoid(x):
    return 1.0 / (1.0 + jnp.exp(-x))


def _rot(x, cos, sin, rmat):
    return x * cos + jnp.dot(x, rmat, precision=lax.Precision.HIGHEST, preferred_element_type=F32) * sin


def _rms_fwd(h, g, name, deps=()):
    return _rowcall(lambda x, gg: (_rmsn(x, gg),), [h], [g], [(h.shape[1], BF16)], bm=512, deps=deps, name=name)[0]


def _rope_tables(pos, name):
    inv = np.float32(ROPE_BASE) ** (-(np.arange(0, ROPE, 2, dtype=np.float32) / np.float32(ROPE)))
    inv = jnp.asarray(np.concatenate([inv, inv])[None, :].astype(np.float32))

    def fn(pp, iv):
        ang = pp.astype(F32) * iv
        return jnp.cos(ang), jnp.sin(ang)

    return _rowcall(fn, [pos], [inv], [(ROPE, F32), (ROPE, F32)], name=name)


def _rot_matrix():
    r = np.zeros((ROPE, ROPE), np.float32)
    half = ROPE // 2
    for j in range(half):
        r[j + half, j] = -1.0
        r[j, j + half] = 1.0
    return jnp.asarray(r)


def _prep1_fwd(lat, gq, gkv, name):
    def fn(l, a, b):
        return _rmsn(l[:, :Q_LORA], a), _rmsn(l[:, Q_LORA:Q_LORA + KV_LORA], b)

    return _rowcall(fn, [lat], [gq, gkv], [(Q_LORA, BF16), (KV_LORA, BF16)], name=name)


def _prep1_bwd(lat, d_cq, d_ckv, d_kr, gq, gkv, name):
    def fn(l, dq, dkv, dkr, a, b):
        _, vq = jax.vjp(_rmsn, l[:, :Q_LORA], a)
        _, vkv = jax.vjp(_rmsn, l[:, Q_LORA:Q_LORA + KV_LORA], b)
        dxq, dga = vq(dq.astype(F32))
        dxkv, dgb = vkv(dkv.astype(F32))
        return jnp.concatenate([dxq, dxkv, dkr], axis=1), dga, dgb

    return _rowcall(fn, [lat, d_cq, d_ckv, d_kr], [gq, gkv], [(lat.shape[1], BF16)], [gq.shape, gkv.shape], name=name)


def _qk_fn(qn_raw, qr_raw, kn_raw, kr_raw, gqn, gqr, gkn, gkr, cos, sin, rmat):
    return (_rmsn(qn_raw, gqn), _rot(_rmsn(qr_raw, gqr), cos, sin, rmat),
            _rmsn(kn_raw, gkn), _rot(_rmsn(kr_raw, gkr), cos, sin, rmat))


def _prep2_fwd(q_raw, kv_raw, lat, cos, sin, rmat, gains, name, bm=1024):
    H, T, _ = q_raw.shape
    bm = _pick(T, bm, 8)
    kr0 = Q_LORA + KV_LORA

    def body(q_ref, kv_ref, lat_ref, cos_ref, sin_ref, r_ref, gqn, gqr, gkn, gkr, qo, ko, vo):
        qr, kvr = q_ref[...], kv_ref[...]
        qn, qro, kn, kro = _qk_fn(qr[:, :NOPE], qr[:, NOPE:], kvr[:, :NOPE], lat_ref[:, kr0:kr0 + ROPE],
                                  gqn[...], gqr[...], gkn[...], gkr[...], cos_ref[...], sin_ref[...], r_ref[...])
        qo[:, :NOPE] = qn.astype(BF16)
        qo[:, NOPE:] = qro.astype(BF16)
        ko[:, :NOPE] = kn.astype(BF16)
        ko[:, NOPE:] = kro.astype(BF16)
        vo[...] = kvr[:, NOPE:].astype(BF16)

    hb = lambda c: pl.BlockSpec((None, bm, c), lambda m, h: (h, m, 0))
    rb = lambda c: pl.BlockSpec((bm, c), lambda m, h: (m, 0))
    wb = lambda s: pl.BlockSpec(tuple(s), lambda m, h: (0, 0))
    return pl.pallas_call(
        body, name=name, grid=(T // bm, H),
        in_specs=[hb(QK), hb(NOPE + VDIM), rb(lat.shape[1]), rb(ROPE), rb(ROPE), wb(rmat.shape)]
        + [wb(g.shape) for g in gains],
        out_specs=[hb(QK), hb(QK), hb(VDIM)],
        out_shape=[jax.ShapeDtypeStruct((H, T, QK), BF16), jax.ShapeDtypeStruct((H, T, QK), BF16),
                   jax.ShapeDtypeStruct((H, T, VDIM), BF16)],
        compiler_params=_params(("parallel", "parallel")),
    )(q_raw, kv_raw, lat, cos, sin, rmat, *gains)


def _prep2_bwd(q_raw, kv_raw, lat, cos, sin, rmat, gains, dq, dk, dv, name, bm=1024):
    H, T, _ = q_raw.shape
    bm = _pick(T, bm, 8)
    kr0 = Q_LORA + KV_LORA

    def body(q_ref, kv_ref, lat_ref, cos_ref, sin_ref, r_ref, gqn, gqr, gkn, gkr, dq_ref, dk_ref, dv_ref,
             dqo, dkvo, dkro, o_gqn, o_gqr, o_gkn, o_gkr):
        m, h = pl.program_id(0), pl.program_id(1)
        qr, kvr = q_ref[...], kv_ref[...]
        cos_v, sin_v, r_v = cos_ref[...], sin_ref[...], r_ref[...]
        f = lambda a, b, c, d, g1, g2, g3, g4: _qk_fn(a, b, c, d, g1, g2, g3, g4, cos_v, sin_v, r_v)
        _, vjp = jax.vjp(f, qr[:, :NOPE], qr[:, NOPE:], kvr[:, :NOPE], lat_ref[:, kr0:kr0 + ROPE],
                         gqn[...], gqr[...], gkn[...], gkr[...])
        dqv, dkv_ = dq_ref[...], dk_ref[...]
        d_qn, d_qr, d_kn, d_kr, g1, g2, g3, g4 = vjp((dqv[:, :NOPE], dqv[:, NOPE:], dkv_[:, :NOPE], dkv_[:, NOPE:]))
        dqo[:, :NOPE] = d_qn.astype(BF16)
        dqo[:, NOPE:] = d_qr.astype(BF16)
        dkvo[:, :NOPE] = d_kn.astype(BF16)
        dkvo[:, NOPE:] = dv_ref[...].astype(BF16)

        @pl.when(h == 0)
        def _():
            dkro[...] = jnp.zeros_like(dkro)

        dkro[...] += d_kr

        @pl.when((h == 0) & (m == 0))
        def _():
            for o in (o_gqn, o_gqr, o_gkn, o_gkr):
                o[...] = jnp.zeros_like(o)

        for o, g in zip((o_gqn, o_gqr, o_gkn, o_gkr), (g1, g2, g3, g4)):
            o[...] += g

    hb = lambda c: pl.BlockSpec((None, bm, c), lambda m, h: (h, m, 0))
    rb = lambda c: pl.BlockSpec((bm, c), lambda m, h: (m, 0))
    wb = lambda s: pl.BlockSpec(tuple(s), lambda m, h: (0, 0))
    return pl.pallas_call(
        body, name=name, grid=(T // bm, H),
        in_specs=[hb(QK), hb(NOPE + VDIM), rb(lat.shape[1]), rb(ROPE), rb(ROPE), wb(rmat.shape)]
        + [wb(g.shape) for g in gains] + [hb(QK), hb(QK), hb(VDIM)],
        out_specs=[hb(QK), hb(NOPE + VDIM), rb(ROPE)] + [wb(g.shape) for g in gains],
        out_shape=[jax.ShapeDtypeStruct((H, T, QK), BF16), jax.ShapeDtypeStruct((H, T, NOPE + VDIM), BF16),
                   jax.ShapeDtypeStruct((T, ROPE), F32)] + [jax.ShapeDtypeStruct(g.shape, F32) for g in gains],
        compiler_params=_params(("arbitrary", "arbitrary")),
    )(q_raw, kv_raw, lat, cos, sin, rmat, *gains, dq, dk, dv)


_NT = (((1,), (1,)), ((), ()))
_TN = (((0,), (0,)), ((), ()))


def _causal(blk):
    return lax.broadcasted_iota(jnp.int32, (blk, blk), 1) <= lax.broadcasted_iota(jnp.int32, (blk, blk), 0)


def _attn_fwd(q, k, v, seq, name, blk=512):
    H, T, _ = q.shape
    nb = T // seq
    blk = _pick(seq, blk)
    nq = seq // blk
    scale = float(QK) ** -0.5

    def body(q_ref, k_ref, v_ref, o_ref, lse_ref):
        qi = pl.program_id(2)
        qb = q_ref[...]

        def step(j, carry, diagonal):
            m, l, acc = carry
            ks = pl.ds(pl.multiple_of(j * blk, blk), blk)
            s = lax.dot_general(qb, k_ref[ks, :], _NT, preferred_element_type=F32) * scale
            if diagonal:
                s = jnp.where(_causal(blk), s, -jnp.inf)
            m_new = jnp.maximum(m, jnp.max(s, axis=1, keepdims=True))
            pr = jnp.exp(s - m_new)
            alpha = jnp.exp(m - m_new)
            l = alpha * l + jnp.sum(pr, axis=1, keepdims=True)
            acc = alpha * acc + jnp.dot(pr.astype(BF16), v_ref[ks, :], preferred_element_type=F32)
            return m_new, l, acc

        init = (jnp.full((blk, 1), -jnp.inf, F32), jnp.zeros((blk, 1), F32), jnp.zeros((blk, VDIM), F32))
        below = lax.fori_loop(0, qi, lambda j, c: step(j, c, False), init)
        m, l, acc = step(qi, below, True)
        o_ref[...] = (acc / l).astype(o_ref.dtype)
        lse_ref[...] = m + jnp.log(l)

    return pl.pallas_call(
        body, name=name, grid=(H, nb, nq),
        in_specs=[pl.BlockSpec((None, blk, QK), lambda h, b, i: (h, b * nq + i, 0)),
                  pl.BlockSpec((None, seq, QK), lambda h, b, i: (h, b, 0)),
                  pl.BlockSpec((None, seq, VDIM), lambda h, b, i: (h, b, 0))],
        out_specs=[pl.BlockSpec((blk, VDIM), lambda h, b, i: (b * nq + i, h)),
                   pl.BlockSpec((None, blk, 1), lambda h, b, i: (h, b * nq + i, 0))],
        out_shape=[jax.ShapeDtypeStruct((T, H * VDIM), BF16), jax.ShapeDtypeStruct((H, T, 1), F32)],
        compiler_params=_params(("parallel", "parallel", "parallel")),
    )(q, k, v)


def _attn_bwd(q, k, v, o, do, lse, seq, name, blk=512):
    H, T, _ = q.shape
    nb = T // seq
    blk = _pick(seq, blk)
    nq = seq // blk
    scale = float(QK) ** -0.5

    def body(q_ref, k_ref, v_ref, o_ref, do_ref, lse_ref, dq_ref, dk_ref, dv_ref):
        dk_ref[...] = jnp.zeros_like(dk_ref)
        dv_ref[...] = jnp.zeros_like(dv_ref)

        def qloop(i, carry):
            qs = pl.ds(pl.multiple_of(i * blk, blk), blk)
            qb = q_ref[qs, :]
            dob = do_ref[qs, :]
            dof = dob.astype(F32)
            lse_b = lse_ref[qs, :]
            delta = jnp.sum(dof * o_ref[qs, :].astype(F32), axis=1, keepdims=True)

            def kstep(j, dq_acc, diagonal):
                ks = pl.ds(pl.multiple_of(j * blk, blk), blk)
                kb = k_ref[ks, :]
                vb = v_ref[ks, :]
                s = lax.dot_general(qb, kb, _NT, preferred_element_type=F32) * scale
                pr = jnp.exp(s - lse_b)
                if diagonal:
                    pr = jnp.where(_causal(blk), pr, 0.0)
                dp = lax.dot_general(dob, vb, _NT, preferred_element_type=F32)
                ds = (pr * (dp - delta) * scale).astype(BF16)
                prb = pr.astype(BF16)
                dv_ref[ks, :] += lax.dot_general(prb, dob, _TN, preferred_element_type=F32)
                dk_ref[ks, :] += lax.dot_general(ds, qb, _TN, preferred_element_type=F32)
                return dq_acc + jnp.dot(ds, kb, preferred_element_type=F32)

            below = lax.fori_loop(0, i, lambda j, c: kstep(j, c, False), jnp.zeros((blk, QK), F32))
            dq_ref[qs, :] = kstep(i, below, True)
            return carry

        lax.fori_loop(0, nq, qloop, 0)

    hb = lambda c: pl.BlockSpec((None, seq, c), lambda h, b: (h, b, 0))
    cb = lambda: pl.BlockSpec((seq, VDIM), lambda h, b: (b, h))
    return pl.pallas_call(
        body, name=name, grid=(H, nb),
        in_specs=[hb(QK), hb(QK), hb(VDIM), cb(), cb(), hb(1)],
        out_specs=[hb(QK), hb(QK), hb(VDIM)],
        out_shape=[jax.ShapeDtypeStruct((H, T, QK), F32), jax.ShapeDtypeStruct((H, T, QK), F32),
                   jax.ShapeDtypeStruct((H, T, VDIM), F32)],
        compiler_params=_params(("parallel", "parallel")),
    )(q, k, v, o, do, lse)


def _gelu_ln_fwd(z, g, b, name):
    half = z.shape[1] // 2

    def fn(zz, gg, bb):
        return _gelu(zz[:, :half].astype(F32)), _layer_norm(_gelu(zz[:, half:].astype(F32)), gg, bb)

    return _rowcall(fn, [z], [g, b], [(half, BF16), (half, BF16)], bm=512, name=name)


def _gelu_ln_bwd(z, d_u, d_vn, g, b, name):
    half = z.shape[1] // 2

    def gelu_and_slope(x):
        c, a = math.sqrt(2.0 / math.pi), 0.044715
        x2 = x * x
        t = jnp.tanh(c * x * (1.0 + a * x2))
        return 0.5 * x * (1.0 + t), 0.5 * (1.0 + t) + 0.5 * x * (1.0 - t * t) * (c * (1.0 + 3.0 * a * x2))

    def fn(zz, du, dvn, gg, bb):
        du, dvn = du.astype(F32), dvn.astype(F32)
        _, su = gelu_and_slope(zz[:, :half].astype(F32))
        v, sv = gelu_and_slope(zz[:, half:].astype(F32))
        xc = v - jnp.mean(v, axis=-1, keepdims=True)
        rstd = lax.rsqrt(jnp.mean(xc * xc, axis=-1, keepdims=True) + EPS)
        y = xc * rstd
        dy = dvn * gg
        dv = rstd * (dy - jnp.mean(dy, axis=-1, keepdims=True) - y * jnp.mean(dy * y, axis=-1, keepdims=True))
        dg = jnp.sum(dvn * y, axis=0, keepdims=True)
        db = jnp.sum(dvn, axis=0, keepdims=True)
        return jnp.concatenate([du * su, dv * sv], axis=1), dg, db

    return _rowcall(fn, [z, d_u, d_vn], [g, b], [(z.shape[1], BF16)], [g.shape, b.shape], bm=128, name=name)


def _tril_bf16(ws):
    t = lax.broadcasted_iota(jnp.int32, ws.shape, 0)
    s = lax.broadcasted_iota(jnp.int32, ws.shape, 1)
    return jnp.where(s <= t, ws, 0.0).astype(BF16)


def _sgu_fwd(u, vn, ws, bs, name, bm=2048):
    T, half = u.shape
    gd = half // GROUPS
    bm = _pick(T, bm, CHUNK)
    nc = bm // CHUNK

    def body(u_ref, vn_ref, ws_ref, bs_ref, y_ref):
        wm = _tril_bf16(ws_ref[...])
        bias = bs_ref[...]
        for c in range(nc):
            rs = slice(c * CHUNK, (c + 1) * CHUNK)
            sv = jnp.dot(wm, vn_ref[rs, :], preferred_element_type=F32) + bias
            y_ref[rs, :] = (u_ref[rs, :].astype(F32) * sv).astype(y_ref.dtype)

    tb = lambda: pl.BlockSpec((bm, gd), lambda g, i: (i, g))
    return pl.pallas_call(
        body, name=name, grid=(GROUPS, T // bm),
        in_specs=[tb(), tb(), pl.BlockSpec((None, CHUNK, CHUNK), lambda g, i: (g, 0, 0)),
                  pl.BlockSpec((None, CHUNK, 1), lambda g, i: (g, 0, 0))],
        out_specs=tb(),
        out_shape=jax.ShapeDtypeStruct((T, half), BF16),
        compiler_params=_params(("parallel", "parallel")),
    )(u, vn, ws, bs)


def _sgu_bwd(u, vn, dy, ws, bs, name, bm=2048):
    T, half = u.shape
    gd = half // GROUPS
    bm = _pick(T, bm, CHUNK)
    nc = bm // CHUNK

    def body(u_ref, vn_ref, dy_ref, ws_ref, bs_ref, du_ref, dvn_ref, dws_ref, dbs_ref):
        @pl.when(pl.program_id(1) == 0)
        def _():
            dws_ref[...] = jnp.zeros_like(dws_ref)
            dbs_ref[...] = jnp.zeros_like(dbs_ref)

        wm = _tril_bf16(ws_ref[...])
        bias = bs_ref[...]
        dws = jnp.zeros((CHUNK, CHUNK), F32)
        dbs = jnp.zeros((CHUNK, 1), F32)
        for c in range(nc):
            rs = slice(c * CHUNK, (c + 1) * CHUNK)
            vb = vn_ref[rs, :]
            dyb = dy_ref[rs, :].astype(F32)
            sv = jnp.dot(wm, vb, preferred_element_type=F32) + bias
            du_ref[rs, :] = (dyb * sv).astype(du_ref.dtype)
            dsv = dyb * u_ref[rs, :].astype(F32)
            dsb = dsv.astype(BF16)
            dvn_ref[rs, :] = lax.dot_general(wm, dsb, _TN, preferred_element_type=F32).astype(dvn_ref.dtype)
            dws = dws + lax.dot_general(dsb, vb, _NT, preferred_element_type=F32)
            dbs = dbs + jnp.sum(dsv, axis=1, keepdims=True)
        t = lax.broadcasted_iota(jnp.int32, (CHUNK, CHUNK), 0)
        s = lax.broadcasted_iota(jnp.int32, (CHUNK, CHUNK), 1)
        dws_ref[...] += jnp.where(s <= t, dws, 0.0)
        dbs_ref[...] += dbs

    tb = lambda: pl.BlockSpec((bm, gd), lambda g, i: (i, g))
    wsb = lambda: pl.BlockSpec((None, CHUNK, CHUNK), lambda g, i: (g, 0, 0))
    bsb = lambda: pl.BlockSpec((None, CHUNK, 1), lambda g, i: (g, 0, 0))
    return pl.pallas_call(
        body, name=name, grid=(GROUPS, T // bm),
        in_specs=[tb(), tb(), tb(), wsb(), bsb()],
        out_specs=[tb(), tb(), wsb(), bsb()],
        out_shape=[jax.ShapeDtypeStruct((T, half), BF16), jax.ShapeDtypeStruct((T, half), BF16),
                   jax.ShapeDtypeStruct(ws.shape, F32), jax.ShapeDtypeStruct(bs.shape, F32)],
        compiler_params=_params(("parallel", "arbitrary")),
    )(u, vn, dy, ws, bs)


def _gate_back(d, gate, proj):
    sg = _sigmoid(gate)
    return d * sg, d * proj * sg * (1.0 - sg)


def _loss_head(y, t, gate, proj, name):
    d_model = y.shape[1]

    def fn(yy, tt, gg, pp):
        d = yy - tt
        part = 0.5 * jnp.sum(jnp.mean(d * d, axis=-1, keepdims=True), axis=0, keepdims=True)
        dy = d / d_model
        return (dy,) + _gate_back(dy, gg, pp) + (jnp.zeros((1, LANE), F32) + part,)

    dy, d_pp, d_gt, part = _rowcall(fn, [y, t, gate, proj], [], [(d_model, F32), (d_model, BF16), (d_model, BF16)],
                                    [(1, LANE)], name=name)
    return dy, d_pp, d_gt, part[0, 0]


def _adamw(parts, w, m, v, prev, layer, name):
    L, R, C = w.shape
    br = _pick(R, max(8, (128 * 1024) // C // 8 * 8), 8)
    c1 = 1.0 - B1 ** STEP
    c2 = 1.0 - B2 ** STEP
    if prev is None:
        prev = [lax.empty(w.shape, F32) for _ in range(4)]

    def body(p_ref, w_ref, m_ref, v_ref, a0, a1, a2, a3, g_o, d_o, m_o, v_o, token):
        g = p_ref[0].astype(F32)
        for d in range(1, N_DEV):
            g = g + p_ref[d].astype(F32)
        mn = B1 * m_ref[...] + (1.0 - B1) * g
        vn = B2 * v_ref[...] + (1.0 - B2) * (g * g)
        g_o[...] = g
        m_o[...] = mn
        v_o[...] = vn
        d_o[...] = -LR * ((mn / c1) / (jnp.sqrt(vn / c2) + ADAM_EPS) + WD * w_ref[...])
        token[...] = jnp.zeros_like(token)

    blk = lambda: pl.BlockSpec((None, br, C), lambda i: (layer, i, 0))
    anywhere = pl.BlockSpec(memory_space=pl.ANY)
    outs = pl.pallas_call(
        body, name=name, grid=(R // br,),
        in_specs=[pl.BlockSpec((N_DEV, br, C), lambda i: (0, i, 0)), blk(), blk(), blk()] + [anywhere] * 4,
        out_specs=[blk(), blk(), blk(), blk(), pl.BlockSpec((8, LANE), lambda i: (0, 0))],
        out_shape=[jax.ShapeDtypeStruct((L, R, C), F32)] * 4 + [jax.ShapeDtypeStruct((8, LANE), F32)],
        input_output_aliases={4: 0, 5: 1, 6: 2, 7: 3},
        compiler_params=_params(("arbitrary",)),
    )(parts, w, m, v, *prev)
    return list(outs[:4]), outs[4]


def _mesh_pos():
    return lax.axis_index("x"), lax.axis_index("y"), lax.axis_index("c")


def _flip(pos, k):
    x, y, c = pos
    px = 1 - x if k & 4 else x
    py = 1 - y if k & 2 else y
    pc = 1 - c if k & 1 else c
    return px, py, pc


HBM_SPEC = pl.BlockSpec(memory_space=pltpu.HBM)
SEM_SPEC = pl.BlockSpec(memory_space=pltpu.SEMAPHORE)
EFFECT = pltpu.SideEffectType.DATAFLOW_SIDE_EFFECTING


def _hbm(a):
    return pltpu.with_memory_space_constraint(a, pltpu.HBM)


def _peer_copy(src, land, send, recv, a, k, pos, scatter):
    peer = _flip(pos, k)
    me = 4 * pos[0] + 2 * pos[1] + pos[2]
    piece = src.at[4 * peer[0] + 2 * peer[1] + peer[2]] if scatter else src
    return pltpu.make_async_remote_copy(
        src_ref=piece, dst_ref=land.at[me], send_sem=send.at[7 * a + k - 1], recv_sem=recv.at[7 * a + k - 1],
        device_id=peer, device_id_type=pl.DeviceIdType.MESH)


def _own_copy(src, land, own, a, pos, scatter):
    me = 4 * pos[0] + 2 * pos[1] + pos[2]
    return pltpu.make_async_copy(src.at[me] if scatter else src, land.at[me], own.at[a])


def _xchg_start(srcs, scatter, after, name):
    n = len(srcs)

    def body(*refs):
        src, land = refs[:n], refs[n:2 * n]
        send, recv, own, token = refs[2 * n + 1], refs[2 * n + 2], refs[2 * n + 3], refs[-1]
        pos = _mesh_pos()
        for k in range(1, N_DEV):
            for a in range(n):
                _peer_copy(src[a], land[a], send, recv, a, k, pos, scatter).start()
        for a in range(n):
            _own_copy(src[a], land[a], own, a, pos, scatter).start()
        token[...] = jnp.zeros_like(token)

    lands = [lax.empty(s.shape if scatter else (N_DEV,) + s.shape, s.dtype) for s in srcs]
    outs = pl.pallas_call(
        body, name=name,
        out_shape=(pltpu.SemaphoreType.DMA((7 * n,)), pltpu.SemaphoreType.DMA((7 * n,)), pltpu.SemaphoreType.DMA((n,)),
                   *[pltpu.HBM(s.shape, s.dtype) for s in srcs], *[pltpu.HBM(l.shape, l.dtype) for l in lands],
                   jax.ShapeDtypeStruct((8, LANE), F32)),
        in_specs=[HBM_SPEC] * (2 * n) + [pl.BlockSpec(memory_space=pl.ANY)],
        out_specs=(SEM_SPEC, SEM_SPEC, SEM_SPEC, *[HBM_SPEC] * (2 * n), pl.BlockSpec(memory_space=pltpu.VMEM)),
        input_output_aliases={q: 3 + q for q in range(2 * n)},
        compiler_params=pltpu.CompilerParams(has_side_effects=EFFECT),
    )(*[_hbm(s) for s in srcs], *[_hbm(l) for l in lands], after)
    handle = dict(send=outs[0], recv=outs[1], own=outs[2], srcs=list(outs[3:3 + n]),
                  lands=list(outs[3 + n:3 + 2 * n]), scatter=scatter)
    return handle, outs[-1]


def _xchg_wait(handle, after, name):
    srcs, lands, scatter = handle['srcs'], handle['lands'], handle['scatter']
    n = len(srcs)

    def body(*refs):
        src, land = refs[:n], refs[n:2 * n]
        send, recv, own = refs[2 * n], refs[2 * n + 1], refs[2 * n + 2]
        pos = _mesh_pos()
        for k in range(1, N_DEV):
            for a in range(n):
                cp = _peer_copy(src[a], land[a], send, recv, a, k, pos, scatter)
                cp.wait_send()
                cp.wait_recv()
        for a in range(n):
            _own_copy(src[a], land[a], own, a, pos, scatter).wait()

    outs = pl.pallas_call(
        body, name=name,
        out_shape=[pltpu.HBM(s.shape, s.dtype) for s in srcs] + [pltpu.HBM(l.shape, l.dtype) for l in lands],
        in_specs=[HBM_SPEC] * (2 * n) + [SEM_SPEC, SEM_SPEC, SEM_SPEC, pl.BlockSpec(memory_space=pl.ANY)],
        out_specs=[HBM_SPEC] * (2 * n),
        input_output_aliases={q: q for q in range(2 * n)},
        compiler_params=pltpu.CompilerParams(has_side_effects=EFFECT),
    )(*srcs, *lands, handle['send'], handle['recv'], handle['own'], after)
    return list(outs[n:])


def _pack(parts):
    flat = jnp.concatenate([q.reshape(-1) for q in parts])
    pad = (-flat.shape[0]) % (8 * LANE)
    return jnp.pad(flat, (0, pad)).reshape(-1, LANE)


def _unpack(packed, like):
    flat = packed.reshape(-1)
    out, o = [], 0
    for q in like:
        out.append(flat[o:o + q.size].reshape(q.shape))
        o += q.size
    return out


def kernel(x, p, positions, norm_mix, norm_ffn, norm_ple, mla_w_down, mla_q_lora_g, mla_kv_lora_g, mla_w_uq, mla_w_ukv, mla_q_nope_g, mla_q_rope_g, mla_k_nope_g, mla_k_rope_g, mla_w_out, gmlp_w_in, gmlp_ln_g, gmlp_ln_b, gmlp_w_s, gmlp_b_s, gmlp_w_out, ffn_w_up, ffn_w_down, ple_w_gate, ple_w_proj, loss_target, m_norm_mix, m_norm_ffn, m_norm_ple, m_mla_w_down, m_mla_q_lora_g, m_mla_kv_lora_g, m_mla_w_uq, m_mla_w_ukv, m_mla_q_nope_g, m_mla_q_rope_g, m_mla_k_nope_g, m_mla_k_rope_g, m_mla_w_out, m_gmlp_w_in, m_gmlp_ln_g, m_gmlp_ln_b, m_gmlp_w_s, m_gmlp_b_s, m_gmlp_w_out, m_ffn_w_up, m_ffn_w_down, m_ple_w_gate, m_ple_w_proj, v_norm_mix, v_norm_ffn, v_norm_ple, v_mla_w_down, v_mla_q_lora_g, v_mla_kv_lora_g, v_mla_w_uq, v_mla_w_ukv, v_mla_q_nope_g, v_mla_q_rope_g, v_mla_k_nope_g, v_mla_k_rope_g, v_mla_w_out, v_gmlp_w_in, v_gmlp_ln_g, v_gmlp_ln_b, v_gmlp_w_s, v_gmlp_b_s, v_gmlp_w_out, v_ffn_w_up, v_ffn_w_down, v_ple_w_gate, v_ple_w_proj):
    W = dict(zip(WEIGHTS, (norm_mix, norm_ffn, norm_ple, mla_w_down, mla_q_lora_g, mla_kv_lora_g, mla_w_uq, mla_w_ukv, mla_q_nope_g, mla_q_rope_g, mla_k_nope_g, mla_k_rope_g, mla_w_out, gmlp_w_in, gmlp_ln_g, gmlp_ln_b, gmlp_w_s, gmlp_b_s, gmlp_w_out, ffn_w_up, ffn_w_down, ple_w_gate, ple_w_proj)))
    M1 = dict(zip(WEIGHTS, (m_norm_mix, m_norm_ffn, m_norm_ple, m_mla_w_down, m_mla_q_lora_g, m_mla_kv_lora_g, m_mla_w_uq, m_mla_w_ukv, m_mla_q_nope_g, m_mla_q_rope_g, m_mla_k_nope_g, m_mla_k_rope_g, m_mla_w_out, m_gmlp_w_in, m_gmlp_ln_g, m_gmlp_ln_b, m_gmlp_w_s, m_gmlp_b_s, m_gmlp_w_out, m_ffn_w_up, m_ffn_w_down, m_ple_w_gate, m_ple_w_proj)))
    M2 = dict(zip(WEIGHTS, (v_norm_mix, v_norm_ffn, v_norm_ple, v_mla_w_down, v_mla_q_lora_g, v_mla_kv_lora_g, v_mla_w_uq, v_mla_w_ukv, v_mla_q_nope_g, v_mla_q_rope_g, v_mla_k_nope_g, v_mla_k_rope_g, v_mla_w_out, v_gmlp_w_in, v_gmlp_ln_g, v_gmlp_ln_b, v_gmlp_w_s, v_gmlp_b_s, v_gmlp_w_out, v_ffn_w_up, v_ffn_w_down, v_ple_w_gate, v_ple_w_proj)))

    nb, seq, d_model = x.shape
    assert d_model <= 1024, "the rms norms fused into matmul epilogues need whole rows in one output tile"
    T = nb * seq
    depth = norm_mix.shape[0]
    h = x.reshape(T, d_model)
    target = loss_target.reshape(T, d_model)
    p_bf = p.reshape(depth, T, p.shape[-1]).astype(BF16)

    stages, carried = [], []
    for i in range(depth):
        gate = [(n, i) for n in ('ple_w_gate', 'ple_w_proj')]
        mlp = [(n, i) for n in ('ffn_w_up', 'ffn_w_down')]
        if i % 2 == 0:
            mixer = [(n, i // 2) for n in ('mla_w_down', 'mla_w_uq', 'mla_w_ukv', 'mla_w_out')]
            stages += [mixer[:1], mixer[1:]] if i == 0 else [carried + mixer]
            stages.append(mlp + gate)
            carried = []
        else:
            stages.append(carried + [(n, i // 2) for n in ('gmlp_w_in', 'gmlp_ln_g', 'gmlp_ln_b', 'gmlp_w_out')])
            stages.append(mlp)
            carried = gate
    if carried:
        stages.append(carried)

    FW = {n: {} for n in SHARDED}

    def start_weights(st, after):
        keys = stages[st]
        srcs = [W[n][l] if n in F32_PAYLOAD else W[n][l].astype(BF16) for n, l in keys]
        handle, token = _xchg_start(srcs, False, after, "weights_start%d" % st)
        return (keys, handle), [token]

    def wait_weights(pending, st, after):
        keys, handle = pending
        landed = _xchg_wait(handle, after, "weights_wait%d" % st)
        for (n, l), full in zip(keys, landed):
            if SHARD_AXIS[n] == 1:
                FW[n][l] = full.reshape((-1,) + full.shape[2:])
            elif n in ('mla_w_uq', 'mla_w_ukv'):
                FW[n][l] = full
            else:
                FW[n][l] = jnp.transpose(full, (1, 0, 2)).reshape(full.shape[1], -1)
        return landed[0]

    row = lambda a: a.reshape(1, -1)
    cos, sin = _rope_tables(positions.reshape(T, 1), "rope_tables")
    rmat = _rot_matrix()

    def add_and_norm(acc, res, g):
        hh = res + acc
        return hh, _rmsn(hh, g)

    saved = []
    chain = {'stage': 0}
    chain['pending'], _ = start_weights(0, h)

    def advance(after):
        st = chain['stage']
        if st >= len(stages):
            return []
        landed = wait_weights(chain['pending'], st, after)
        chain['stage'] = st + 1
        if st + 1 >= len(stages):
            return []
        chain['pending'], token = start_weights(st + 1, landed)
        return token

    hn = _rms_fwd(h, row(W['norm_mix'][0]), "rms_fwd")
    token = advance(hn)
    for i in range(depth):
        j = i // 2
        s = {}
        s['h0'] = h
        s['hn'] = hn
        if i % 2 == 0:
            gains = [row(W['mla_q_nope_g'][j]), row(W['mla_q_rope_g'][j]), row(W['mla_k_nope_g'][j]),
                     row(W['mla_k_rope_g'][j])]
            lat = _mm(hn, FW['mla_w_down'][j], deps=token, name="mla_down")
            if i == 0:
                token = advance(lat)
            cq, ckv = _prep1_fwd(lat, row(W['mla_q_lora_g'][j]), row(W['mla_kv_lora_g'][j]), "mla_prep1")
            q_raw = _mm(cq, FW['mla_w_uq'][j], out_blocks=HEADS, deps=token, name="mla_uq")
            kv_raw = _mm(ckv, FW['mla_w_ukv'][j], out_blocks=HEADS, name="mla_ukv")
            q, k, v = _prep2_fwd(q_raw, kv_raw, lat, cos, sin, rmat, gains, "mla_prep2")
            o, lse = _attn_fwd(q, k, v, seq, "attn_fwd")
            token = advance(o)
            h, hn2 = _mm(o, FW['mla_w_out'][j], extras=(h,), rows=(row(W['norm_ffn'][i]),), epilogue=add_and_norm,
                         out_dtypes=(F32, BF16), deps=token, name="mla_out")
            s.update(lat=lat, cq=cq, ckv=ckv, q_raw=q_raw, kv_raw=kv_raw, q=q, k=k, v=v, o=o, lse=lse, gains=gains)
        else:
            z = _mm(hn, FW['gmlp_w_in'][j], out_dtypes=(BF16,), name="gmlp_in")
            u, vn = _gelu_ln_fwd(z, row(FW['gmlp_ln_g'][j]), row(FW['gmlp_ln_b'][j]), "gmlp_gelu_ln")
            bs3 = W['gmlp_b_s'][j][:, :, None]
            y = _sgu_fwd(u, vn, W['gmlp_w_s'][j], bs3, "gmlp_sgu")
            h, hn2 = _mm(y, FW['gmlp_w_out'][j], extras=(h,), rows=(row(W['norm_ffn'][i]),), epilogue=add_and_norm,
                         out_dtypes=(F32, BF16), name="gmlp_out")
            token = advance(hn2)
            s.update(z=z, u=u, vn=vn, y=y, bs3=bs3)
        s['h1'] = h
        a, r = _mm(hn2, FW['ffn_w_up'][i], epilogue=lambda acc: (acc, jnp.square(jnp.maximum(acc, 0.0))),
                   out_dtypes=(BF16, BF16), deps=token, name="ffn_up")
        h, hn3 = _mm(r, FW['ffn_w_down'][i], extras=(h,), rows=(row(W['norm_ple'][i]),), epilogue=add_and_norm,
                     out_dtypes=(F32, BF16), name="ffn_down")
        s.update(hn2=hn2, a=a, r=r, h2=h)
        if i % 2 == 1:
            token = advance(hn3)
        gt = _mm(hn3, FW['ple_w_gate'][i], deps=token, name="ple_gate")
        if i % 2 == 0:
            token = advance(gt)
        if i + 1 < depth:
            def gate_and_norm(acc, g_, res, gain):
                hh = res + _sigmoid(g_) * acc
                return acc, hh, _rmsn(hh, gain)

            pp, h, hn = _mm(p_bf[i], FW['ple_w_proj'][i], extras=(gt, h), rows=(row(W['norm_mix'][i + 1]),),
                            epilogue=gate_and_norm, out_dtypes=(F32, F32, BF16), deps=token, name="ple_proj")
        else:
            pp, h = _mm(p_bf[i], FW['ple_w_proj'][i], extras=(gt, h),
                        epilogue=lambda acc, g_, res: (acc, res + _sigmoid(g_) * acc), out_dtypes=(F32, F32),
                        name="ple_proj_last")
        s.update(hn3=hn3, gt=gt, pp=pp)
        saved.append(s)

    dh, d_pp, d_gt, loss_part = _loss_head(h, target, saved[-1]['gt'], saved[-1]['pp'], "loss_head")
    loss = lax.psum(loss_part, MESH_AXES)

    G = {n: [None] * W[n].shape[0] for n in REPLICATED}
    res = {}
    flying = []

    def shard3(n):
        shp = W[n].shape
        return shp[0], int(np.prod(shp[1:-1])), shp[-1]

    def by_owner(g):
        return g.reshape((N_DEV, g.shape[0] // N_DEV) + g.shape[1:])

    def norm_back(h_in, dh_in, gain, below=None):
        def epilogue(acc, x, dres, *rest):
            _, vjp = jax.vjp(_rmsn, x, rest[-1])
            dx, dg = vjp(acc)
            dh = dres + dx
            return (dh,) + (_gate_back(dh, *rest[:2]) if below else (dh,)) + (dg,)

        return dict(extras=(h_in, dh_in) + tuple(below or ()), rows=(gain,), epilogue=epilogue,
                    out_dtypes=(F32, BF16, BF16) if below else (F32, BF16), acc_rows=1)

    def send_grads(tag, grads):
        handle, token = _xchg_start([g for _, g in grads], True, cos, "grads_start_" + tag)
        flying.append((tag, [key for key, _ in grads], handle))
        return [token]

    def land_grads(after):
        tag, keys, handle = flying.pop(0)
        done = []
        for (n, l), full in zip(keys, _xchg_wait(handle, after, "grads_wait_" + tag)):
            dims = shard3(n)
            res[n], token = _adamw(full.reshape((N_DEV,) + dims[1:]), W[n].reshape(dims), M1[n].reshape(dims),
                                   M2[n].reshape(dims), res.get(n), l, "adamw_" + n)
            done.append(token)
        return done

    def start_small(names, tag, after):
        handle, token = _xchg_start([_pack([jnp.stack(G[n]) for n in names])], False, after, "small_start_" + tag)
        return (names, handle), [token]

    def land_small(pending_small, tag, after):
        names, handle = pending_small
        (parts,) = _xchg_wait(handle, after, "small_wait_" + tag)
        like = [W[n] for n in names]
        outs, _ = _adamw(parts, _pack(like)[None], _pack([M1[n] for n in names])[None],
                         _pack([M2[n] for n in names])[None], None, 0, "adamw_small_" + tag)
        unpacked = [_unpack(o, like) for o in outs]
        for idx, n in enumerate(names):
            res[n] = [unpacked[q][idx] for q in range(4)]

    spatial = ['gmlp_w_s', 'gmlp_b_s']
    token = []
    for i in reversed(range(depth)):
        j = i // 2
        s = saved[i]
        g_proj = _mm(p_bf[i], d_pp, ta=True, out_dtypes=(BF16,), name="ple_proj_dw")
        g_proj = jnp.transpose(g_proj.reshape(g_proj.shape[0], N_DEV, -1), (1, 0, 2))
        g_gate = _mm(s['hn3'], d_gt, ta=True, out_dtypes=(BF16,), name="ple_gate_dw")
        dh, dh_bf, dg = _mm(d_gt, FW['ple_w_gate'][i], tb=True, deps=token, name="ple_gate_dx",
                            **norm_back(s['h2'], dh, row(W['norm_ple'][i])))
        G['norm_ple'][i] = dg[0]
        d_a = _mm(dh_bf, FW['ffn_w_down'][i], tb=True, extras=(s['a'],),
                  epilogue=lambda acc, a_: (acc * (2.0 * jnp.maximum(a_.astype(F32), 0.0)),), out_dtypes=(BF16,),
                  name="ffn_down_dx")
        g_down = _mm(s['r'], dh_bf, ta=True, out_dtypes=(BF16,), name="ffn_down_dw")
        g_up = _mm(s['hn2'], d_a, ta=True, out_blocks=N_DEV, out_dtypes=(BF16,), name="ffn_up_dw")
        token = send_grads("mlp%d" % i, [(('ple_w_proj', i), g_proj), (('ple_w_gate', i), by_owner(g_gate)),
                                         (('ffn_w_down', i), by_owner(g_down)), (('ffn_w_up', i), g_up)])
        if len(flying) > 1:
            token = token + land_grads(token[0])
        dh, dh_bf, dg = _mm(d_a, FW['ffn_w_up'][i], tb=True, deps=token, name="ffn_up_dx",
                            **norm_back(s['h1'], dh, row(W['norm_ffn'][i])))
        G['norm_ffn'][i] = dg[0]
        if i % 2 == 0:
            d_o = _mm(dh_bf, FW['mla_w_out'][j], tb=True, out_dtypes=(BF16,), name="mla_out_dx")
            g_out = _mm(s['o'], dh_bf, ta=True, out_dtypes=(BF16,), name="mla_out_dw")
            dq, dk, dv = _attn_bwd(s['q'], s['k'], s['v'], s['o'], d_o, s['lse'], seq, "attn_bwd")
            d_q_raw, d_kv_raw, d_kr, g1, g2, g3, g4 = _prep2_bwd(
                s['q_raw'], s['kv_raw'], s['lat'], cos, sin, rmat, s['gains'], dq, dk, dv, "mla_prep2_bwd")
            G['mla_q_nope_g'][j], G['mla_q_rope_g'][j] = g1[0], g2[0]
            G['mla_k_nope_g'][j], G['mla_k_rope_g'][j] = g3[0], g4[0]
            g_uq = _mm(s['cq'], d_q_raw, ta=True, out_blocks=N_DEV, out_dtypes=(BF16,), name="mla_uq_dw")
            g_ukv = _mm(s['ckv'], d_kv_raw, ta=True, out_blocks=N_DEV, out_dtypes=(BF16,), name="mla_ukv_dw")
            d_cq = _mm(d_q_raw, FW['mla_w_uq'][j], tb=True, out_dtypes=(BF16,), name="mla_uq_dx")
            d_ckv = _mm(d_kv_raw, FW['mla_w_ukv'][j], tb=True, out_dtypes=(BF16,), name="mla_ukv_dx")
            d_lat, dga, dgb = _prep1_bwd(s['lat'], d_cq, d_ckv, d_kr, row(W['mla_q_lora_g'][j]),
                                         row(W['mla_kv_lora_g'][j]), "mla_prep1_bwd")
            G['mla_q_lora_g'][j], G['mla_kv_lora_g'][j] = dga[0], dgb[0]
            g_down = _mm(s['hn'], d_lat, ta=True, out_dtypes=(BF16,), name="mla_down_dw")
            grads = [(('mla_w_out', j), by_owner(g_out)), (('mla_w_uq', j), g_uq), (('mla_w_ukv', j), g_ukv),
                     (('mla_w_down', j), by_owner(g_down))]
            last = (d_lat, FW['mla_w_down'][j], "mla_down_dx")
        else:
            d_y = _mm(dh_bf, FW['gmlp_w_out'][j], tb=True, out_dtypes=(BF16,), name="gmlp_out_dx")
            g_out = _mm(s['y'], dh_bf, ta=True, out_dtypes=(BF16,), name="gmlp_out_dw")
            d_u, d_vn, d_ws, d_bs = _sgu_bwd(s['u'], s['vn'], d_y, W['gmlp_w_s'][j], s['bs3'], "gmlp_sgu_bwd")
            G['gmlp_w_s'][j], G['gmlp_b_s'][j] = d_ws, d_bs[:, :, 0]
            d_z, d_lg, d_lb = _gelu_ln_bwd(s['z'], d_u, d_vn, row(FW['gmlp_ln_g'][j]), row(FW['gmlp_ln_b'][j]),
                                           "gmlp_gelu_ln_bwd")
            g_in = _mm(s['hn'], d_z, ta=True, out_blocks=N_DEV, out_dtypes=(BF16,), name="gmlp_in_dw")
            grads = [(('gmlp_w_out', j), by_owner(g_out)), (('gmlp_ln_g', j), by_owner(d_lg[0])),
                     (('gmlp_ln_b', j), by_owner(d_lb[0])), (('gmlp_w_in', j), g_in)]
            last = (d_z, FW['gmlp_w_in'][j], "gmlp_in_dx")
        token = send_grads("mix%d" % i, grads)
        if len(flying) > 1:
            token = token + land_grads(token[0])
        if i > 0:
            dh, d_pp, d_gt, dg = _mm(last[0], last[1], tb=True, deps=token, name=last[2], **norm_back(
                s['h0'], dh, row(W['norm_mix'][i]), (saved[i - 1]['gt'], saved[i - 1]['pp'])))
        else:
            dh, _, dg = _mm(last[0], last[1], tb=True, deps=token, name=last[2] + "_first",
                            **norm_back(s['h0'], dh, row(W['norm_mix'][i])))
        G['norm_mix'][i] = dg[0]
        token = []
        if i == 1:
            small_a, token = start_small(spatial, "spatial", dh)
    grad_x = dh.reshape(x.shape)

    small_b, _ = start_small([n for n in REPLICATED if n not in spatial], "gains", dh)
    while flying:
        land_grads(dh)
    land_small(small_a, "spatial", dh)
    land_small(small_b, "gains", dh)

    out = lambda q: [res[n][q].reshape(W[n].shape) for n in WEIGHTS]
    return (loss, grad_x, *out(0), *out(1), *out(2), *out(3))
```

```python
import math

import numpy as np
import jax
import jax.numpy as jnp
from jax import lax
from jax.experimental import pallas as pl
from jax.experimental.pallas import tpu as pltpu

F32 = jnp.float32
BF16 = jnp.bfloat16

N_DEV = 8
MESH_AXES = ("x", "y", "c")
HEADS = 8
NOPE = 128
ROPE = 64
VDIM = 128
QK = NOPE + ROPE
Q_LORA = 384
KV_LORA = 256
ROPE_BASE = 10000.0
CHUNK = 128
GROUPS = 8
EPS = 1e-6
LR, B1, B2, ADAM_EPS, WD, STEP = 0.001, 0.9, 0.999, 1e-08, 0.01, 10
LANE = 128
VMEM_LIMIT = 56 * 1024 * 1024
MM_VMEM_BUDGET = 40 * 1024 * 1024

WEIGHTS = ['norm_mix', 'norm_ffn', 'norm_ple', 'mla_w_down', 'mla_q_lora_g', 'mla_kv_lora_g', 'mla_w_uq',
           'mla_w_ukv', 'mla_q_nope_g', 'mla_q_rope_g', 'mla_k_nope_g', 'mla_k_rope_g', 'mla_w_out', 'gmlp_w_in',
           'gmlp_ln_g', 'gmlp_ln_b', 'gmlp_w_s', 'gmlp_b_s', 'gmlp_w_out', 'ffn_w_up', 'ffn_w_down', 'ple_w_gate',
           'ple_w_proj']
SHARD_AXIS = {'mla_w_down': 1, 'mla_w_uq': 2, 'mla_w_ukv': 2, 'mla_w_out': 1, 'gmlp_w_in': 2, 'gmlp_ln_g': 1,
              'gmlp_ln_b': 1, 'gmlp_w_out': 1, 'ffn_w_up': 2, 'ffn_w_down': 1, 'ple_w_gate': 1, 'ple_w_proj': 2}
SHARDED = list(SHARD_AXIS)
REPLICATED = [n for n in WEIGHTS if n not in SHARD_AXIS]
F32_PAYLOAD = ('gmlp_ln_g', 'gmlp_ln_b')


def _pick(dim, pref, align=LANE):
    if dim <= pref:
        return dim
    b = (pref // align) * align
    while b >= align:
        if dim % b == 0:
            return b
        b -= align
    return dim


def _params(sem):
    return pltpu.CompilerParams(dimension_semantics=sem, vmem_limit_bytes=VMEM_LIMIT)


def _mm(a, b, *, ta=False, tb=False, extras=(), rows=(), epilogue=None, out_dtypes=(F32,), out_blocks=None,
        acc_rows=0, deps=(), name, bn=1024):
    a3, b3 = a.ndim == 3, b.ndim == 3
    assert not (ta and a3)
    if ta:
        K, M = a.shape
        ka = K
    elif a3:
        M, ka = a.shape[1:]
        K = a.shape[0] * ka
    else:
        M, K = a.shape
        ka = K
    if tb:
        N, kb = b.shape[-2:]
        nb = N
        K2 = b.shape[0] * kb if b3 else kb
    else:
        kb, nb = b.shape[-2:]
        K2 = kb
        N = b.shape[0] * nb if b3 else nb
    assert K == K2, (a.shape, b.shape, ta, tb)
    no_ = N // out_blocks if out_blocks else N
    assert not (out_blocks and extras)
    size = lambda t: jnp.dtype(t).itemsize
    per_out = sum(size(e.dtype) for e in extras) + sum(size(t) for t in out_dtypes)
    bn = _pick(min(nb, no_), bn)
    assert not acc_rows or bn == N, "row sums are kept across row tiles only when one tile spans the columns"
    k_lim = min(ka, kb)
    fits = lambda m, k: 2 * (m * k * size(a.dtype) + k * bn * size(b.dtype) + m * bn * per_out) + 4 * m * bn
    ms = [m for m in sorted({min(M, c) for c in (2048, 1024, 512, 256)}, reverse=True) if M % m == 0]
    ks = [k for k in dict.fromkeys((k_lim, 2048, 1024, 512, 256)) if k <= k_lim and k_lim % k == 0]
    bm, bk = next(((m, k) for k in ks for m in ms if fits(m, k) <= MM_VMEM_BUDGET), (ms[-1], ks[-1]))
    nk = K // bk
    ne, no = len(extras) + len(rows), len(out_dtypes)
    first_out = 2 + ne + len(deps)
    dims = (((0,) if ta else (1,), (1,) if tb else (0,)), ((), ()))

    def finish(r, e_refs, o_refs):
        outs = epilogue(r, *[e[...] for e in e_refs]) if epilogue is not None else (r,)
        for o, v in zip(o_refs[:no], outs):
            o[...] = v.astype(o.dtype)
        for o, v in zip(o_refs[no:], outs[no:]):
            first = pl.program_id(0) == 0

            @pl.when(first)
            def _():
                o[...] = v

            @pl.when(jnp.logical_not(first))
            def _():
                o[...] += v

    def body(*refs):
        a_ref, b_ref = refs[0], refs[1]
        e_refs = refs[2:2 + ne]
        o_refs = refs[first_out:first_out + no + acc_rows]
        part = lax.dot_general(a_ref[...].astype(BF16), b_ref[...].astype(BF16), dims, preferred_element_type=F32)
        if nk == 1:
            finish(part, e_refs, o_refs)
            return
        acc = refs[-1]
        k = pl.program_id(2)

        @pl.when(k == 0)
        def _():
            acc[...] = part

        @pl.when(k > 0)
        def _():
            acc[...] += part

        @pl.when(k == nk - 1)
        def _():
            finish(acc[...], e_refs, o_refs)

    ka_t, kb_t, nb_t, no_t = ka // bk, kb // bk, nb // bn, no_ // bn
    if ta:
        a_spec = pl.BlockSpec((bk, bm), lambda i, j, k: (k, i))
    elif a3:
        a_spec = pl.BlockSpec((None, bm, bk), lambda i, j, k: (k // ka_t, i, k % ka_t))
    else:
        a_spec = pl.BlockSpec((bm, bk), lambda i, j, k: (i, k))
    if tb:
        b_spec = (pl.BlockSpec((None, bn, bk), lambda i, j, k: (k // kb_t, j, k % kb_t)) if b3 else
                  pl.BlockSpec((bn, bk), lambda i, j, k: (j, k)))
    else:
        b_spec = (pl.BlockSpec((None, bk, bn), lambda i, j, k: (j // nb_t, k, j % nb_t)) if b3 else
                  pl.BlockSpec((bk, bn), lambda i, j, k: (k, j)))
    if out_blocks:
        o_spec = lambda: pl.BlockSpec((None, bm, bn), lambda i, j, k: (j // no_t, i, j % no_t))
        o_shape = (out_blocks, M, no_)
    else:
        o_spec = lambda: pl.BlockSpec((bm, bn), lambda i, j, k: (i, j))
        o_shape = (M, N)
    outs = pl.pallas_call(
        body, name=name,
        grid=(M // bm, N // bn, nk),
        in_specs=[a_spec, b_spec] + [pl.BlockSpec((bm, bn), lambda i, j, k: (i, j)) for _ in extras]
        + [pl.BlockSpec((1, bn), lambda i, j, k: (0, j)) for _ in rows]
        + [pl.BlockSpec(memory_space=pl.ANY)] * len(deps),
        out_specs=[o_spec() for _ in out_dtypes] + [pl.BlockSpec((1, bn), lambda i, j, k: (0, j))] * acc_rows,
        out_shape=[jax.ShapeDtypeStruct(o_shape, dt) for dt in out_dtypes]
        + [jax.ShapeDtypeStruct((1, N), F32)] * acc_rows,
        scratch_shapes=[pltpu.VMEM((bm, bn), F32)] if nk > 1 else [],
        compiler_params=_params(("arbitrary",) * 3 if acc_rows else ("parallel", "parallel", "arbitrary")),
    )(a, b, *extras, *rows, *deps)
    return outs[0] if no + acc_rows == 1 else outs


def _rowcall(fn, rows, params, row_outs, acc_outs=(), *, bm=256, deps=(), name):
    T = rows[0].shape[0]
    bm = _pick(T, bm, 8)
    nr, npar, nro, nao = len(rows), len(params), len(row_outs), len(acc_outs)
    first_out = nr + npar + len(deps)

    def body(*refs):
        vals = [r[...] for r in refs[:nr + npar]]
        res = fn(*vals)
        ro = refs[first_out:first_out + nro]
        ao = refs[first_out + nro:]
        for r, v in zip(ro, res[:nro]):
            r[...] = v.astype(r.dtype)
        if nao:
            @pl.when(pl.program_id(0) == 0)
            def _():
                for r in ao:
                    r[...] = jnp.zeros_like(r)

            for r, v in zip(ao, res[nro:]):
                r[...] += v

    def whole(shape):
        nd = len(shape)
        return pl.BlockSpec(tuple(shape), lambda i: (0,) * nd)

    outs = pl.pallas_call(
        body, name=name,
        grid=(T // bm,),
        in_specs=[pl.BlockSpec((bm, r.shape[1]), lambda i: (i, 0)) for r in rows] + [whole(q.shape) for q in params]
        + [pl.BlockSpec(memory_space=pl.ANY)] * len(deps),
        out_specs=[pl.BlockSpec((bm, c), lambda i: (i, 0)) for c, _ in row_outs] + [whole(s) for s in acc_outs],
        out_shape=[jax.ShapeDtypeStruct((T, c), dt) for c, dt in row_outs]
        + [jax.ShapeDtypeStruct(tuple(s), F32) for s in acc_outs],
        compiler_params=_params(("arbitrary",) if nao else ("parallel",)),
    )(*rows, *params, *deps)
    return outs


def _rmsn(x, g):
    return x * lax.rsqrt(jnp.mean(x * x, axis=-1, keepdims=True) + EPS) * g


def _gelu(x):
    return 0.5 * x * (1.0 + jnp.tanh(math.sqrt(2.0 / math.pi) * (x + 0.044715 * (x * x * x))))


def _layer_norm(x, g, b):
    mu = jnp.mean(x, axis=-1, keepdims=True)
    xc = x - mu
    return xc * lax.rsqrt(jnp.mean(xc * xc, axis=-1, keepdims=True) + EPS) * g + b


def _sigmoid(x):
    return 1.0 / (1.0 + jnp.exp(-x))


def _rot(x, cos, sin, rmat):
    return x * cos + jnp.dot(x, rmat, precision=lax.Precision.HIGHEST, preferred_element_type=F32) * sin


def _rms_fwd(h, g, name, deps=()):
    return _rowcall(lambda x, gg: (_rmsn(x, gg),), [h], [g], [(h.shape[1], BF16)], bm=512, deps=deps, name=name)[0]


def _rope_tables(pos, name):
    inv = np.float32(ROPE_BASE) ** (-(np.arange(0, ROPE, 2, dtype=np.float32) / np.float32(ROPE)))
    inv = jnp.asarray(np.concatenate([inv, inv])[None, :].astype(np.float32))

    def fn(pp, iv):
        ang = pp.astype(F32) * iv
        return jnp.cos(ang), jnp.sin(ang)

    return _rowcall(fn, [pos], [inv], [(ROPE, F32), (ROPE, F32)], name=name)


def _rot_matrix():
    r = np.zeros((ROPE, ROPE), np.float32)
    half = ROPE // 2
    for j in range(half):
        r[j + half, j] = -1.0
        r[j, j + half] = 1.0
    return jnp.asarray(r)


def _prep1_fwd(lat, gq, gkv, name):
    def fn(l, a, b):
        return _rmsn(l[:, :Q_LORA], a), _rmsn(l[:, Q_LORA:Q_LORA + KV_LORA], b)

    return _rowcall(fn, [lat], [gq, gkv], [(Q_LORA, BF16), (KV_LORA, BF16)], bm=512, name=name)


def _prep1_bwd(lat, d_cq, d_ckv, d_kr, gq, gkv, name):
    def fn(l, dq, dkv, dkr, a, b):
        _, vq = jax.vjp(_rmsn, l[:, :Q_LORA], a)
        _, vkv = jax.vjp(_rmsn, l[:, Q_LORA:Q_LORA + KV_LORA], b)
        dxq, dga = vq(dq.astype(F32))
        dxkv, dgb = vkv(dkv.astype(F32))
        return jnp.concatenate([dxq, dxkv, dkr], axis=1), dga, dgb

    return _rowcall(fn, [lat, d_cq, d_ckv, d_kr], [gq, gkv], [(lat.shape[1], BF16)], [gq.shape, gkv.shape], bm=512,
                    name=name)


def _qk_fn(qn_raw, qr_raw, kn_raw, kr_raw, gqn, gqr, gkn, gkr, cos, sin, rmat):
    return (_rmsn(qn_raw, gqn), _rot(_rmsn(qr_raw, gqr), cos, sin, rmat),
            _rmsn(kn_raw, gkn), _rot(_rmsn(kr_raw, gkr), cos, sin, rmat))


def _prep2_fwd(q_raw, kv_raw, lat, cos, sin, rmat, gains, name, bm=1024):
    H, T, _ = q_raw.shape
    bm = _pick(T, bm, 8)
    kr0 = Q_LORA + KV_LORA

    def body(q_ref, kv_ref, lat_ref, cos_ref, sin_ref, r_ref, gqn, gqr, gkn, gkr, qo, ko, vo):
        qr, kvr = q_ref[...], kv_ref[...]
        qn, qro, kn, kro = _qk_fn(qr[:, :NOPE], qr[:, NOPE:], kvr[:, :NOPE], lat_ref[:, kr0:kr0 + ROPE],
                                  gqn[...], gqr[...], gkn[...], gkr[...], cos_ref[...], sin_ref[...], r_ref[...])
        qo[:, :NOPE] = qn.astype(BF16)
        qo[:, NOPE:] = qro.astype(BF16)
        ko[:, :NOPE] = kn.astype(BF16)
        ko[:, NOPE:] = kro.astype(BF16)
        vo[...] = kvr[:, NOPE:].astype(BF16)

    hb = lambda c: pl.BlockSpec((None, bm, c), lambda m, h: (h, m, 0))
    rb = lambda c: pl.BlockSpec((bm, c), lambda m, h: (m, 0))
    wb = lambda s: pl.BlockSpec(tuple(s), lambda m, h: (0, 0))
    return pl.pallas_call(
        body, name=name, grid=(T // bm, H),
        in_specs=[hb(QK), hb(NOPE + VDIM), rb(lat.shape[1]), rb(ROPE), rb(ROPE), wb(rmat.shape)]
        + [wb(g.shape) for g in gains],
        out_specs=[hb(QK), hb(QK), hb(VDIM)],
        out_shape=[jax.ShapeDtypeStruct((H, T, QK), BF16), jax.ShapeDtypeStruct((H, T, QK), BF16),
                   jax.ShapeDtypeStruct((H, T, VDIM), BF16)],
        compiler_params=_params(("parallel", "parallel")),
    )(q_raw, kv_raw, lat, cos, sin, rmat, *gains)


def _prep2_bwd(q_raw, kv_raw, lat, cos, sin, rmat, gains, dq, dk, dv, name, bm=1024):
    H, T, _ = q_raw.shape
    bm = _pick(T, bm, 8)
    kr0 = Q_LORA + KV_LORA

    def body(q_ref, kv_ref, lat_ref, cos_ref, sin_ref, r_ref, gqn, gqr, gkn, gkr, dq_ref, dk_ref, dv_ref,
             dqo, dkvo, dkro, o_gqn, o_gqr, o_gkn, o_gkr):
        m, h = pl.program_id(0), pl.program_id(1)
        qr, kvr = q_ref[...], kv_ref[...]
        cos_v, sin_v, r_v = cos_ref[...], sin_ref[...], r_ref[...]
        f = lambda a, b, c, d, g1, g2, g3, g4: _qk_fn(a, b, c, d, g1, g2, g3, g4, cos_v, sin_v, r_v)
        _, vjp = jax.vjp(f, qr[:, :NOPE], qr[:, NOPE:], kvr[:, :NOPE], lat_ref[:, kr0:kr0 + ROPE],
                         gqn[...], gqr[...], gkn[...], gkr[...])
        dqv, dkv_ = dq_ref[...], dk_ref[...]
        d_qn, d_qr, d_kn, d_kr, g1, g2, g3, g4 = vjp((dqv[:, :NOPE], dqv[:, NOPE:], dkv_[:, :NOPE], dkv_[:, NOPE:]))
        dqo[:, :NOPE] = d_qn.astype(BF16)
        dqo[:, NOPE:] = d_qr.astype(BF16)
        dkvo[:, :NOPE] = d_kn.astype(BF16)
        dkvo[:, NOPE:] = dv_ref[...].astype(BF16)

        @pl.when(h == 0)
        def _():
            dkro[...] = jnp.zeros_like(dkro)

        dkro[...] += d_kr

        @pl.when((h == 0) & (m == 0))
        def _():
            for o in (o_gqn, o_gqr, o_gkn, o_gkr):
                o[...] = jnp.zeros_like(o)

        for o, g in zip((o_gqn, o_gqr, o_gkn, o_gkr), (g1, g2, g3, g4)):
            o[...] += g

    hb = lambda c: pl.BlockSpec((None, bm, c), lambda m, h: (h, m, 0))
    rb = lambda c: pl.BlockSpec((bm, c), lambda m, h: (m, 0))
    wb = lambda s: pl.BlockSpec(tuple(s), lambda m, h: (0, 0))
    return pl.pallas_call(
        body, name=name, grid=(T // bm, H),
        in_specs=[hb(QK), hb(NOPE + VDIM), rb(lat.shape[1]), rb(ROPE), rb(ROPE), wb(rmat.shape)]
        + [wb(g.shape) for g in gains] + [hb(QK), hb(QK), hb(VDIM)],
        out_specs=[hb(QK), hb(NOPE + VDIM), rb(ROPE)] + [wb(g.shape) for g in gains],
        out_shape=[jax.ShapeDtypeStruct((H, T, QK), BF16), jax.ShapeDtypeStruct((H, T, NOPE + VDIM), BF16),
                   jax.ShapeDtypeStruct((T, ROPE), F32)] + [jax.ShapeDtypeStruct(g.shape, F32) for g in gains],
        compiler_params=_params(("arbitrary", "arbitrary")),
    )(q_raw, kv_raw, lat, cos, sin, rmat, *gains, dq, dk, dv)


_NT = (((1,), (1,)), ((), ()))
_TN = (((0,), (0,)), ((), ()))


def _causal(blk):
    return lax.broadcasted_iota(jnp.int32, (blk, blk), 1) <= lax.broadcasted_iota(jnp.int32, (blk, blk), 0)


def _attn_fwd(q, k, v, seq, name, blk=512):
    H, T, _ = q.shape
    nb = T // seq
    blk = _pick(seq, blk)
    nq = seq // blk
    scale = float(QK) ** -0.5

    def body(q_ref, k_ref, v_ref, o_ref, lse_ref):
        qi = pl.program_id(2)
        qb = q_ref[...]

        def step(j, carry, diagonal):
            m, l, acc = carry
            ks = pl.ds(pl.multiple_of(j * blk, blk), blk)
            s = lax.dot_general(qb, k_ref[ks, :], _NT, preferred_element_type=F32) * scale
            if diagonal:
                s = jnp.where(_causal(blk), s, -jnp.inf)
            m_new = jnp.maximum(m, jnp.max(s, axis=1, keepdims=True))
            pr = jnp.exp(s - m_new)
            alpha = jnp.exp(m - m_new)
            l = alpha * l + jnp.sum(pr, axis=1, keepdims=True)
            acc = alpha * acc + jnp.dot(pr.astype(BF16), v_ref[ks, :], preferred_element_type=F32)
            return m_new, l, acc

        init = (jnp.full((blk, 1), -jnp.inf, F32), jnp.zeros((blk, 1), F32), jnp.zeros((blk, VDIM), F32))
        below = lax.fori_loop(0, qi, lambda j, c: step(j, c, False), init)
        m, l, acc = step(qi, below, True)
        o_ref[...] = (acc / l).astype(o_ref.dtype)
        lse_ref[...] = m + jnp.log(l)

    return pl.pallas_call(
        body, name=name, grid=(H, nb, nq),
        in_specs=[pl.BlockSpec((None, blk, QK), lambda h, b, i: (h, b * nq + i, 0)),
                  pl.BlockSpec((None, seq, QK), lambda h, b, i: (h, b, 0)),
                  pl.BlockSpec((None, seq, VDIM), lambda h, b, i: (h, b, 0))],
        out_specs=[pl.BlockSpec((blk, VDIM), lambda h, b, i: (b * nq + i, h)),
                   pl.BlockSpec((None, blk, 1), lambda h, b, i: (h, b * nq + i, 0))],
        out_shape=[jax.ShapeDtypeStruct((T, H * VDIM), BF16), jax.ShapeDtypeStruct((H, T, 1), F32)],
        compiler_params=_params(("parallel", "parallel", "parallel")),
    )(q, k, v)


def _attn_bwd(q, k, v, o, do, lse, seq, name, blk=512):
    H, T, _ = q.shape
    nb = T // seq
    blk = _pick(seq, blk)
    nq = seq // blk
    scale = float(QK) ** -0.5

    def body(q_ref, k_ref, v_ref, o_ref, do_ref, lse_ref, dq_ref, dk_ref, dv_ref):
        dk_ref[...] = jnp.zeros_like(dk_ref)
        dv_ref[...] = jnp.zeros_like(dv_ref)

        def qloop(i, carry):
            qs = pl.ds(pl.multiple_of(i * blk, blk), blk)
            qb = q_ref[qs, :]
            dob = do_ref[qs, :]
            dof = dob.astype(F32)
            lse_b = lse_ref[qs, :]
            delta = jnp.sum(dof * o_ref[qs, :].astype(F32), axis=1, keepdims=True)

            def kstep(j, dq_acc, diagonal):
                ks = pl.ds(pl.multiple_of(j * blk, blk), blk)
                kb = k_ref[ks, :]
                vb = v_ref[ks, :]
                s = lax.dot_general(qb, kb, _NT, preferred_element_type=F32) * scale
                pr = jnp.exp(s - lse_b)
                if diagonal:
                    pr = jnp.where(_causal(blk), pr, 0.0)
                dp = lax.dot_general(dob, vb, _NT, preferred_element_type=F32)
                ds = (pr * (dp - delta) * scale).astype(BF16)
                prb = pr.astype(BF16)
                dv_ref[ks, :] += lax.dot_general(prb, dob, _TN, preferred_element_type=F32)
                dk_ref[ks, :] += lax.dot_general(ds, qb, _TN, preferred_element_type=F32)
                return dq_acc + jnp.dot(ds, kb, preferred_element_type=F32)

            below = lax.fori_loop(0, i, lambda j, c: kstep(j, c, False), jnp.zeros((blk, QK), F32))
            dq_ref[qs, :] = kstep(i, below, True)
            return carry

        lax.fori_loop(0, nq, qloop, 0)

    hb = lambda c: pl.BlockSpec((None, seq, c), lambda h, b: (h, b, 0))
    cb = lambda: pl.BlockSpec((seq, VDIM), lambda h, b: (b, h))
    return pl.pallas_call(
        body, name=name, grid=(H, nb),
        in_specs=[hb(QK), hb(QK), hb(VDIM), cb(), cb(), hb(1)],
        out_specs=[hb(QK), hb(QK), hb(VDIM)],
        out_shape=[jax.ShapeDtypeStruct((H, T, QK), F32), jax.ShapeDtypeStruct((H, T, QK), F32),
                   jax.ShapeDtypeStruct((H, T, VDIM), F32)],
        compiler_params=_params(("parallel", "parallel")),
    )(q, k, v, o, do, lse)


def _gelu_ln_fwd(z, g, b, name):
    half = z.shape[1] // 2

    def fn(zz, gg, bb):
        return _gelu(zz[:, :half].astype(F32)), _layer_norm(_gelu(zz[:, half:].astype(F32)), gg, bb)

    return _rowcall(fn, [z], [g, b], [(half, BF16), (half, BF16)], bm=512, name=name)


def _gelu_ln_bwd(z, d_u, d_vn, g, b, name):
    half = z.shape[1] // 2

    def gelu_and_slope(x):
        c, a = math.sqrt(2.0 / math.pi), 0.044715
        x2 = x * x
        t = jnp.tanh(c * x * (1.0 + a * x2))
        return 0.5 * x * (1.0 + t), 0.5 * (1.0 + t) + 0.5 * x * (1.0 - t * t) * (c * (1.0 + 3.0 * a * x2))

    def fn(zz, du, dvn, gg, bb):
        du, dvn = du.astype(F32), dvn.astype(F32)
        _, su = gelu_and_slope(zz[:, :half].astype(F32))
        v, sv = gelu_and_slope(zz[:, half:].astype(F32))
        xc = v - jnp.mean(v, axis=-1, keepdims=True)
        rstd = lax.rsqrt(jnp.mean(xc * xc, axis=-1, keepdims=True) + EPS)
        y = xc * rstd
        dy = dvn * gg
        dv = rstd * (dy - jnp.mean(dy, axis=-1, keepdims=True) - y * jnp.mean(dy * y, axis=-1, keepdims=True))
        dg = jnp.sum(dvn * y, axis=0, keepdims=True)
        db = jnp.sum(dvn, axis=0, keepdims=True)
        return jnp.concatenate([du * su, dv * sv], axis=1), dg, db

    return _rowcall(fn, [z, d_u, d_vn], [g, b], [(z.shape[1], BF16)], [g.shape, b.shape], bm=128, name=name)


def _tril_bf16(ws):
    t = lax.broadcasted_iota(jnp.int32, ws.shape, 0)
    s = lax.broadcasted_iota(jnp.int32, ws.shape, 1)
    return jnp.where(s <= t, ws, 0.0).astype(BF16)


def _sgu_fwd(u, vn, ws, bs, name, bm=2048):
    T, half = u.shape
    gd = half // GROUPS
    bm = _pick(T, bm, CHUNK)
    nc = bm // CHUNK

    def body(u_ref, vn_ref, ws_ref, bs_ref, y_ref):
        wm = _tril_bf16(ws_ref[...])
        bias = bs_ref[...]
        for c in range(nc):
            rs = slice(c * CHUNK, (c + 1) * CHUNK)
            sv = jnp.dot(wm, vn_ref[rs, :], preferred_element_type=F32) + bias
            y_ref[rs, :] = (u_ref[rs, :].astype(F32) * sv).astype(y_ref.dtype)

    tb = lambda: pl.BlockSpec((bm, gd), lambda g, i: (i, g))
    return pl.pallas_call(
        body, name=name, grid=(GROUPS, T // bm),
        in_specs=[tb(), tb(), pl.BlockSpec((None, CHUNK, CHUNK), lambda g, i: (g, 0, 0)),
                  pl.BlockSpec((None, CHUNK, 1), lambda g, i: (g, 0, 0))],
        out_specs=tb(),
        out_shape=jax.ShapeDtypeStruct((T, half), BF16),
        compiler_params=_params(("parallel", "parallel")),
    )(u, vn, ws, bs)


def _sgu_bwd(u, vn, dy, ws, bs, name, bm=2048):
    T, half = u.shape
    gd = half // GROUPS
    bm = _pick(T, bm, CHUNK)
    nc = bm // CHUNK

    def body(u_ref, vn_ref, dy_ref, ws_ref, bs_ref, du_ref, dvn_ref, dws_ref, dbs_ref):
        @pl.when(pl.program_id(1) == 0)
        def _():
            dws_ref[...] = jnp.zeros_like(dws_ref)
            dbs_ref[...] = jnp.zeros_like(dbs_ref)

        wm = _tril_bf16(ws_ref[...])
        bias = bs_ref[...]
        dws = jnp.zeros((CHUNK, CHUNK), F32)
        dbs = jnp.zeros((CHUNK, 1), F32)
        for c in range(nc):
            rs = slice(c * CHUNK, (c + 1) * CHUNK)
            vb = vn_ref[rs, :]
            dyb = dy_ref[rs, :].astype(F32)
            sv = jnp.dot(wm, vb, preferred_element_type=F32) + bias
            du_ref[rs, :] = (dyb * sv).astype(du_ref.dtype)
            dsv = dyb * u_ref[rs, :].astype(F32)
            dsb = dsv.astype(BF16)
            dvn_ref[rs, :] = lax.dot_general(wm, dsb, _TN, preferred_element_type=F32).astype(dvn_ref.dtype)
            dws = dws + lax.dot_general(dsb, vb, _NT, preferred_element_type=F32)
            dbs = dbs + jnp.sum(dsv, axis=1, keepdims=True)
        t = lax.broadcasted_iota(jnp.int32, (CHUNK, CHUNK), 0)
        s = lax.broadcasted_iota(jnp.int32, (CHUNK, CHUNK), 1)
        dws_ref[...] += jnp.where(s <= t, dws, 0.0)
        dbs_ref[...] += dbs

    tb = lambda: pl.BlockSpec((bm, gd), lambda g, i: (i, g))
    wsb = lambda: pl.BlockSpec((None, CHUNK, CHUNK), lambda g, i: (g, 0, 0))
    bsb = lambda: pl.BlockSpec((None, CHUNK, 1), lambda g, i: (g, 0, 0))
    return pl.pallas_call(
        body, name=name, grid=(GROUPS, T // bm),
        in_specs=[tb(), tb(), tb(), wsb(), bsb()],
        out_specs=[tb(), tb(), wsb(), bsb()],
        out_shape=[jax.ShapeDtypeStruct((T, half), BF16), jax.ShapeDtypeStruct((T, half), BF16),
                   jax.ShapeDtypeStruct(ws.shape, F32), jax.ShapeDtypeStruct(bs.shape, F32)],
        compiler_params=_params(("parallel", "arbitrary")),
    )(u, vn, dy, ws, bs)


def _gate_back(d, gate, proj):
    sg = _sigmoid(gate)
    return d * sg, d * proj * sg * (1.0 - sg)


def _loss_head(y, t, gate, proj, name):
    d_model = y.shape[1]

    def fn(yy, tt, gg, pp):
        d = yy - tt
        part = 0.5 * jnp.sum(jnp.mean(d * d, axis=-1, keepdims=True), axis=0, keepdims=True)
        dy = d / d_model
        return (dy,) + _gate_back(dy, gg, pp) + (jnp.zeros((1, LANE), F32) + part,)

    dy, d_pp, d_gt, part = _rowcall(fn, [y, t, gate, proj], [], [(d_model, F32), (d_model, BF16), (d_model, BF16)],
                                    [(1, LANE)], name=name)
    return dy, d_pp, d_gt, part[0, 0]


def _adamw(parts, w, m, v, prev, layer, name):
    L, R, C = w.shape
    br = _pick(R, max(8, (256 * 1024) // C // 8 * 8), 8)
    c1 = 1.0 - B1 ** STEP
    c2 = 1.0 - B2 ** STEP
    if prev is None:
        prev = [lax.empty(w.shape, F32) for _ in range(4)]

    def body(p_ref, w_ref, m_ref, v_ref, a0, a1, a2, a3, g_o, d_o, m_o, v_o, token):
        g = p_ref[0].astype(F32)
        for d in range(1, N_DEV):
            g = g + p_ref[d].astype(F32)
        mn = B1 * m_ref[...] + (1.0 - B1) * g
        vn = B2 * v_ref[...] + (1.0 - B2) * (g * g)
        g_o[...] = g
        m_o[...] = mn
        v_o[...] = vn
        d_o[...] = -LR * ((mn / c1) / (jnp.sqrt(vn / c2) + ADAM_EPS) + WD * w_ref[...])
        token[...] = jnp.zeros_like(token)

    blk = lambda: pl.BlockSpec((None, br, C), lambda i: (layer, i, 0))
    anywhere = pl.BlockSpec(memory_space=pl.ANY)
    outs = pl.pallas_call(
        body, name=name, grid=(R // br,),
        in_specs=[pl.BlockSpec((N_DEV, br, C), lambda i: (0, i, 0)), blk(), blk(), blk()] + [anywhere] * 4,
        out_specs=[blk(), blk(), blk(), blk(), pl.BlockSpec((8, LANE), lambda i: (0, 0))],
        out_shape=[jax.ShapeDtypeStruct((L, R, C), F32)] * 4 + [jax.ShapeDtypeStruct((8, LANE), F32)],
        input_output_aliases={4: 0, 5: 1, 6: 2, 7: 3},
        compiler_params=_params(("arbitrary",)),
    )(parts, w, m, v, *prev)
    return list(outs[:4]), outs[4]


def _mesh_pos():
    return lax.axis_index("x"), lax.axis_index("y"), lax.axis_index("c")


def _flip(pos, k):
    x, y, c = pos
    px = 1 - x if k & 4 else x
    py = 1 - y if k & 2 else y
    pc = 1 - c if k & 1 else c
    return px, py, pc


HBM_SPEC = pl.BlockSpec(memory_space=pltpu.HBM)
SEM_SPEC = pl.BlockSpec(memory_space=pltpu.SEMAPHORE)
EFFECT = pltpu.SideEffectType.DATAFLOW_SIDE_EFFECTING


def _hbm(a):
    return pltpu.with_memory_space_constraint(a, pltpu.HBM)


def _peer_copy(src, land, send, recv, a, k, pos, scatter):
    peer = _flip(pos, k)
    me = 4 * pos[0] + 2 * pos[1] + pos[2]
    piece = src.at[4 * peer[0] + 2 * peer[1] + peer[2]] if scatter else src
    return pltpu.make_async_remote_copy(
        src_ref=piece, dst_ref=land.at[me], send_sem=send.at[7 * a + k - 1], recv_sem=recv.at[7 * a + k - 1],
        device_id=peer, device_id_type=pl.DeviceIdType.MESH)


def _own_copy(src, land, own, a, pos, scatter):
    me = 4 * pos[0] + 2 * pos[1] + pos[2]
    return pltpu.make_async_copy(src.at[me] if scatter else src, land.at[me], own.at[a])


def _xchg_start(srcs, scatter, after, name):
    n = len(srcs)

    def body(*refs):
        src, land = refs[:n], refs[n:2 * n]
        send, recv, own, token = refs[2 * n + 1], refs[2 * n + 2], refs[2 * n + 3], refs[-1]
        pos = _mesh_pos()
        for k in range(1, N_DEV):
            for a in range(n):
                _peer_copy(src[a], land[a], send, recv, a, k, pos, scatter).start()
        for a in range(n):
            _own_copy(src[a], land[a], own, a, pos, scatter).start()
        token[...] = jnp.zeros_like(token)

    lands = [lax.empty(s.shape if scatter else (N_DEV,) + s.shape, s.dtype) for s in srcs]
    outs = pl.pallas_call(
        body, name=name,
        out_shape=(pltpu.SemaphoreType.DMA((7 * n,)), pltpu.SemaphoreType.DMA((7 * n,)), pltpu.SemaphoreType.DMA((n,)),
                   *[pltpu.HBM(s.shape, s.dtype) for s in srcs], *[pltpu.HBM(l.shape, l.dtype) for l in lands],
                   jax.ShapeDtypeStruct((8, LANE), F32)),
        in_specs=[HBM_SPEC] * (2 * n) + [pl.BlockSpec(memory_space=pl.ANY)],
        out_specs=(SEM_SPEC, SEM_SPEC, SEM_SPEC, *[HBM_SPEC] * (2 * n), pl.BlockSpec(memory_space=pltpu.VMEM)),
        input_output_aliases={q: 3 + q for q in range(2 * n)},
        compiler_params=pltpu.CompilerParams(has_side_effects=EFFECT),
    )(*[_hbm(s) for s in srcs], *[_hbm(l) for l in lands], after)
    handle = dict(send=outs[0], recv=outs[1], own=outs[2], srcs=list(outs[3:3 + n]),
                  lands=list(outs[3 + n:3 + 2 * n]), scatter=scatter)
    return handle, outs[-1]


def _xchg_wait(handle, after, name):
    srcs, lands, scatter = handle['srcs'], handle['lands'], handle['scatter']
    n = len(srcs)

    def body(*refs):
        src, land = refs[:n], refs[n:2 * n]
        send, recv, own = refs[2 * n], refs[2 * n + 1], refs[2 * n + 2]
        pos = _mesh_pos()
        for k in range(1, N_DEV):
            for a in range(n):
                cp = _peer_copy(src[a], land[a], send, recv, a, k, pos, scatter)
                cp.wait_send()
                cp.wait_recv()
        for a in range(n):
            _own_copy(src[a], land[a], own, a, pos, scatter).wait()

    outs = pl.pallas_call(
        body, name=name,
        out_shape=[pltpu.HBM(s.shape, s.dtype) for s in srcs] + [pltpu.HBM(l.shape, l.dtype) for l in lands],
        in_specs=[HBM_SPEC] * (2 * n) + [SEM_SPEC, SEM_SPEC, SEM_SPEC, pl.BlockSpec(memory_space=pl.ANY)],
        out_specs=[HBM_SPEC] * (2 * n),
        input_output_aliases={q: q for q in range(2 * n)},
        compiler_params=pltpu.CompilerParams(has_side_effects=EFFECT),
    )(*srcs, *lands, handle['send'], handle['recv'], handle['own'], after)
    return list(outs[n:])


def _pack(parts):
    flat = jnp.concatenate([q.reshape(-1) for q in parts])
    pad = (-flat.shape[0]) % (8 * LANE)
    return jnp.pad(flat, (0, pad)).reshape(-1, LANE)


def _unpack(packed, like):
    flat = packed.reshape(-1)
    out, o = [], 0
    for q in like:
        out.append(flat[o:o + q.size].reshape(q.shape))
        o += q.size
    return out


def kernel(x, p, positions, norm_mix, norm_ffn, norm_ple, mla_w_down, mla_q_lora_g, mla_kv_lora_g, mla_w_uq, mla_w_ukv, mla_q_nope_g, mla_q_rope_g, mla_k_nope_g, mla_k_rope_g, mla_w_out, gmlp_w_in, gmlp_ln_g, gmlp_ln_b, gmlp_w_s, gmlp_b_s, gmlp_w_out, ffn_w_up, ffn_w_down, ple_w_gate, ple_w_proj, loss_target, m_norm_mix, m_norm_ffn, m_norm_ple, m_mla_w_down, m_mla_q_lora_g, m_mla_kv_lora_g, m_mla_w_uq, m_mla_w_ukv, m_mla_q_nope_g, m_mla_q_rope_g, m_mla_k_nope_g, m_mla_k_rope_g, m_mla_w_out, m_gmlp_w_in, m_gmlp_ln_g, m_gmlp_ln_b, m_gmlp_w_s, m_gmlp_b_s, m_gmlp_w_out, m_ffn_w_up, m_ffn_w_down, m_ple_w_gate, m_ple_w_proj, v_norm_mix, v_norm_ffn, v_norm_ple, v_mla_w_down, v_mla_q_lora_g, v_mla_kv_lora_g, v_mla_w_uq, v_mla_w_ukv, v_mla_q_nope_g, v_mla_q_rope_g, v_mla_k_nope_g, v_mla_k_rope_g, v_mla_w_out, v_gmlp_w_in, v_gmlp_ln_g, v_gmlp_ln_b, v_gmlp_w_s, v_gmlp_b_s, v_gmlp_w_out, v_ffn_w_up, v_ffn_w_down, v_ple_w_gate, v_ple_w_proj):
    W = dict(zip(WEIGHTS, (norm_mix, norm_ffn, norm_ple, mla_w_down, mla_q_lora_g, mla_kv_lora_g, mla_w_uq, mla_w_ukv, mla_q_nope_g, mla_q_rope_g, mla_k_nope_g, mla_k_rope_g, mla_w_out, gmlp_w_in, gmlp_ln_g, gmlp_ln_b, gmlp_w_s, gmlp_b_s, gmlp_w_out, ffn_w_up, ffn_w_down, ple_w_gate, ple_w_proj)))
    M1 = dict(zip(WEIGHTS, (m_norm_mix, m_norm_ffn, m_norm_ple, m_mla_w_down, m_mla_q_lora_g, m_mla_kv_lora_g, m_mla_w_uq, m_mla_w_ukv, m_mla_q_nope_g, m_mla_q_rope_g, m_mla_k_nope_g, m_mla_k_rope_g, m_mla_w_out, m_gmlp_w_in, m_gmlp_ln_g, m_gmlp_ln_b, m_gmlp_w_s, m_gmlp_b_s, m_gmlp_w_out, m_ffn_w_up, m_ffn_w_down, m_ple_w_gate, m_ple_w_proj)))
    M2 = dict(zip(WEIGHTS, (v_norm_mix, v_norm_ffn, v_norm_ple, v_mla_w_down, v_mla_q_lora_g, v_mla_kv_lora_g, v_mla_w_uq, v_mla_w_ukv, v_mla_q_nope_g, v_mla_q_rope_g, v_mla_k_nope_g, v_mla_k_rope_g, v_mla_w_out, v_gmlp_w_in, v_gmlp_ln_g, v_gmlp_ln_b, v_gmlp_w_s, v_gmlp_b_s, v_gmlp_w_out, v_ffn_w_up, v_ffn_w_down, v_ple_w_gate, v_ple_w_proj)))

    nb, seq, d_model = x.shape
    assert d_model <= 1024, "the rms norms fused into matmul epilogues need whole rows in one output tile"
    T = nb * seq
    depth = norm_mix.shape[0]
    h = x.reshape(T, d_model)
    target = loss_target.reshape(T, d_model)
    p_bf = p.reshape(depth, T, p.shape[-1]).astype(BF16)

    stages, carried = [], []
    for i in range(depth):
        gate = [(n, i) for n in ('ple_w_gate', 'ple_w_proj')]
        mlp = [(n, i) for n in ('ffn_w_up', 'ffn_w_down')]
        if i % 2 == 0:
            mixer = [(n, i // 2) for n in ('mla_w_down', 'mla_w_uq', 'mla_w_ukv', 'mla_w_out')]
            stages += [mixer[:1], mixer[1:]] if i == 0 else [carried + mixer]
            stages.append(mlp + gate)
            carried = []
        else:
            stages.append(carried + [(n, i // 2) for n in ('gmlp_w_in', 'gmlp_ln_g', 'gmlp_ln_b', 'gmlp_w_out')])
            stages.append(mlp)
            carried = gate
    if carried:
        stages.append(carried)

    FW = {n: {} for n in SHARDED}

    def start_weights(st, after):
        keys = stages[st]
        srcs = [W[n][l] if n in F32_PAYLOAD else W[n][l].astype(BF16) for n, l in keys]
        handle, token = _xchg_start(srcs, False, after, "weights_start%d" % st)
        return (keys, handle), [token]

    def wait_weights(pending, st, after):
        keys, handle = pending
        landed = _xchg_wait(handle, after, "weights_wait%d" % st)
        for (n, l), full in zip(keys, landed):
            if SHARD_AXIS[n] == 1:
                FW[n][l] = full.reshape((-1,) + full.shape[2:])
            elif n in ('mla_w_uq', 'mla_w_ukv'):
                FW[n][l] = full
            else:
                FW[n][l] = jnp.transpose(full, (1, 0, 2)).reshape(full.shape[1], -1)
        return landed[0]

    row = lambda a: a.reshape(1, -1)
    cos, sin = _rope_tables(positions.reshape(T, 1), "rope_tables")
    rmat = _rot_matrix()

    def add_and_norm(acc, res, g):
        hh = res + acc
        return hh, _rmsn(hh, g)

    saved = []
    chain = {'stage': 0}
    chain['pending'], _ = start_weights(0, h)

    def advance(after):
        st = chain['stage']
        if st >= len(stages):
            return []
        landed = wait_weights(chain['pending'], st, after)
        chain['stage'] = st + 1
        if st + 1 >= len(stages):
            return []
        chain['pending'], token = start_weights(st + 1, landed)
        return token

    hn = _rms_fwd(h, row(W['norm_mix'][0]), "rms_fwd")
    token = advance(hn)
    for i in range(depth):
        j = i // 2
        s = {}
        s['h0'] = h
        s['hn'] = hn
        if i % 2 == 0:
            gains = [row(W['mla_q_nope_g'][j]), row(W['mla_q_rope_g'][j]), row(W['mla_k_nope_g'][j]),
                     row(W['mla_k_rope_g'][j])]
            lat = _mm(hn, FW['mla_w_down'][j], deps=token, name="mla_down")
            if i == 0:
                token = advance(lat)
            cq, ckv = _prep1_fwd(lat, row(W['mla_q_lora_g'][j]), row(W['mla_kv_lora_g'][j]), "mla_prep1")
            q_raw = _mm(cq, FW['mla_w_uq'][j], out_blocks=HEADS, deps=token, name="mla_uq")
            kv_raw = _mm(ckv, FW['mla_w_ukv'][j], out_blocks=HEADS, name="mla_ukv")
            q, k, v = _prep2_fwd(q_raw, kv_raw, lat, cos, sin, rmat, gains, "mla_prep2")
            o, lse = _attn_fwd(q, k, v, seq, "attn_fwd")
            token = advance(o)
            h, hn2 = _mm(o, FW['mla_w_out'][j], extras=(h,), rows=(row(W['norm_ffn'][i]),), epilogue=add_and_norm,
                         out_dtypes=(F32, BF16), deps=token, name="mla_out")
            s.update(lat=lat, cq=cq, ckv=ckv, q_raw=q_raw, kv_raw=kv_raw, q=q, k=k, v=v, o=o, lse=lse, gains=gains)
        else:
            z = _mm(hn, FW['gmlp_w_in'][j], out_dtypes=(BF16,), name="gmlp_in")
            u, vn = _gelu_ln_fwd(z, row(FW['gmlp_ln_g'][j]), row(FW['gmlp_ln_b'][j]), "gmlp_gelu_ln")
            bs3 = W['gmlp_b_s'][j][:, :, None]
            y = _sgu_fwd(u, vn, W['gmlp_w_s'][j], bs3, "gmlp_sgu")
            h, hn2 = _mm(y, FW['gmlp_w_out'][j], extras=(h,), rows=(row(W['norm_ffn'][i]),), epilogue=add_and_norm,
                         out_dtypes=(F32, BF16), name="gmlp_out")
            token = advance(hn2)
            s.update(z=z, u=u, vn=vn, y=y, bs3=bs3)
        s['h1'] = h
        a, r = _mm(hn2, FW['ffn_w_up'][i], epilogue=lambda acc: (acc, jnp.square(jnp.maximum(acc, 0.0))),
                   out_dtypes=(BF16, BF16), deps=token, name="ffn_up")
        h, hn3 = _mm(r, FW['ffn_w_down'][i], extras=(h,), rows=(row(W['norm_ple'][i]),), epilogue=add_and_norm,
                     out_dtypes=(F32, BF16), name="ffn_down")
        s.update(hn2=hn2, a=a, r=r, h2=h)
        if i % 2 == 1:
            token = advance(hn3)
        gt = _mm(hn3, FW['ple_w_gate'][i], deps=token, name="ple_gate")
        if i % 2 == 0:
            token = advance(gt)
        if i + 1 < depth:
            def gate_and_norm(acc, g_, res, gain):
                hh = res + _sigmoid(g_) * acc
                return acc, hh, _rmsn(hh, gain)

            pp, h, hn = _mm(p_bf[i], FW['ple_w_proj'][i], extras=(gt, h), rows=(row(W['norm_mix'][i + 1]),),
                            epilogue=gate_and_norm, out_dtypes=(F32, F32, BF16), deps=token, name="ple_proj")
        else:
            pp, h = _mm(p_bf[i], FW['ple_w_proj'][i], extras=(gt, h),
                        epilogue=lambda acc, g_, res: (acc, res + _sigmoid(g_) * acc), out_dtypes=(F32, F32),
                        name="ple_proj_last")
        s.update(hn3=hn3, gt=gt, pp=pp)
        saved.append(s)

    dh, d_pp, d_gt, loss_part = _loss_head(h, target, saved[-1]['gt'], saved[-1]['pp'], "loss_head")
    loss = lax.psum(loss_part, MESH_AXES)

    G = {n: [None] * W[n].shape[0] for n in REPLICATED}
    res = {}
    flying = []

    def shard3(n):
        shp = W[n].shape
        return shp[0], int(np.prod(shp[1:-1])), shp[-1]

    def by_owner(g):
        return g.reshape((N_DEV, g.shape[0] // N_DEV) + g.shape[1:])

    def norm_back(h_in, dh_in, gain, below=None):
        def epilogue(acc, x, dres, *rest):
            _, vjp = jax.vjp(_rmsn, x, rest[-1])
            dx, dg = vjp(acc)
            dh = dres + dx
            return (dh,) + (_gate_back(dh, *rest[:2]) if below else (dh,)) + (dg,)

        return dict(extras=(h_in, dh_in) + tuple(below or ()), rows=(gain,), epilogue=epilogue,
                    out_dtypes=(F32, BF16, BF16) if below else (F32, BF16), acc_rows=1)

    def send_grads(tag, grads):
        handle, token = _xchg_start([g for _, g in grads], True, cos, "grads_start_" + tag)
        flying.append((tag, [key for key, _ in grads], handle))
        return [token]

    def land_grads(after):
        tag, keys, handle = flying.pop(0)
        done = []
        for (n, l), full in zip(keys, _xchg_wait(handle, after, "grads_wait_" + tag)):
            dims = shard3(n)
            res[n], token = _adamw(full.reshape((N_DEV,) + dims[1:]), W[n].reshape(dims), M1[n].reshape(dims),
                                   M2[n].reshape(dims), res.get(n), l, "adamw_" + n)
            done.append(token)
        return done

    def start_small(names, tag, after):
        handle, token = _xchg_start([_pack([jnp.stack(G[n]) for n in names])], False, after, "small_start_" + tag)
        return (names, handle), [token]

    def land_small(pending_small, tag, after):
        names, handle = pending_small
        (parts,) = _xchg_wait(handle, after, "small_wait_" + tag)
        like = [W[n] for n in names]
        outs, _ = _adamw(parts, _pack(like)[None], _pack([M1[n] for n in names])[None],
                         _pack([M2[n] for n in names])[None], None, 0, "adamw_small_" + tag)
        unpacked = [_unpack(o, like) for o in outs]
        for idx, n in enumerate(names):
            res[n] = [unpacked[q][idx] for q in range(4)]

    spatial = ['gmlp_w_s', 'gmlp_b_s']
    token = []
    for i in reversed(range(depth)):
        j = i // 2
        s = saved[i]
        g_proj = _mm(p_bf[i], d_pp, ta=True, out_dtypes=(BF16,), name="ple_proj_dw")
        g_proj = jnp.transpose(g_proj.reshape(g_proj.shape[0], N_DEV, -1), (1, 0, 2))
        g_gate = _mm(s['hn3'], d_gt, ta=True, out_dtypes=(BF16,), name="ple_gate_dw")
        dh, dh_bf, dg = _mm(d_gt, FW['ple_w_gate'][i], tb=True, deps=token, name="ple_gate_dx",
                            **norm_back(s['h2'], dh, row(W['norm_ple'][i])))
        G['norm_ple'][i] = dg[0]
        d_a = _mm(dh_bf, FW['ffn_w_down'][i], tb=True, extras=(s['a'],),
                  epilogue=lambda acc, a_: (acc * (2.0 * jnp.maximum(a_.astype(F32), 0.0)),), out_dtypes=(BF16,),
                  name="ffn_down_dx")
        g_down = _mm(s['r'], dh_bf, ta=True, out_dtypes=(BF16,), name="ffn_down_dw")
        g_up = _mm(s['hn2'], d_a, ta=True, out_blocks=N_DEV, out_dtypes=(BF16,), name="ffn_up_dw")
        token = send_grads("mlp%d" % i, [(('ple_w_proj', i), g_proj), (('ple_w_gate', i), by_owner(g_gate)),
                                         (('ffn_w_down', i), by_owner(g_down)), (('ffn_w_up', i), g_up)])
        if len(flying) > 1:
            token = token + land_grads(token[0])
        dh, dh_bf, dg = _mm(d_a, FW['ffn_w_up'][i], tb=True, deps=token, name="ffn_up_dx",
                            **norm_back(s['h1'], dh, row(W['norm_ffn'][i])))
        G['norm_ffn'][i] = dg[0]
        if i % 2 == 0:
            d_o = _mm(dh_bf, FW['mla_w_out'][j], tb=True, out_dtypes=(BF16,), name="mla_out_dx")
            g_out = _mm(s['o'], dh_bf, ta=True, out_dtypes=(BF16,), name="mla_out_dw")
            dq, dk, dv = _attn_bwd(s['q'], s['k'], s['v'], s['o'], d_o, s['lse'], seq, "attn_bwd")
            d_q_raw, d_kv_raw, d_kr, g1, g2, g3, g4 = _prep2_bwd(
                s['q_raw'], s['kv_raw'], s['lat'], cos, sin, rmat, s['gains'], dq, dk, dv, "mla_prep2_bwd")
            G['mla_q_nope_g'][j], G['mla_q_rope_g'][j] = g1[0], g2[0]
            G['mla_k_nope_g'][j], G['mla_k_rope_g'][j] = g3[0], g4[0]
            g_uq = _mm(s['cq'], d_q_raw, ta=True, out_blocks=N_DEV, out_dtypes=(BF16,), name="mla_uq_dw")
            g_ukv = _mm(s['ckv'], d_kv_raw, ta=True, out_blocks=N_DEV, out_dtypes=(BF16,), name="mla_ukv_dw")
            d_cq = _mm(d_q_raw, FW['mla_w_uq'][j], tb=True, out_dtypes=(BF16,), name="mla_uq_dx")
            d_ckv = _mm(d_kv_raw, FW['mla_w_ukv'][j], tb=True, out_dtypes=(BF16,), name="mla_ukv_dx")
            d_lat, dga, dgb = _prep1_bwd(s['lat'], d_cq, d_ckv, d_kr, row(W['mla_q_lora_g'][j]),
                                         row(W['mla_kv_lora_g'][j]), "mla_prep1_bwd")
            G['mla_q_lora_g'][j], G['mla_kv_lora_g'][j] = dga[0], dgb[0]
            g_down = _mm(s['hn'], d_lat, ta=True, out_dtypes=(BF16,), name="mla_down_dw")
            grads = [(('mla_w_out', j), by_owner(g_out)), (('mla_w_uq', j), g_uq), (('mla_w_ukv', j), g_ukv),
                     (('mla_w_down', j), by_owner(g_down))]
            last = (d_lat, FW['mla_w_down'][j], "mla_down_dx")
        else:
            d_y = _mm(dh_bf, FW['gmlp_w_out'][j], tb=True, out_dtypes=(BF16,), name="gmlp_out_dx")
            g_out = _mm(s['y'], dh_bf, ta=True, out_dtypes=(BF16,), name="gmlp_out_dw")
            d_u, d_vn, d_ws, d_bs = _sgu_bwd(s['u'], s['vn'], d_y, W['gmlp_w_s'][j], s['bs3'], "gmlp_sgu_bwd")
            G['gmlp_w_s'][j], G['gmlp_b_s'][j] = d_ws, d_bs[:, :, 0]
            d_z, d_lg, d_lb = _gelu_ln_bwd(s['z'], d_u, d_vn, row(FW['gmlp_ln_g'][j]), row(FW['gmlp_ln_b'][j]),
                                           "gmlp_gelu_ln_bwd")
            g_in = _mm(s['hn'], d_z, ta=True, out_blocks=N_DEV, out_dtypes=(BF16,), name="gmlp_in_dw")
            grads = [(('gmlp_w_out', j), by_owner(g_out)), (('gmlp_ln_g', j), by_owner(d_lg[0])),
                     (('gmlp_ln_b', j), by_owner(d_lb[0])), (('gmlp_w_in', j), g_in)]
            last = (d_z, FW['gmlp_w_in'][j], "gmlp_in_dx")
        token = send_grads("mix%d" % i, grads)
        if len(flying) > 1:
            token = token + land_grads(token[0])
        if i > 0:
            dh, d_pp, d_gt, dg = _mm(last[0], last[1], tb=True, deps=token, name=last[2], **norm_back(
                s['h0'], dh, row(W['norm_mix'][i]), (saved[i - 1]['gt'], saved[i - 1]['pp'])))
        else:
            dh, _, dg = _mm(last[0], last[1], tb=True, deps=token, name=last[2] + "_first",
                            **norm_back(s['h0'], dh, row(W['norm_mix'][i])))
        G['norm_mix'][i] = dg[0]
        token = []
        if i == 1:
            small_a, token = start_small(spatial, "spatial", dh)
    grad_x = dh.reshape(x.shape)

    small_b, _ = start_small([n for n in REPLICATED if n not in spatial], "gains", dh)
    while flying:
        land_grads(dh)
    land_small(small_a, "spatial", dh)
    land_small(small_b, "gains", dh)

    out = lambda q: [res[n][q].reshape(W[n].shape) for n in WEIGHTS]
    return (loss, grad_x, *out(0), *out(1), *out(2), *out(3))
```
